```python
import math
import jax, jax.numpy as jnp
from jax import lax
import numpy as np

D_MODEL = 1024
BATCH = 8
SEQ = 4096
DEPTH = 4

N_MIXERS = 2
N_MLA = (DEPTH + 1) // 2
N_SSD = DEPTH // 2
N_SUB = 3
EPS = 1e-6

D_FF = 2816

MLA_HEADS = 16
Q_LORA = 384
KV_LORA = 256
QK_NOPE = 64
QK_ROPE = 32
QK_HEAD = QK_NOPE + QK_ROPE
V_HEAD = 64
MLA_A_DIM = Q_LORA + KV_LORA + QK_ROPE
ROPE_THETA = 10000.0
Q_BLOCK = 128
MAX_POS_OFFSET = 1024

SSD_EXPAND = 2
D_INNER = SSD_EXPAND * D_MODEL
SSD_HEAD_DIM = 64
SSD_HEADS = D_INNER // SSD_HEAD_DIM
SSD_GROUPS = 4
SSD_STATE = 128
CONV_WIDTH = 4
CHUNK = 128
CONV_DIM = D_INNER + 2 * SSD_GROUPS * SSD_STATE
IN_PROJ_DIM = 2 * D_INNER + 2 * SSD_GROUPS * SSD_STATE + SSD_HEADS
DT_MIN = 0.001
DT_MAX = 0.1

kernel_name = 'hybrid_mla_ssd_macaron_adaln'


def rms_norm(x, gain):
    xf = x.astype(jnp.float32)
    y = xf * lax.rsqrt(jnp.mean(xf * xf, axis=-1, keepdims=True) + EPS)
    return (y * gain.astype(jnp.float32)).astype(x.dtype)


def modulate(x, gain, mod):
    return rms_norm(x, gain) * (1 + mod[:, None, 1]) + mod[:, None, 0]


def swiglu(h, w_gu, w_down):
    g, u = jnp.split(h @ w_gu, 2, axis=-1)
    return (jax.nn.silu(g) * u) @ w_down


def rope_tables(positions):
    inv = 1.0 / (ROPE_THETA ** (jnp.arange(0, QK_ROPE, 2, dtype=jnp.float32) / QK_ROPE))
    ang = positions.astype(jnp.float32)[..., None] * inv
    return jnp.cos(ang), jnp.sin(ang)


def apply_rope(x, cos, sin):
    x1, x2 = jnp.split(x, 2, axis=-1)
    cos = cos[:, :, None].astype(x.dtype)
    sin = sin[:, :, None].astype(x.dtype)
    return jnp.concatenate([x1 * cos - x2 * sin, x1 * sin + x2 * cos], axis=-1)


def causal_block_attention(q, k, v):
    Bn, S, H, Dh = q.shape
    Dv = v.shape[-1]
    nb = S // Q_BLOCK
    scale = Dh ** -0.5
    qb = q.reshape(Bn, nb, Q_BLOCK, H, Dh).transpose(1, 0, 3, 2, 4)
    kt = k.transpose(0, 2, 1, 3)
    vt = v.transpose(0, 2, 1, 3)
    k_pos = jnp.arange(S)

    def one_block(args):
        q_blk, i = args
        s = jnp.einsum('bhqd,bhkd->bhqk', q_blk, kt, preferred_element_type=jnp.float32) * scale
        q_pos = i * Q_BLOCK + jnp.arange(Q_BLOCK)
        s = jnp.where(k_pos[None, :] <= q_pos[:, None], s, -jnp.inf)
        p = jax.nn.softmax(s, axis=-1).astype(vt.dtype)
        return jnp.einsum('bhqk,bhkd->bhqd', p, vt)

    o = lax.map(one_block, (qb, jnp.arange(nb)))
    return o.transpose(1, 0, 3, 2, 4).reshape(Bn, S, H, Dv)


def mla_mixer(h, cos, sin, w_a, q_a_gain, kv_a_gain, w_qb, w_kvb, q_gain, k_gain, w_o):
    Bn, S, _ = h.shape
    q_lat, kv_lat, k_rope = jnp.split(h @ w_a, [Q_LORA, Q_LORA + KV_LORA], axis=-1)
    q = (rms_norm(q_lat, q_a_gain) @ w_qb).reshape(Bn, S, MLA_HEADS, QK_HEAD)
    kv = (rms_norm(kv_lat, kv_a_gain) @ w_kvb).reshape(Bn, S, MLA_HEADS, QK_NOPE + V_HEAD)
    k_nope, v = jnp.split(kv, [QK_NOPE], axis=-1)
    k_rope = jnp.broadcast_to(k_rope[:, :, None, :], (Bn, S, MLA_HEADS, QK_ROPE))
    k = jnp.concatenate([k_nope, k_rope], axis=-1)
    q = rms_norm(q, q_gain)
    k = rms_norm(k, k_gain)
    q = jnp.concatenate([q[..., :QK_NOPE], apply_rope(q[..., QK_NOPE:], cos, sin)], axis=-1)
    k = jnp.concatenate([k[..., :QK_NOPE], apply_rope(k[..., QK_NOPE:], cos, sin)], axis=-1)
    o = causal_block_attention(q, k, v)
    return o.reshape(Bn, S, MLA_HEADS * V_HEAD) @ w_o


def causal_depthwise_conv(u, w, b):
    out = lax.conv_general_dilated(
        u, w[:, None, :].astype(u.dtype), window_strides=(1,), padding=[(CONV_WIDTH - 1, 0)],
        dimension_numbers=('NWC', 'WIO', 'NWC'), feature_group_count=u.shape[-1])
    return out + b


def ssd_chunked_scan(x, dt, A, Bm, Cm):
    Bn, S, H, P = x.shape
    G, N = Bm.shape[2], Bm.shape[3]
    K = H // G
    nc = S // CHUNK
    f32 = jnp.float32
    xdt = (x.astype(f32) * dt[..., None]).reshape(Bn, nc, CHUNK, G, K, P)
    a = (dt * A).reshape(Bn, nc, CHUNK, G, K).transpose(0, 1, 3, 4, 2)
    Bc = Bm.astype(f32).reshape(Bn, nc, CHUNK, G, N)
    Cc = Cm.astype(f32).reshape(Bn, nc, CHUNK, G, N)
    a_cum = jnp.cumsum(a, axis=-1)
    seg = a_cum[..., :, None] - a_cum[..., None, :]
    causal = jnp.tril(jnp.ones((CHUNK, CHUNK), dtype=bool))
    decay_ls = jnp.exp(jnp.where(causal, seg, -jnp.inf))
    cb = jnp.einsum('bclgn,bcsgn->bcgls', Cc, Bc)
    y_diag = jnp.einsum('bcgls,bcgkls,bcsgkp->bclgkp', cb, decay_ls, xdt)
    decay_to_end = jnp.exp(a_cum[..., -1:] - a_cum)
    states = jnp.einsum('bclgn,bcgkl,bclgkp->bcgkpn', Bc, decay_to_end, xdt)
    chunk_decay = jnp.exp(a_cum[..., -1])

    def step(carry, inp):
        st, dec = inp
        return carry * dec[..., None, None] + st, carry

    init = jnp.zeros((Bn, G, K, P, N), f32)
    _, prev = lax.scan(step, init, (states.transpose(1, 0, 2, 3, 4, 5), chunk_decay.transpose(1, 0, 2, 3)))
    prev = prev.transpose(1, 0, 2, 3, 4, 5)
    y_off = jnp.einsum('bclgn,bcgkpn,bcgkl->bclgkp', Cc, prev, jnp.exp(a_cum))
    return (y_diag + y_off).reshape(Bn, S, H, P).astype(x.dtype)


def ssd_mixer(h, w_in, conv_w, conv_b, dt_bias, a_log, d_skip, norm_gain, w_out):
    Bn, S, _ = h.shape
    z, xbc, dt = jnp.split(h @ w_in, [D_INNER, D_INNER + CONV_DIM], axis=-1)
    xbc = jax.nn.silu(causal_depthwise_conv(xbc, conv_w, conv_b))
    xs, Bm, Cm = jnp.split(xbc, [D_INNER, D_INNER + SSD_GROUPS * SSD_STATE], axis=-1)
    xs = xs.reshape(Bn, S, SSD_HEADS, SSD_HEAD_DIM)
    Bm = Bm.reshape(Bn, S, SSD_GROUPS, SSD_STATE)
    Cm = Cm.reshape(Bn, S, SSD_GROUPS, SSD_STATE)
    dt = jax.nn.softplus(dt.astype(jnp.float32) + dt_bias.astype(jnp.float32))
    A = -jnp.exp(a_log.astype(jnp.float32))
    y = ssd_chunked_scan(xs, dt, A, Bm, Cm)
    y = (y + d_skip[:, None] * xs).reshape(Bn, S, D_INNER)
    g = (y * jax.nn.silu(z)).reshape(Bn, S, SSD_GROUPS, D_INNER // SSD_GROUPS)
    g = rms_norm(g, norm_gain.reshape(SSD_GROUPS, D_INNER // SSD_GROUPS)).reshape(Bn, S, D_INNER)
    return g @ w_out


def _fwd_setup_inputs(seed: int = 0) -> dict:
    key = jax.random.key(seed)
    ks = jax.random.split(key, 32)
    f32 = jnp.float32

    def nrm(k, shape, fan_in, mult=1.0):
        return jax.random.normal(k, shape, f32) * (mult * fan_in ** -0.5)

    def gain(k, shape):
        return 1.0 + 0.02 * jax.random.normal(k, shape, f32)

    x = jax.random.normal(ks[0], (BATCH, SEQ, D_MODEL), f32)
    c = jax.random.normal(ks[1], (BATCH, D_MODEL), f32)
    positions = (jax.random.randint(ks[2], (BATCH, 1), 0, MAX_POS_OFFSET, dtype=jnp.int32)
                 + jnp.arange(SEQ, dtype=jnp.int32)[None, :])
    norm_gain = gain(ks[3], (DEPTH, N_SUB, D_MODEL))
    ada_w = nrm(ks[4], (DEPTH, D_MODEL, N_SUB * 3 * D_MODEL), D_MODEL, 0.5)
    ada_b = 0.02 * jax.random.normal(ks[5], (DEPTH, N_SUB * 3 * D_MODEL), f32)
    ffn_w_gu = nrm(ks[6], (DEPTH, 2, D_MODEL, 2 * D_FF), D_MODEL)
    ffn_w_down = nrm(ks[7], (DEPTH, 2, D_FF, D_MODEL), D_FF)
    mla_w_a = nrm(ks[8], (N_MLA, D_MODEL, MLA_A_DIM), D_MODEL)
    mla_q_a_gain = gain(ks[9], (N_MLA, Q_LORA))
    mla_kv_a_gain = gain(ks[10], (N_MLA, KV_LORA))
    mla_w_qb = nrm(ks[11], (N_MLA, Q_LORA, MLA_HEADS * QK_HEAD), Q_LORA)
    mla_w_kvb = nrm(ks[12], (N_MLA, KV_LORA, MLA_HEADS * (QK_NOPE + V_HEAD)), KV_LORA)
    mla_q_gain = gain(ks[13], (N_MLA, QK_HEAD))
    mla_k_gain = gain(ks[14], (N_MLA, QK_HEAD))
    mla_w_o = nrm(ks[15], (N_MLA, MLA_HEADS * V_HEAD, D_MODEL), MLA_HEADS * V_HEAD)
    ssd_w_in = nrm(ks[16], (N_SSD, D_MODEL, IN_PROJ_DIM), D_MODEL)
    ssd_conv_w = nrm(ks[17], (N_SSD, CONV_WIDTH, CONV_DIM), CONV_WIDTH)
    ssd_conv_b = 0.02 * jax.random.normal(ks[18], (N_SSD, CONV_DIM), f32)
    dt0 = jnp.exp(jax.random.uniform(ks[19], (N_SSD, SSD_HEADS), f32, math.log(DT_MIN), math.log(DT_MAX)))
    ssd_dt_bias = dt0 + jnp.log(-jnp.expm1(-dt0))
    ssd_a_log = jnp.log(jax.random.uniform(ks[20], (N_SSD, SSD_HEADS), f32, 1.0, 16.0))
    ssd_d = gain(ks[21], (N_SSD, SSD_HEADS))
    ssd_norm_gain = gain(ks[22], (N_SSD, D_INNER))
    ssd_w_out = nrm(ks[23], (N_SSD, D_INNER, D_MODEL), D_INNER)
    return {
        'x': x, 'c': c, 'positions': positions,
        'norm_gain': norm_gain, 'ada_w': ada_w, 'ada_b': ada_b,
        'ffn_w_gu': ffn_w_gu, 'ffn_w_down': ffn_w_down,
        'mla_w_a': mla_w_a, 'mla_q_a_gain': mla_q_a_gain, 'mla_kv_a_gain': mla_kv_a_gain,
        'mla_w_qb': mla_w_qb, 'mla_w_kvb': mla_w_kvb, 'mla_q_gain': mla_q_gain,
        'mla_k_gain': mla_k_gain, 'mla_w_o': mla_w_o,
        'ssd_w_in': ssd_w_in, 'ssd_conv_w': ssd_conv_w, 'ssd_conv_b': ssd_conv_b,
        'ssd_dt_bias': ssd_dt_bias, 'ssd_a_log': ssd_a_log, 'ssd_d': ssd_d,
        'ssd_norm_gain': ssd_norm_gain, 'ssd_w_out': ssd_w_out,
    }


def _fwd_reference(x, c, positions, norm_gain, ada_w, ada_b, ffn_w_gu, ffn_w_down,
              mla_w_a, mla_q_a_gain, mla_kv_a_gain, mla_w_qb, mla_w_kvb, mla_q_gain,
              mla_k_gain, mla_w_o, ssd_w_in, ssd_conv_w, ssd_conv_b, ssd_dt_bias,
              ssd_a_log, ssd_d, ssd_norm_gain, ssd_w_out):
    Bn = x.shape[0]
    cos, sin = rope_tables(positions)
    mods = jnp.einsum('bd,lde->lbe', jax.nn.silu(c), ada_w) + ada_b[:, None, :]
    mods = mods.reshape(DEPTH, Bn, N_SUB, 3, D_MODEL)
    for i in range(DEPTH):
        m = mods[i]
        j = i // N_MIXERS
        h = modulate(x, norm_gain[i, 0], m[:, 0])
        x = x + 0.5 * m[:, None, 0, 2] * swiglu(h, ffn_w_gu[i, 0], ffn_w_down[i, 0])
        h = modulate(x, norm_gain[i, 1], m[:, 1])
        if i % N_MIXERS == 0:
            y = mla_mixer(h, cos, sin, mla_w_a[j], mla_q_a_gain[j], mla_kv_a_gain[j], mla_w_qb[j],
                          mla_w_kvb[j], mla_q_gain[j], mla_k_gain[j], mla_w_o[j])
        else:
            y = ssd_mixer(h, ssd_w_in[j], ssd_conv_w[j], ssd_conv_b[j], ssd_dt_bias[j], ssd_a_log[j],
                          ssd_d[j], ssd_norm_gain[j], ssd_w_out[j])
        x = x + m[:, None, 1, 2] * y
        h = modulate(x, norm_gain[i, 2], m[:, 2])
        x = x + 0.5 * m[:, None, 2, 2] * swiglu(h, ffn_w_gu[i, 1], ffn_w_down[i, 1])
    return x


import jax as _jax
import jax.numpy as _jnp

TWIN_FORMAT = 'train_step'
FWD_PARAMS = ['x', 'c', 'positions', 'norm_gain', 'ada_w', 'ada_b', 'ffn_w_gu', 'ffn_w_down', 'mla_w_a', 'mla_q_a_gain', 'mla_kv_a_gain', 'mla_w_qb', 'mla_w_kvb', 'mla_q_gain', 'mla_k_gain', 'mla_w_o', 'ssd_w_in', 'ssd_conv_w', 'ssd_conv_b', 'ssd_dt_bias', 'ssd_a_log', 'ssd_d', 'ssd_norm_gain', 'ssd_w_out']
TWIN_WEIGHTS = ['norm_gain', 'ada_w', 'ada_b', 'ffn_w_gu', 'ffn_w_down', 'mla_w_a', 'mla_q_a_gain', 'mla_kv_a_gain', 'mla_w_qb', 'mla_w_kvb', 'mla_q_gain', 'mla_k_gain', 'mla_w_o', 'ssd_w_in', 'ssd_conv_w', 'ssd_conv_b', 'ssd_dt_bias', 'ssd_a_log', 'ssd_d', 'ssd_norm_gain', 'ssd_w_out']
TWIN_DIFF_INPUT = 'x'
TWIN_INPUTS = ['x', 'c', 'positions', 'norm_gain', 'ada_w', 'ada_b', 'ffn_w_gu', 'ffn_w_down', 'mla_w_a', 'mla_q_a_gain', 'mla_kv_a_gain', 'mla_w_qb', 'mla_w_kvb', 'mla_q_gain', 'mla_k_gain', 'mla_w_o', 'ssd_w_in', 'ssd_conv_w', 'ssd_conv_b', 'ssd_dt_bias', 'ssd_a_log', 'ssd_d', 'ssd_norm_gain', 'ssd_w_out', 'loss_target', 'm_norm_gain', 'm_ada_w', 'm_ada_b', 'm_ffn_w_gu', 'm_ffn_w_down', 'm_mla_w_a', 'm_mla_q_a_gain', 'm_mla_kv_a_gain', 'm_mla_w_qb', 'm_mla_w_kvb', 'm_mla_q_gain', 'm_mla_k_gain', 'm_mla_w_o', 'm_ssd_w_in', 'm_ssd_conv_w', 'm_ssd_conv_b', 'm_ssd_dt_bias', 'm_ssd_a_log', 'm_ssd_d', 'm_ssd_norm_gain', 'm_ssd_w_out', 'v_norm_gain', 'v_ada_w', 'v_ada_b', 'v_ffn_w_gu', 'v_ffn_w_down', 'v_mla_w_a', 'v_mla_q_a_gain', 'v_mla_kv_a_gain', 'v_mla_w_qb', 'v_mla_w_kvb', 'v_mla_q_gain', 'v_mla_k_gain', 'v_mla_w_o', 'v_ssd_w_in', 'v_ssd_conv_w', 'v_ssd_conv_b', 'v_ssd_dt_bias', 'v_ssd_a_log', 'v_ssd_d', 'v_ssd_norm_gain', 'v_ssd_w_out']
TWIN_OUTPUTS = ['loss', 'grad_x', 'grad_norm_gain', 'grad_ada_w', 'grad_ada_b', 'grad_ffn_w_gu', 'grad_ffn_w_down', 'grad_mla_w_a', 'grad_mla_q_a_gain', 'grad_mla_kv_a_gain', 'grad_mla_w_qb', 'grad_mla_w_kvb', 'grad_mla_q_gain', 'grad_mla_k_gain', 'grad_mla_w_o', 'grad_ssd_w_in', 'grad_ssd_conv_w', 'grad_ssd_conv_b', 'grad_ssd_dt_bias', 'grad_ssd_a_log', 'grad_ssd_d', 'grad_ssd_norm_gain', 'grad_ssd_w_out', 'delta_norm_gain', 'delta_ada_w', 'delta_ada_b', 'delta_ffn_w_gu', 'delta_ffn_w_down', 'delta_mla_w_a', 'delta_mla_q_a_gain', 'delta_mla_kv_a_gain', 'delta_mla_w_qb', 'delta_mla_w_kvb', 'delta_mla_q_gain', 'delta_mla_k_gain', 'delta_mla_w_o', 'delta_ssd_w_in', 'delta_ssd_conv_w', 'delta_ssd_conv_b', 'delta_ssd_dt_bias', 'delta_ssd_a_log', 'delta_ssd_d', 'delta_ssd_norm_gain', 'delta_ssd_w_out', 'new_m_norm_gain', 'new_m_ada_w', 'new_m_ada_b', 'new_m_ffn_w_gu', 'new_m_ffn_w_down', 'new_m_mla_w_a', 'new_m_mla_q_a_gain', 'new_m_mla_kv_a_gain', 'new_m_mla_w_qb', 'new_m_mla_w_kvb', 'new_m_mla_q_gain', 'new_m_mla_k_gain', 'new_m_mla_w_o', 'new_m_ssd_w_in', 'new_m_ssd_conv_w', 'new_m_ssd_conv_b', 'new_m_ssd_dt_bias', 'new_m_ssd_a_log', 'new_m_ssd_d', 'new_m_ssd_norm_gain', 'new_m_ssd_w_out', 'new_v_norm_gain', 'new_v_ada_w', 'new_v_ada_b', 'new_v_ffn_w_gu', 'new_v_ffn_w_down', 'new_v_mla_w_a', 'new_v_mla_q_a_gain', 'new_v_mla_kv_a_gain', 'new_v_mla_w_qb', 'new_v_mla_w_kvb', 'new_v_mla_q_gain', 'new_v_mla_k_gain', 'new_v_mla_w_o', 'new_v_ssd_w_in', 'new_v_ssd_conv_w', 'new_v_ssd_conv_b', 'new_v_ssd_dt_bias', 'new_v_ssd_a_log', 'new_v_ssd_d', 'new_v_ssd_norm_gain', 'new_v_ssd_w_out']
TWIN_LEAF_KINDS = {'loss': 'loss', 'grad_x': 'grad_x', 'grad_norm_gain': 'grad_w', 'grad_ada_w': 'grad_w', 'grad_ada_b': 'grad_w', 'grad_ffn_w_gu': 'grad_w', 'grad_ffn_w_down': 'grad_w', 'grad_mla_w_a': 'grad_w', 'grad_mla_q_a_gain': 'grad_w', 'grad_mla_kv_a_gain': 'grad_w', 'grad_mla_w_qb': 'grad_w', 'grad_mla_w_kvb': 'grad_w', 'grad_mla_q_gain': 'grad_w', 'grad_mla_k_gain': 'grad_w', 'grad_mla_w_o': 'grad_w', 'grad_ssd_w_in': 'grad_w', 'grad_ssd_conv_w': 'grad_w', 'grad_ssd_conv_b': 'grad_w', 'grad_ssd_dt_bias': 'grad_w', 'grad_ssd_a_log': 'grad_w', 'grad_ssd_d': 'grad_w', 'grad_ssd_norm_gain': 'grad_w', 'grad_ssd_w_out': 'grad_w', 'delta_norm_gain': 'delta_w', 'delta_ada_w': 'delta_w', 'delta_ada_b': 'delta_w', 'delta_ffn_w_gu': 'delta_w', 'delta_ffn_w_down': 'delta_w', 'delta_mla_w_a': 'delta_w', 'delta_mla_q_a_gain': 'delta_w', 'delta_mla_kv_a_gain': 'delta_w', 'delta_mla_w_qb': 'delta_w', 'delta_mla_w_kvb': 'delta_w', 'delta_mla_q_gain': 'delta_w', 'delta_mla_k_gain': 'delta_w', 'delta_mla_w_o': 'delta_w', 'delta_ssd_w_in': 'delta_w', 'delta_ssd_conv_w': 'delta_w', 'delta_ssd_conv_b': 'delta_w', 'delta_ssd_dt_bias': 'delta_w', 'delta_ssd_a_log': 'delta_w', 'delta_ssd_d': 'delta_w', 'delta_ssd_norm_gain': 'delta_w', 'delta_ssd_w_out': 'delta_w', 'new_m_norm_gain': 'new_m', 'new_m_ada_w': 'new_m', 'new_m_ada_b': 'new_m', 'new_m_ffn_w_gu': 'new_m', 'new_m_ffn_w_down': 'new_m', 'new_m_mla_w_a': 'new_m', 'new_m_mla_q_a_gain': 'new_m', 'new_m_mla_kv_a_gain': 'new_m', 'new_m_mla_w_qb': 'new_m', 'new_m_mla_w_kvb': 'new_m', 'new_m_mla_q_gain': 'new_m', 'new_m_mla_k_gain': 'new_m', 'new_m_mla_w_o': 'new_m', 'new_m_ssd_w_in': 'new_m', 'new_m_ssd_conv_w': 'new_m', 'new_m_ssd_conv_b': 'new_m', 'new_m_ssd_dt_bias': 'new_m', 'new_m_ssd_a_log': 'new_m', 'new_m_ssd_d': 'new_m', 'new_m_ssd_norm_gain': 'new_m', 'new_m_ssd_w_out': 'new_m', 'new_v_norm_gain': 'new_v', 'new_v_ada_w': 'new_v', 'new_v_ada_b': 'new_v', 'new_v_ffn_w_gu': 'new_v', 'new_v_ffn_w_down': 'new_v', 'new_v_mla_w_a': 'new_v', 'new_v_mla_q_a_gain': 'new_v', 'new_v_mla_kv_a_gain': 'new_v', 'new_v_mla_w_qb': 'new_v', 'new_v_mla_w_kvb': 'new_v', 'new_v_mla_q_gain': 'new_v', 'new_v_mla_k_gain': 'new_v', 'new_v_mla_w_o': 'new_v', 'new_v_ssd_w_in': 'new_v', 'new_v_ssd_conv_w': 'new_v', 'new_v_ssd_conv_b': 'new_v', 'new_v_ssd_dt_bias': 'new_v', 'new_v_ssd_a_log': 'new_v', 'new_v_ssd_d': 'new_v', 'new_v_ssd_norm_gain': 'new_v', 'new_v_ssd_w_out': 'new_v'}


def _forward(args):
    return _fwd_reference(*[args[k] for k in FWD_PARAMS])


def _output_shape():
    def fwd():
        inp = _fwd_setup_inputs(0)
        return _fwd_reference(*[inp[k] for k in FWD_PARAMS])
    out = _jax.eval_shape(fwd)
    return out.shape, out.dtype

N_MICROBATCH = 1
ADAM_LR = 0.001
ADAM_B1 = 0.9
ADAM_B2 = 0.999
ADAM_EPS = 1e-08
ADAM_WD = 0.01
ADAM_STEP = 10
PER_EXAMPLE_BATCH_AXIS = {'x': 0, 'c': 0, 'positions': 0, 'loss_target': 0}
SHARED_INPUTS = []
_WEIGHT_DTYPES = {'norm_gain': _jnp.float32, 'ada_w': _jnp.float32, 'ada_b': _jnp.float32, 'ffn_w_gu': _jnp.float32, 'ffn_w_down': _jnp.float32, 'mla_w_a': _jnp.float32, 'mla_q_a_gain': _jnp.float32, 'mla_kv_a_gain': _jnp.float32, 'mla_w_qb': _jnp.float32, 'mla_w_kvb': _jnp.float32, 'mla_q_gain': _jnp.float32, 'mla_k_gain': _jnp.float32, 'mla_w_o': _jnp.float32, 'ssd_w_in': _jnp.float32, 'ssd_conv_w': _jnp.float32, 'ssd_conv_b': _jnp.float32, 'ssd_dt_bias': _jnp.float32, 'ssd_a_log': _jnp.float32, 'ssd_d': _jnp.float32, 'ssd_norm_gain': _jnp.float32, 'ssd_w_out': _jnp.float32}
MOMENT_SCALE = {'norm_gain': 6.254948e-01, 'ada_w': 5.672353e-01, 'ada_b': 1.241663e+00, 'ffn_w_gu': 2.717812e-02, 'ffn_w_down': 4.545689e-02, 'mla_w_a': 3.464724e-01, 'mla_q_a_gain': 2.717838e-02, 'mla_kv_a_gain': 1.183013e+00, 'mla_w_qb': 1.458561e-02, 'mla_w_kvb': 1.653554e-01, 'mla_q_gain': 1.873658e-01, 'mla_k_gain': 1.871686e-01, 'mla_w_o': 2.209346e-01, 'ssd_w_in': 9.118692e-02, 'ssd_conv_w': 1.168263e-01, 'ssd_conv_b': 2.598088e-01, 'ssd_dt_bias': 2.494548e-01, 'ssd_a_log': 6.467643e-01, 'ssd_d': 1.003568e+00, 'ssd_norm_gain': 2.208165e+00, 'ssd_w_out': 2.549074e-01}


def _to_microbatches(a, axis):
    t = _jnp.moveaxis(a, axis, 0)
    t = t.reshape((N_MICROBATCH, t.shape[0] // N_MICROBATCH) + t.shape[1:])
    return _jnp.moveaxis(t, 1, axis + 1)


def setup_inputs(seed: int = 0) -> dict:
    inp = _fwd_setup_inputs(seed)
    key = _jax.random.fold_in(_jax.random.key(seed), 7919)
    shape, _ = _output_shape()
    out = dict(inp)
    out["loss_target"] = _jax.random.normal(_jax.random.fold_in(key, 0), shape, _jnp.float32)
    for i, name in enumerate(TWIN_WEIGHTS):
        w = inp[name].astype(_jnp.float32)
        if MOMENT_SCALE is None:
            s = _jnp.sqrt(_jnp.mean(_jnp.square(w)) + 1e-30)
        else:
            s = MOMENT_SCALE[name]
        km, kv = _jax.random.split(_jax.random.fold_in(key, i + 1))
        out[name] = w
        out["m_" + name] = s * _jax.random.normal(km, w.shape, _jnp.float32)
        out["v_" + name] = (s * s) * _jax.random.uniform(kv, w.shape, _jnp.float32, 0.5, 1.5)
    if N_MICROBATCH > 1:
        for name, axis in PER_EXAMPLE_BATCH_AXIS.items():
            out[name] = _to_microbatches(out[name], axis)
    return {'x': out['x'], 'c': out['c'], 'positions': out['positions'], 'norm_gain': out['norm_gain'], 'ada_w': out['ada_w'], 'ada_b': out['ada_b'], 'ffn_w_gu': out['ffn_w_gu'], 'ffn_w_down': out['ffn_w_down'], 'mla_w_a': out['mla_w_a'], 'mla_q_a_gain': out['mla_q_a_gain'], 'mla_kv_a_gain': out['mla_kv_a_gain'], 'mla_w_qb': out['mla_w_qb'], 'mla_w_kvb': out['mla_w_kvb'], 'mla_q_gain': out['mla_q_gain'], 'mla_k_gain': out['mla_k_gain'], 'mla_w_o': out['mla_w_o'], 'ssd_w_in': out['ssd_w_in'], 'ssd_conv_w': out['ssd_conv_w'], 'ssd_conv_b': out['ssd_conv_b'], 'ssd_dt_bias': out['ssd_dt_bias'], 'ssd_a_log': out['ssd_a_log'], 'ssd_d': out['ssd_d'], 'ssd_norm_gain': out['ssd_norm_gain'], 'ssd_w_out': out['ssd_w_out'], 'loss_target': out['loss_target'], 'm_norm_gain': out['m_norm_gain'], 'm_ada_w': out['m_ada_w'], 'm_ada_b': out['m_ada_b'], 'm_ffn_w_gu': out['m_ffn_w_gu'], 'm_ffn_w_down': out['m_ffn_w_down'], 'm_mla_w_a': out['m_mla_w_a'], 'm_mla_q_a_gain': out['m_mla_q_a_gain'], 'm_mla_kv_a_gain': out['m_mla_kv_a_gain'], 'm_mla_w_qb': out['m_mla_w_qb'], 'm_mla_w_kvb': out['m_mla_w_kvb'], 'm_mla_q_gain': out['m_mla_q_gain'], 'm_mla_k_gain': out['m_mla_k_gain'], 'm_mla_w_o': out['m_mla_w_o'], 'm_ssd_w_in': out['m_ssd_w_in'], 'm_ssd_conv_w': out['m_ssd_conv_w'], 'm_ssd_conv_b': out['m_ssd_conv_b'], 'm_ssd_dt_bias': out['m_ssd_dt_bias'], 'm_ssd_a_log': out['m_ssd_a_log'], 'm_ssd_d': out['m_ssd_d'], 'm_ssd_norm_gain': out['m_ssd_norm_gain'], 'm_ssd_w_out': out['m_ssd_w_out'], 'v_norm_gain': out['v_norm_gain'], 'v_ada_w': out['v_ada_w'], 'v_ada_b': out['v_ada_b'], 'v_ffn_w_gu': out['v_ffn_w_gu'], 'v_ffn_w_down': out['v_ffn_w_down'], 'v_mla_w_a': out['v_mla_w_a'], 'v_mla_q_a_gain': out['v_mla_q_a_gain'], 'v_mla_kv_a_gain': out['v_mla_kv_a_gain'], 'v_mla_w_qb': out['v_mla_w_qb'], 'v_mla_w_kvb': out['v_mla_w_kvb'], 'v_mla_q_gain': out['v_mla_q_gain'], 'v_mla_k_gain': out['v_mla_k_gain'], 'v_mla_w_o': out['v_mla_w_o'], 'v_ssd_w_in': out['v_ssd_w_in'], 'v_ssd_conv_w': out['v_ssd_conv_w'], 'v_ssd_conv_b': out['v_ssd_conv_b'], 'v_ssd_dt_bias': out['v_ssd_dt_bias'], 'v_ssd_a_log': out['v_ssd_a_log'], 'v_ssd_d': out['v_ssd_d'], 'v_ssd_norm_gain': out['v_ssd_norm_gain'], 'v_ssd_w_out': out['v_ssd_w_out']}


def _loss(weights, diff, rest, loss_target):
    with _jax.named_scope("forward"):
        args = {**rest, TWIN_DIFF_INPUT: diff, **{k: w.astype(_WEIGHT_DTYPES[k]) for k, w in weights.items()}}
        y = _forward(args)
    with _jax.named_scope("loss_head"):
        err = _jnp.square(y.astype(_jnp.float32) - loss_target)
        return 0.5 * _jnp.sum(_jnp.mean(err, axis=-1)) if err.ndim else 0.5 * err


def _adamw(w, g, m, v):
    m = ADAM_B1 * m + (1.0 - ADAM_B1) * g
    v = ADAM_B2 * v + (1.0 - ADAM_B2) * _jnp.square(g)
    m_hat = m / (1.0 - ADAM_B1 ** ADAM_STEP)
    v_hat = v / (1.0 - ADAM_B2 ** ADAM_STEP)
    delta = -ADAM_LR * (m_hat / (_jnp.sqrt(v_hat) + ADAM_EPS) + ADAM_WD * w)
    return delta, m, v


def reference(x, c, positions, norm_gain, ada_w, ada_b, ffn_w_gu, ffn_w_down, mla_w_a, mla_q_a_gain, mla_kv_a_gain, mla_w_qb, mla_w_kvb, mla_q_gain, mla_k_gain, mla_w_o, ssd_w_in, ssd_conv_w, ssd_conv_b, ssd_dt_bias, ssd_a_log, ssd_d, ssd_norm_gain, ssd_w_out, loss_target, m_norm_gain, m_ada_w, m_ada_b, m_ffn_w_gu, m_ffn_w_down, m_mla_w_a, m_mla_q_a_gain, m_mla_kv_a_gain, m_mla_w_qb, m_mla_w_kvb, m_mla_q_gain, m_mla_k_gain, m_mla_w_o, m_ssd_w_in, m_ssd_conv_w, m_ssd_conv_b, m_ssd_dt_bias, m_ssd_a_log, m_ssd_d, m_ssd_norm_gain, m_ssd_w_out, v_norm_gain, v_ada_w, v_ada_b, v_ffn_w_gu, v_ffn_w_down, v_mla_w_a, v_mla_q_a_gain, v_mla_kv_a_gain, v_mla_w_qb, v_mla_w_kvb, v_mla_q_gain, v_mla_k_gain, v_mla_w_o, v_ssd_w_in, v_ssd_conv_w, v_ssd_conv_b, v_ssd_dt_bias, v_ssd_a_log, v_ssd_d, v_ssd_norm_gain, v_ssd_w_out):
    given = dict(x=x, c=c, positions=positions, norm_gain=norm_gain, ada_w=ada_w, ada_b=ada_b, ffn_w_gu=ffn_w_gu, ffn_w_down=ffn_w_down, mla_w_a=mla_w_a, mla_q_a_gain=mla_q_a_gain, mla_kv_a_gain=mla_kv_a_gain, mla_w_qb=mla_w_qb, mla_w_kvb=mla_w_kvb, mla_q_gain=mla_q_gain, mla_k_gain=mla_k_gain, mla_w_o=mla_w_o, ssd_w_in=ssd_w_in, ssd_conv_w=ssd_conv_w, ssd_conv_b=ssd_conv_b, ssd_dt_bias=ssd_dt_bias, ssd_a_log=ssd_a_log, ssd_d=ssd_d, ssd_norm_gain=ssd_norm_gain, ssd_w_out=ssd_w_out, loss_target=loss_target, m_norm_gain=m_norm_gain, m_ada_w=m_ada_w, m_ada_b=m_ada_b, m_ffn_w_gu=m_ffn_w_gu, m_ffn_w_down=m_ffn_w_down, m_mla_w_a=m_mla_w_a, m_mla_q_a_gain=m_mla_q_a_gain, m_mla_kv_a_gain=m_mla_kv_a_gain, m_mla_w_qb=m_mla_w_qb, m_mla_w_kvb=m_mla_w_kvb, m_mla_q_gain=m_mla_q_gain, m_mla_k_gain=m_mla_k_gain, m_mla_w_o=m_mla_w_o, m_ssd_w_in=m_ssd_w_in, m_ssd_conv_w=m_ssd_conv_w, m_ssd_conv_b=m_ssd_conv_b, m_ssd_dt_bias=m_ssd_dt_bias, m_ssd_a_log=m_ssd_a_log, m_ssd_d=m_ssd_d, m_ssd_norm_gain=m_ssd_norm_gain, m_ssd_w_out=m_ssd_w_out, v_norm_gain=v_norm_gain, v_ada_w=v_ada_w, v_ada_b=v_ada_b, v_ffn_w_gu=v_ffn_w_gu, v_ffn_w_down=v_ffn_w_down, v_mla_w_a=v_mla_w_a, v_mla_q_a_gain=v_mla_q_a_gain, v_mla_kv_a_gain=v_mla_kv_a_gain, v_mla_w_qb=v_mla_w_qb, v_mla_w_kvb=v_mla_w_kvb, v_mla_q_gain=v_mla_q_gain, v_mla_k_gain=v_mla_k_gain, v_mla_w_o=v_mla_w_o, v_ssd_w_in=v_ssd_w_in, v_ssd_conv_w=v_ssd_conv_w, v_ssd_conv_b=v_ssd_conv_b, v_ssd_dt_bias=v_ssd_dt_bias, v_ssd_a_log=v_ssd_a_log, v_ssd_d=v_ssd_d, v_ssd_norm_gain=v_ssd_norm_gain, v_ssd_w_out=v_ssd_w_out)
    weights = {n: given[n] for n in TWIN_WEIGHTS}
    shared = {n: given[n] for n in SHARED_INPUTS}
    per_example = {n: given[n] for n in ['x', 'c', 'positions']}
    grad_fn = _jax.value_and_grad(_loss, argnums=(0, 1))

    def one_microbatch(ex, loss_target):
        ex = dict(ex)
        diff = ex.pop(TWIN_DIFF_INPUT)
        return grad_fn(weights, diff, {**shared, **ex}, loss_target)

    if N_MICROBATCH == 1:
        loss, (grad_w, grad_x) = one_microbatch(per_example, given["loss_target"])
    else:
        def body(carry, xs):
            loss_sum, grad_sum = carry
            l_k, (gw_k, gx_k) = one_microbatch(xs[0], xs[1])
            with _jax.named_scope("update"):
                return (loss_sum + l_k, _jax.tree.map(_jnp.add, grad_sum, gw_k)), gx_k

        init = (_jnp.zeros((), _jnp.float32), _jax.tree.map(_jnp.zeros_like, weights))
        (loss, grad_w), grad_x = _jax.lax.scan(body, init, (per_example, given["loss_target"]))
    with _jax.named_scope("update"):
        delta_w, new_m, new_v = {}, {}, {}
        for n in TWIN_WEIGHTS:
            delta_w[n], new_m[n], new_v[n] = _adamw(weights[n], grad_w[n], given["m_" + n], given["v_" + n])
    return (loss, grad_x, *[grad_w[n] for n in TWIN_WEIGHTS], *[delta_w[n] for n in TWIN_WEIGHTS],
            *[new_m[n] for n in TWIN_WEIGHTS], *[new_v[n] for n in TWIN_WEIGHTS])
```

```python
import functools
import math

import jax
import jax.numpy as jnp
from jax import lax
from jax.experimental import pallas as pl
from jax.experimental.pallas import tpu as pltpu

F32 = jnp.float32
BF16 = jnp.bfloat16
MESH = pl.DeviceIdType.MESH

D_MODEL = 1024
DEPTH = 4
EPS = 1e-6
D_FF = 2816
MLA_HEADS = 16
Q_LORA = 384
KV_LORA = 256
QK_NOPE = 64
QK_ROPE = 32
QK_HEAD = QK_NOPE + QK_ROPE
V_HEAD = 64
ROPE_THETA = 10000.0
D_INNER = 2048
SSD_HEAD_DIM = 64
SSD_HEADS = 32
SSD_GROUPS = 4
SSD_STATE = 128
CONV_WIDTH = 4
CHUNK = 128
CONV_DIM = D_INNER + 2 * SSD_GROUPS * SSD_STATE
ADAM_LR = 0.001
ADAM_B1 = 0.9
ADAM_B2 = 0.999
ADAM_EPS = 1e-08
ADAM_WD = 0.01
ADAM_STEP = 10

N_CHIPS = 4
N_DEV = 8
V7X_VMEM_LIMIT = 56 * 1024 * 1024
PACK_W = 512


def _params(sem=None):
    return pltpu.CompilerParams(dimension_semantics=sem, vmem_limit_bytes=V7X_VMEM_LIMIT)


def _div_tile(n, pref, quantum):
    if n <= pref:
        return n
    t = (pref // quantum) * quantum
    while t >= quantum:
        if n % t == 0:
            return t
        t -= quantum
    return n


def matmul(a, b, mode, name, tm=512, tn=512, tk=2816):
    if mode == "nn":
        (M, K), (K2, N) = a.shape, b.shape
    elif mode == "nt":
        (M, K), (N, K2) = a.shape, b.shape
    else:
        (K, M), (K2, N) = a.shape, b.shape
    assert K == K2, (a.shape, b.shape, mode)
    tm = _div_tile(M, tm, 128 if mode == "tn" else 8)
    tn = _div_tile(N, tn, 128)
    tk = _div_tile(K, tk, 128)
    nk = K // tk
    if mode == "nn":
        a_spec = pl.BlockSpec((tm, tk), lambda i, j, k: (i, k))
        b_spec = pl.BlockSpec((tk, tn), lambda i, j, k: (k, j))
        dims = (((1,), (0,)), ((), ()))
    elif mode == "nt":
        a_spec = pl.BlockSpec((tm, tk), lambda i, j, k: (i, k))
        b_spec = pl.BlockSpec((tn, tk), lambda i, j, k: (j, k))
        dims = (((1,), (1,)), ((), ()))
    else:
        a_spec = pl.BlockSpec((tk, tm), lambda i, j, k: (k, i))
        b_spec = pl.BlockSpec((tk, tn), lambda i, j, k: (k, j))
        dims = (((0,), (0,)), ((), ()))

    def body(a_ref, b_ref, o_ref):
        p = lax.dot_general(a_ref[...].astype(BF16), b_ref[...].astype(BF16), dims, preferred_element_type=F32)
        if nk == 1:
            o_ref[...] = p
        else:
            k = pl.program_id(2)

            @pl.when(k == 0)
            def _():
                o_ref[...] = p

            @pl.when(k > 0)
            def _():
                o_ref[...] += p

    return pl.pallas_call(
        body, name=name, grid=(M // tm, N // tn, nk), in_specs=[a_spec, b_spec],
        out_specs=pl.BlockSpec((tm, tn), lambda i, j, k: (i, j)), out_shape=jax.ShapeDtypeStruct((M, N), F32),
        compiler_params=_params(("parallel", "parallel", "arbitrary")),
    )(a, b)


def mm_op(name):
    @jax.custom_vjp
    def op(a, w, wp):
        return matmul(a, w, "nn", name + "_fwd")

    def fwd(a, w, wp):
        return op(a, w, wp), (a, w)

    def bwd(res, g):
        a, w = res
        return matmul(g, w, "nt", name + "_dx"), jnp.zeros_like(w), matmul(a, g, "tn", name + "_dw")

    op.defvjp(fwd, bwd)
    return op


def _row_tile(rows, widths):
    w = max(widths)
    t = 128 if w > 4096 else (256 if w > 1024 else 512)
    return _div_tile(rows, t, 8)


def _rowwise_fwd(f, rows, vecs, name):
    n_r, n_v = len(rows), len(vecs)
    S = rows[0].shape[0]
    tm = _row_tile(S, [r.shape[1] for r in rows])
    outs = jax.eval_shape(f, *[jax.ShapeDtypeStruct((tm, r.shape[1]), F32) for r in rows], *vecs)

    def body(*refs):
        res = f(*[r[...] for r in refs[: n_r + n_v]])
        for o, r in zip(refs[n_r + n_v:], res):
            o[...] = r

    return pl.pallas_call(
        body, name=name, grid=(S // tm,),
        in_specs=[pl.BlockSpec((tm, r.shape[1]), lambda i: (i, 0)) for r in rows]
        + [pl.BlockSpec(v.shape, lambda i: (0, 0)) for v in vecs],
        out_specs=tuple(pl.BlockSpec((tm, o.shape[1]), lambda i: (i, 0)) for o in outs),
        out_shape=tuple(jax.ShapeDtypeStruct((S, o.shape[1]), F32) for o in outs),
        compiler_params=_params(("parallel",)),
    )(*rows, *vecs)


def _rowwise_bwd(f, rows, vecs, douts, diff_rows, name):
    n_r, n_v, n_o = len(rows), len(vecs), len(douts)
    S = rows[0].shape[0]
    tm = _row_tile(S, [r.shape[1] for r in rows] + [d.shape[1] for d in douts])
    d_idx = [i for i in range(n_r) if diff_rows[i]]

    def body(*refs):
        row_v = [r[...] for r in refs[:n_r]]
        vec_v = [r[...] for r in refs[n_r:n_r + n_v]]
        dout_v = tuple(r[...] for r in refs[n_r + n_v:n_r + n_v + n_o])
        out_refs = refs[n_r + n_v + n_o:]

        def g(*args):
            full = list(row_v)
            for j, i in enumerate(d_idx):
                full[i] = args[j]
            return f(*full, *args[len(d_idx):])

        _, vjp = jax.vjp(g, *[row_v[i] for i in d_idx], *vec_v)
        grads = vjp(dout_v)
        for j in range(len(d_idx)):
            out_refs[j][...] = grads[j]
        step = pl.program_id(0)
        for j in range(n_v):
            gv, o = grads[len(d_idx) + j], out_refs[len(d_idx) + j]

            @pl.when(step == 0)
            def _(gv=gv, o=o):
                o[...] = gv

            @pl.when(step > 0)
            def _(gv=gv, o=o):
                o[...] += gv

    res = pl.pallas_call(
        body, name=name, grid=(S // tm,),
        in_specs=[pl.BlockSpec((tm, r.shape[1]), lambda i: (i, 0)) for r in rows]
        + [pl.BlockSpec(v.shape, lambda i: (0, 0)) for v in vecs]
        + [pl.BlockSpec((tm, d.shape[1]), lambda i: (i, 0)) for d in douts],
        out_specs=tuple([pl.BlockSpec((tm, rows[i].shape[1]), lambda i_: (i_, 0)) for i in d_idx]
                        + [pl.BlockSpec(v.shape, lambda i: (0, 0)) for v in vecs]),
        out_shape=tuple([jax.ShapeDtypeStruct(rows[i].shape, F32) for i in d_idx]
                        + [jax.ShapeDtypeStruct(v.shape, F32) for v in vecs]),
        compiler_params=_params(("arbitrary",)),
    )(*rows, *vecs, *douts)
    drows = [None] * n_r
    for j, i in enumerate(d_idx):
        drows[i] = res[j]
    for i in range(n_r):
        if drows[i] is None:
            drows[i] = jnp.zeros_like(rows[i])
    return tuple(drows) + tuple(res[len(d_idx):])


def rowwise_op(f, name, n_rows, diff_rows=None):
    diff = tuple(diff_rows) if diff_rows is not None else (True,) * n_rows

    @jax.custom_vjp
    def op(*args):
        return _rowwise_fwd(f, args[:n_rows], args[n_rows:], name + "_fwd")

    def fwd(*args):
        return op(*args), args

    def bwd(args, douts):
        return _rowwise_bwd(f, args[:n_rows], args[n_rows:], douts, diff, name + "_bwd")

    op.defvjp(fwd, bwd)
    return op


def _rms(x, gain):
    return x * lax.rsqrt(jnp.mean(x * x, axis=-1, keepdims=True) + EPS) * gain


def _silu(x):
    return x * jax.nn.sigmoid(x)


def _f_modulate(x, gain, shift, scale):
    return (_rms(x, gain) * (1.0 + scale) + shift,)


def _f_rms(x, gain):
    return (_rms(x, gain),)


def _f_swiglu(gu):
    n = gu.shape[1] // 2
    return (_silu(gu[:, :n]) * gu[:, n:],)


def _f_resid(coef):
    def f(x, y, gate):
        return (x + (coef * gate) * y,)
    return f


def _f_gated_norm(y, z, gain):
    g = y * _silu(z)
    n = g.shape[1] // SSD_GROUPS
    return (jnp.concatenate([_rms(g[:, i * n:(i + 1) * n], gain[:, i * n:(i + 1) * n]) for i in range(SSD_GROUPS)], axis=1),)


_NT = (((1,), (1,)), ((), ()))


def _dot(a, b):
    return jnp.dot(a.astype(BF16), b.astype(BF16), preferred_element_type=F32)


def _dot_nt(a, b):
    return lax.dot_general(a.astype(BF16), b.astype(BF16), _NT, preferred_element_type=F32)


def _attn_tile(S):
    return _div_tile(S, 512, 128)


def _causal(t, transposed=False):
    r = lax.broadcasted_iota(jnp.int32, (t, t), 0)
    c = lax.broadcasted_iota(jnp.int32, (t, t), 1)
    return r <= c if transposed else r >= c


def attn_fwd(q, k, v):
    H, S, dk = q.shape
    dv = v.shape[-1]
    t = _attn_tile(S)
    n = S // t
    scale = dk ** -0.5

    def body(q_ref, k_ref, v_ref, o_ref, lse_ref, m_s, l_s, acc_s):
        qi, kj = pl.program_id(1), pl.program_id(2)

        @pl.when(kj == 0)
        def _():
            m_s[...] = jnp.full((t, 1), -jnp.inf, F32)
            l_s[...] = jnp.zeros((t, 1), F32)
            acc_s[...] = jnp.zeros((t, dv), F32)

        def step(masked):
            s = _dot_nt(q_ref[...], k_ref[...]) * scale
            if masked:
                s = jnp.where(_causal(t), s, -jnp.inf)
            m_new = jnp.maximum(m_s[...], jnp.max(s, axis=-1, keepdims=True))
            alpha = jnp.exp(m_s[...] - m_new)
            p = jnp.exp(s - m_new)
            l_s[...] = alpha * l_s[...] + jnp.sum(p, axis=-1, keepdims=True)
            acc_s[...] = alpha * acc_s[...] + _dot(p, v_ref[...])
            m_s[...] = m_new

        @pl.when(kj < qi)
        def _():
            step(False)

        @pl.when(kj == qi)
        def _():
            step(True)

        @pl.when(kj == n - 1)
        def _():
            o_ref[...] = acc_s[...] / l_s[...]
            lse_ref[...] = m_s[...] + jnp.log(l_s[...])

    return pl.pallas_call(
        body, name="attn_fwd", grid=(H, n, n),
        in_specs=[pl.BlockSpec((None, t, dk), lambda h, i, j: (h, i, 0)),
                  pl.BlockSpec((None, t, dk), lambda h, i, j: (h, jnp.minimum(i, j), 0)),
                  pl.BlockSpec((None, t, dv), lambda h, i, j: (h, jnp.minimum(i, j), 0))],
        out_specs=(pl.BlockSpec((None, t, dv), lambda h, i, j: (h, i, 0)),
                   pl.BlockSpec((None, t, 1), lambda h, i, j: (h, i, 0))),
        out_shape=(jax.ShapeDtypeStruct((H, S, dv), F32), jax.ShapeDtypeStruct((H, S, 1), F32)),
        scratch_shapes=[pltpu.VMEM((t, 1), F32), pltpu.VMEM((t, 1), F32), pltpu.VMEM((t, dv), F32)],
        compiler_params=_params(("parallel", "parallel", "arbitrary")),
    )(q, k, v)


def attn_bwd(q, k, v, o, lse, do):
    H, S, dk = q.shape
    dv = v.shape[-1]
    t = _attn_tile(S)
    n = S // t
    scale = dk ** -0.5

    def dq_body(q_ref, k_ref, v_ref, o_ref, do_ref, lse_ref, dq_ref, delta_ref):
        qi, kj = pl.program_id(1), pl.program_id(2)

        @pl.when(kj == 0)
        def _():
            delta_ref[...] = jnp.sum(do_ref[...] * o_ref[...], axis=-1, keepdims=True)
            dq_ref[...] = jnp.zeros((t, dk), F32)

        def step(masked):
            s = _dot_nt(q_ref[...], k_ref[...]) * scale
            p = jnp.exp(s - lse_ref[...])
            if masked:
                p = jnp.where(_causal(t), p, 0.0)
            dp = _dot_nt(do_ref[...], v_ref[...])
            ds = p * (dp - delta_ref[...])
            dq_ref[...] += _dot(ds, k_ref[...]) * scale

        @pl.when(kj < qi)
        def _():
            step(False)

        @pl.when(kj == qi)
        def _():
            step(True)

    q_spec = pl.BlockSpec((None, t, dk), lambda h, i, j: (h, i, 0))
    row_v = pl.BlockSpec((None, t, dv), lambda h, i, j: (h, i, 0))
    col1 = pl.BlockSpec((None, t, 1), lambda h, i, j: (h, i, 0))
    dq, delta = pl.pallas_call(
        dq_body, name="attn_bwd_dq", grid=(H, n, n),
        in_specs=[q_spec,
                  pl.BlockSpec((None, t, dk), lambda h, i, j: (h, jnp.minimum(i, j), 0)),
                  pl.BlockSpec((None, t, dv), lambda h, i, j: (h, jnp.minimum(i, j), 0)),
                  row_v, row_v, col1],
        out_specs=(q_spec, col1),
        out_shape=(jax.ShapeDtypeStruct((H, S, dk), F32), jax.ShapeDtypeStruct((H, S, 1), F32)),
        compiler_params=_params(("parallel", "parallel", "arbitrary")),
    )(q, k, v, o, do, lse)

    lse_r = lse.reshape(H, 1, S)
    delta_r = delta.reshape(H, 1, S)

    def dkv_body(q_ref, k_ref, v_ref, do_ref, lse_ref, delta_ref, dk_ref, dv_ref):
        kj, qi = pl.program_id(1), pl.program_id(2)

        @pl.when(qi == 0)
        def _():
            dk_ref[...] = jnp.zeros((t, dk), F32)
            dv_ref[...] = jnp.zeros((t, dv), F32)

        def step(masked):
            st = _dot_nt(k_ref[...], q_ref[...]) * scale
            pt = jnp.exp(st - lse_ref[...])
            if masked:
                pt = jnp.where(_causal(t, transposed=True), pt, 0.0)
            dv_ref[...] += _dot(pt, do_ref[...])
            dpt = _dot_nt(v_ref[...], do_ref[...])
            dst = pt * (dpt - delta_ref[...])
            dk_ref[...] += _dot(dst, q_ref[...]) * scale

        @pl.when(qi > kj)
        def _():
            step(False)

        @pl.when(qi == kj)
        def _():
            step(True)

    qrow = lambda h, j, i: (h, jnp.maximum(i, j), 0)
    dk_, dv_ = pl.pallas_call(
        dkv_body, name="attn_bwd_dkv", grid=(H, n, n),
        in_specs=[pl.BlockSpec((None, t, dk), qrow),
                  pl.BlockSpec((None, t, dk), lambda h, j, i: (h, j, 0)),
                  pl.BlockSpec((None, t, dv), lambda h, j, i: (h, j, 0)),
                  pl.BlockSpec((None, t, dv), qrow),
                  pl.BlockSpec((None, 1, t), lambda h, j, i: (h, 0, jnp.maximum(i, j))),
                  pl.BlockSpec((None, 1, t), lambda h, j, i: (h, 0, jnp.maximum(i, j)))],
        out_specs=(pl.BlockSpec((None, t, dk), lambda h, j, i: (h, j, 0)),
                   pl.BlockSpec((None, t, dv), lambda h, j, i: (h, j, 0))),
        out_shape=(jax.ShapeDtypeStruct((H, S, dk), F32), jax.ShapeDtypeStruct((H, S, dv), F32)),
        compiler_params=_params(("parallel", "parallel", "arbitrary")),
    )(q, k, v, do, lse_r, delta_r)
    return dq, dk_, dv_


@jax.custom_vjp
def attention(q, k, v):
    return attn_fwd(q, k, v)[0]


def _attention_fwd(q, k, v):
    o, lse = attn_fwd(q, k, v)
    return o, (q, k, v, o, lse)


def _attention_bwd(res, do):
    return attn_bwd(*res, do)


attention.defvjp(_attention_fwd, _attention_bwd)


def _ssd_specs(hb, L, P, N, order):
    xs = pl.BlockSpec((hb, L, P), lambda g, c: (g, order(c), 0))
    col = pl.BlockSpec((hb, L, 1), lambda g, c: (g, order(c), 0))
    row = pl.BlockSpec((hb, 1, L), lambda g, c: (g, 0, order(c)))
    bc = pl.BlockSpec((None, L, N), lambda g, c: (g, order(c), 0))
    st = pl.BlockSpec((hb, None, N, P), lambda g, c: (g, order(c), 0, 0))
    return xs, col, row, bc, st


def ssd_fwd(x, dt, ac, Bm, Cm):
    H, S, P = x.shape
    G, _, N = Bm.shape
    hb, L = H // G, CHUNK
    nc = S // L
    acr = ac.reshape(H, 1, S)

    def body(x_ref, dt_ref, ac_ref, acr_ref, b_ref, c_ref, y_ref, hp_ref, h_s):
        @pl.when(pl.program_id(1) == 0)
        def _():
            h_s[...] = jnp.zeros((hb, N, P), F32)

        Bv, Cv = b_ref[...], c_ref[...]
        cb = _dot_nt(Cv, Bv)
        bt = Bv.T
        mask = _causal(L)
        for j in range(hb):
            a = ac_ref[j]
            lm = jnp.exp(jnp.where(mask, a - acr_ref[j], -jnp.inf))
            xdt = x_ref[j] * dt_ref[j]
            h = h_s[j]
            hp_ref[j] = h
            y_ref[j] = _dot(cb * lm, xdt) + jnp.exp(a) * _dot(Cv, h)
            al = ac_ref[j, L - 1:L, :]
            h_s[j] = jnp.exp(al) * h + _dot(bt, xdt * jnp.exp(al - a))

    xs, col, row, bc, st = _ssd_specs(hb, L, P, N, lambda c: c)
    return pl.pallas_call(
        body, name="ssd_fwd", grid=(G, nc), in_specs=[xs, col, col, row, bc, bc], out_specs=(xs, st),
        out_shape=(jax.ShapeDtypeStruct((H, S, P), F32), jax.ShapeDtypeStruct((H, nc, N, P), F32)),
        scratch_shapes=[pltpu.VMEM((hb, N, P), F32)],
        compiler_params=_params(("parallel", "arbitrary")),
    )(x, dt, ac, acr, Bm, Cm)


def ssd_bwd(x, dt, ac, Bm, Cm, hp, dy):
    H, S, P = x.shape
    G, _, N = Bm.shape
    hb, L = H // G, CHUNK
    nc = S // L
    acr = ac.reshape(H, 1, S)

    def body(x_ref, dt_ref, ac_ref, acr_ref, b_ref, c_ref, hp_ref, dy_ref, dx_ref, ddt_ref, dac_ref, db_ref, dc_ref, dh_s):
        @pl.when(pl.program_id(1) == 0)
        def _():
            dh_s[...] = jnp.zeros((hb, N, P), F32)

        Bv, Cv = b_ref[...], c_ref[...]
        cb = _dot_nt(Cv, Bv)
        ct = Cv.T
        mask = _causal(L)
        last = lax.broadcasted_iota(jnp.int32, (L, 1), 0) == L - 1
        db = jnp.zeros((L, N), F32)
        dc = jnp.zeros((L, N), F32)
        for j in range(hb):
            a, xv, dtv, g, h, dh = ac_ref[j], x_ref[j], dt_ref[j], dy_ref[j], hp_ref[j], dh_s[j]
            lm = jnp.exp(jnp.where(mask, a - acr_ref[j], -jnp.inf))
            gm = cb * lm
            xdt = xv * dtv
            e = jnp.exp(a)
            al = ac_ref[j, L - 1:L, :]
            dte = jnp.exp(al - a)
            el = jnp.exp(al)
            dcb = _dot_nt(g, xdt) * lm
            dseg = dcb * cb
            ch = _dot(Cv, h)
            bdh = _dot(Bv, dh)
            dxdt = _dot(gm.T, g) + dte * bdh
            dc += _dot(dcb, Bv) + e * _dot_nt(g, h)
            db += _dot(dcb.T, Cv) + _dot_nt(xdt * dte, dh)
            d_e = jnp.sum(g * ch, axis=-1, keepdims=True)
            d_dte = jnp.sum(xdt * bdh, axis=-1, keepdims=True)
            d_el = jnp.sum(h * dh, keepdims=True)
            dac = (jnp.sum(dseg, axis=-1, keepdims=True) - jnp.sum(dseg.T, axis=-1, keepdims=True) + d_e * e - d_dte * dte)
            d_al = jnp.sum(d_dte * dte, keepdims=True) + d_el * el
            dac_ref[j] = dac + jnp.where(last, d_al, 0.0)
            dx_ref[j] = dxdt * dtv
            ddt_ref[j] = jnp.sum(dxdt * xv, axis=-1, keepdims=True)
            dh_s[j] = el * dh + _dot(ct, e * g)
        db_ref[...] = db
        dc_ref[...] = dc

    xs, col, row, bc, st = _ssd_specs(hb, L, P, N, lambda c: nc - 1 - c)
    return pl.pallas_call(
        body, name="ssd_bwd", grid=(G, nc), in_specs=[xs, col, col, row, bc, bc, st, xs],
        out_specs=(xs, col, col, bc, bc),
        out_shape=(jax.ShapeDtypeStruct((H, S, P), F32), jax.ShapeDtypeStruct((H, S, 1), F32),
                   jax.ShapeDtypeStruct((H, S, 1), F32), jax.ShapeDtypeStruct((G, S, N), F32),
                   jax.ShapeDtypeStruct((G, S, N), F32)),
        scratch_shapes=[pltpu.VMEM((hb, N, P), F32)],
        compiler_params=_params(("parallel", "arbitrary")),
    )(x, dt, ac, acr, Bm, Cm, hp, dy)


@jax.custom_vjp
def ssd_scan(x, dt, ac, Bm, Cm):
    return ssd_fwd(x, dt, ac, Bm, Cm)[0]


def _ssd_scan_fwd(x, dt, ac, Bm, Cm):
    y, hp = ssd_fwd(x, dt, ac, Bm, Cm)
    return y, (x, dt, ac, Bm, Cm, hp)


def _ssd_scan_bwd(res, dy):
    return ssd_bwd(*res, dy)


ssd_scan.defvjp(_ssd_scan_fwd, _ssd_scan_bwd)


def _vec(v):
    return v.reshape(1, -1)


def _ffn(x, gain, m3, w_gu, w_down, x_gu, x_down):
    h, = rowwise_op(_f_modulate, "modulate", 1)(x, _vec(gain), _vec(m3[0]), _vec(m3[1]))
    gu = mm_op("ffn_gu")(h, w_gu, x_gu)
    a, = rowwise_op(_f_swiglu, "swiglu", 1)(gu)
    y = mm_op("ffn_down")(a, w_down, x_down)
    return rowwise_op(_f_resid(0.5), "resid_half", 2)(x, y, _vec(m3[2]))[0]


def _rope_tables(positions):
    inv = 1.0 / (ROPE_THETA ** (jnp.arange(0, QK_ROPE, 2, dtype=F32) / QK_ROPE))
    ang = positions.astype(F32)[:, None] * inv
    return jnp.cos(ang), jnp.sin(ang)


def _rope(x, cos, sin):
    x1, x2 = jnp.split(x, 2, axis=-1)
    cos, sin = cos[:, None], sin[:, None]
    return jnp.concatenate([x1 * cos - x2 * sin, x1 * sin + x2 * cos], axis=-1)


def _mla(h, cos, sin, P, W, X, j):
    S = h.shape[0]
    lat = mm_op("mla_a")(h, W["mla_w_a"][j], X["mla_w_a"][j])
    q_lat, kv_lat, k_rope = lat[:, :Q_LORA], lat[:, Q_LORA:Q_LORA + KV_LORA], lat[:, Q_LORA + KV_LORA:]
    qn, = rowwise_op(_f_rms, "rms_lat", 1)(q_lat, _vec(P["mla_q_a_gain"][j]))
    kvn, = rowwise_op(_f_rms, "rms_lat", 1)(kv_lat, _vec(P["mla_kv_a_gain"][j]))
    q = mm_op("mla_qb")(qn, W["mla_w_qb"][j], X["mla_w_qb"][j]).reshape(S, MLA_HEADS, QK_HEAD)
    kv = mm_op("mla_kvb")(kvn, W["mla_w_kvb"][j], X["mla_w_kvb"][j]).reshape(S, MLA_HEADS, QK_NOPE + V_HEAD)
    k_nope, v = kv[..., :QK_NOPE], kv[..., QK_NOPE:]
    k = jnp.concatenate([k_nope, jnp.broadcast_to(k_rope[:, None, :], (S, MLA_HEADS, QK_ROPE))], axis=-1)
    q = _rms(q, P["mla_q_gain"][j])
    k = _rms(k, P["mla_k_gain"][j])
    q = jnp.concatenate([q[..., :QK_NOPE], _rope(q[..., QK_NOPE:], cos, sin)], axis=-1)
    k = jnp.concatenate([k[..., :QK_NOPE], _rope(k[..., QK_NOPE:], cos, sin)], axis=-1)
    o = attention(q.transpose(1, 0, 2), k.transpose(1, 0, 2), v.transpose(1, 0, 2))
    o = o.transpose(1, 0, 2).reshape(S, MLA_HEADS * V_HEAD)
    return mm_op("mla_o")(o, W["mla_w_o"][j], X["mla_w_o"][j])


def _ssd(h, P, W, X, j):
    S = h.shape[0]
    z = mm_op("ssd_in_z")(h, W["ssd_w_z"][j], X["ssd_w_z"][j])
    xbc = mm_op("ssd_in_xbc")(h, W["ssd_w_xbc"][j], X["ssd_w_xbc"][j])
    dtr = mm_op("ssd_in_dt")(h, W["ssd_w_dt"][j], X["ssd_w_dt"][j])
    cw, cb = P["ssd_conv_w"][j], P["ssd_conv_b"][j]
    up = jnp.pad(xbc, ((CONV_WIDTH - 1, 0), (0, 0)))
    conv = sum(cw[t][None, :] * up[t:t + S] for t in range(CONV_WIDTH)) + cb[None, :]
    xbc = _silu(conv)
    xs = xbc[:, :D_INNER].reshape(S, SSD_HEADS, SSD_HEAD_DIM).transpose(1, 0, 2)
    Bm = xbc[:, D_INNER:D_INNER + SSD_GROUPS * SSD_STATE].reshape(S, SSD_GROUPS, SSD_STATE).transpose(1, 0, 2)
    Cm = xbc[:, D_INNER + SSD_GROUPS * SSD_STATE:].reshape(S, SSD_GROUPS, SSD_STATE).transpose(1, 0, 2)
    dt = jax.nn.softplus(dtr + P["ssd_dt_bias"][j][None, :])
    A = -jnp.exp(P["ssd_a_log"][j])
    a = (dt * A[None, :]).reshape(S // CHUNK, CHUNK, SSD_HEADS)
    ac = jnp.cumsum(a, axis=1).reshape(S, SSD_HEADS)
    y = ssd_scan(xs, dt.T[:, :, None], ac.T[:, :, None], Bm, Cm)
    y = y + P["ssd_d"][j][:, None, None] * xs
    y = y.transpose(1, 0, 2).reshape(S, D_INNER)
    g, = rowwise_op(_f_gated_norm, "gated_norm", 2)(y, z, _vec(P["ssd_norm_gain"][j]))
    return mm_op("ssd_out")(g, W["ssd_w_out"][j], X["ssd_w_out"][j])


def trunk(x, mods, P, W, X, positions):
    cos, sin = _rope_tables(positions)
    for i in range(DEPTH):
        m, j = mods[i], i // 2
        x = _ffn(x, P["norm_gain"][i, 0], m[0], W["ffn_w_gu"][i][0], W["ffn_w_down"][i][0],
                 X["ffn_w_gu"][i][0], X["ffn_w_down"][i][0])
        h, = rowwise_op(_f_modulate, "modulate", 1)(x, _vec(P["norm_gain"][i, 1]), _vec(m[1, 0]), _vec(m[1, 1]))
        y = _mla(h, cos, sin, P, W, X, j) if i % 2 == 0 else _ssd(h, P, W, X, j)
        x, = rowwise_op(_f_resid(1.0), "resid_full", 2)(x, y, _vec(m[1, 2]))
        x = _ffn(x, P["norm_gain"][i, 2], m[2], W["ffn_w_gu"][i][1], W["ffn_w_down"][i][1],
                 X["ffn_w_gu"][i][1], X["ffn_w_down"][i][1])
    return x


def loss_head(y, target):
    S, D = y.shape
    tm = _row_tile(S, [D])

    def body(y_ref, t_ref, dy_ref, l_ref):
        d = y_ref[...] - t_ref[...]
        dy_ref[...] = d * (1.0 / D)
        part = jnp.sum(d * d, axis=0, keepdims=True) * (0.5 / D)

        @pl.when(pl.program_id(0) == 0)
        def _():
            l_ref[...] = part

        @pl.when(pl.program_id(0) > 0)
        def _():
            l_ref[...] += part

    return pl.pallas_call(
        body, name="loss_head", grid=(S // tm,),
        in_specs=[pl.BlockSpec((tm, D), lambda i: (i, 0))] * 2,
        out_specs=(pl.BlockSpec((tm, D), lambda i: (i, 0)), pl.BlockSpec((1, D), lambda i: (0, 0))),
        out_shape=(jax.ShapeDtypeStruct((S, D), F32), jax.ShapeDtypeStruct((1, D), F32)),
        compiler_params=_params(("arbitrary",)),
    )(y, target)


def adamw(w, m, v, g):
    R, C = w.shape
    tr = _div_tile(R, max(8, (1 << 19) // C // 8 * 8), 8)
    c1 = 1.0 / (1.0 - ADAM_B1 ** ADAM_STEP)
    c2 = 1.0 / (1.0 - ADAM_B2 ** ADAM_STEP)

    def body(w_ref, m_ref, v_ref, g_ref, d_ref, nm_ref, nv_ref):
        gv = g_ref[...]
        nm = ADAM_B1 * m_ref[...] + (1.0 - ADAM_B1) * gv
        nv = ADAM_B2 * v_ref[...] + (1.0 - ADAM_B2) * (gv * gv)
        d_ref[...] = -ADAM_LR * ((nm * c1) / (jnp.sqrt(nv * c2) + ADAM_EPS) + ADAM_WD * w_ref[...])
        nm_ref[...] = nm
        nv_ref[...] = nv

    spec = pl.BlockSpec((tr, C), lambda i: (i, 0))
    return pl.pallas_call(
        body, name="adamw", grid=(R // tr,), in_specs=[spec] * 4, out_specs=(spec,) * 3,
        out_shape=(jax.ShapeDtypeStruct((R, C), F32),) * 3, compiler_params=_params(("parallel",)),
    )(w, m, v, g)


def sum_parts(parts, name, out_dtype=F32):
    R, C = parts[0].shape
    tr = _div_tile(R, max(8, (1 << 19) // C // 8 * 8), 8)
    n = len(parts)

    def body(*refs):
        acc = refs[0][...].astype(F32)
        for r in refs[1:n]:
            acc = acc + r[...].astype(F32)
        refs[n][...] = acc.astype(out_dtype)

    spec = pl.BlockSpec((tr, C), lambda i: (i, 0))
    return pl.pallas_call(
        body, name=name, grid=(R // tr,), in_specs=[spec] * n, out_specs=spec,
        out_shape=jax.ShapeDtypeStruct((R, C), out_dtype), compiler_params=_params(("parallel",)),
    )(*parts)


def _coords():
    return lax.axis_index("x"), lax.axis_index("y"), lax.axis_index("c")


def allgather_small(v, name):
    m_per, n = v.shape

    def body(x_ref, out_ref, send_sems, recv_sems, local_sem):
        x, y, c = _coords()
        me, sibling = (x, y, c), (x, y, 1 - c)
        chips = [(1 - x, y), (x, 1 - y), (1 - x, 1 - y)]

        def rows(px, py, pc):
            return out_ref.at[pl.ds((4 * px + 2 * py + pc) * m_per, m_per), :]

        def copy(k, block, to, src=None):
            return pltpu.make_async_remote_copy(
                src_ref=rows(*block) if src is None else src, dst_ref=rows(*block),
                send_sem=send_sems.at[k], recv_sem=recv_sems.at[k], device_id=to, device_id_type=MESH)

        mine = pltpu.make_async_copy(x_ref, rows(*me), local_sem)
        mine.start()
        first = [copy(0, me, sibling, src=x_ref)]
        first += [copy(1 + j, me, (*chip, c), src=x_ref) for j, chip in enumerate(chips)]
        for cp in first:
            cp.start()
        passed = [copy(4 + j, (*chip, c), sibling) for j, chip in enumerate(chips)]
        for j, chip in enumerate(chips):
            copy(1 + j, (*chip, c), me).wait_recv()
            passed[j].start()
        copy(0, sibling, me).wait_recv()
        for j, chip in enumerate(chips):
            copy(4 + j, (*chip, 1 - c), me).wait_recv()
        for cp in first + passed:
            cp.wait_send()
        mine.wait()

    return pl.pallas_call(
        body, name=name, out_shape=jax.ShapeDtypeStruct((N_DEV * m_per, n), v.dtype),
        in_specs=[pl.BlockSpec(memory_space=pltpu.VMEM)], out_specs=pl.BlockSpec(memory_space=pltpu.VMEM),
        scratch_shapes=[pltpu.SemaphoreType.DMA((7,)), pltpu.SemaphoreType.DMA((7,)), pltpu.SemaphoreType.DMA],
        compiler_params=pltpu.CompilerParams(vmem_limit_bytes=V7X_VMEM_LIMIT),
    )(v)


def allgather_chips(v, name):
    R, C = v.shape

    def body(x_ref, out_ref, send_sems, recv_sems, local_sem):
        x, y, c = _coords()
        chips = [(1 - x, y), (x, 1 - y), (1 - x, 1 - y)]
        mine = pltpu.make_async_copy(x_ref, out_ref.at[2 * x + y], local_sem)
        mine.start()

        def copy(k, slot, to):
            return pltpu.make_async_remote_copy(
                src_ref=x_ref, dst_ref=out_ref.at[slot], send_sem=send_sems.at[k], recv_sem=recv_sems.at[k],
                device_id=to, device_id_type=MESH)

        sends = [copy(k, 2 * x + y, (*chip, c)) for k, chip in enumerate(chips)]
        for cp in sends:
            cp.start()
        for k, (px, py) in enumerate(chips):
            copy(k, 2 * px + py, (px, py, c)).wait_recv()
        for cp in sends:
            cp.wait_send()
        mine.wait()

    return pl.pallas_call(
        body, name=name, out_shape=jax.ShapeDtypeStruct((N_CHIPS, R, C), v.dtype),
        in_specs=[pl.BlockSpec(memory_space=pl.ANY)], out_specs=pl.BlockSpec(memory_space=pl.ANY),
        scratch_shapes=[pltpu.SemaphoreType.DMA((3,)), pltpu.SemaphoreType.DMA((3,)), pltpu.SemaphoreType.DMA],
    )(v)


def scatter_chips(v, name):
    _, R, C = v.shape

    def body(x_ref, out_ref, send_sems, recv_sems, local_sem):
        x, y, c = _coords()
        me = 2 * x + y
        chips = [(1 - x, y), (x, 1 - y), (1 - x, 1 - y)]
        mine = pltpu.make_async_copy(x_ref.at[me], out_ref.at[me], local_sem)
        mine.start()

        def copy(k, src_slot, dst_slot, to):
            return pltpu.make_async_remote_copy(
                src_ref=x_ref.at[src_slot], dst_ref=out_ref.at[dst_slot], send_sem=send_sems.at[k],
                recv_sem=recv_sems.at[k], device_id=to, device_id_type=MESH)

        sends = [copy(k, 2 * px + py, me, (px, py, c)) for k, (px, py) in enumerate(chips)]
        for cp in sends:
            cp.start()
        for k, (px, py) in enumerate(chips):
            copy(k, me, 2 * px + py, (px, py, c)).wait_recv()
        for cp in sends:
            cp.wait_send()
        mine.wait()

    return pl.pallas_call(
        body, name=name, out_shape=jax.ShapeDtypeStruct(v.shape, v.dtype),
        in_specs=[pl.BlockSpec(memory_space=pl.ANY)], out_specs=pl.BlockSpec(memory_space=pl.ANY),
        scratch_shapes=[pltpu.SemaphoreType.DMA((3,)), pltpu.SemaphoreType.DMA((3,)), pltpu.SemaphoreType.DMA],
    )(v)


def sibling_swap_halves(v, name):
    half_shape = v.shape[1:]

    def body(x_ref, out_ref, send_sem, recv_sem):
        x, y, c = _coords()
        cp = pltpu.make_async_remote_copy(
            src_ref=x_ref.at[1 - c], dst_ref=out_ref, send_sem=send_sem, recv_sem=recv_sem,
            device_id=(x, y, 1 - c), device_id_type=MESH)
        cp.start()
        cp.wait()

    return pl.pallas_call(
        body, name=name, out_shape=jax.ShapeDtypeStruct(half_shape, v.dtype),
        in_specs=[pl.BlockSpec(memory_space=pl.ANY)], out_specs=pl.BlockSpec(memory_space=pl.ANY),
        scratch_shapes=[pltpu.SemaphoreType.DMA, pltpu.SemaphoreType.DMA],
    )(v)


def sibling_send(v, name):
    def body(x_ref, out_ref, send_sem, recv_sem):
        x, y, c = _coords()
        cp = pltpu.make_async_remote_copy(
            src_ref=x_ref, dst_ref=out_ref, send_sem=send_sem, recv_sem=recv_sem,
            device_id=(x, y, 1 - c), device_id_type=MESH)
        cp.start()
        cp.wait()

    return pl.pallas_call(
        body, name=name, out_shape=jax.ShapeDtypeStruct(v.shape, v.dtype),
        in_specs=[pl.BlockSpec(memory_space=pl.ANY)], out_specs=pl.BlockSpec(memory_space=pl.ANY),
        scratch_shapes=[pltpu.SemaphoreType.DMA, pltpu.SemaphoreType.DMA],
    )(v)


BIG = (("ffn_w_gu", 3), ("ffn_w_down", 2), ("mla_w_a", 1), ("mla_w_qb", 2), ("mla_w_kvb", 2), ("mla_w_o", 1),
       ("ssd_w_in", 2), ("ssd_w_out", 1))


def _pad_rows(a, mult):
    r = (-a.shape[0]) % mult
    return a if r == 0 else jnp.concatenate([a, jnp.zeros((r, a.shape[1]), a.dtype)], axis=0)


def _flat(a, width):
    return a.reshape(-1, width)


def _unstack(st, axis):
    full = jnp.moveaxis(st, 0, axis)
    sh = list(full.shape)
    sh[axis:axis + 2] = [sh[axis] * sh[axis + 1]]
    return full.reshape(sh)


def _stack(full, axis):
    sh = list(full.shape)
    sh[axis:axis + 1] = [N_CHIPS, sh[axis] // N_CHIPS]
    return jnp.moveaxis(full.reshape(sh), axis, 0)


def kernel(x, c, positions, norm_gain, ada_w, ada_b, ffn_w_gu, ffn_w_down, mla_w_a, mla_q_a_gain, mla_kv_a_gain, mla_w_qb, mla_w_kvb, mla_q_gain, mla_k_gain, mla_w_o, ssd_w_in, ssd_conv_w, ssd_conv_b, ssd_dt_bias, ssd_a_log, ssd_d, ssd_norm_gain, ssd_w_out, loss_target, m_norm_gain, m_ada_w, m_ada_b, m_ffn_w_gu, m_ffn_w_down, m_mla_w_a, m_mla_q_a_gain, m_mla_kv_a_gain, m_mla_w_qb, m_mla_w_kvb, m_mla_q_gain, m_mla_k_gain, m_mla_w_o, m_ssd_w_in, m_ssd_conv_w, m_ssd_conv_b, m_ssd_dt_bias, m_ssd_a_log, m_ssd_d, m_ssd_norm_gain, m_ssd_w_out, v_norm_gain, v_ada_w, v_ada_b, v_ffn_w_gu, v_ffn_w_down, v_mla_w_a, v_mla_q_a_gain, v_mla_kv_a_gain, v_mla_w_qb, v_mla_w_kvb, v_mla_q_gain, v_mla_k_gain, v_mla_w_o, v_ssd_w_in, v_ssd_conv_w, v_ssd_conv_b, v_ssd_dt_bias, v_ssd_a_log, v_ssd_d, v_ssd_norm_gain, v_ssd_w_out):
    w_in = dict(norm_gain=norm_gain, ada_w=ada_w, ada_b=ada_b, ffn_w_gu=ffn_w_gu, ffn_w_down=ffn_w_down, mla_w_a=mla_w_a, mla_q_a_gain=mla_q_a_gain, mla_kv_a_gain=mla_kv_a_gain, mla_w_qb=mla_w_qb, mla_w_kvb=mla_w_kvb, mla_q_gain=mla_q_gain, mla_k_gain=mla_k_gain, mla_w_o=mla_w_o, ssd_w_in=ssd_w_in, ssd_conv_w=ssd_conv_w, ssd_conv_b=ssd_conv_b, ssd_dt_bias=ssd_dt_bias, ssd_a_log=ssd_a_log, ssd_d=ssd_d, ssd_norm_gain=ssd_norm_gain, ssd_w_out=ssd_w_out)
    m_in = dict(norm_gain=m_norm_gain, ada_w=m_ada_w, ada_b=m_ada_b, ffn_w_gu=m_ffn_w_gu, ffn_w_down=m_ffn_w_down, mla_w_a=m_mla_w_a, mla_q_a_gain=m_mla_q_a_gain, mla_kv_a_gain=m_mla_kv_a_gain, mla_w_qb=m_mla_w_qb, mla_w_kvb=m_mla_w_kvb, mla_q_gain=m_mla_q_gain, mla_k_gain=m_mla_k_gain, mla_w_o=m_mla_w_o, ssd_w_in=m_ssd_w_in, ssd_conv_w=m_ssd_conv_w, ssd_conv_b=m_ssd_conv_b, ssd_dt_bias=m_ssd_dt_bias, ssd_a_log=m_ssd_a_log, ssd_d=m_ssd_d, ssd_norm_gain=m_ssd_norm_gain, ssd_w_out=m_ssd_w_out)
    v_in = dict(norm_gain=v_norm_gain, ada_w=v_ada_w, ada_b=v_ada_b, ffn_w_gu=v_ffn_w_gu, ffn_w_down=v_ffn_w_down, mla_w_a=v_mla_w_a, mla_q_a_gain=v_mla_q_a_gain, mla_kv_a_gain=v_mla_kv_a_gain, mla_w_qb=v_mla_w_qb, mla_w_kvb=v_mla_w_kvb, mla_q_gain=v_mla_q_gain, mla_k_gain=v_mla_k_gain, mla_w_o=v_mla_w_o, ssd_w_in=v_ssd_w_in, ssd_conv_w=v_ssd_conv_w, ssd_conv_b=v_ssd_conv_b, ssd_dt_bias=v_ssd_dt_bias, ssd_a_log=v_ssd_a_log, ssd_d=v_ssd_d, ssd_norm_gain=v_ssd_norm_gain, ssd_w_out=v_ssd_w_out)
    names = list(w_in)
    xi, yi, ci = _coords()
    chip = 2 * xi + yi
    batch = 4 * xi + 2 * yi + ci
    x2, target = x[0], loss_target[0]
    S = x2.shape[0]

    small_sharded = ("norm_gain", "ssd_conv_w", "ssd_conv_b", "ssd_norm_gain")
    pack0 = jnp.concatenate([c.reshape(-1)] + [w_in[n].reshape(-1) for n in small_sharded])
    pack0 = _pad_rows(_flat(pack0, 128), 8)
    g0 = allgather_small(pack0, "gather_small").reshape(N_DEV, -1)
    c_all = g0[:, :D_MODEL]
    P, off = {}, D_MODEL
    for n in small_sharded:
        sz = w_in[n].size
        st = g0[0::2, off:off + sz].reshape((N_CHIPS,) + w_in[n].shape)
        P[n] = _unstack(st, w_in[n].ndim - 1)
        off += sz
    for n in ("mla_q_a_gain", "mla_kv_a_gain", "mla_q_gain", "mla_k_gain", "ssd_dt_bias", "ssd_a_log", "ssd_d"):
        P[n] = w_in[n]

    sc = _silu(c_all)
    n_ada = ada_w.shape[2]
    b_sh = lax.dynamic_slice_in_dim(ada_b, chip * n_ada, n_ada, axis=1)
    mods_sh = jnp.stack([matmul(sc, ada_w[l], "nn", "ada_fwd") for l in range(DEPTH)]) + b_sh[:, None, :]
    g1 = allgather_small(_flat(mods_sh, 128), "gather_mods").reshape(N_DEV, DEPTH, N_DEV, n_ada)
    mods = lax.dynamic_index_in_dim(g1[0::2], batch, axis=2, keepdims=False)
    mods = mods.transpose(1, 0, 2).reshape(DEPTH, 3, 3, D_MODEL)

    shards = [w_in[n] for n, _ in BIG]
    pack_w = jnp.concatenate([_flat(s.astype(BF16), PACK_W) for s in shards], axis=0)
    gw = allgather_chips(pack_w, "gather_weights")
    W, off = {}, 0
    for (n, axis), s in zip(BIG, shards):
        rows = s.size // PACK_W
        st = gw[:, off:off + rows].reshape((N_CHIPS,) + s.shape)
        off += rows
        if n.startswith("ffn"):
            W[n] = [[_unstack(st[:, i, t], axis - 2) for t in range(2)] for i in range(DEPTH)]
        else:
            W[n] = [_unstack(st[:, j], axis - 1) for j in range(s.shape[0])]
    W["ssd_w_z"] = [w[:, :D_INNER] for w in W["ssd_w_in"]]
    W["ssd_w_xbc"] = [w[:, D_INNER:D_INNER + CONV_DIM] for w in W["ssd_w_in"]]
    W["ssd_w_dt"] = [w[:, D_INNER + CONV_DIM:] for w in W["ssd_w_in"]]
    del W["ssd_w_in"]
    X = jax.tree.map(lambda w: jnp.zeros(w.shape, F32), W)

    pos = positions[0]
    y, vjp = jax.vjp(lambda a, b, p_, x_: trunk(a, b, p_, W, x_, pos), x2, mods, P, X)
    dy, loss_cols = loss_head(y, target)
    dx, dmods, dP, dX = vjp(dy)
    loss = lax.psum(jnp.sum(loss_cols), ("x", "y", "c"))

    small_names = ("norm_gain", "ssd_conv_w", "ssd_conv_b", "ssd_norm_gain", "mla_q_a_gain", "mla_kv_a_gain",
                   "mla_q_gain", "mla_k_gain", "ssd_dt_bias", "ssd_a_log", "ssd_d")
    pack1 = jnp.concatenate([dmods.reshape(-1)] + [dP[n].reshape(-1) for n in small_names])
    pack1 = _pad_rows(_flat(jnp.pad(pack1, (0, (-pack1.size) % 128)), 128), 8)
    rows1 = pack1.shape[0]
    g2 = allgather_small(pack1, "gather_small_grads")
    tot = sum_parts([g2[d * rows1:(d + 1) * rows1] for d in range(N_DEV)], "sum_small_grads").reshape(-1)
    n_mod = DEPTH * 9 * D_MODEL
    grads = {"ada_b": tot[:n_mod].reshape(DEPTH, 9 * D_MODEL)}
    off = n_mod
    for n in small_names:
        sz = dP[n].size
        full = tot[off:off + sz].reshape(dP[n].shape)
        off += sz
        if n in small_sharded:
            k = w_in[n].shape[-1]
            full = lax.dynamic_slice_in_dim(full, chip * k, k, axis=full.ndim - 1)
        grads[n] = full
    dmods_all = g2.reshape(N_DEV, -1)[:, :n_mod].reshape(N_DEV, DEPTH, 9 * D_MODEL)
    dm_sh = lax.dynamic_slice_in_dim(dmods_all, chip * n_ada, n_ada, axis=2)
    grads["ada_w"] = jnp.stack([matmul(sc, dm_sh[:, l], "tn", "ada_dw") for l in range(DEPTH)])

    dW = dict(dX)
    dW["ssd_w_in"] = [jnp.concatenate([dX["ssd_w_z"][j], dX["ssd_w_xbc"][j], dX["ssd_w_dt"][j]], axis=1) for j in range(2)]
    pieces = []
    for n, axis in BIG:
        if n.startswith("ffn"):
            st = jnp.stack([jnp.stack([_stack(dW[n][i][t], axis - 2) for t in range(2)], axis=1) for i in range(DEPTH)], axis=1)
        else:
            st = jnp.stack([_stack(g, axis - 1) for g in dW[n]], axis=1)
        pieces.append(st.reshape(N_CHIPS, -1, PACK_W))
    pack_g = jnp.concatenate(pieces, axis=1)
    R = pack_g.shape[1]
    halves = pack_g.reshape(N_CHIPS, 2, R // 2, PACK_W).transpose(1, 0, 2, 3)
    theirs = sibling_swap_halves(halves, "grads_to_sibling")
    mine = lax.dynamic_index_in_dim(halves, ci, axis=0, keepdims=False)
    pair = sum_parts([mine.reshape(-1, PACK_W), theirs.reshape(-1, PACK_W)], "sum_pair").reshape(N_CHIPS, R // 2, PACK_W)
    landed = scatter_chips(pair, "grads_to_chips")
    half_sum = sum_parts([landed[p] for p in range(N_CHIPS)], "sum_chips")
    other = sibling_send(half_sum, "grad_halves_swap")
    lo = jnp.where(ci == 0, half_sum, other)
    hi = jnp.where(ci == 0, other, half_sum)
    total_g = jnp.concatenate([lo, hi], axis=0)
    off = 0
    for (n, _), s in zip(BIG, shards):
        rows = s.size // PACK_W
        grads[n] = total_g[off:off + rows].reshape(s.shape)
        off += rows

    deltas, new_m, new_v = {}, {}, {}
    for n in names:
        w = w_in[n]
        sh2 = (-1, w.shape[-1])
        d, nm, nv = adamw(w.reshape(sh2), m_in[n].reshape(sh2), v_in[n].reshape(sh2), grads[n].reshape(sh2))
        deltas[n], new_m[n], new_v[n] = d.reshape(w.shape), nm.reshape(w.shape), nv.reshape(w.shape)

    return (loss, dx[None], *[grads[n] for n in names], *[deltas[n] for n in names],
            *[new_m[n] for n in names], *[new_v[n] for n in names])
```

```python
import numpy as np

import jax
import jax.numpy as jnp
from jax import lax
from jax.experimental import pallas as pl
from jax.experimental.pallas import tpu as pltpu

F32 = jnp.float32
BF16 = jnp.bfloat16
MESH = pl.DeviceIdType.MESH

D_MODEL = 1024
DEPTH = 4
EPS = 1e-6
D_FF = 2816
MLA_HEADS = 16
Q_LORA = 384
KV_LORA = 256
QK_NOPE = 64
QK_ROPE = 32
QK_HEAD = QK_NOPE + QK_ROPE
V_HEAD = 64
ROPE_THETA = 10000.0
D_INNER = 2048
SSD_HEAD_DIM = 64
SSD_HEADS = 32
SSD_GROUPS = 4
SSD_STATE = 128
CONV_WIDTH = 4
CHUNK = 128
CONV_DIM = D_INNER + 2 * SSD_GROUPS * SSD_STATE
ADAM_LR = 0.001
ADAM_B1 = 0.9
ADAM_B2 = 0.999
ADAM_EPS = 1e-08
ADAM_WD = 0.01
ADAM_STEP = 10

N_CHIPS = 4
N_DEV = 8
V7X_VMEM_LIMIT = 56 * 1024 * 1024
ATTN_HEADS_PER_STEP = 2


def _params(sem=None):
    return pltpu.CompilerParams(dimension_semantics=sem, vmem_limit_bytes=V7X_VMEM_LIMIT)


def _div_tile(n, pref, quantum):
    if n <= pref:
        return n
    t = (pref // quantum) * quantum
    while t >= quantum:
        if n % t == 0:
            return t
        t -= quantum
    return n


def matmul(a, b, mode, name, out_dtype=F32, stack=0, tm=1024, tn=1024, tk=2816):
    if mode == "nn":
        (M, K), (K2, N) = a.shape, b.shape
    elif mode == "nt":
        (M, K), (N, K2) = a.shape, b.shape
    else:
        (K, M), (K2, N) = a.shape, b.shape
    assert K == K2, (a.shape, b.shape, mode)
    tm = _div_tile(M, tm, 128 if mode == "tn" else 16)
    tn = _div_tile(N // stack if stack else N, tn, 128)
    tk = _div_tile(K, tk, 128)
    nk = K // tk
    if mode == "nn":
        a_spec = pl.BlockSpec((tm, tk), lambda i, j, k: (i, k))
        b_spec = pl.BlockSpec((tk, tn), lambda i, j, k: (k, j))
        dims = (((1,), (0,)), ((), ()))
    elif mode == "nt":
        a_spec = pl.BlockSpec((tm, tk), lambda i, j, k: (i, k))
        b_spec = pl.BlockSpec((tn, tk), lambda i, j, k: (j, k))
        dims = (((1,), (1,)), ((), ()))
    else:
        a_spec = pl.BlockSpec((tk, tm), lambda i, j, k: (k, i))
        b_spec = pl.BlockSpec((tk, tn), lambda i, j, k: (k, j))
        dims = (((0,), (0,)), ((), ()))
    if stack:
        nb = N // stack // tn
        out_spec = pl.BlockSpec((None, tm, tn), lambda i, j, k: (j // nb, i, j % nb))
        out_shape = jax.ShapeDtypeStruct((stack, M, N // stack), out_dtype)
    else:
        out_spec = pl.BlockSpec((tm, tn), lambda i, j, k: (i, j))
        out_shape = jax.ShapeDtypeStruct((M, N), out_dtype)
    use_acc = nk > 1 and out_dtype != F32

    def body(a_ref, b_ref, o_ref, *acc):
        p = lax.dot_general(a_ref[...].astype(BF16), b_ref[...].astype(BF16), dims, preferred_element_type=F32)
        if nk == 1:
            o_ref[...] = p.astype(out_dtype)
            return
        acc_ref = acc[0] if use_acc else o_ref
        k = pl.program_id(2)

        @pl.when(k == 0)
        def _():
            acc_ref[...] = p

        @pl.when(k > 0)
        def _():
            acc_ref[...] += p

        if use_acc:
            @pl.when(k == nk - 1)
            def _():
                o_ref[...] = acc_ref[...].astype(out_dtype)

    return pl.pallas_call(
        body, name=name, grid=(M // tm, N // tn, nk), in_specs=[a_spec, b_spec], out_specs=out_spec, out_shape=out_shape,
        scratch_shapes=[pltpu.VMEM((tm, tn), F32)] if use_acc else [],
        compiler_params=_params(("parallel", "parallel", "arbitrary")),
    )(a, b)


def mm_op(name, out_dtype=F32, stack=0):
    @jax.custom_vjp
    def op(a, w, wp):
        return matmul(a, w, "nn", name + "_fwd", out_dtype=out_dtype)

    def fwd(a, w, wp):
        return op(a, w, wp), (a, w)

    def bwd(res, g):
        a, w = res
        return (matmul(g, w, "nt", name + "_dx", out_dtype=a.dtype), jnp.zeros_like(w),
                matmul(a, g, "tn", name + "_dw", stack=stack, tn=1408, tk=1024))

    op.defvjp(fwd, bwd)
    return op


def _row_tile(rows, widths):
    w = max(widths)
    t = 128 if w > 4096 else (256 if w > 1024 else 512)
    return _div_tile(rows, t, 16)


def _rowwise_fwd(f, rows, vecs, name, out_dtype):
    n_r, n_v = len(rows), len(vecs)
    S = rows[0].shape[0]
    tm = _row_tile(S, [r.shape[1] for r in rows])
    outs = jax.eval_shape(f, *[jax.ShapeDtypeStruct((tm, r.shape[1]), F32) for r in rows], *vecs)

    def body(*refs):
        res = f(*[r[...].astype(F32) for r in refs[: n_r + n_v]])
        for o, r in zip(refs[n_r + n_v:], res):
            o[...] = r.astype(out_dtype)

    return pl.pallas_call(
        body, name=name, grid=(S // tm,),
        in_specs=[pl.BlockSpec((tm, r.shape[1]), lambda i: (i, 0)) for r in rows]
        + [pl.BlockSpec(v.shape, lambda i: (0, 0)) for v in vecs],
        out_specs=tuple(pl.BlockSpec((tm, o.shape[1]), lambda i: (i, 0)) for o in outs),
        out_shape=tuple(jax.ShapeDtypeStruct((S, o.shape[1]), out_dtype) for o in outs),
        compiler_params=_params(("parallel",)),
    )(*rows, *vecs)


def _rowwise_bwd(f, rows, vecs, douts, diff_rows, name):
    n_r, n_v, n_o = len(rows), len(vecs), len(douts)
    S = rows[0].shape[0]
    tm = _row_tile(S, [r.shape[1] for r in rows] + [d.shape[1] for d in douts])
    d_idx = [i for i in range(n_r) if diff_rows[i]]

    def body(*refs):
        row_v = [r[...].astype(F32) for r in refs[:n_r]]
        vec_v = [r[...] for r in refs[n_r:n_r + n_v]]
        dout_v = tuple(r[...].astype(F32) for r in refs[n_r + n_v:n_r + n_v + n_o])
        out_refs = refs[n_r + n_v + n_o:]

        def g(*args):
            full = list(row_v)
            for j, i in enumerate(d_idx):
                full[i] = args[j]
            return f(*full, *args[len(d_idx):])

        _, vjp = jax.vjp(g, *[row_v[i] for i in d_idx], *vec_v)
        grads = vjp(dout_v)
        for j in range(len(d_idx)):
            out_refs[j][...] = grads[j].astype(out_refs[j].dtype)
        step = pl.program_id(0)
        for j in range(n_v):
            gv, o = grads[len(d_idx) + j], out_refs[len(d_idx) + j]

            @pl.when(step == 0)
            def _(gv=gv, o=o):
                o[...] = gv

            @pl.when(step > 0)
            def _(gv=gv, o=o):
                o[...] += gv

    res = pl.pallas_call(
        body, name=name, grid=(S // tm,),
        in_specs=[pl.BlockSpec((tm, r.shape[1]), lambda i: (i, 0)) for r in rows]
        + [pl.BlockSpec(v.shape, lambda i: (0, 0)) for v in vecs]
        + [pl.BlockSpec((tm, d.shape[1]), lambda i: (i, 0)) for d in douts],
        out_specs=tuple([pl.BlockSpec((tm, rows[i].shape[1]), lambda i_: (i_, 0)) for i in d_idx]
                        + [pl.BlockSpec(v.shape, lambda i: (0, 0)) for v in vecs]),
        out_shape=tuple([jax.ShapeDtypeStruct(rows[i].shape, rows[i].dtype) for i in d_idx]
                        + [jax.ShapeDtypeStruct(v.shape, F32) for v in vecs]),
        compiler_params=_params(("arbitrary",)),
    )(*rows, *vecs, *douts)
    drows = [None] * n_r
    for j, i in enumerate(d_idx):
        drows[i] = res[j]
    for i in range(n_r):
        if drows[i] is None:
            drows[i] = jnp.zeros_like(rows[i])
    return tuple(drows) + tuple(res[len(d_idx):])


def rowwise_op(f, name, n_rows, diff_rows=None, out_dtype=F32):
    diff = tuple(diff_rows) if diff_rows is not None else (True,) * n_rows

    @jax.custom_vjp
    def op(*args):
        return _rowwise_fwd(f, args[:n_rows], args[n_rows:], name + "_fwd", out_dtype)

    def fwd(*args):
        return op(*args), args

    def bwd(args, douts):
        return _rowwise_bwd(f, args[:n_rows], args[n_rows:], douts, diff, name + "_bwd")

    op.defvjp(fwd, bwd)
    return op


def _rms(x, gain):
    return x * lax.rsqrt(jnp.mean(x * x, axis=-1, keepdims=True) + EPS) * gain


def _silu(x):
    return x * jax.nn.sigmoid(x)


def _f_modulate(x, gain, shift, scale):
    return (_rms(x, gain) * (1.0 + scale) + shift,)


def _f_rms(x, gain):
    return (_rms(x, gain),)


def _f_swiglu(gu):
    n = gu.shape[1] // 2
    return (_silu(gu[:, :n]) * gu[:, n:],)


def _f_resid(coef):
    def f(x, y, gate):
        return (x + (coef * gate) * y,)
    return f


def _f_gated_norm(y, z, gain):
    g = y * _silu(z)
    n = g.shape[1] // SSD_GROUPS
    return (jnp.concatenate([_rms(g[:, i * n:(i + 1) * n], gain[:, i * n:(i + 1) * n]) for i in range(SSD_GROUPS)], axis=1),)


_NT = (((1,), (1,)), ((), ()))


def _dot(a, b):
    return jnp.dot(a.astype(BF16), b.astype(BF16), preferred_element_type=F32)


def _dot_nt(a, b):
    return lax.dot_general(a.astype(BF16), b.astype(BF16), _NT, preferred_element_type=F32)


def _attn_tile(S):
    return _div_tile(S, 512, 128)


def _causal(t, transposed=False):
    r = lax.broadcasted_iota(jnp.int32, (t, t), 0)
    c = lax.broadcasted_iota(jnp.int32, (t, t), 1)
    return r <= c if transposed else r >= c


def _tri_tables(n, by_key):
    if by_key:
        pairs = [(i, j) for j in range(n) for i in range(j, n)]
    else:
        pairs = [(i, j) for i in range(n) for j in range(i + 1)]
    return (jnp.asarray(np.array([p[0] for p in pairs], np.int32)), jnp.asarray(np.array([p[1] for p in pairs], np.int32)))


def attn_fwd(q, k, v):
    H, S, dk = q.shape
    dv = v.shape[-1]
    t, hb = _attn_tile(S), ATTN_HEADS_PER_STEP
    n = S // t
    scale = dk ** -0.5
    qi_tab, kj_tab = _tri_tables(n, by_key=False)

    def body(qi_ref, kj_ref, q_ref, k_ref, v_ref, o_ref, lse_ref, m_s, l_s, acc_s):
        qi, kj = qi_ref[pl.program_id(1)], kj_ref[pl.program_id(1)]

        @pl.when(kj == 0)
        def _():
            m_s[...] = jnp.full((hb, t, 1), -jnp.inf, F32)
            l_s[...] = jnp.zeros((hb, t, 1), F32)
            acc_s[...] = jnp.zeros((hb, t, dv), F32)

        def step(masked):
            for j in range(hb):
                s = _dot_nt(q_ref[j], k_ref[j]) * scale
                if masked:
                    s = jnp.where(_causal(t), s, -jnp.inf)
                m_new = jnp.maximum(m_s[j], jnp.max(s, axis=-1, keepdims=True))
                alpha = jnp.exp(m_s[j] - m_new)
                p = jnp.exp(s - m_new)
                l_s[j] = alpha * l_s[j] + jnp.sum(p, axis=-1, keepdims=True)
                acc_s[j] = alpha * acc_s[j] + _dot(p, v_ref[j])
                m_s[j] = m_new

        @pl.when(kj < qi)
        def _():
            step(False)

        @pl.when(kj == qi)
        def _():
            step(True)
            o_ref[...] = acc_s[...] / l_s[...]
            lse_ref[...] = m_s[...] + jnp.log(l_s[...])

    qmap = lambda h, s, qi, kj: (h, qi[s], 0)
    kmap = lambda h, s, qi, kj: (h, kj[s], 0)
    return pl.pallas_call(
        body, name="attn_fwd",
        grid_spec=pltpu.PrefetchScalarGridSpec(
            num_scalar_prefetch=2, grid=(H // hb, qi_tab.shape[0]),
            in_specs=[pl.BlockSpec((hb, t, dk), qmap), pl.BlockSpec((hb, t, dk), kmap), pl.BlockSpec((hb, t, dv), kmap)],
            out_specs=(pl.BlockSpec((hb, t, dv), qmap), pl.BlockSpec((hb, t, 1), qmap)),
            scratch_shapes=[pltpu.VMEM((hb, t, 1), F32), pltpu.VMEM((hb, t, 1), F32), pltpu.VMEM((hb, t, dv), F32)]),
        out_shape=(jax.ShapeDtypeStruct((H, S, dv), F32), jax.ShapeDtypeStruct((H, S, 1), F32)),
        compiler_params=_params(("parallel", "arbitrary")),
    )(qi_tab, kj_tab, q, k, v)


def attn_bwd(q, k, v, o, lse, do):
    H, S, dk = q.shape
    dv = v.shape[-1]
    t, hb = _attn_tile(S), ATTN_HEADS_PER_STEP
    n = S // t
    scale = dk ** -0.5
    qi_tab, kj_tab = _tri_tables(n, by_key=False)

    def dq_body(qi_ref, kj_ref, q_ref, k_ref, v_ref, o_ref, do_ref, lse_ref, dq_ref, delta_ref):
        qi, kj = qi_ref[pl.program_id(1)], kj_ref[pl.program_id(1)]

        @pl.when(kj == 0)
        def _():
            delta_ref[...] = jnp.sum(do_ref[...] * o_ref[...], axis=-1, keepdims=True)
            dq_ref[...] = jnp.zeros((hb, t, dk), F32)

        def step(masked):
            for j in range(hb):
                s = _dot_nt(q_ref[j], k_ref[j]) * scale
                p = jnp.exp(s - lse_ref[j])
                if masked:
                    p = jnp.where(_causal(t), p, 0.0)
                dp = _dot_nt(do_ref[j], v_ref[j])
                ds = p * (dp - delta_ref[j])
                dq_ref[j] += _dot(ds, k_ref[j]) * scale

        @pl.when(kj < qi)
        def _():
            step(False)

        @pl.when(kj == qi)
        def _():
            step(True)

    qmap = lambda h, s, qi, kj: (h, qi[s], 0)
    kmap = lambda h, s, qi, kj: (h, kj[s], 0)
    dq, delta = pl.pallas_call(
        dq_body, name="attn_bwd_dq",
        grid_spec=pltpu.PrefetchScalarGridSpec(
            num_scalar_prefetch=2, grid=(H // hb, qi_tab.shape[0]),
            in_specs=[pl.BlockSpec((hb, t, dk), qmap), pl.BlockSpec((hb, t, dk), kmap), pl.BlockSpec((hb, t, dv), kmap),
                      pl.BlockSpec((hb, t, dv), qmap), pl.BlockSpec((hb, t, dv), qmap), pl.BlockSpec((hb, t, 1), qmap)],
            out_specs=(pl.BlockSpec((hb, t, dk), qmap), pl.BlockSpec((hb, t, 1), qmap))),
        out_shape=(jax.ShapeDtypeStruct((H, S, dk), F32), jax.ShapeDtypeStruct((H, S, 1), F32)),
        compiler_params=_params(("parallel", "arbitrary")),
    )(qi_tab, kj_tab, q, k, v, o, do, lse)

    lse_r = lse.reshape(H, 1, S)
    delta_r = delta.reshape(H, 1, S)
    do_b = do.astype(BF16)
    qi_tab2, kj_tab2 = _tri_tables(n, by_key=True)

    def dkv_body(qi_ref, kj_ref, q_ref, k_ref, v_ref, do_ref, lse_ref, delta_ref, dk_ref, dv_ref):
        qi, kj = qi_ref[pl.program_id(1)], kj_ref[pl.program_id(1)]

        def step(masked):
            for j in range(hb):
                st = _dot_nt(k_ref[j], q_ref[j]) * scale
                pt = jnp.exp(st - lse_ref[j])
                if masked:
                    pt = jnp.where(_causal(t, transposed=True), pt, 0.0)
                dvj = _dot(pt, do_ref[j])
                dpt = _dot_nt(v_ref[j], do_ref[j])
                dst = pt * (dpt - delta_ref[j])
                dkj = _dot(dst, q_ref[j]) * scale
                if masked:
                    dv_ref[j] = dvj
                    dk_ref[j] = dkj
                else:
                    dv_ref[j] += dvj
                    dk_ref[j] += dkj

        @pl.when(qi == kj)
        def _():
            step(True)

        @pl.when(qi > kj)
        def _():
            step(False)

    qmap2 = lambda h, s, qi, kj: (h, qi[s], 0)
    kmap2 = lambda h, s, qi, kj: (h, kj[s], 0)
    rowq = lambda h, s, qi, kj: (h, 0, qi[s])
    dk_, dv_ = pl.pallas_call(
        dkv_body, name="attn_bwd_dkv",
        grid_spec=pltpu.PrefetchScalarGridSpec(
            num_scalar_prefetch=2, grid=(H // hb, qi_tab2.shape[0]),
            in_specs=[pl.BlockSpec((hb, t, dk), qmap2), pl.BlockSpec((hb, t, dk), kmap2), pl.BlockSpec((hb, t, dv), kmap2),
                      pl.BlockSpec((hb, t, dv), qmap2), pl.BlockSpec((hb, 1, t), rowq), pl.BlockSpec((hb, 1, t), rowq)],
            out_specs=(pl.BlockSpec((hb, t, dk), kmap2), pl.BlockSpec((hb, t, dv), kmap2))),
        out_shape=(jax.ShapeDtypeStruct((H, S, dk), F32), jax.ShapeDtypeStruct((H, S, dv), F32)),
        compiler_params=_params(("parallel", "arbitrary")),
    )(qi_tab2, kj_tab2, q, k, v, do_b, lse_r, delta_r)
    return dq, dk_, dv_


@jax.custom_vjp
def attention(q, k, v):
    return attn_fwd(q.astype(BF16), k.astype(BF16), v.astype(BF16))[0]


def _attention_fwd(q, k, v):
    qb, kb, vb = q.astype(BF16), k.astype(BF16), v.astype(BF16)
    o, lse = attn_fwd(qb, kb, vb)
    return o, (qb, kb, vb, o, lse)


def _attention_bwd(res, do):
    return attn_bwd(*res, do)


attention.defvjp(_attention_fwd, _attention_bwd)


def _ssd_specs(hb, L, P, N, order):
    xs = pl.BlockSpec((hb, L, P), lambda g, c: (g, order(c), 0))
    col = pl.BlockSpec((hb, L, 1), lambda g, c: (g, order(c), 0))
    row = pl.BlockSpec((hb, 1, L), lambda g, c: (g, 0, order(c)))
    bc = pl.BlockSpec((None, L, N), lambda g, c: (g, order(c), 0))
    st = pl.BlockSpec((hb, None, N, P), lambda g, c: (g, order(c), 0, 0))
    return xs, col, row, bc, st


def ssd_fwd(x, dt, ac, Bm, Cm):
    H, S, P = x.shape
    G, _, N = Bm.shape
    hb, L = H // G, CHUNK
    nc = S // L
    acr = ac.reshape(H, 1, S)

    def body(x_ref, dt_ref, ac_ref, acr_ref, b_ref, c_ref, y_ref, hp_ref, h_s):
        @pl.when(pl.program_id(1) == 0)
        def _():
            h_s[...] = jnp.zeros((hb, N, P), F32)

        Bv, Cv = b_ref[...], c_ref[...]
        cb = _dot_nt(Cv, Bv)
        bt = Bv.T
        mask = _causal(L)
        for j in range(hb):
            a = ac_ref[j]
            lm = jnp.exp(jnp.where(mask, a - acr_ref[j], -jnp.inf))
            xdt = x_ref[j] * dt_ref[j]
            h = h_s[j]
            hp_ref[j] = h
            y_ref[j] = _dot(cb * lm, xdt) + jnp.exp(a) * _dot(Cv, h)
            al = ac_ref[j, L - 1:L, :]
            h_s[j] = jnp.exp(al) * h + _dot(bt, xdt * jnp.exp(al - a))

    xs, col, row, bc, st = _ssd_specs(hb, L, P, N, lambda c: c)
    return pl.pallas_call(
        body, name="ssd_fwd", grid=(G, nc), in_specs=[xs, col, col, row, bc, bc], out_specs=(xs, st),
        out_shape=(jax.ShapeDtypeStruct((H, S, P), F32), jax.ShapeDtypeStruct((H, nc, N, P), F32)),
        scratch_shapes=[pltpu.VMEM((hb, N, P), F32)],
        compiler_params=_params(("parallel", "arbitrary")),
    )(x, dt, ac, acr, Bm, Cm)


def ssd_bwd(x, dt, ac, Bm, Cm, hp, dy):
    H, S, P = x.shape
    G, _, N = Bm.shape
    hb, L = H // G, CHUNK
    nc = S // L
    acr = ac.reshape(H, 1, S)

    def body(x_ref, dt_ref, ac_ref, acr_ref, b_ref, c_ref, hp_ref, dy_ref, dx_ref, ddt_ref, dac_ref, db_ref, dc_ref, dh_s):
        @pl.when(pl.program_id(1) == 0)
        def _():
            dh_s[...] = jnp.zeros((hb, N, P), F32)

        Bv, Cv = b_ref[...], c_ref[...]
        cb = _dot_nt(Cv, Bv)
        ct = Cv.T
        mask = _causal(L)
        last = lax.broadcasted_iota(jnp.int32, (L, 1), 0) == L - 1
        db = jnp.zeros((L, N), F32)
        dc = jnp.zeros((L, N), F32)
        for j in range(hb):
            a, xv, dtv, g, h, dh = ac_ref[j], x_ref[j], dt_ref[j], dy_ref[j], hp_ref[j], dh_s[j]
            lm = jnp.exp(jnp.where(mask, a - acr_ref[j], -jnp.inf))
            gm = cb * lm
            xdt = xv * dtv
            e = jnp.exp(a)
            al = ac_ref[j, L - 1:L, :]
            dte = jnp.exp(al - a)
            el = jnp.exp(al)
            dcb = _dot_nt(g, xdt) * lm
            dseg = dcb * cb
            ch = _dot(Cv, h)
            bdh = _dot(Bv, dh)
            dxdt = _dot(gm.T, g) + dte * bdh
            dc += _dot(dcb, Bv) + e * _dot_nt(g, h)
            db += _dot(dcb.T, Cv) + _dot_nt(xdt * dte, dh)
            d_e = jnp.sum(g * ch, axis=-1, keepdims=True)
            d_dte = jnp.sum(xdt * bdh, axis=-1, keepdims=True)
            d_el = jnp.sum(h * dh, keepdims=True)
            dac = (jnp.sum(dseg, axis=-1, keepdims=True) - jnp.sum(dseg.T, axis=-1, keepdims=True) + d_e * e - d_dte * dte)
            d_al = jnp.sum(d_dte * dte, keepdims=True) + d_el * el
            dac_ref[j] = dac + jnp.where(last, d_al, 0.0)
            dx_ref[j] = dxdt * dtv
            ddt_ref[j] = jnp.sum(dxdt * xv, axis=-1, keepdims=True)
            dh_s[j] = el * dh + _dot(ct, e * g)
        db_ref[...] = db
        dc_ref[...] = dc

    xs, col, row, bc, st = _ssd_specs(hb, L, P, N, lambda c: nc - 1 - c)
    return pl.pallas_call(
        body, name="ssd_bwd", grid=(G, nc), in_specs=[xs, col, col, row, bc, bc, st, xs],
        out_specs=(xs, col, col, bc, bc),
        out_shape=(jax.ShapeDtypeStruct((H, S, P), F32), jax.ShapeDtypeStruct((H, S, 1), F32),
                   jax.ShapeDtypeStruct((H, S, 1), F32), jax.ShapeDtypeStruct((G, S, N), F32),
                   jax.ShapeDtypeStruct((G, S, N), F32)),
        scratch_shapes=[pltpu.VMEM((hb, N, P), F32)],
        compiler_params=_params(("parallel", "arbitrary")),
    )(x, dt, ac, acr, Bm, Cm, hp, dy)


@jax.custom_vjp
def ssd_scan(x, dt, ac, Bm, Cm):
    return ssd_fwd(x, dt, ac, Bm, Cm)[0]


def _ssd_scan_fwd(x, dt, ac, Bm, Cm):
    y, hp = ssd_fwd(x, dt, ac, Bm, Cm)
    return y, (x, dt, ac, Bm, Cm, hp)


def _ssd_scan_bwd(res, dy):
    return ssd_bwd(*res, dy)


ssd_scan.defvjp(_ssd_scan_fwd, _ssd_scan_bwd)


def _vec(v):
    return v.reshape(1, -1)


def _ffn(x, gain, m3, w_gu, w_down, x_gu, x_down):
    h, = rowwise_op(_f_modulate, "modulate", 1, out_dtype=BF16)(x, _vec(gain), _vec(m3[0]), _vec(m3[1]))
    gu = mm_op("ffn_gu", out_dtype=BF16, stack=N_CHIPS)(h, w_gu, x_gu)
    a, = rowwise_op(_f_swiglu, "swiglu", 1, out_dtype=BF16)(gu)
    y = mm_op("ffn_down")(a, w_down, x_down)
    return rowwise_op(_f_resid(0.5), "resid_half", 2)(x, y, _vec(m3[2]))[0]


def _rope_tables(positions):
    inv = 1.0 / (ROPE_THETA ** (jnp.arange(0, QK_ROPE, 2, dtype=F32) / QK_ROPE))
    ang = positions.astype(F32)[:, None] * inv
    return jnp.cos(ang), jnp.sin(ang)


def _rope(x, cos, sin):
    x1, x2 = jnp.split(x, 2, axis=-1)
    cos, sin = cos[:, None], sin[:, None]
    return jnp.concatenate([x1 * cos - x2 * sin, x1 * sin + x2 * cos], axis=-1)


def _mla(h, cos, sin, P, W, X, j):
    S = h.shape[0]
    lat = mm_op("mla_a")(h, W["mla_w_a"][j], X["mla_w_a"][j])
    q_lat, kv_lat, k_rope = lat[:, :Q_LORA], lat[:, Q_LORA:Q_LORA + KV_LORA], lat[:, Q_LORA + KV_LORA:]
    qn, = rowwise_op(_f_rms, "rms_lat", 1, out_dtype=BF16)(q_lat, _vec(P["mla_q_a_gain"][j]))
    kvn, = rowwise_op(_f_rms, "rms_lat", 1, out_dtype=BF16)(kv_lat, _vec(P["mla_kv_a_gain"][j]))
    q = mm_op("mla_qb", stack=N_CHIPS)(qn, W["mla_w_qb"][j], X["mla_w_qb"][j]).reshape(S, MLA_HEADS, QK_HEAD)
    kv = mm_op("mla_kvb", stack=N_CHIPS)(kvn, W["mla_w_kvb"][j], X["mla_w_kvb"][j]).reshape(S, MLA_HEADS, QK_NOPE + V_HEAD)
    k_nope, v = kv[..., :QK_NOPE], kv[..., QK_NOPE:]
    k = jnp.concatenate([k_nope, jnp.broadcast_to(k_rope[:, None, :], (S, MLA_HEADS, QK_ROPE))], axis=-1)
    q = _rms(q, P["mla_q_gain"][j])
    k = _rms(k, P["mla_k_gain"][j])
    q = jnp.concatenate([q[..., :QK_NOPE], _rope(q[..., QK_NOPE:], cos, sin)], axis=-1)
    k = jnp.concatenate([k[..., :QK_NOPE], _rope(k[..., QK_NOPE:], cos, sin)], axis=-1)
    o = attention(q.transpose(1, 0, 2), k.transpose(1, 0, 2), v.transpose(1, 0, 2))
    o = o.transpose(1, 0, 2).reshape(S, MLA_HEADS * V_HEAD).astype(BF16)
    return mm_op("mla_o")(o, W["mla_w_o"][j], X["mla_w_o"][j])


def _ssd(h, P, W, X, j):
    S = h.shape[0]
    z = mm_op("ssd_in_z")(h, W["ssd_w_z"][j], X["ssd_w_z"][j])
    xbc = mm_op("ssd_in_xbc")(h, W["ssd_w_xbc"][j], X["ssd_w_xbc"][j])
    dtr = mm_op("ssd_in_dt")(h, W["ssd_w_dt"][j], X["ssd_w_dt"][j])
    cw, cb = P["ssd_conv_w"][j], P["ssd_conv_b"][j]
    up = jnp.pad(xbc, ((CONV_WIDTH - 1, 0), (0, 0)))
    conv = sum(cw[t][None, :] * up[t:t + S] for t in range(CONV_WIDTH)) + cb[None, :]
    xbc = _silu(conv)
    xs = xbc[:, :D_INNER].reshape(S, SSD_HEADS, SSD_HEAD_DIM).transpose(1, 0, 2)
    Bm = xbc[:, D_INNER:D_INNER + SSD_GROUPS * SSD_STATE].reshape(S, SSD_GROUPS, SSD_STATE).transpose(1, 0, 2)
    Cm = xbc[:, D_INNER + SSD_GROUPS * SSD_STATE:].reshape(S, SSD_GROUPS, SSD_STATE).transpose(1, 0, 2)
    dt = jax.nn.softplus(dtr + P["ssd_dt_bias"][j][None, :])
    A = -jnp.exp(P["ssd_a_log"][j])
    a = (dt * A[None, :]).reshape(S // CHUNK, CHUNK, SSD_HEADS)
    ac = jnp.cumsum(a, axis=1).reshape(S, SSD_HEADS)
    y = ssd_scan(xs, dt.T[:, :, None], ac.T[:, :, None], Bm, Cm)
    y = y + P["ssd_d"][j][:, None, None] * xs
    y = y.transpose(1, 0, 2).reshape(S, D_INNER)
    g, = rowwise_op(_f_gated_norm, "gated_norm", 2, out_dtype=BF16)(y, z, _vec(P["ssd_norm_gain"][j]))
    return mm_op("ssd_out")(g, W["ssd_w_out"][j], X["ssd_w_out"][j])


def trunk(x, mods, P, W, X, positions):
    cos, sin = _rope_tables(positions)
    for i in range(DEPTH):
        m, j = mods[i], i // 2
        x = _ffn(x, P["norm_gain"][i, 0], m[0], W["ffn_w_gu"][i][0], W["ffn_w_down"][i][0],
                 X["ffn_w_gu"][i][0], X["ffn_w_down"][i][0])
        h, = rowwise_op(_f_modulate, "modulate", 1, out_dtype=BF16)(x, _vec(P["norm_gain"][i, 1]), _vec(m[1, 0]), _vec(m[1, 1]))
        y = _mla(h, cos, sin, P, W, X, j) if i % 2 == 0 else _ssd(h, P, W, X, j)
        x, = rowwise_op(_f_resid(1.0), "resid_full", 2)(x, y, _vec(m[1, 2]))
        x = _ffn(x, P["norm_gain"][i, 2], m[2], W["ffn_w_gu"][i][1], W["ffn_w_down"][i][1],
                 X["ffn_w_gu"][i][1], X["ffn_w_down"][i][1])
    return x


def stand_ins(W):
    def one(name, w):
        if name in ("ffn_w_gu", "mla_w_qb", "mla_w_kvb"):
            return jnp.zeros((N_CHIPS, w.shape[0], w.shape[1] // N_CHIPS), F32)
        return jnp.zeros(w.shape, F32)
    return {n: jax.tree.map(lambda w, n=n: one(n, w), W[n]) for n in W}


def loss_head(y, target):
    S, D = y.shape
    tm = _row_tile(S, [D])

    def body(y_ref, t_ref, dy_ref, l_ref):
        d = y_ref[...] - t_ref[...]
        dy_ref[...] = d * (1.0 / D)
        part = jnp.sum(d * d, axis=0, keepdims=True) * (0.5 / D)

        @pl.when(pl.program_id(0) == 0)
        def _():
            l_ref[...] = part

        @pl.when(pl.program_id(0) > 0)
        def _():
            l_ref[...] += part

    return pl.pallas_call(
        body, name="loss_head", grid=(S // tm,),
        in_specs=[pl.BlockSpec((tm, D), lambda i: (i, 0))] * 2,
        out_specs=(pl.BlockSpec((tm, D), lambda i: (i, 0)), pl.BlockSpec((1, D), lambda i: (0, 0))),
        out_shape=(jax.ShapeDtypeStruct((S, D), F32), jax.ShapeDtypeStruct((1, D), F32)),
        compiler_params=_params(("arbitrary",)),
    )(y, target)


def _stream_rows(R, C):
    return _div_tile(R, max(16, (1 << 19) // C // 16 * 16), 16)


def adamw(w, m, v, g):
    R, C = w.shape
    tr = _stream_rows(R, C)
    c1 = 1.0 / (1.0 - ADAM_B1 ** ADAM_STEP)
    c2 = 1.0 / (1.0 - ADAM_B2 ** ADAM_STEP)

    def body(w_ref, m_ref, v_ref, g_ref, d_ref, nm_ref, nv_ref):
        gv = g_ref[...]
        nm = ADAM_B1 * m_ref[...] + (1.0 - ADAM_B1) * gv
        nv = ADAM_B2 * v_ref[...] + (1.0 - ADAM_B2) * (gv * gv)
        d_ref[...] = -ADAM_LR * ((nm * c1) / (jnp.sqrt(nv * c2) + ADAM_EPS) + ADAM_WD * w_ref[...])
        nm_ref[...] = nm
        nv_ref[...] = nv

    spec = pl.BlockSpec((tr, C), lambda i: (i, 0))
    return pl.pallas_call(
        body, name="adamw", grid=(R // tr,), in_specs=[spec] * 4, out_specs=(spec,) * 3,
        out_shape=(jax.ShapeDtypeStruct((R, C), F32),) * 3, compiler_params=_params(("parallel",)),
    )(w, m, v, g)


def sum_parts(parts, name, out_dtype=F32):
    R, C = parts[0].shape
    tr = _stream_rows(R, C)
    n = len(parts)

    def body(*refs):
        acc = refs[0][...].astype(F32)
        for r in refs[1:n]:
            acc = acc + r[...].astype(F32)
        refs[n][...] = acc.astype(out_dtype)

    spec = pl.BlockSpec((tr, C), lambda i: (i, 0))
    return pl.pallas_call(
        body, name=name, grid=(R // tr,), in_specs=[spec] * n, out_specs=spec,
        out_shape=jax.ShapeDtypeStruct((R, C), out_dtype), compiler_params=_params(("parallel",)),
    )(*parts)


def sum_slots(a, name, out_dtype=F32):
    n, R, C = a.shape
    tr = _stream_rows(R, C)

    def body(*refs):
        acc = refs[0][...].astype(F32)
        for r in refs[1:n]:
            acc = acc + r[...].astype(F32)
        refs[n][...] = acc.astype(out_dtype)

    return pl.pallas_call(
        body, name=name, grid=(R // tr,),
        in_specs=[pl.BlockSpec((None, tr, C), lambda i, p=p: (p, i, 0)) for p in range(n)],
        out_specs=pl.BlockSpec((tr, C), lambda i: (i, 0)),
        out_shape=jax.ShapeDtypeStruct((R, C), out_dtype), compiler_params=_params(("parallel",)),
    )(*([a] * n))


def _coords():
    return lax.axis_index("x"), lax.axis_index("y"), lax.axis_index("c")


def _other_chips(x, y):
    return [(1 - x, y), (x, 1 - y), (1 - x, 1 - y)]


def _hbm_call(body, name, ins, out_shapes, n_sems, n_local):
    scratch = [pltpu.SemaphoreType.DMA((n_sems,)), pltpu.SemaphoreType.DMA((n_sems,))]
    if n_local:
        scratch.append(pltpu.SemaphoreType.DMA((n_local,)))
    return pl.pallas_call(
        body, name=name, out_shape=tuple(out_shapes),
        in_specs=[pl.BlockSpec(memory_space=pl.ANY)] * len(ins),
        out_specs=tuple(pl.BlockSpec(memory_space=pl.ANY) for _ in out_shapes),
        scratch_shapes=scratch,
    )(*ins)


def allgather_small(v, name):
    m_per, n = v.shape

    def body(x_ref, out_ref, send_sems, recv_sems, local_sem):
        x, y, c = _coords()
        me, sibling = (x, y, c), (x, y, 1 - c)
        chips = _other_chips(x, y)

        def rows(px, py, pc):
            return out_ref.at[pl.ds((4 * px + 2 * py + pc) * m_per, m_per), :]

        def copy(k, block, to, src=None):
            return pltpu.make_async_remote_copy(
                src_ref=rows(*block) if src is None else src, dst_ref=rows(*block),
                send_sem=send_sems.at[k], recv_sem=recv_sems.at[k], device_id=to, device_id_type=MESH)

        mine = pltpu.make_async_copy(x_ref, rows(*me), local_sem)
        mine.start()
        first = [copy(0, me, sibling, src=x_ref)]
        first += [copy(1 + j, me, (*chip, c), src=x_ref) for j, chip in enumerate(chips)]
        for cp in first:
            cp.start()
        passed = [copy(4 + j, (*chip, c), sibling) for j, chip in enumerate(chips)]
        for j, chip in enumerate(chips):
            copy(1 + j, (*chip, c), me).wait_recv()
            passed[j].start()
        copy(0, sibling, me).wait_recv()
        for j, chip in enumerate(chips):
            copy(4 + j, (*chip, 1 - c), me).wait_recv()
        for cp in first + passed:
            cp.wait_send()
        mine.wait()

    return pl.pallas_call(
        body, name=name, out_shape=jax.ShapeDtypeStruct((N_DEV * m_per, n), v.dtype),
        in_specs=[pl.BlockSpec(memory_space=pltpu.VMEM)], out_specs=pl.BlockSpec(memory_space=pltpu.VMEM),
        scratch_shapes=[pltpu.SemaphoreType.DMA((7,)), pltpu.SemaphoreType.DMA((7,)), pltpu.SemaphoreType.DMA],
        compiler_params=pltpu.CompilerParams(vmem_limit_bytes=V7X_VMEM_LIMIT),
    )(v)


def allgather_chips(arrs, name):
    n = len(arrs)
    halves = [a.shape[0] // 2 for a in arrs]

    def body(*refs):
        xs, outs = refs[:n], refs[n:2 * n]
        send_sems, recv_sems, local_sems = refs[2 * n:]
        x, y, c = _coords()
        me, sibling, chips = 2 * x + y, (x, y, 1 - c), _other_chips(x, y)
        local = [pltpu.make_async_copy(xs[i], outs[i].at[me], local_sems.at[i]) for i in range(n)]
        for cp in local:
            cp.start()

        def half(ref, cc, i):
            return ref.at[pl.ds(cc * halves[i], halves[i]), :]

        def copy(i, k, src, dst, to):
            return pltpu.make_async_remote_copy(src_ref=src, dst_ref=dst, send_sem=send_sems.at[6 * i + k],
                                                recv_sem=recv_sems.at[6 * i + k], device_id=to, device_id_type=MESH)

        sends = [copy(i, k, half(xs[i], c, i), half(outs[i].at[me], c, i), (*chip, c))
                 for k, chip in enumerate(chips) for i in range(n)]
        for cp in sends:
            cp.start()
        passed = []
        for k, (px, py) in enumerate(chips):
            for i in range(n):
                landed = half(outs[i].at[2 * px + py], c, i)
                copy(i, k, landed, landed, (px, py, c)).wait_recv()
                passed.append(copy(i, 3 + k, landed, landed, sibling))
                passed[-1].start()
        for k, (px, py) in enumerate(chips):
            for i in range(n):
                theirs = half(outs[i].at[2 * px + py], 1 - c, i)
                copy(i, 3 + k, theirs, theirs, sibling).wait_recv()
        for cp in sends + passed:
            cp.wait_send()
        for cp in local:
            cp.wait()

    return _hbm_call(body, name, arrs, [jax.ShapeDtypeStruct((N_CHIPS,) + a.shape, a.dtype) for a in arrs], 6 * n, n)


def pair_exchange(arrs, name):
    n = len(arrs)
    halves = [a.shape[1] // 2 for a in arrs]

    def body(*refs):
        xs, mine, theirs = refs[:n], refs[n:2 * n], refs[2 * n:3 * n]
        send_sems, recv_sems, local_sems = refs[3 * n:]
        x, y, c = _coords()
        cps = []
        for i in range(n):
            cps.append(pltpu.make_async_copy(xs[i].at[:, pl.ds(c * halves[i], halves[i]), :], mine[i], local_sems.at[i]))
            cps.append(pltpu.make_async_remote_copy(
                src_ref=xs[i].at[:, pl.ds((1 - c) * halves[i], halves[i]), :], dst_ref=theirs[i],
                send_sem=send_sems.at[i], recv_sem=recv_sems.at[i], device_id=(x, y, 1 - c), device_id_type=MESH))
        for cp in cps:
            cp.start()
        for cp in cps:
            cp.wait()

    shapes = [jax.ShapeDtypeStruct((a.shape[0], a.shape[1] // 2, a.shape[2]), a.dtype) for a in arrs]
    res = _hbm_call(body, name, arrs, shapes + shapes, n, n)
    return res[:n], res[n:]


def scatter_chips(arrs, name):
    n = len(arrs)

    def body(*refs):
        xs, outs = refs[:n], refs[n:2 * n]
        send_sems, recv_sems, local_sems = refs[2 * n:]
        x, y, c = _coords()
        me, chips = 2 * x + y, _other_chips(x, y)
        local = [pltpu.make_async_copy(xs[i].at[me], outs[i].at[me], local_sems.at[i]) for i in range(n)]
        for cp in local:
            cp.start()

        def copy(i, k, src_slot, dst_slot, to):
            return pltpu.make_async_remote_copy(
                src_ref=xs[i].at[src_slot], dst_ref=outs[i].at[dst_slot], send_sem=send_sems.at[3 * i + k],
                recv_sem=recv_sems.at[3 * i + k], device_id=to, device_id_type=MESH)

        sends = [copy(i, k, 2 * px + py, me, (px, py, c)) for k, (px, py) in enumerate(chips) for i in range(n)]
        for cp in sends:
            cp.start()
        for k, (px, py) in enumerate(chips):
            for i in range(n):
                copy(i, k, me, 2 * px + py, (px, py, c)).wait_recv()
        for cp in sends:
            cp.wait_send()
        for cp in local:
            cp.wait()

    return _hbm_call(body, name, arrs, [jax.ShapeDtypeStruct(a.shape, a.dtype) for a in arrs], 3 * n, n)


def pair_allgather(arrs, name):
    n = len(arrs)

    def body(*refs):
        xs, outs = refs[:n], refs[n:2 * n]
        send_sems, recv_sems, local_sems = refs[2 * n:]
        x, y, c = _coords()
        cps = []
        for i in range(n):
            r = xs[i].shape[0]
            dst = outs[i].at[pl.ds(c * r, r), :]
            cps.append(pltpu.make_async_copy(xs[i], dst, local_sems.at[i]))
            cps.append(pltpu.make_async_remote_copy(src_ref=xs[i], dst_ref=dst, send_sem=send_sems.at[i],
                                                    recv_sem=recv_sems.at[i], device_id=(x, y, 1 - c), device_id_type=MESH))
        for cp in cps:
            cp.start()
        for cp in cps:
            cp.wait()

    return _hbm_call(body, name, arrs, [jax.ShapeDtypeStruct((2 * a.shape[0], a.shape[1]), a.dtype) for a in arrs], n, n)


GROUPS = (("ffn_w_gu",), ("ffn_w_down", "mla_w_o", "ssd_w_out"), ("mla_w_a",), ("mla_w_qb",), ("mla_w_kvb",), ("ssd_w_in",))


def _pad_rows(a, mult):
    r = (-a.shape[0]) % mult
    return a if r == 0 else jnp.concatenate([a, jnp.zeros((r, a.shape[1]), a.dtype)], axis=0)


def _rows2d(a):
    return a.reshape(-1, a.shape[-1])


def _unstack(st, axis):
    full = jnp.moveaxis(st, 0, axis)
    sh = list(full.shape)
    sh[axis:axis + 2] = [sh[axis] * sh[axis + 1]]
    return full.reshape(sh)


def _stack(full, axis):
    sh = list(full.shape)
    sh[axis:axis + 1] = [N_CHIPS, sh[axis] // N_CHIPS]
    return jnp.moveaxis(full.reshape(sh), axis, 0)


def kernel(x, c, positions, norm_gain, ada_w, ada_b, ffn_w_gu, ffn_w_down, mla_w_a, mla_q_a_gain, mla_kv_a_gain, mla_w_qb, mla_w_kvb, mla_q_gain, mla_k_gain, mla_w_o, ssd_w_in, ssd_conv_w, ssd_conv_b, ssd_dt_bias, ssd_a_log, ssd_d, ssd_norm_gain, ssd_w_out, loss_target, m_norm_gain, m_ada_w, m_ada_b, m_ffn_w_gu, m_ffn_w_down, m_mla_w_a, m_mla_q_a_gain, m_mla_kv_a_gain, m_mla_w_qb, m_mla_w_kvb, m_mla_q_gain, m_mla_k_gain, m_mla_w_o, m_ssd_w_in, m_ssd_conv_w, m_ssd_conv_b, m_ssd_dt_bias, m_ssd_a_log, m_ssd_d, m_ssd_norm_gain, m_ssd_w_out, v_norm_gain, v_ada_w, v_ada_b, v_ffn_w_gu, v_ffn_w_down, v_mla_w_a, v_mla_q_a_gain, v_mla_kv_a_gain, v_mla_w_qb, v_mla_w_kvb, v_mla_q_gain, v_mla_k_gain, v_mla_w_o, v_ssd_w_in, v_ssd_conv_w, v_ssd_conv_b, v_ssd_dt_bias, v_ssd_a_log, v_ssd_d, v_ssd_norm_gain, v_ssd_w_out):
    w_in = dict(norm_gain=norm_gain, ada_w=ada_w, ada_b=ada_b, ffn_w_gu=ffn_w_gu, ffn_w_down=ffn_w_down, mla_w_a=mla_w_a, mla_q_a_gain=mla_q_a_gain, mla_kv_a_gain=mla_kv_a_gain, mla_w_qb=mla_w_qb, mla_w_kvb=mla_w_kvb, mla_q_gain=mla_q_gain, mla_k_gain=mla_k_gain, mla_w_o=mla_w_o, ssd_w_in=ssd_w_in, ssd_conv_w=ssd_conv_w, ssd_conv_b=ssd_conv_b, ssd_dt_bias=ssd_dt_bias, ssd_a_log=ssd_a_log, ssd_d=ssd_d, ssd_norm_gain=ssd_norm_gain, ssd_w_out=ssd_w_out)
    m_in = dict(norm_gain=m_norm_gain, ada_w=m_ada_w, ada_b=m_ada_b, ffn_w_gu=m_ffn_w_gu, ffn_w_down=m_ffn_w_down, mla_w_a=m_mla_w_a, mla_q_a_gain=m_mla_q_a_gain, mla_kv_a_gain=m_mla_kv_a_gain, mla_w_qb=m_mla_w_qb, mla_w_kvb=m_mla_w_kvb, mla_q_gain=m_mla_q_gain, mla_k_gain=m_mla_k_gain, mla_w_o=m_mla_w_o, ssd_w_in=m_ssd_w_in, ssd_conv_w=m_ssd_conv_w, ssd_conv_b=m_ssd_conv_b, ssd_dt_bias=m_ssd_dt_bias, ssd_a_log=m_ssd_a_log, ssd_d=m_ssd_d, ssd_norm_gain=m_ssd_norm_gain, ssd_w_out=m_ssd_w_out)
    v_in = dict(norm_gain=v_norm_gain, ada_w=v_ada_w, ada_b=v_ada_b, ffn_w_gu=v_ffn_w_gu, ffn_w_down=v_ffn_w_down, mla_w_a=v_mla_w_a, mla_q_a_gain=v_mla_q_a_gain, mla_kv_a_gain=v_mla_kv_a_gain, mla_w_qb=v_mla_w_qb, mla_w_kvb=v_mla_w_kvb, mla_q_gain=v_mla_q_gain, mla_k_gain=v_mla_k_gain, mla_w_o=v_mla_w_o, ssd_w_in=v_ssd_w_in, ssd_conv_w=v_ssd_conv_w, ssd_conv_b=v_ssd_conv_b, ssd_dt_bias=v_ssd_dt_bias, ssd_a_log=v_ssd_a_log, ssd_d=v_ssd_d, ssd_norm_gain=v_ssd_norm_gain, ssd_w_out=v_ssd_w_out)
    names = list(w_in)
    xi, yi, ci = _coords()
    chip = 2 * xi + yi
    batch = 4 * xi + 2 * yi + ci
    x2, target = x[0], loss_target[0]

    small_sharded = ("norm_gain", "ssd_conv_w", "ssd_conv_b", "ssd_norm_gain")
    pack0 = jnp.concatenate([c.reshape(-1)] + [w_in[n].reshape(-1) for n in small_sharded])
    pack0 = _pad_rows(pack0.reshape(-1, 128), 8)
    g0 = allgather_small(pack0, "gather_small").reshape(N_DEV, -1)
    c_all = g0[:, :D_MODEL]
    P, off = {}, D_MODEL
    for n in small_sharded:
        sz = w_in[n].size
        st = g0[0::2, off:off + sz].reshape((N_CHIPS,) + w_in[n].shape)
        P[n] = _unstack(st, w_in[n].ndim - 1)
        off += sz
    for n in ("mla_q_a_gain", "mla_kv_a_gain", "mla_q_gain", "mla_k_gain", "ssd_dt_bias", "ssd_a_log", "ssd_d"):
        P[n] = w_in[n]

    sc = _silu(c_all)
    n_ada = ada_w.shape[2]
    b_sh = lax.dynamic_slice_in_dim(ada_b, chip * n_ada, n_ada, axis=1)
    mods_sh = jnp.stack([matmul(sc, ada_w[l], "nn", "ada_fwd") for l in range(DEPTH)]) + b_sh[:, None, :]
    g1 = allgather_small(mods_sh.reshape(-1, 128), "gather_mods").reshape(N_DEV, DEPTH, N_DEV, n_ada)
    mods = lax.dynamic_index_in_dim(g1[0::2], batch, axis=2, keepdims=False)
    mods = mods.transpose(1, 0, 2).reshape(DEPTH, 3, 3, D_MODEL)

    shard_groups = [jnp.concatenate([_rows2d(w_in[n]).astype(BF16) for n in grp], axis=0) for grp in GROUPS]
    gathered = allgather_chips(shard_groups, "gather_weights")
    G = {}
    for grp, arr in zip(GROUPS, gathered):
        off = 0
        for n in grp:
            rows = w_in[n].size // w_in[n].shape[-1]
            G[n] = arr[:, off:off + rows].reshape((N_CHIPS,) + w_in[n].shape)
            off += rows
    W = {
        "ffn_w_gu": [[_unstack(G["ffn_w_gu"][:, i, t], 1) for t in range(2)] for i in range(DEPTH)],
        "ffn_w_down": [[_unstack(G["ffn_w_down"][:, i, t], 0) for t in range(2)] for i in range(DEPTH)],
        "mla_w_a": [_unstack(G["mla_w_a"][:, j], 0) for j in range(2)],
        "mla_w_qb": [_unstack(G["mla_w_qb"][:, j], 1) for j in range(2)],
        "mla_w_kvb": [_unstack(G["mla_w_kvb"][:, j], 1) for j in range(2)],
        "mla_w_o": [_unstack(G["mla_w_o"][:, j], 0) for j in range(2)],
        "ssd_w_out": [_unstack(G["ssd_w_out"][:, j], 0) for j in range(2)],
    }
    w_in_full = [_unstack(G["ssd_w_in"][:, j], 1) for j in range(2)]
    W["ssd_w_z"] = [w[:, :D_INNER] for w in w_in_full]
    W["ssd_w_xbc"] = [w[:, D_INNER:D_INNER + CONV_DIM] for w in w_in_full]
    W["ssd_w_dt"] = [w[:, D_INNER + CONV_DIM:] for w in w_in_full]
    X = stand_ins(W)

    pos = positions[0]
    y, vjp = jax.vjp(lambda a, b, p_, x_: trunk(a, b, p_, W, x_, pos), x2, mods, P, X)
    dy, loss_cols = loss_head(y, target)
    dx, dmods, dP, dX = vjp(dy)
    loss = lax.psum(jnp.sum(loss_cols), ("x", "y", "c"))

    small_names = ("norm_gain", "ssd_conv_w", "ssd_conv_b", "ssd_norm_gain", "mla_q_a_gain", "mla_kv_a_gain",
                   "mla_q_gain", "mla_k_gain", "ssd_dt_bias", "ssd_a_log", "ssd_d")
    pack1 = jnp.concatenate([dmods.reshape(-1)] + [dP[n].reshape(-1) for n in small_names])
    pack1 = _pad_rows(jnp.pad(pack1, (0, (-pack1.size) % 128)).reshape(-1, 128), 8)
    rows1 = pack1.shape[0]
    g2 = allgather_small(pack1, "gather_small_grads")
    tot = sum_slots(g2.reshape(N_DEV, rows1, 128), "sum_small_grads").reshape(-1)
    n_mod = DEPTH * 9 * D_MODEL
    grads = {"ada_b": tot[:n_mod].reshape(DEPTH, 9 * D_MODEL)}
    off = n_mod
    for n in small_names:
        sz = dP[n].size
        full = tot[off:off + sz].reshape(dP[n].shape)
        off += sz
        if n in small_sharded:
            k = w_in[n].shape[-1]
            full = lax.dynamic_slice_in_dim(full, chip * k, k, axis=full.ndim - 1)
        grads[n] = full
    dmods_all = g2.reshape(N_DEV, -1)[:, :n_mod].reshape(N_DEV, DEPTH, 9 * D_MODEL)
    dm_sh = lax.dynamic_slice_in_dim(dmods_all, chip * n_ada, n_ada, axis=2)
    grads["ada_w"] = jnp.stack([matmul(sc, dm_sh[:, l], "tn", "ada_dw") for l in range(DEPTH)])

    w_in_g = [jnp.concatenate([dX["ssd_w_z"][j], dX["ssd_w_xbc"][j], dX["ssd_w_dt"][j]], axis=1) for j in range(2)]
    per_name = {
        "ffn_w_gu": [dX["ffn_w_gu"][i][t] for i in range(DEPTH) for t in range(2)],
        "ffn_w_down": [dX["ffn_w_down"][i][t].reshape(N_CHIPS, -1, D_MODEL) for i in range(DEPTH) for t in range(2)],
        "mla_w_a": [g.reshape(N_CHIPS, -1, g.shape[-1]) for g in dX["mla_w_a"]],
        "mla_w_qb": dX["mla_w_qb"], "mla_w_kvb": dX["mla_w_kvb"],
        "mla_w_o": [g.reshape(N_CHIPS, -1, D_MODEL) for g in dX["mla_w_o"]],
        "ssd_w_out": [g.reshape(N_CHIPS, -1, D_MODEL) for g in dX["ssd_w_out"]],
        "ssd_w_in": [_stack(g, 1) for g in w_in_g],
    }
    grad_groups = [jnp.concatenate([p for n in grp for p in per_name[n]], axis=1) for grp in GROUPS]
    mine, theirs = pair_exchange(grad_groups, "grads_to_sibling")
    pair = [sum_parts([_rows2d(a), _rows2d(b)], "sum_pair", out_dtype=BF16).reshape(a.shape) for a, b in zip(mine, theirs)]
    landed = scatter_chips(pair, "grads_to_chips")
    half_sums = [sum_slots(a, "sum_chips") for a in landed]
    totals = pair_allgather(half_sums, "grad_halves_swap")
    for grp, arr in zip(GROUPS, totals):
        off = 0
        for n in grp:
            rows = w_in[n].size // w_in[n].shape[-1]
            grads[n] = arr[off:off + rows].reshape(w_in[n].shape)
            off += rows

    deltas, new_m, new_v = {}, {}, {}
    for n in names:
        w = w_in[n]
        d, nm, nv = adamw(_rows2d(w), _rows2d(m_in[n]), _rows2d(v_in[n]), _rows2d(grads[n]))
        deltas[n], new_m[n], new_v[n] = d.reshape(w.shape), nm.reshape(w.shape), nv.reshape(w.shape)

    return (loss, dx[None], *[grads[n] for n in names], *[deltas[n] for n in names],
            *[new_m[n] for n in names], *[new_v[n] for n in names])
```

```python
import numpy as np

import jax
import jax.numpy as jnp
from jax import lax
from jax.experimental import pallas as pl
from jax.experimental.pallas import tpu as pltpu

F32 = jnp.float32
BF16 = jnp.bfloat16
MESH = pl.DeviceIdType.MESH

D_MODEL = 1024
DEPTH = 4
EPS = 1e-6
D_FF = 2816
MLA_HEADS = 16
Q_LORA = 384
KV_LORA = 256
QK_NOPE = 64
QK_ROPE = 32
QK_HEAD = QK_NOPE + QK_ROPE
V_HEAD = 64
ROPE_THETA = 10000.0
D_INNER = 2048
SSD_HEAD_DIM = 64
SSD_HEADS = 32
SSD_GROUPS = 4
SSD_STATE = 128
CONV_WIDTH = 4
CHUNK = 128
CONV_DIM = D_INNER + 2 * SSD_GROUPS * SSD_STATE
ADAM_LR = 0.001
ADAM_B1 = 0.9
ADAM_B2 = 0.999
ADAM_EPS = 1e-08
ADAM_WD = 0.01
ADAM_STEP = 10

N_CHIPS = 4
N_DEV = 8
V7X_VMEM_LIMIT = 56 * 1024 * 1024
ATTN_HEADS_PER_STEP = 2


def _params(sem=None):
    return pltpu.CompilerParams(dimension_semantics=sem, vmem_limit_bytes=V7X_VMEM_LIMIT)


def _div_tile(n, pref, quantum):
    if n <= pref:
        return n
    t = (pref // quantum) * quantum
    while t >= quantum:
        if n % t == 0:
            return t
        t -= quantum
    return n


def matmul(a, b, mode, name, out_dtype=F32, stack=0, tm=1024, tn=1024, tk=2816):
    if mode == "nn":
        (M, K), (K2, N) = a.shape, b.shape
    elif mode == "nt":
        (M, K), (N, K2) = a.shape, b.shape
    else:
        (K, M), (K2, N) = a.shape, b.shape
    assert K == K2, (a.shape, b.shape, mode)
    tm = _div_tile(M, tm, 128 if mode == "tn" else 16)
    tn = _div_tile(N // stack if stack else N, tn, 128)
    tk = _div_tile(K, tk, 128)
    nk = K // tk
    if mode == "nn":
        a_spec = pl.BlockSpec((tm, tk), lambda i, j, k: (i, k))
        b_spec = pl.BlockSpec((tk, tn), lambda i, j, k: (k, j))
        dims = (((1,), (0,)), ((), ()))
    elif mode == "nt":
        a_spec = pl.BlockSpec((tm, tk), lambda i, j, k: (i, k))
        b_spec = pl.BlockSpec((tn, tk), lambda i, j, k: (j, k))
        dims = (((1,), (1,)), ((), ()))
    else:
        a_spec = pl.BlockSpec((tk, tm), lambda i, j, k: (k, i))
        b_spec = pl.BlockSpec((tk, tn), lambda i, j, k: (k, j))
        dims = (((0,), (0,)), ((), ()))
    if stack:
        nb = N // stack // tn
        out_spec = pl.BlockSpec((None, tm, tn), lambda i, j, k: (j // nb, i, j % nb))
        out_shape = jax.ShapeDtypeStruct((stack, M, N // stack), out_dtype)
    else:
        out_spec = pl.BlockSpec((tm, tn), lambda i, j, k: (i, j))
        out_shape = jax.ShapeDtypeStruct((M, N), out_dtype)
    use_acc = nk > 1 and out_dtype != F32

    def body(a_ref, b_ref, o_ref, *acc):
        p = lax.dot_general(a_ref[...].astype(BF16), b_ref[...].astype(BF16), dims, preferred_element_type=F32)
        if nk == 1:
            o_ref[...] = p.astype(out_dtype)
            return
        acc_ref = acc[0] if use_acc else o_ref
        k = pl.program_id(2)

        @pl.when(k == 0)
        def _():
            acc_ref[...] = p

        @pl.when(k > 0)
        def _():
            acc_ref[...] += p

        if use_acc:
            @pl.when(k == nk - 1)
            def _():
                o_ref[...] = acc_ref[...].astype(out_dtype)

    return pl.pallas_call(
        body, name=name, grid=(M // tm, N // tn, nk), in_specs=[a_spec, b_spec], out_specs=out_spec, out_shape=out_shape,
        scratch_shapes=[pltpu.VMEM((tm, tn), F32)] if use_acc else [],
        compiler_params=_params(("parallel", "parallel", "arbitrary")),
    )(a, b)


def mm_op(name, out_dtype=F32, stack=0):
    @jax.custom_vjp
    def op(a, w, wp):
        return matmul(a, w, "nn", name + "_fwd", out_dtype=out_dtype)

    def fwd(a, w, wp):
        return op(a, w, wp), (a, w)

    def bwd(res, g):
        a, w = res
        return (matmul(g, w, "nt", name + "_dx", out_dtype=a.dtype), jnp.zeros_like(w),
                matmul(a, g, "tn", name + "_dw", stack=stack, tn=1408, tk=1024))

    op.defvjp(fwd, bwd)
    return op


def _row_tile(rows, widths):
    w = max(widths)
    t = 128 if w > 4096 else (256 if w > 1024 else 512)
    return _div_tile(rows, t, 16)


def _rowwise_fwd(f, rows, vecs, name, out_dtype):
    n_r, n_v = len(rows), len(vecs)
    S = rows[0].shape[0]
    tm = _row_tile(S, [r.shape[1] for r in rows])
    outs = jax.eval_shape(f, *[jax.ShapeDtypeStruct((tm, r.shape[1]), F32) for r in rows], *vecs)

    def body(*refs):
        res = f(*[r[...].astype(F32) for r in refs[: n_r + n_v]])
        for o, r in zip(refs[n_r + n_v:], res):
            o[...] = r.astype(out_dtype)

    return pl.pallas_call(
        body, name=name, grid=(S // tm,),
        in_specs=[pl.BlockSpec((tm, r.shape[1]), lambda i: (i, 0)) for r in rows]
        + [pl.BlockSpec(v.shape, lambda i: (0, 0)) for v in vecs],
        out_specs=tuple(pl.BlockSpec((tm, o.shape[1]), lambda i: (i, 0)) for o in outs),
        out_shape=tuple(jax.ShapeDtypeStruct((S, o.shape[1]), out_dtype) for o in outs),
        compiler_params=_params(("parallel",)),
    )(*rows, *vecs)


def _rowwise_bwd(f, rows, vecs, douts, diff_rows, name):
    n_r, n_v, n_o = len(rows), len(vecs), len(douts)
    S = rows[0].shape[0]
    tm = _row_tile(S, [r.shape[1] for r in rows] + [d.shape[1] for d in douts])
    d_idx = [i for i in range(n_r) if diff_rows[i]]

    def body(*refs):
        row_v = [r[...].astype(F32) for r in refs[:n_r]]
        vec_v = [r[...] for r in refs[n_r:n_r + n_v]]
        dout_v = tuple(r[...].astype(F32) for r in refs[n_r + n_v:n_r + n_v + n_o])
        out_refs = refs[n_r + n_v + n_o:]

        def g(*args):
            full = list(row_v)
            for j, i in enumerate(d_idx):
                full[i] = args[j]
            return f(*full, *args[len(d_idx):])

        _, vjp = jax.vjp(g, *[row_v[i] for i in d_idx], *vec_v)
        grads = vjp(dout_v)
        for j in range(len(d_idx)):
            out_refs[j][...] = grads[j].astype(out_refs[j].dtype)
        step = pl.program_id(0)
        for j in range(n_v):
            gv, o = grads[len(d_idx) + j], out_refs[len(d_idx) + j]

            @pl.when(step == 0)
            def _(gv=gv, o=o):
                o[...] = gv

            @pl.when(step > 0)
            def _(gv=gv, o=o):
                o[...] += gv

    res = pl.pallas_call(
        body, name=name, grid=(S // tm,),
        in_specs=[pl.BlockSpec((tm, r.shape[1]), lambda i: (i, 0)) for r in rows]
        + [pl.BlockSpec(v.shape, lambda i: (0, 0)) for v in vecs]
        + [pl.BlockSpec((tm, d.shape[1]), lambda i: (i, 0)) for d in douts],
        out_specs=tuple([pl.BlockSpec((tm, rows[i].shape[1]), lambda i_: (i_, 0)) for i in d_idx]
                        + [pl.BlockSpec(v.shape, lambda i: (0, 0)) for v in vecs]),
        out_shape=tuple([jax.ShapeDtypeStruct(rows[i].shape, rows[i].dtype) for i in d_idx]
                        + [jax.ShapeDtypeStruct(v.shape, F32) for v in vecs]),
        compiler_params=_params(("arbitrary",)),
    )(*rows, *vecs, *douts)
    drows = [None] * n_r
    for j, i in enumerate(d_idx):
        drows[i] = res[j]
    for i in range(n_r):
        if drows[i] is None:
            drows[i] = jnp.zeros_like(rows[i])
    return tuple(drows) + tuple(res[len(d_idx):])


def rowwise_op(f, name, n_rows, diff_rows=None, out_dtype=F32):
    diff = tuple(diff_rows) if diff_rows is not None else (True,) * n_rows

    @jax.custom_vjp
    def op(*args):
        return _rowwise_fwd(f, args[:n_rows], args[n_rows:], name + "_fwd", out_dtype)

    def fwd(*args):
        return op(*args), args

    def bwd(args, douts):
        return _rowwise_bwd(f, args[:n_rows], args[n_rows:], douts, diff, name + "_bwd")

    op.defvjp(fwd, bwd)
    return op


def _rms(x, gain):
    return x * lax.rsqrt(jnp.mean(x * x, axis=-1, keepdims=True) + EPS) * gain


def _silu(x):
    return x * jax.nn.sigmoid(x)


def _f_modulate(x, gain, shift, scale):
    return (_rms(x, gain) * (1.0 + scale) + shift,)


def _f_rms(x, gain):
    return (_rms(x, gain),)


def _f_swiglu(gu):
    n = gu.shape[1] // 2
    return (_silu(gu[:, :n]) * gu[:, n:],)


def _f_resid(coef):
    def f(x, y, gate):
        return (x + (coef * gate) * y,)
    return f


def _f_gated_norm(y, z, gain):
    g = y * _silu(z)
    n = g.shape[1] // SSD_GROUPS
    return (jnp.concatenate([_rms(g[:, i * n:(i + 1) * n], gain[:, i * n:(i + 1) * n]) for i in range(SSD_GROUPS)], axis=1),)


_NT = (((1,), (1,)), ((), ()))


def _dot(a, b):
    return jnp.dot(a.astype(BF16), b.astype(BF16), preferred_element_type=F32)


def _dot_nt(a, b):
    return lax.dot_general(a.astype(BF16), b.astype(BF16), _NT, preferred_element_type=F32)


def _attn_tile(S):
    return _div_tile(S, 512, 128)


def _causal(t, transposed=False):
    r = lax.broadcasted_iota(jnp.int32, (t, t), 0)
    c = lax.broadcasted_iota(jnp.int32, (t, t), 1)
    return r <= c if transposed else r >= c


def _tri_tables(n, by_key):
    if by_key:
        pairs = [(i, j) for j in range(n) for i in range(j, n)]
    else:
        pairs = [(i, j) for i in range(n) for j in range(i + 1)]
    return (jnp.asarray(np.array([p[0] for p in pairs], np.int32)), jnp.asarray(np.array([p[1] for p in pairs], np.int32)))


def attn_fwd(q, k, v):
    H, S, dk = q.shape
    dv = v.shape[-1]
    t, hb = _attn_tile(S), ATTN_HEADS_PER_STEP
    n = S // t
    scale = dk ** -0.5
    qi_tab, kj_tab = _tri_tables(n, by_key=False)

    def body(qi_ref, kj_ref, q_ref, k_ref, v_ref, o_ref, lse_ref, m_s, l_s, acc_s):
        qi, kj = qi_ref[pl.program_id(1)], kj_ref[pl.program_id(1)]

        @pl.when(kj == 0)
        def _():
            m_s[...] = jnp.full((hb, t, 1), -jnp.inf, F32)
            l_s[...] = jnp.zeros((hb, t, 1), F32)
            acc_s[...] = jnp.zeros((hb, t, dv), F32)

        def step(masked):
            for j in range(hb):
                s = _dot_nt(q_ref[j], k_ref[j]) * scale
                if masked:
                    s = jnp.where(_causal(t), s, -jnp.inf)
                m_new = jnp.maximum(m_s[j], jnp.max(s, axis=-1, keepdims=True))
                alpha = jnp.exp(m_s[j] - m_new)
                p = jnp.exp(s - m_new)
                l_s[j] = alpha * l_s[j] + jnp.sum(p, axis=-1, keepdims=True)
                acc_s[j] = alpha * acc_s[j] + _dot(p, v_ref[j])
                m_s[j] = m_new

        @pl.when(kj < qi)
        def _():
            step(False)

        @pl.when(kj == qi)
        def _():
            step(True)
            o_ref[...] = acc_s[...] / l_s[...]
            lse_ref[...] = m_s[...] + jnp.log(l_s[...])

    qmap = lambda h, s, qi, kj: (h, qi[s], 0)
    kmap = lambda h, s, qi, kj: (h, kj[s], 0)
    return pl.pallas_call(
        body, name="attn_fwd",
        grid_spec=pltpu.PrefetchScalarGridSpec(
            num_scalar_prefetch=2, grid=(H // hb, qi_tab.shape[0]),
            in_specs=[pl.BlockSpec((hb, t, dk), qmap), pl.BlockSpec((hb, t, dk), kmap), pl.BlockSpec((hb, t, dv), kmap)],
            out_specs=(pl.BlockSpec((hb, t, dv), qmap), pl.BlockSpec((hb, t, 1), qmap)),
            scratch_shapes=[pltpu.VMEM((hb, t, 1), F32), pltpu.VMEM((hb, t, 1), F32), pltpu.VMEM((hb, t, dv), F32)]),
        out_shape=(jax.ShapeDtypeStruct((H, S, dv), F32), jax.ShapeDtypeStruct((H, S, 1), F32)),
        compiler_params=_params(("parallel", "arbitrary")),
    )(qi_tab, kj_tab, q, k, v)


def attn_bwd(q, k, v, o, lse, do):
    H, S, dk = q.shape
    dv = v.shape[-1]
    t, hb = _attn_tile(S), ATTN_HEADS_PER_STEP
    n = S // t
    scale = dk ** -0.5
    qi_tab, kj_tab = _tri_tables(n, by_key=False)

    def dq_body(qi_ref, kj_ref, q_ref, k_ref, v_ref, o_ref, do_ref, lse_ref, dq_ref, delta_ref):
        qi, kj = qi_ref[pl.program_id(1)], kj_ref[pl.program_id(1)]

        @pl.when(kj == 0)
        def _():
            delta_ref[...] = jnp.sum(do_ref[...] * o_ref[...], axis=-1, keepdims=True)
            dq_ref[...] = jnp.zeros((hb, t, dk), F32)

        def step(masked):
            for j in range(hb):
                s = _dot_nt(q_ref[j], k_ref[j]) * scale
                p = jnp.exp(s - lse_ref[j])
                if masked:
                    p = jnp.where(_causal(t), p, 0.0)
                dp = _dot_nt(do_ref[j], v_ref[j])
                ds = p * (dp - delta_ref[j])
                dq_ref[j] += _dot(ds, k_ref[j]) * scale

        @pl.when(kj < qi)
        def _():
            step(False)

        @pl.when(kj == qi)
        def _():
            step(True)

    qmap = lambda h, s, qi, kj: (h, qi[s], 0)
    kmap = lambda h, s, qi, kj: (h, kj[s], 0)
    dq, delta = pl.pallas_call(
        dq_body, name="attn_bwd_dq",
        grid_spec=pltpu.PrefetchScalarGridSpec(
            num_scalar_prefetch=2, grid=(H // hb, qi_tab.shape[0]),
            in_specs=[pl.BlockSpec((hb, t, dk), qmap), pl.BlockSpec((hb, t, dk), kmap), pl.BlockSpec((hb, t, dv), kmap),
                      pl.BlockSpec((hb, t, dv), qmap), pl.BlockSpec((hb, t, dv), qmap), pl.BlockSpec((hb, t, 1), qmap)],
            out_specs=(pl.BlockSpec((hb, t, dk), qmap), pl.BlockSpec((hb, t, 1), qmap))),
        out_shape=(jax.ShapeDtypeStruct((H, S, dk), F32), jax.ShapeDtypeStruct((H, S, 1), F32)),
        compiler_params=_params(("parallel", "arbitrary")),
    )(qi_tab, kj_tab, q, k, v, o, do, lse)

    lse_r = lse.reshape(H, 1, S)
    delta_r = delta.reshape(H, 1, S)
    do_b = do.astype(BF16)
    qi_tab2, kj_tab2 = _tri_tables(n, by_key=True)

    def dkv_body(qi_ref, kj_ref, q_ref, k_ref, v_ref, do_ref, lse_ref, delta_ref, dk_ref, dv_ref):
        qi, kj = qi_ref[pl.program_id(1)], kj_ref[pl.program_id(1)]

        def step(masked):
            for j in range(hb):
                st = _dot_nt(k_ref[j], q_ref[j]) * scale
                pt = jnp.exp(st - lse_ref[j])
                if masked:
                    pt = jnp.where(_causal(t, transposed=True), pt, 0.0)
                dvj = _dot(pt, do_ref[j])
                dpt = _dot_nt(v_ref[j], do_ref[j])
                dst = pt * (dpt - delta_ref[j])
                dkj = _dot(dst, q_ref[j]) * scale
                if masked:
                    dv_ref[j] = dvj
                    dk_ref[j] = dkj
                else:
                    dv_ref[j] += dvj
                    dk_ref[j] += dkj

        @pl.when(qi == kj)
        def _():
            step(True)

        @pl.when(qi > kj)
        def _():
            step(False)

    qmap2 = lambda h, s, qi, kj: (h, qi[s], 0)
    kmap2 = lambda h, s, qi, kj: (h, kj[s], 0)
    rowq = lambda h, s, qi, kj: (h, 0, qi[s])
    dk_, dv_ = pl.pallas_call(
        dkv_body, name="attn_bwd_dkv",
        grid_spec=pltpu.PrefetchScalarGridSpec(
            num_scalar_prefetch=2, grid=(H // hb, qi_tab2.shape[0]),
            in_specs=[pl.BlockSpec((hb, t, dk), qmap2), pl.BlockSpec((hb, t, dk), kmap2), pl.BlockSpec((hb, t, dv), kmap2),
                      pl.BlockSpec((hb, t, dv), qmap2), pl.BlockSpec((hb, 1, t), rowq), pl.BlockSpec((hb, 1, t), rowq)],
            out_specs=(pl.BlockSpec((hb, t, dk), kmap2), pl.BlockSpec((hb, t, dv), kmap2))),
        out_shape=(jax.ShapeDtypeStruct((H, S, dk), F32), jax.ShapeDtypeStruct((H, S, dv), F32)),
        compiler_params=_params(("parallel", "arbitrary")),
    )(qi_tab2, kj_tab2, q, k, v, do_b, lse_r, delta_r)
    return dq, dk_, dv_


@jax.custom_vjp
def attention(q, k, v):
    return attn_fwd(q.astype(BF16), k.astype(BF16), v.astype(BF16))[0]


def _attention_fwd(q, k, v):
    qb, kb, vb = q.astype(BF16), k.astype(BF16), v.astype(BF16)
    o, lse = attn_fwd(qb, kb, vb)
    return o, (qb, kb, vb, o, lse)


def _attention_bwd(res, do):
    return attn_bwd(*res, do)


attention.defvjp(_attention_fwd, _attention_bwd)


def _ssd_specs(hb, L, P, N, order):
    xs = pl.BlockSpec((hb, L, P), lambda g, c: (g, order(c), 0))
    col = pl.BlockSpec((hb, L, 1), lambda g, c: (g, order(c), 0))
    row = pl.BlockSpec((hb, 1, L), lambda g, c: (g, 0, order(c)))
    bc = pl.BlockSpec((None, L, N), lambda g, c: (g, order(c), 0))
    st = pl.BlockSpec((hb, None, N, P), lambda g, c: (g, order(c), 0, 0))
    return xs, col, row, bc, st


def ssd_fwd(x, dt, ac, Bm, Cm):
    H, S, P = x.shape
    G, _, N = Bm.shape
    hb, L = H // G, CHUNK
    nc = S // L
    acr = ac.reshape(H, 1, S)

    def body(x_ref, dt_ref, ac_ref, acr_ref, b_ref, c_ref, y_ref, hp_ref, h_s):
        @pl.when(pl.program_id(1) == 0)
        def _():
            h_s[...] = jnp.zeros((hb, N, P), F32)

        Bv, Cv = b_ref[...], c_ref[...]
        cb = _dot_nt(Cv, Bv)
        bt = Bv.T
        mask = _causal(L)
        for j in range(hb):
            a = ac_ref[j]
            lm = jnp.exp(jnp.where(mask, a - acr_ref[j], -jnp.inf))
            xdt = x_ref[j] * dt_ref[j]
            h = h_s[j]
            hp_ref[j] = h
            y_ref[j] = _dot(cb * lm, xdt) + jnp.exp(a) * _dot(Cv, h)
            al = ac_ref[j, L - 1:L, :]
            h_s[j] = jnp.exp(al) * h + _dot(bt, xdt * jnp.exp(al - a))

    xs, col, row, bc, st = _ssd_specs(hb, L, P, N, lambda c: c)
    return pl.pallas_call(
        body, name="ssd_fwd", grid=(G, nc), in_specs=[xs, col, col, row, bc, bc], out_specs=(xs, st),
        out_shape=(jax.ShapeDtypeStruct((H, S, P), F32), jax.ShapeDtypeStruct((H, nc, N, P), F32)),
        scratch_shapes=[pltpu.VMEM((hb, N, P), F32)],
        compiler_params=_params(("parallel", "arbitrary")),
    )(x, dt, ac, acr, Bm, Cm)


def ssd_bwd(x, dt, ac, Bm, Cm, hp, dy):
    H, S, P = x.shape
    G, _, N = Bm.shape
    hb, L = H // G, CHUNK
    nc = S // L
    acr = ac.reshape(H, 1, S)

    def body(x_ref, dt_ref, ac_ref, acr_ref, b_ref, c_ref, hp_ref, dy_ref, dx_ref, ddt_ref, dac_ref, db_ref, dc_ref, dh_s):
        @pl.when(pl.program_id(1) == 0)
        def _():
            dh_s[...] = jnp.zeros((hb, N, P), F32)

        Bv, Cv = b_ref[...], c_ref[...]
        cb = _dot_nt(Cv, Bv)
        ct = Cv.T
        mask = _causal(L)
        last = lax.broadcasted_iota(jnp.int32, (L, 1), 0) == L - 1
        db = jnp.zeros((L, N), F32)
        dc = jnp.zeros((L, N), F32)
        for j in range(hb):
            a, xv, dtv, g, h, dh = ac_ref[j], x_ref[j], dt_ref[j], dy_ref[j], hp_ref[j], dh_s[j]
            lm = jnp.exp(jnp.where(mask, a - acr_ref[j], -jnp.inf))
            gm = cb * lm
            xdt = xv * dtv
            e = jnp.exp(a)
            al = ac_ref[j, L - 1:L, :]
            dte = jnp.exp(al - a)
            el = jnp.exp(al)
            dcb = _dot_nt(g, xdt) * lm
            dseg = dcb * cb
            ch = _dot(Cv, h)
            bdh = _dot(Bv, dh)
            dxdt = _dot(gm.T, g) + dte * bdh
            dc += _dot(dcb, Bv) + e * _dot_nt(g, h)
            db += _dot(dcb.T, Cv) + _dot_nt(xdt * dte, dh)
            d_e = jnp.sum(g * ch, axis=-1, keepdims=True)
            d_dte = jnp.sum(xdt * bdh, axis=-1, keepdims=True)
            d_el = jnp.sum(h * dh, keepdims=True)
            dac = (jnp.sum(dseg, axis=-1, keepdims=True) - jnp.sum(dseg.T, axis=-1, keepdims=True) + d_e * e - d_dte * dte)
            d_al = jnp.sum(d_dte * dte, keepdims=True) + d_el * el
            dac_ref[j] = dac + jnp.where(last, d_al, 0.0)
            dx_ref[j] = dxdt * dtv
            ddt_ref[j] = jnp.sum(dxdt * xv, axis=-1, keepdims=True)
            dh_s[j] = el * dh + _dot(ct, e * g)
        db_ref[...] = db
        dc_ref[...] = dc

    xs, col, row, bc, st = _ssd_specs(hb, L, P, N, lambda c: nc - 1 - c)
    return pl.pallas_call(
        body, name="ssd_bwd", grid=(G, nc), in_specs=[xs, col, col, row, bc, bc, st, xs],
        out_specs=(xs, col, col, bc, bc),
        out_shape=(jax.ShapeDtypeStruct((H, S, P), F32), jax.ShapeDtypeStruct((H, S, 1), F32),
                   jax.ShapeDtypeStruct((H, S, 1), F32), jax.ShapeDtypeStruct((G, S, N), F32),
                   jax.ShapeDtypeStruct((G, S, N), F32)),
        scratch_shapes=[pltpu.VMEM((hb, N, P), F32)],
        compiler_params=_params(("parallel", "arbitrary")),
    )(x, dt, ac, acr, Bm, Cm, hp, dy)


@jax.custom_vjp
def ssd_scan(x, dt, ac, Bm, Cm):
    return ssd_fwd(x, dt, ac, Bm, Cm)[0]


def _ssd_scan_fwd(x, dt, ac, Bm, Cm):
    y, hp = ssd_fwd(x, dt, ac, Bm, Cm)
    return y, (x, dt, ac, Bm, Cm, hp)


def _ssd_scan_bwd(res, dy):
    return ssd_bwd(*res, dy)


ssd_scan.defvjp(_ssd_scan_fwd, _ssd_scan_bwd)


def _vec(v):
    return v.reshape(1, -1)


def _ffn(x, gain, m3, w_gu, w_down, x_gu, x_down):
    h, = rowwise_op(_f_modulate, "modulate", 1, out_dtype=BF16)(x, _vec(gain), _vec(m3[0]), _vec(m3[1]))
    gu = mm_op("ffn_gu", out_dtype=BF16, stack=N_CHIPS)(h, w_gu, x_gu)
    a, = rowwise_op(_f_swiglu, "swiglu", 1, out_dtype=BF16)(gu)
    y = mm_op("ffn_down")(a, w_down, x_down)
    return rowwise_op(_f_resid(0.5), "resid_half", 2)(x, y, _vec(m3[2]))[0]


def _rope_tables(positions):
    inv = 1.0 / (ROPE_THETA ** (jnp.arange(0, QK_ROPE, 2, dtype=F32) / QK_ROPE))
    ang = positions.astype(F32)[:, None] * inv
    return jnp.cos(ang), jnp.sin(ang)


def _rope(x, cos, sin):
    x1, x2 = jnp.split(x, 2, axis=-1)
    cos, sin = cos[:, None], sin[:, None]
    return jnp.concatenate([x1 * cos - x2 * sin, x1 * sin + x2 * cos], axis=-1)


def _mla(h, cos, sin, P, W, X, j):
    S = h.shape[0]
    lat = mm_op("mla_a")(h, W["mla_w_a"][j], X["mla_w_a"][j])
    q_lat, kv_lat, k_rope = lat[:, :Q_LORA], lat[:, Q_LORA:Q_LORA + KV_LORA], lat[:, Q_LORA + KV_LORA:]
    qn, = rowwise_op(_f_rms, "rms_lat", 1, out_dtype=BF16)(q_lat, _vec(P["mla_q_a_gain"][j]))
    kvn, = rowwise_op(_f_rms, "rms_lat", 1, out_dtype=BF16)(kv_lat, _vec(P["mla_kv_a_gain"][j]))
    q = mm_op("mla_qb", stack=N_CHIPS)(qn, W["mla_w_qb"][j], X["mla_w_qb"][j]).reshape(S, MLA_HEADS, QK_HEAD)
    kv = mm_op("mla_kvb", stack=N_CHIPS)(kvn, W["mla_w_kvb"][j], X["mla_w_kvb"][j]).reshape(S, MLA_HEADS, QK_NOPE + V_HEAD)
    k_nope, v = kv[..., :QK_NOPE], kv[..., QK_NOPE:]
    k = jnp.concatenate([k_nope, jnp.broadcast_to(k_rope[:, None, :], (S, MLA_HEADS, QK_ROPE))], axis=-1)
    q = _rms(q, P["mla_q_gain"][j])
    k = _rms(k, P["mla_k_gain"][j])
    q = jnp.concatenate([q[..., :QK_NOPE], _rope(q[..., QK_NOPE:], cos, sin)], axis=-1)
    k = jnp.concatenate([k[..., :QK_NOPE], _rope(k[..., QK_NOPE:], cos, sin)], axis=-1)
    o = attention(q.transpose(1, 0, 2), k.transpose(1, 0, 2), v.transpose(1, 0, 2))
    o = o.transpose(1, 0, 2).reshape(S, MLA_HEADS * V_HEAD).astype(BF16)
    return mm_op("mla_o")(o, W["mla_w_o"][j], X["mla_w_o"][j])


def _ssd(h, P, W, X, j):
    S = h.shape[0]
    z = mm_op("ssd_in_z")(h, W["ssd_w_z"][j], X["ssd_w_z"][j])
    xbc = mm_op("ssd_in_xbc")(h, W["ssd_w_xbc"][j], X["ssd_w_xbc"][j])
    dtr = mm_op("ssd_in_dt")(h, W["ssd_w_dt"][j], X["ssd_w_dt"][j])
    cw, cb = P["ssd_conv_w"][j], P["ssd_conv_b"][j]
    up = jnp.pad(xbc, ((CONV_WIDTH - 1, 0), (0, 0)))
    conv = sum(cw[t][None, :] * up[t:t + S] for t in range(CONV_WIDTH)) + cb[None, :]
    xbc = _silu(conv)
    xs = xbc[:, :D_INNER].reshape(S, SSD_HEADS, SSD_HEAD_DIM).transpose(1, 0, 2)
    Bm = xbc[:, D_INNER:D_INNER + SSD_GROUPS * SSD_STATE].reshape(S, SSD_GROUPS, SSD_STATE).transpose(1, 0, 2)
    Cm = xbc[:, D_INNER + SSD_GROUPS * SSD_STATE:].reshape(S, SSD_GROUPS, SSD_STATE).transpose(1, 0, 2)
    dt = jax.nn.softplus(dtr + P["ssd_dt_bias"][j][None, :])
    A = -jnp.exp(P["ssd_a_log"][j])
    a = (dt * A[None, :]).reshape(S // CHUNK, CHUNK, SSD_HEADS)
    ac = jnp.cumsum(a, axis=1).reshape(S, SSD_HEADS)
    y = ssd_scan(xs, dt.T[:, :, None], ac.T[:, :, None], Bm, Cm)
    y = y + P["ssd_d"][j][:, None, None] * xs
    y = y.transpose(1, 0, 2).reshape(S, D_INNER)
    g, = rowwise_op(_f_gated_norm, "gated_norm", 2, out_dtype=BF16)(y, z, _vec(P["ssd_norm_gain"][j]))
    return mm_op("ssd_out")(g, W["ssd_w_out"][j], X["ssd_w_out"][j])


def trunk(x, mods, P, W, X, positions):
    cos, sin = _rope_tables(positions)
    for i in range(DEPTH):
        m, j = mods[i], i // 2
        x = _ffn(x, P["norm_gain"][i, 0], m[0], W["ffn_w_gu"][i][0], W["ffn_w_down"][i][0],
                 X["ffn_w_gu"][i][0], X["ffn_w_down"][i][0])
        h, = rowwise_op(_f_modulate, "modulate", 1, out_dtype=BF16)(x, _vec(P["norm_gain"][i, 1]), _vec(m[1, 0]), _vec(m[1, 1]))
        y = _mla(h, cos, sin, P, W, X, j) if i % 2 == 0 else _ssd(h, P, W, X, j)
        x, = rowwise_op(_f_resid(1.0), "resid_full", 2)(x, y, _vec(m[1, 2]))
        x = _ffn(x, P["norm_gain"][i, 2], m[2], W["ffn_w_gu"][i][1], W["ffn_w_down"][i][1],
                 X["ffn_w_gu"][i][1], X["ffn_w_down"][i][1])
    return x


def stand_ins(W):
    def one(name, w):
        if name in ("ffn_w_gu", "mla_w_qb", "mla_w_kvb"):
            return jnp.zeros((N_CHIPS, w.shape[0], w.shape[1] // N_CHIPS), F32)
        return jnp.zeros(w.shape, F32)
    return {n: jax.tree.map(lambda w, n=n: one(n, w), W[n]) for n in W}


def loss_head(y, target):
    S, D = y.shape
    tm = _row_tile(S, [D])

    def body(y_ref, t_ref, dy_ref, l_ref):
        d = y_ref[...] - t_ref[...]
        dy_ref[...] = d * (1.0 / D)
        part = jnp.sum(d * d, axis=0, keepdims=True) * (0.5 / D)

        @pl.when(pl.program_id(0) == 0)
        def _():
            l_ref[...] = part

        @pl.when(pl.program_id(0) > 0)
        def _():
            l_ref[...] += part

    return pl.pallas_call(
        body, name="loss_head", grid=(S // tm,),
        in_specs=[pl.BlockSpec((tm, D), lambda i: (i, 0))] * 2,
        out_specs=(pl.BlockSpec((tm, D), lambda i: (i, 0)), pl.BlockSpec((1, D), lambda i: (0, 0))),
        out_shape=(jax.ShapeDtypeStruct((S, D), F32), jax.ShapeDtypeStruct((1, D), F32)),
        compiler_params=_params(("arbitrary",)),
    )(y, target)


def _stream_rows(R, C):
    return _div_tile(R, max(16, (1 << 19) // C // 16 * 16), 16)


def adamw(w, m, v, g):
    R, C = w.shape
    tr = _stream_rows(R, C)
    c1 = 1.0 / (1.0 - ADAM_B1 ** ADAM_STEP)
    c2 = 1.0 / (1.0 - ADAM_B2 ** ADAM_STEP)

    def body(w_ref, m_ref, v_ref, g_ref, d_ref, nm_ref, nv_ref):
        gv = g_ref[...]
        nm = ADAM_B1 * m_ref[...] + (1.0 - ADAM_B1) * gv
        nv = ADAM_B2 * v_ref[...] + (1.0 - ADAM_B2) * (gv * gv)
        d_ref[...] = -ADAM_LR * ((nm * c1) / (jnp.sqrt(nv * c2) + ADAM_EPS) + ADAM_WD * w_ref[...])
        nm_ref[...] = nm
        nv_ref[...] = nv

    spec = pl.BlockSpec((tr, C), lambda i: (i, 0))
    return pl.pallas_call(
        body, name="adamw", grid=(R // tr,), in_specs=[spec] * 4, out_specs=(spec,) * 3,
        out_shape=(jax.ShapeDtypeStruct((R, C), F32),) * 3, compiler_params=_params(("parallel",)),
    )(w, m, v, g)


def sum_parts(parts, name, out_dtype=F32):
    R, C = parts[0].shape
    tr = _stream_rows(R, C)
    n = len(parts)

    def body(*refs):
        acc = refs[0][...].astype(F32)
        for r in refs[1:n]:
            acc = acc + r[...].astype(F32)
        refs[n][...] = acc.astype(out_dtype)

    spec = pl.BlockSpec((tr, C), lambda i: (i, 0))
    return pl.pallas_call(
        body, name=name, grid=(R // tr,), in_specs=[spec] * n, out_specs=spec,
        out_shape=jax.ShapeDtypeStruct((R, C), out_dtype), compiler_params=_params(("parallel",)),
    )(*parts)


def sum_own_half(full, theirs, c, name, out_dtype):
    n, R, C = full.shape
    h = R // 2
    tr = _stream_rows(h, C)
    nb = h // tr

    def body(c_ref, a_ref, b_ref, o_ref):
        o_ref[...] = (a_ref[...].astype(F32) + b_ref[...].astype(F32)).astype(out_dtype)

    return pl.pallas_call(
        body, name=name,
        grid_spec=pltpu.PrefetchScalarGridSpec(
            num_scalar_prefetch=1, grid=(n, nb),
            in_specs=[pl.BlockSpec((None, tr, C), lambda p, i, cr: (p, cr[0] * nb + i, 0)),
                      pl.BlockSpec((None, tr, C), lambda p, i, cr: (p, i, 0))],
            out_specs=pl.BlockSpec((None, tr, C), lambda p, i, cr: (p, i, 0))),
        out_shape=jax.ShapeDtypeStruct((n, h, C), out_dtype), compiler_params=_params(("parallel", "parallel")),
    )(jnp.reshape(c, (1,)).astype(jnp.int32), full, theirs)


def sum_slots(a, name, out_dtype=F32):
    n, R, C = a.shape
    tr = _stream_rows(R, C)

    def body(*refs):
        acc = refs[0][...].astype(F32)
        for r in refs[1:n]:
            acc = acc + r[...].astype(F32)
        refs[n][...] = acc.astype(out_dtype)

    return pl.pallas_call(
        body, name=name, grid=(R // tr,),
        in_specs=[pl.BlockSpec((None, tr, C), lambda i, p=p: (p, i, 0)) for p in range(n)],
        out_specs=pl.BlockSpec((tr, C), lambda i: (i, 0)),
        out_shape=jax.ShapeDtypeStruct((R, C), out_dtype), compiler_params=_params(("parallel",)),
    )(*([a] * n))


def _coords():
    return lax.axis_index("x"), lax.axis_index("y"), lax.axis_index("c")


def _other_chips(x, y):
    return [(1 - x, y), (x, 1 - y), (1 - x, 1 - y)]


def _hbm_call(body, name, ins, out_shapes, n_sems):
    return pl.pallas_call(
        body, name=name, out_shape=tuple(out_shapes),
        in_specs=[pl.BlockSpec(memory_space=pl.ANY)] * len(ins),
        out_specs=tuple(pl.BlockSpec(memory_space=pl.ANY) for _ in out_shapes),
        scratch_shapes=[pltpu.SemaphoreType.DMA((n_sems,)), pltpu.SemaphoreType.DMA((n_sems,))],
    )(*ins)


def allgather_small(v, name):
    m_per, n = v.shape

    def body(x_ref, out_ref, send_sems, recv_sems, local_sem):
        x, y, c = _coords()
        me, sibling = (x, y, c), (x, y, 1 - c)
        chips = _other_chips(x, y)

        def rows(px, py, pc):
            return out_ref.at[pl.ds((4 * px + 2 * py + pc) * m_per, m_per), :]

        def copy(k, block, to, src=None):
            return pltpu.make_async_remote_copy(
                src_ref=rows(*block) if src is None else src, dst_ref=rows(*block),
                send_sem=send_sems.at[k], recv_sem=recv_sems.at[k], device_id=to, device_id_type=MESH)

        mine = pltpu.make_async_copy(x_ref, rows(*me), local_sem)
        mine.start()
        first = [copy(0, me, sibling, src=x_ref)]
        first += [copy(1 + j, me, (*chip, c), src=x_ref) for j, chip in enumerate(chips)]
        for cp in first:
            cp.start()
        passed = [copy(4 + j, (*chip, c), sibling) for j, chip in enumerate(chips)]
        for j, chip in enumerate(chips):
            copy(1 + j, (*chip, c), me).wait_recv()
            passed[j].start()
        copy(0, sibling, me).wait_recv()
        for j, chip in enumerate(chips):
            copy(4 + j, (*chip, 1 - c), me).wait_recv()
        for cp in first + passed:
            cp.wait_send()
        mine.wait()

    return pl.pallas_call(
        body, name=name, out_shape=jax.ShapeDtypeStruct((N_DEV * m_per, n), v.dtype),
        in_specs=[pl.BlockSpec(memory_space=pltpu.VMEM)], out_specs=pl.BlockSpec(memory_space=pltpu.VMEM),
        scratch_shapes=[pltpu.SemaphoreType.DMA((7,)), pltpu.SemaphoreType.DMA((7,)), pltpu.SemaphoreType.DMA],
        compiler_params=pltpu.CompilerParams(vmem_limit_bytes=V7X_VMEM_LIMIT),
    )(v)


def allgather_chips(arrs, name):
    n = len(arrs)
    halves = [a.shape[0] // 2 for a in arrs]

    def body(*refs):
        xs, outs = refs[:n], refs[n:2 * n]
        send_sems, recv_sems = refs[2 * n:]
        x, y, c = _coords()
        me, sibling, chips = 2 * x + y, (x, y, 1 - c), _other_chips(x, y)

        def half(ref, cc, i):
            return ref.at[pl.ds(cc * halves[i], halves[i]), :]

        def copy(i, k, src, dst, to):
            return pltpu.make_async_remote_copy(src_ref=src, dst_ref=dst, send_sem=send_sems.at[6 * i + k],
                                                recv_sem=recv_sems.at[6 * i + k], device_id=to, device_id_type=MESH)

        sends = [copy(i, k, half(xs[i], c, i), half(outs[i].at[me], c, i), (*chip, c))
                 for k, chip in enumerate(chips) for i in range(n)]
        for cp in sends:
            cp.start()
        passed = []
        for k, (px, py) in enumerate(chips):
            for i in range(n):
                landed = half(outs[i].at[2 * px + py], c, i)
                copy(i, k, landed, landed, (px, py, c)).wait_recv()
                passed.append(copy(i, 3 + k, landed, landed, sibling))
                passed[-1].start()
        for k, (px, py) in enumerate(chips):
            for i in range(n):
                theirs = half(outs[i].at[2 * px + py], 1 - c, i)
                copy(i, 3 + k, theirs, theirs, sibling).wait_recv()
        for cp in sends + passed:
            cp.wait_send()

    return _hbm_call(body, name, arrs, [jax.ShapeDtypeStruct((N_CHIPS,) + a.shape, a.dtype) for a in arrs], 6 * n)


def pair_exchange(arrs, name):
    n = len(arrs)
    halves = [a.shape[1] // 2 for a in arrs]

    def body(*refs):
        xs, theirs = refs[:n], refs[n:2 * n]
        send_sems, recv_sems = refs[2 * n:]
        x, y, c = _coords()
        cps = [pltpu.make_async_remote_copy(
            src_ref=xs[i].at[:, pl.ds((1 - c) * halves[i], halves[i]), :], dst_ref=theirs[i],
            send_sem=send_sems.at[i], recv_sem=recv_sems.at[i], device_id=(x, y, 1 - c), device_id_type=MESH)
            for i in range(n)]
        for cp in cps:
            cp.start()
        for cp in cps:
            cp.wait()

    shapes = [jax.ShapeDtypeStruct((a.shape[0], a.shape[1] // 2, a.shape[2]), a.dtype) for a in arrs]
    return _hbm_call(body, name, arrs, shapes, n)


def scatter_chips(arrs, name):
    n = len(arrs)

    def body(*refs):
        xs, outs = refs[:n], refs[n:2 * n]
        send_sems, recv_sems = refs[2 * n:]
        x, y, c = _coords()
        me, chips = 2 * x + y, _other_chips(x, y)

        def copy(i, k, src_slot, dst_slot, to):
            return pltpu.make_async_remote_copy(
                src_ref=xs[i].at[src_slot], dst_ref=outs[i].at[dst_slot], send_sem=send_sems.at[3 * i + k],
                recv_sem=recv_sems.at[3 * i + k], device_id=to, device_id_type=MESH)

        sends = [copy(i, k, 2 * px + py, me, (px, py, c)) for k, (px, py) in enumerate(chips) for i in range(n)]
        for cp in sends:
            cp.start()
        for k, (px, py) in enumerate(chips):
            for i in range(n):
                copy(i, k, me, 2 * px + py, (px, py, c)).wait_recv()
        for cp in sends:
            cp.wait_send()

    return _hbm_call(body, name, arrs, [jax.ShapeDtypeStruct(a.shape, a.dtype) for a in arrs], 3 * n)


def pair_allgather(arrs, name):
    n = len(arrs)

    def body(*refs):
        xs, outs = refs[:n], refs[n:2 * n]
        send_sems, recv_sems = refs[2 * n:]
        x, y, c = _coords()
        cps = [pltpu.make_async_remote_copy(
            src_ref=xs[i], dst_ref=outs[i].at[pl.ds(c * xs[i].shape[0], xs[i].shape[0]), :], send_sem=send_sems.at[i],
            recv_sem=recv_sems.at[i], device_id=(x, y, 1 - c), device_id_type=MESH) for i in range(n)]
        for cp in cps:
            cp.start()
        for cp in cps:
            cp.wait()

    return _hbm_call(body, name, arrs, [jax.ShapeDtypeStruct((2 * a.shape[0], a.shape[1]), a.dtype) for a in arrs], n)


GROUPS = (("ffn_w_gu",), ("ffn_w_down", "mla_w_o", "ssd_w_out"), ("mla_w_a",), ("mla_w_qb",), ("mla_w_kvb",), ("ssd_w_in",))


def _pad_rows(a, mult):
    r = (-a.shape[0]) % mult
    return a if r == 0 else jnp.concatenate([a, jnp.zeros((r, a.shape[1]), a.dtype)], axis=0)


def _rows2d(a):
    return a.reshape(-1, a.shape[-1])


def _unstack(st, axis):
    full = jnp.moveaxis(st, 0, axis)
    sh = list(full.shape)
    sh[axis:axis + 2] = [sh[axis] * sh[axis + 1]]
    return full.reshape(sh)


def _stack(full, axis):
    sh = list(full.shape)
    sh[axis:axis + 1] = [N_CHIPS, sh[axis] // N_CHIPS]
    return jnp.moveaxis(full.reshape(sh), axis, 0)


def kernel(x, c, positions, norm_gain, ada_w, ada_b, ffn_w_gu, ffn_w_down, mla_w_a, mla_q_a_gain, mla_kv_a_gain, mla_w_qb, mla_w_kvb, mla_q_gain, mla_k_gain, mla_w_o, ssd_w_in, ssd_conv_w, ssd_conv_b, ssd_dt_bias, ssd_a_log, ssd_d, ssd_norm_gain, ssd_w_out, loss_target, m_norm_gain, m_ada_w, m_ada_b, m_ffn_w_gu, m_ffn_w_down, m_mla_w_a, m_mla_q_a_gain, m_mla_kv_a_gain, m_mla_w_qb, m_mla_w_kvb, m_mla_q_gain, m_mla_k_gain, m_mla_w_o, m_ssd_w_in, m_ssd_conv_w, m_ssd_conv_b, m_ssd_dt_bias, m_ssd_a_log, m_ssd_d, m_ssd_norm_gain, m_ssd_w_out, v_norm_gain, v_ada_w, v_ada_b, v_ffn_w_gu, v_ffn_w_down, v_mla_w_a, v_mla_q_a_gain, v_mla_kv_a_gain, v_mla_w_qb, v_mla_w_kvb, v_mla_q_gain, v_mla_k_gain, v_mla_w_o, v_ssd_w_in, v_ssd_conv_w, v_ssd_conv_b, v_ssd_dt_bias, v_ssd_a_log, v_ssd_d, v_ssd_norm_gain, v_ssd_w_out):
    w_in = dict(norm_gain=norm_gain, ada_w=ada_w, ada_b=ada_b, ffn_w_gu=ffn_w_gu, ffn_w_down=ffn_w_down, mla_w_a=mla_w_a, mla_q_a_gain=mla_q_a_gain, mla_kv_a_gain=mla_kv_a_gain, mla_w_qb=mla_w_qb, mla_w_kvb=mla_w_kvb, mla_q_gain=mla_q_gain, mla_k_gain=mla_k_gain, mla_w_o=mla_w_o, ssd_w_in=ssd_w_in, ssd_conv_w=ssd_conv_w, ssd_conv_b=ssd_conv_b, ssd_dt_bias=ssd_dt_bias, ssd_a_log=ssd_a_log, ssd_d=ssd_d, ssd_norm_gain=ssd_norm_gain, ssd_w_out=ssd_w_out)
    m_in = dict(norm_gain=m_norm_gain, ada_w=m_ada_w, ada_b=m_ada_b, ffn_w_gu=m_ffn_w_gu, ffn_w_down=m_ffn_w_down, mla_w_a=m_mla_w_a, mla_q_a_gain=m_mla_q_a_gain, mla_kv_a_gain=m_mla_kv_a_gain, mla_w_qb=m_mla_w_qb, mla_w_kvb=m_mla_w_kvb, mla_q_gain=m_mla_q_gain, mla_k_gain=m_mla_k_gain, mla_w_o=m_mla_w_o, ssd_w_in=m_ssd_w_in, ssd_conv_w=m_ssd_conv_w, ssd_conv_b=m_ssd_conv_b, ssd_dt_bias=m_ssd_dt_bias, ssd_a_log=m_ssd_a_log, ssd_d=m_ssd_d, ssd_norm_gain=m_ssd_norm_gain, ssd_w_out=m_ssd_w_out)
    v_in = dict(norm_gain=v_norm_gain, ada_w=v_ada_w, ada_b=v_ada_b, ffn_w_gu=v_ffn_w_gu, ffn_w_down=v_ffn_w_down, mla_w_a=v_mla_w_a, mla_q_a_gain=v_mla_q_a_gain, mla_kv_a_gain=v_mla_kv_a_gain, mla_w_qb=v_mla_w_qb, mla_w_kvb=v_mla_w_kvb, mla_q_gain=v_mla_q_gain, mla_k_gain=v_mla_k_gain, mla_w_o=v_mla_w_o, ssd_w_in=v_ssd_w_in, ssd_conv_w=v_ssd_conv_w, ssd_conv_b=v_ssd_conv_b, ssd_dt_bias=v_ssd_dt_bias, ssd_a_log=v_ssd_a_log, ssd_d=v_ssd_d, ssd_norm_gain=v_ssd_norm_gain, ssd_w_out=v_ssd_w_out)
    names = list(w_in)
    xi, yi, ci = _coords()
    chip = 2 * xi + yi
    batch = 4 * xi + 2 * yi + ci
    x2, target = x[0], loss_target[0]

    small_sharded = ("norm_gain", "ssd_conv_w", "ssd_conv_b", "ssd_norm_gain")
    pack0 = jnp.concatenate([c.reshape(-1)] + [w_in[n].reshape(-1) for n in small_sharded])
    pack0 = _pad_rows(pack0.reshape(-1, 128), 8)
    g0 = allgather_small(pack0, "gather_small").reshape(N_DEV, -1)
    c_all = g0[:, :D_MODEL]
    P, off = {}, D_MODEL
    for n in small_sharded:
        sz = w_in[n].size
        st = g0[0::2, off:off + sz].reshape((N_CHIPS,) + w_in[n].shape)
        P[n] = _unstack(st, w_in[n].ndim - 1)
        off += sz
    for n in ("mla_q_a_gain", "mla_kv_a_gain", "mla_q_gain", "mla_k_gain", "ssd_dt_bias", "ssd_a_log", "ssd_d"):
        P[n] = w_in[n]

    sc = _silu(c_all)
    n_ada = ada_w.shape[2]
    b_sh = lax.dynamic_slice_in_dim(ada_b, chip * n_ada, n_ada, axis=1)
    mods_sh = jnp.stack([matmul(sc, ada_w[l], "nn", "ada_fwd") for l in range(DEPTH)]) + b_sh[:, None, :]
    g1 = allgather_small(mods_sh.reshape(-1, 128), "gather_mods").reshape(N_DEV, DEPTH, N_DEV, n_ada)
    mods = lax.dynamic_index_in_dim(g1[0::2], batch, axis=2, keepdims=False)
    mods = mods.transpose(1, 0, 2).reshape(DEPTH, 3, 3, D_MODEL)

    shard_groups = [jnp.concatenate([_rows2d(w_in[n]).astype(BF16) for n in grp], axis=0) for grp in GROUPS]
    gathered = allgather_chips(shard_groups, "gather_weights")
    gathered = [lax.dynamic_update_slice(g, s[None], (chip, 0, 0)) for g, s in zip(gathered, shard_groups)]
    G = {}
    for grp, arr in zip(GROUPS, gathered):
        off = 0
        for n in grp:
            rows = w_in[n].size // w_in[n].shape[-1]
            G[n] = arr[:, off:off + rows].reshape((N_CHIPS,) + w_in[n].shape)
            off += rows
    W = {
        "ffn_w_gu": [[_unstack(G["ffn_w_gu"][:, i, t], 1) for t in range(2)] for i in range(DEPTH)],
        "ffn_w_down": [[_unstack(G["ffn_w_down"][:, i, t], 0) for t in range(2)] for i in range(DEPTH)],
        "mla_w_a": [_unstack(G["mla_w_a"][:, j], 0) for j in range(2)],
        "mla_w_qb": [_unstack(G["mla_w_qb"][:, j], 1) for j in range(2)],
        "mla_w_kvb": [_unstack(G["mla_w_kvb"][:, j], 1) for j in range(2)],
        "mla_w_o": [_unstack(G["mla_w_o"][:, j], 0) for j in range(2)],
        "ssd_w_out": [_unstack(G["ssd_w_out"][:, j], 0) for j in range(2)],
    }
    w_in_full = [_unstack(G["ssd_w_in"][:, j], 1) for j in range(2)]
    W["ssd_w_z"] = [w[:, :D_INNER] for w in w_in_full]
    W["ssd_w_xbc"] = [w[:, D_INNER:D_INNER + CONV_DIM] for w in w_in_full]
    W["ssd_w_dt"] = [w[:, D_INNER + CONV_DIM:] for w in w_in_full]
    X = stand_ins(W)

    pos = positions[0]
    y, vjp = jax.vjp(lambda a, b, p_, x_: trunk(a, b, p_, W, x_, pos), x2, mods, P, X)
    dy, loss_cols = loss_head(y, target)
    dx, dmods, dP, dX = vjp(dy)
    loss = lax.psum(jnp.sum(loss_cols), ("x", "y", "c"))

    small_names = ("norm_gain", "ssd_conv_w", "ssd_conv_b", "ssd_norm_gain", "mla_q_a_gain", "mla_kv_a_gain",
                   "mla_q_gain", "mla_k_gain", "ssd_dt_bias", "ssd_a_log", "ssd_d")
    pack1 = jnp.concatenate([dmods.reshape(-1)] + [dP[n].reshape(-1) for n in small_names])
    pack1 = _pad_rows(jnp.pad(pack1, (0, (-pack1.size) % 128)).reshape(-1, 128), 8)
    rows1 = pack1.shape[0]
    g2 = allgather_small(pack1, "gather_small_grads")
    tot = sum_slots(g2.reshape(N_DEV, rows1, 128), "sum_small_grads").reshape(-1)
    n_mod = DEPTH * 9 * D_MODEL
    grads = {"ada_b": tot[:n_mod].reshape(DEPTH, 9 * D_MODEL)}
    off = n_mod
    for n in small_names:
        sz = dP[n].size
        full = tot[off:off + sz].reshape(dP[n].shape)
        off += sz
        if n in small_sharded:
            k = w_in[n].shape[-1]
            full = lax.dynamic_slice_in_dim(full, chip * k, k, axis=full.ndim - 1)
        grads[n] = full
    dmods_all = g2.reshape(N_DEV, -1)[:, :n_mod].reshape(N_DEV, DEPTH, 9 * D_MODEL)
    dm_sh = lax.dynamic_slice_in_dim(dmods_all, chip * n_ada, n_ada, axis=2)
    grads["ada_w"] = jnp.stack([matmul(sc, dm_sh[:, l], "tn", "ada_dw") for l in range(DEPTH)])

    w_in_g = [jnp.concatenate([dX["ssd_w_z"][j], dX["ssd_w_xbc"][j], dX["ssd_w_dt"][j]], axis=1) for j in range(2)]
    per_name = {
        "ffn_w_gu": [dX["ffn_w_gu"][i][t] for i in range(DEPTH) for t in range(2)],
        "ffn_w_down": [dX["ffn_w_down"][i][t].reshape(N_CHIPS, -1, D_MODEL) for i in range(DEPTH) for t in range(2)],
        "mla_w_a": [g.reshape(N_CHIPS, -1, g.shape[-1]) for g in dX["mla_w_a"]],
        "mla_w_qb": dX["mla_w_qb"], "mla_w_kvb": dX["mla_w_kvb"],
        "mla_w_o": [g.reshape(N_CHIPS, -1, D_MODEL) for g in dX["mla_w_o"]],
        "ssd_w_out": [g.reshape(N_CHIPS, -1, D_MODEL) for g in dX["ssd_w_out"]],
        "ssd_w_in": [_stack(g, 1) for g in w_in_g],
    }
    grad_groups = [jnp.concatenate([p for n in grp for p in per_name[n]], axis=1) for grp in GROUPS]
    theirs = pair_exchange(grad_groups, "grads_to_sibling")
    pair = [sum_own_half(a, b, ci, "sum_pair", BF16) for a, b in zip(grad_groups, theirs)]
    landed = scatter_chips(pair, "grads_to_chips")
    landed = [lax.dynamic_update_slice(a, lax.dynamic_slice_in_dim(p, chip, 1, axis=0), (chip, 0, 0)) for a, p in zip(landed, pair)]
    half_sums = [sum_slots(a, "sum_chips") for a in landed]
    totals = pair_allgather(half_sums, "grad_halves_swap")
    totals = [lax.dynamic_update_slice(t, h, (ci * h.shape[0], 0)) for t, h in zip(totals, half_sums)]
    for grp, arr in zip(GROUPS, totals):
        off = 0
        for n in grp:
            rows = w_in[n].size // w_in[n].shape[-1]
            grads[n] = arr[off:off + rows].reshape(w_in[n].shape)
            off += rows

    deltas, new_m, new_v = {}, {}, {}
    for n in names:
        w = w_in[n]
        d, nm, nv = adamw(_rows2d(w), _rows2d(m_in[n]), _rows2d(v_in[n]), _rows2d(grads[n]))
        deltas[n], new_m[n], new_v[n] = d.reshape(w.shape), nm.reshape(w.shape), nv.reshape(w.shape)

    return (loss, dx[None], *[grads[n] for n in names], *[deltas[n] for n in names],
            *[new_m[n] for n in names], *[new_v[n] for n in names])
```

```python
import numpy as np

import jax
import jax.numpy as jnp
from jax import lax
from jax.experimental import pallas as pl
from jax.experimental.pallas import tpu as pltpu

F32 = jnp.float32
BF16 = jnp.bfloat16
MESH = pl.DeviceIdType.MESH

D_MODEL = 1024
DEPTH = 4
EPS = 1e-6
D_FF = 2816
MLA_HEADS = 16
Q_LORA = 384
KV_LORA = 256
QK_NOPE = 64
QK_ROPE = 32
QK_HEAD = QK_NOPE + QK_ROPE
V_HEAD = 64
ROPE_THETA = 10000.0
D_INNER = 2048
SSD_HEAD_DIM = 64
SSD_HEADS = 32
SSD_GROUPS = 4
SSD_STATE = 128
CONV_WIDTH = 4
CHUNK = 128
CONV_DIM = D_INNER + 2 * SSD_GROUPS * SSD_STATE
ADAM_LR = 0.001
ADAM_B1 = 0.9
ADAM_B2 = 0.999
ADAM_EPS = 1e-08
ADAM_WD = 0.01
ADAM_STEP = 10

N_CHIPS = 4
N_DEV = 8
V7X_VMEM_LIMIT = 56 * 1024 * 1024
ATTN_HEADS_PER_STEP = 4
LANES = 128
LOG2E = 1.4426950408889634


def _params(sem=None):
    return pltpu.CompilerParams(dimension_semantics=sem, vmem_limit_bytes=V7X_VMEM_LIMIT)


def _div_tile(n, pref, quantum):
    if n <= pref:
        return n
    t = (pref // quantum) * quantum
    while t >= quantum:
        if n % t == 0:
            return t
        t -= quantum
    return n


def matmul(a, b, mode, name, out_dtype=F32, stack=0, tm=1024, tn=1024, tk=2816):
    if mode == "nn":
        (M, K), (K2, N) = a.shape, b.shape
    elif mode == "nt":
        (M, K), (N, K2) = a.shape, b.shape
    else:
        (K, M), (K2, N) = a.shape, b.shape
    assert K == K2, (a.shape, b.shape, mode)
    tm = _div_tile(M, tm, 128 if mode == "tn" else 16)
    tn = _div_tile(N // stack if stack else N, tn, 128)
    tk = _div_tile(K, tk, 128)
    nk = K // tk
    if mode == "nn":
        a_spec = pl.BlockSpec((tm, tk), lambda i, j, k: (i, k))
        b_spec = pl.BlockSpec((tk, tn), lambda i, j, k: (k, j))
        dims = (((1,), (0,)), ((), ()))
    elif mode == "nt":
        a_spec = pl.BlockSpec((tm, tk), lambda i, j, k: (i, k))
        b_spec = pl.BlockSpec((tn, tk), lambda i, j, k: (j, k))
        dims = (((1,), (1,)), ((), ()))
    else:
        a_spec = pl.BlockSpec((tk, tm), lambda i, j, k: (k, i))
        b_spec = pl.BlockSpec((tk, tn), lambda i, j, k: (k, j))
        dims = (((0,), (0,)), ((), ()))
    if stack:
        nb = N // stack // tn
        out_spec = pl.BlockSpec((None, tm, tn), lambda i, j, k: (j // nb, i, j % nb))
        out_shape = jax.ShapeDtypeStruct((stack, M, N // stack), out_dtype)
    else:
        out_spec = pl.BlockSpec((tm, tn), lambda i, j, k: (i, j))
        out_shape = jax.ShapeDtypeStruct((M, N), out_dtype)
    use_acc = nk > 1 and out_dtype != F32

    def body(a_ref, b_ref, o_ref, *acc):
        p = lax.dot_general(a_ref[...].astype(BF16), b_ref[...].astype(BF16), dims, preferred_element_type=F32)
        if nk == 1:
            o_ref[...] = p.astype(out_dtype)
            return
        acc_ref = acc[0] if use_acc else o_ref
        k = pl.program_id(2)

        @pl.when(k == 0)
        def _():
            acc_ref[...] = p

        @pl.when(k > 0)
        def _():
            acc_ref[...] += p

        if use_acc:
            @pl.when(k == nk - 1)
            def _():
                o_ref[...] = acc_ref[...].astype(out_dtype)

    return pl.pallas_call(
        body, name=name, grid=(M // tm, N // tn, nk), in_specs=[a_spec, b_spec], out_specs=out_spec, out_shape=out_shape,
        scratch_shapes=[pltpu.VMEM((tm, tn), F32)] if use_acc else [],
        compiler_params=_params(("parallel", "parallel", "arbitrary")),
    )(a, b)


def mm_op(name, out_dtype=F32, stack=0):
    @jax.custom_vjp
    def op(a, w, wp):
        return matmul(a, w, "nn", name + "_fwd", out_dtype=out_dtype)

    def fwd(a, w, wp):
        return op(a, w, wp), (a, w)

    def bwd(res, g):
        a, w = res
        return (matmul(g, w, "nt", name + "_dx", out_dtype=a.dtype), jnp.zeros_like(w),
                matmul(a, g, "tn", name + "_dw", stack=stack, tn=1408, tk=1024))

    op.defvjp(fwd, bwd)
    return op


def _row_tile(rows, widths):
    w = max(widths)
    t = 128 if w > 4096 else (256 if w > 1024 else 512)
    return _div_tile(rows, t, 16)


def _rowwise_fwd(f, rows, vecs, name, out_dtype):
    n_r, n_v = len(rows), len(vecs)
    S = rows[0].shape[0]
    tm = _row_tile(S, [r.shape[1] for r in rows])
    outs = jax.eval_shape(f, *[jax.ShapeDtypeStruct((tm, r.shape[1]), F32) for r in rows], *vecs)

    def body(*refs):
        res = f(*[r[...].astype(F32) for r in refs[: n_r + n_v]])
        for o, r in zip(refs[n_r + n_v:], res):
            o[...] = r.astype(out_dtype)

    return pl.pallas_call(
        body, name=name, grid=(S // tm,),
        in_specs=[pl.BlockSpec((tm, r.shape[1]), lambda i: (i, 0)) for r in rows]
        + [pl.BlockSpec(v.shape, lambda i: (0, 0)) for v in vecs],
        out_specs=tuple(pl.BlockSpec((tm, o.shape[1]), lambda i: (i, 0)) for o in outs),
        out_shape=tuple(jax.ShapeDtypeStruct((S, o.shape[1]), out_dtype) for o in outs),
        compiler_params=_params(("parallel",)),
    )(*rows, *vecs)


def _rowwise_bwd(f, rows, vecs, douts, diff_rows, name):
    n_r, n_v, n_o = len(rows), len(vecs), len(douts)
    S = rows[0].shape[0]
    tm = _row_tile(S, [r.shape[1] for r in rows] + [d.shape[1] for d in douts])
    d_idx = [i for i in range(n_r) if diff_rows[i]]

    def body(*refs):
        row_v = [r[...].astype(F32) for r in refs[:n_r]]
        vec_v = [r[...] for r in refs[n_r:n_r + n_v]]
        dout_v = tuple(r[...].astype(F32) for r in refs[n_r + n_v:n_r + n_v + n_o])
        out_refs = refs[n_r + n_v + n_o:]

        def g(*args):
            full = list(row_v)
            for j, i in enumerate(d_idx):
                full[i] = args[j]
            return f(*full, *args[len(d_idx):])

        _, vjp = jax.vjp(g, *[row_v[i] for i in d_idx], *vec_v)
        grads = vjp(dout_v)
        for j in range(len(d_idx)):
            out_refs[j][...] = grads[j].astype(out_refs[j].dtype)
        step = pl.program_id(0)
        for j in range(n_v):
            gv, o = grads[len(d_idx) + j], out_refs[len(d_idx) + j]

            @pl.when(step == 0)
            def _(gv=gv, o=o):
                o[...] = gv

            @pl.when(step > 0)
            def _(gv=gv, o=o):
                o[...] += gv

    res = pl.pallas_call(
        body, name=name, grid=(S // tm,),
        in_specs=[pl.BlockSpec((tm, r.shape[1]), lambda i: (i, 0)) for r in rows]
        + [pl.BlockSpec(v.shape, lambda i: (0, 0)) for v in vecs]
        + [pl.BlockSpec((tm, d.shape[1]), lambda i: (i, 0)) for d in douts],
        out_specs=tuple([pl.BlockSpec((tm, rows[i].shape[1]), lambda i_: (i_, 0)) for i in d_idx]
                        + [pl.BlockSpec(v.shape, lambda i: (0, 0)) for v in vecs]),
        out_shape=tuple([jax.ShapeDtypeStruct(rows[i].shape, rows[i].dtype) for i in d_idx]
                        + [jax.ShapeDtypeStruct(v.shape, F32) for v in vecs]),
        compiler_params=_params(("arbitrary",)),
    )(*rows, *vecs, *douts)
    drows = [None] * n_r
    for j, i in enumerate(d_idx):
        drows[i] = res[j]
    for i in range(n_r):
        if drows[i] is None:
            drows[i] = jnp.zeros_like(rows[i])
    return tuple(drows) + tuple(res[len(d_idx):])


def rowwise_op(f, name, n_rows, diff_rows=None, out_dtype=F32):
    diff = tuple(diff_rows) if diff_rows is not None else (True,) * n_rows

    @jax.custom_vjp
    def op(*args):
        return _rowwise_fwd(f, args[:n_rows], args[n_rows:], name + "_fwd", out_dtype)

    def fwd(*args):
        return op(*args), args

    def bwd(args, douts):
        return _rowwise_bwd(f, args[:n_rows], args[n_rows:], douts, diff, name + "_bwd")

    op.defvjp(fwd, bwd)
    return op


def _rms(x, gain):
    return x * lax.rsqrt(jnp.mean(x * x, axis=-1, keepdims=True) + EPS) * gain


def _silu(x):
    return x * jax.nn.sigmoid(x)


def _f_modulate(x, gain, shift, scale):
    return (_rms(x, gain) * (1.0 + scale) + shift,)


def _f_rms(x, gain):
    return (_rms(x, gain),)


def _f_swiglu(gu):
    n = gu.shape[1] // 2
    return (_silu(gu[:, :n]) * gu[:, n:],)


def _f_resid(coef):
    def f(x, y, gate):
        return (x + (coef * gate) * y,)
    return f


def _f_gated_norm(y, z, gain):
    g = y * _silu(z)
    n = g.shape[1] // SSD_GROUPS
    return (jnp.concatenate([_rms(g[:, i * n:(i + 1) * n], gain[:, i * n:(i + 1) * n]) for i in range(SSD_GROUPS)], axis=1),)


_NT = (((1,), (1,)), ((), ()))


def _dot(a, b):
    return jnp.dot(a.astype(BF16), b.astype(BF16), preferred_element_type=F32)


def _dot_nt(a, b):
    return lax.dot_general(a.astype(BF16), b.astype(BF16), _NT, preferred_element_type=F32)


def _attn_tile(S):
    return _div_tile(S, 512, 128)


def _causal(t, transposed=False):
    r = lax.broadcasted_iota(jnp.int32, (t, t), 0)
    c = lax.broadcasted_iota(jnp.int32, (t, t), 1)
    return r <= c if transposed else r >= c


def _tri_tables(n, by_key):
    if by_key:
        pairs = [(i, j) for j in range(n) for i in range(j, n)]
    else:
        pairs = [(i, j) for i in range(n) for j in range(i + 1)]
    return (jnp.asarray(np.array([p[0] for p in pairs], np.int32)), jnp.asarray(np.array([p[1] for p in pairs], np.int32)))


def attn_fwd(q, k, v):
    H, S, dk = q.shape
    dv = v.shape[-1]
    t, hb = _attn_tile(S), ATTN_HEADS_PER_STEP
    n = S // t
    scale = dk ** -0.5
    qi_tab, kj_tab = _tri_tables(n, by_key=False)

    def body(qi_ref, kj_ref, q_ref, k_ref, v_ref, o_ref, lse_ref, m_s, l_s, acc_s):
        qi, kj = qi_ref[pl.program_id(1)], kj_ref[pl.program_id(1)]

        @pl.when(kj == 0)
        def _():
            m_s[...] = jnp.full(m_s.shape, -jnp.inf, F32)
            l_s[...] = jnp.zeros(l_s.shape, F32)
            acc_s[...] = jnp.zeros(acc_s.shape, F32)

        def step(masked):
            ss = [_dot_nt(q_ref[j], k_ref[j]) for j in range(hb)]
            new = []
            for j in range(hb):
                s = ss[j] * (scale * LOG2E)
                if masked:
                    s = jnp.where(_causal(t), s, -jnp.inf)
                m_old = m_s[j]
                m_new = jnp.maximum(m_old, jnp.max(s, axis=-1, keepdims=True))
                alpha = jnp.exp2(m_old - m_new)
                p = jnp.exp2(s - jnp.tile(m_new, (1, t // LANES)))
                new.append((m_new, alpha * l_s[j] + jnp.sum(p, axis=-1, keepdims=True),
                            alpha[:, :dv] * acc_s[j] + _dot(p, v_ref[j])))
            for j in range(hb):
                m_s[j], l_s[j], acc_s[j] = new[j]

        @pl.when(kj < qi)
        def _():
            step(False)

        @pl.when(kj == qi)
        def _():
            step(True)
            l = l_s[...]
            o_ref[...] = acc_s[...] / l[:, :, :dv]
            lse_ref[...] = m_s[...] + jnp.log2(l)

    qmap = lambda h, s, qi, kj: (h, qi[s], 0)
    kmap = lambda h, s, qi, kj: (h, kj[s], 0)
    return pl.pallas_call(
        body, name="attn_fwd",
        grid_spec=pltpu.PrefetchScalarGridSpec(
            num_scalar_prefetch=2, grid=(H // hb, qi_tab.shape[0]),
            in_specs=[pl.BlockSpec((hb, t, dk), qmap), pl.BlockSpec((hb, t, dk), kmap), pl.BlockSpec((hb, t, dv), kmap)],
            out_specs=(pl.BlockSpec((hb, t, dv), qmap), pl.BlockSpec((hb, t, LANES), qmap)),
            scratch_shapes=[pltpu.VMEM((hb, t, LANES), F32), pltpu.VMEM((hb, t, LANES), F32), pltpu.VMEM((hb, t, dv), F32)]),
        out_shape=(jax.ShapeDtypeStruct((H, S, dv), F32), jax.ShapeDtypeStruct((H, S, LANES), F32)),
        compiler_params=_params(("parallel", "arbitrary")),
    )(qi_tab, kj_tab, q, k, v)


def attn_bwd(q, k, v, o, lse, do):
    H, S, dk = q.shape
    dv = v.shape[-1]
    t, hb = _attn_tile(S), ATTN_HEADS_PER_STEP
    n = S // t
    scale = dk ** -0.5
    qi_tab, kj_tab = _tri_tables(n, by_key=False)

    def dq_body(qi_ref, kj_ref, q_ref, k_ref, v_ref, o_ref, do_ref, lse_ref, dq_ref, delta_ref):
        qi, kj = qi_ref[pl.program_id(1)], kj_ref[pl.program_id(1)]

        @pl.when(kj == 0)
        def _():
            d = jnp.sum(do_ref[...] * o_ref[...], axis=-1, keepdims=True)
            delta_ref[...] = jnp.broadcast_to(d, delta_ref.shape)
            dq_ref[...] = jnp.zeros((hb, t, dk), F32)

        def step(masked):
            ss = [_dot_nt(q_ref[j], k_ref[j]) for j in range(hb)]
            dps = [_dot_nt(do_ref[j], v_ref[j]) for j in range(hb)]
            for j in range(hb):
                p = jnp.exp2(ss[j] * (scale * LOG2E) - jnp.tile(lse_ref[j], (1, t // LANES)))
                if masked:
                    p = jnp.where(_causal(t), p, 0.0)
                ds = p * (dps[j] - jnp.tile(delta_ref[j], (1, t // LANES)))
                dq_ref[j] += _dot(ds, k_ref[j]) * scale

        @pl.when(kj < qi)
        def _():
            step(False)

        @pl.when(kj == qi)
        def _():
            step(True)

    qmap = lambda h, s, qi, kj: (h, qi[s], 0)
    kmap = lambda h, s, qi, kj: (h, kj[s], 0)
    dq, delta = pl.pallas_call(
        dq_body, name="attn_bwd_dq",
        grid_spec=pltpu.PrefetchScalarGridSpec(
            num_scalar_prefetch=2, grid=(H // hb, qi_tab.shape[0]),
            in_specs=[pl.BlockSpec((hb, t, dk), qmap), pl.BlockSpec((hb, t, dk), kmap), pl.BlockSpec((hb, t, dv), kmap),
                      pl.BlockSpec((hb, t, dv), qmap), pl.BlockSpec((hb, t, dv), qmap), pl.BlockSpec((hb, t, LANES), qmap)],
            out_specs=(pl.BlockSpec((hb, t, dk), qmap), pl.BlockSpec((hb, t, LANES), qmap))),
        out_shape=(jax.ShapeDtypeStruct((H, S, dk), F32), jax.ShapeDtypeStruct((H, S, LANES), F32)),
        compiler_params=_params(("parallel", "arbitrary")),
    )(qi_tab, kj_tab, q, k, v, o, do, lse)

    lse_r = lse[:, :, 0].reshape(H, 1, S)
    delta_r = delta[:, :, 0].reshape(H, 1, S)
    do_b = do.astype(BF16)
    qi_tab2, kj_tab2 = _tri_tables(n, by_key=True)

    def dkv_body(qi_ref, kj_ref, q_ref, k_ref, v_ref, do_ref, lse_ref, delta_ref, dk_ref, dv_ref):
        qi, kj = qi_ref[pl.program_id(1)], kj_ref[pl.program_id(1)]

        def step(masked):
            sts = [_dot_nt(k_ref[j], q_ref[j]) for j in range(hb)]
            dpts = [_dot_nt(v_ref[j], do_ref[j]) for j in range(hb)]
            for j in range(hb):
                pt = jnp.exp2(sts[j] * (scale * LOG2E) - lse_ref[j])
                if masked:
                    pt = jnp.where(_causal(t, transposed=True), pt, 0.0)
                dvj = _dot(pt, do_ref[j])
                dst = pt * (dpts[j] - delta_ref[j])
                dkj = _dot(dst, q_ref[j]) * scale
                if masked:
                    dv_ref[j] = dvj
                    dk_ref[j] = dkj
                else:
                    dv_ref[j] += dvj
                    dk_ref[j] += dkj

        @pl.when(qi == kj)
        def _():
            step(True)

        @pl.when(qi > kj)
        def _():
            step(False)

    qmap2 = lambda h, s, qi, kj: (h, qi[s], 0)
    kmap2 = lambda h, s, qi, kj: (h, kj[s], 0)
    rowq = lambda h, s, qi, kj: (h, 0, qi[s])
    dk_, dv_ = pl.pallas_call(
        dkv_body, name="attn_bwd_dkv",
        grid_spec=pltpu.PrefetchScalarGridSpec(
            num_scalar_prefetch=2, grid=(H // hb, qi_tab2.shape[0]),
            in_specs=[pl.BlockSpec((hb, t, dk), qmap2), pl.BlockSpec((hb, t, dk), kmap2), pl.BlockSpec((hb, t, dv), kmap2),
                      pl.BlockSpec((hb, t, dv), qmap2), pl.BlockSpec((hb, 1, t), rowq), pl.BlockSpec((hb, 1, t), rowq)],
            out_specs=(pl.BlockSpec((hb, t, dk), kmap2), pl.BlockSpec((hb, t, dv), kmap2))),
        out_shape=(jax.ShapeDtypeStruct((H, S, dk), F32), jax.ShapeDtypeStruct((H, S, dv), F32)),
        compiler_params=_params(("parallel", "arbitrary")),
    )(qi_tab2, kj_tab2, q, k, v, do_b, lse_r, delta_r)
    return dq, dk_, dv_


@jax.custom_vjp
def attention(q, k, v):
    return attn_fwd(q.astype(BF16), k.astype(BF16), v.astype(BF16))[0]


def _attention_fwd(q, k, v):
    qb, kb, vb = q.astype(BF16), k.astype(BF16), v.astype(BF16)
    o, lse = attn_fwd(qb, kb, vb)
    return o, (qb, kb, vb, o, lse)


def _attention_bwd(res, do):
    return attn_bwd(*res, do)


attention.defvjp(_attention_fwd, _attention_bwd)


def _ssd_specs(hb, L, P, N, order):
    xs = pl.BlockSpec((hb, L, P), lambda g, c: (g, order(c), 0))
    col = pl.BlockSpec((None, L, hb), lambda g, c: (g, order(c), 0))
    row = pl.BlockSpec((hb, 1, L), lambda g, c: (g, 0, order(c)))
    bc = pl.BlockSpec((None, L, N), lambda g, c: (g, order(c), 0))
    st = pl.BlockSpec((hb, None, N, P), lambda g, c: (g, order(c), 0, 0))
    return xs, col, row, bc, st


def _ssd_rows(cols, H):
    G, S, hb = cols.shape
    return cols.transpose(0, 2, 1).reshape(H, 1, S)


def ssd_fwd(x, dt, ac, Bm, Cm):
    H, S, P = x.shape
    G, _, N = Bm.shape
    hb, L = H // G, CHUNK
    nc = S // L
    acr = _ssd_rows(ac, H)

    def body(x_ref, dt_ref, ac_ref, acr_ref, b_ref, c_ref, y_ref, hp_ref, h_s):
        @pl.when(pl.program_id(1) == 0)
        def _():
            h_s[...] = jnp.zeros((hb, N, P), F32)

        Bv, Cv = b_ref[...], c_ref[...]
        cb = _dot_nt(Cv, Bv)
        bt = Bv.T
        mask = _causal(L)
        ac_all, dt_all = ac_ref[...], dt_ref[...]
        for j in range(hb):
            a = jnp.broadcast_to(ac_all[:, j:j + 1], (L, L))
            dtv = jnp.broadcast_to(dt_all[:, j:j + 1], (L, L))
            lm = jnp.exp(jnp.where(mask, a - acr_ref[j], -jnp.inf))
            xdt = x_ref[j] * dtv[:, :P]
            h = h_s[j]
            hp_ref[j] = h
            y_ref[j] = _dot(cb * lm, xdt) + jnp.exp(a)[:, :P] * _dot(Cv, h)
            al = a[L - 1:L, :]
            h_s[j] = jnp.exp(al)[:, :P] * h + _dot(bt, xdt * jnp.exp(al - a)[:, :P])

    xs, col, row, bc, st = _ssd_specs(hb, L, P, N, lambda c: c)
    return pl.pallas_call(
        body, name="ssd_fwd", grid=(G, nc), in_specs=[xs, col, col, row, bc, bc], out_specs=(xs, st),
        out_shape=(jax.ShapeDtypeStruct((H, S, P), F32), jax.ShapeDtypeStruct((H, nc, N, P), F32)),
        scratch_shapes=[pltpu.VMEM((hb, N, P), F32)],
        compiler_params=_params(("parallel", "arbitrary")),
    )(x, dt, ac, acr, Bm, Cm)


def ssd_bwd(x, dt, ac, Bm, Cm, hp, dy):
    H, S, P = x.shape
    G, _, N = Bm.shape
    hb, L = H // G, CHUNK
    nc = S // L
    acr = _ssd_rows(ac, H)

    def body(x_ref, dt_ref, ac_ref, acr_ref, b_ref, c_ref, hp_ref, dy_ref,
             dx_ref, ddt_ref, dac_ref, dacr_ref, db_ref, dc_ref, dh_s):
        @pl.when(pl.program_id(1) == 0)
        def _():
            dh_s[...] = jnp.zeros((hb, N, P), F32)

        Bv, Cv = b_ref[...], c_ref[...]
        cb = _dot_nt(Cv, Bv)
        cbt = _dot_nt(Bv, Cv)
        ct = Cv.T
        mask, maskt = _causal(L), _causal(L, transposed=True)
        last = lax.broadcasted_iota(jnp.int32, (L, 1), 0) == L - 1
        lane = lax.broadcasted_iota(jnp.int32, (L, hb), 1)
        db = jnp.zeros((L, N), F32)
        dc = jnp.zeros((L, N), F32)
        dac_all = jnp.zeros((L, hb), F32)
        ddt_all = jnp.zeros((L, hb), F32)
        ac_all, dt_all = ac_ref[...], dt_ref[...]
        for j in range(hb):
            ar, xv, g, h, dh = acr_ref[j], x_ref[j], dy_ref[j], hp_ref[j], dh_s[j]
            a = jnp.broadcast_to(ac_all[:, j:j + 1], (L, L))
            dtv = jnp.broadcast_to(dt_all[:, j:j + 1], (L, L))
            lm = jnp.exp(jnp.where(mask, a - ar, -jnp.inf))
            lmt = jnp.exp(jnp.where(maskt, ar - a, -jnp.inf))
            xdt = xv * dtv[:, :P]
            e = jnp.exp(a)
            al = a[L - 1:L, :]
            dte = jnp.exp(al - a)
            el = jnp.exp(al)
            dcb = _dot_nt(g, xdt) * lm
            dcbt = _dot_nt(xdt, g) * lmt
            dseg = dcb * cb
            ch = _dot(Cv, h)
            bdh = _dot(Bv, dh)
            dxdt = _dot(cbt * lmt, g) + dte[:, :P] * bdh
            dc += _dot(dcb, Bv) + e * _dot_nt(g, h)
            db += _dot(dcbt, Cv) + _dot_nt(xdt * dte[:, :P], dh)
            d_e = jnp.sum(g * ch, axis=-1, keepdims=True)
            d_dte = jnp.sum(xdt * bdh, axis=-1, keepdims=True)
            d_el = jnp.sum(h * dh, keepdims=True)
            d_al = jnp.sum(d_dte * dte[:, :1], keepdims=True) + d_el * el[:, :1]
            dac_j = (jnp.sum(dseg, axis=-1, keepdims=True) + d_e * e[:, :1] - d_dte * dte[:, :1]
                     + jnp.where(last, d_al, 0.0))
            dac_all = jnp.where(lane == j, dac_j, dac_all)
            dacr_ref[j] = -jnp.sum(dseg, axis=0, keepdims=True)
            dx_ref[j] = dxdt * dtv[:, :P]
            ddt_all = jnp.where(lane == j, jnp.sum(dxdt * xv, axis=-1, keepdims=True), ddt_all)
            dh_s[j] = el[:, :P] * dh + _dot(ct, e[:, :P] * g)
        dac_ref[...] = dac_all
        ddt_ref[...] = ddt_all
        db_ref[...] = db
        dc_ref[...] = dc

    xs, col, row, bc, st = _ssd_specs(hb, L, P, N, lambda c: nc - 1 - c)
    dx, ddt, dac, dacr, db, dc = pl.pallas_call(
        body, name="ssd_bwd", grid=(G, nc), in_specs=[xs, col, col, row, bc, bc, st, xs],
        out_specs=(xs, col, col, row, bc, bc),
        out_shape=(jax.ShapeDtypeStruct((H, S, P), F32), jax.ShapeDtypeStruct((G, S, hb), F32),
                   jax.ShapeDtypeStruct((G, S, hb), F32), jax.ShapeDtypeStruct((H, 1, S), F32),
                   jax.ShapeDtypeStruct((G, S, N), F32), jax.ShapeDtypeStruct((G, S, N), F32)),
        scratch_shapes=[pltpu.VMEM((hb, N, P), F32)],
        compiler_params=_params(("parallel", "arbitrary")),
    )(x, dt, ac, acr, Bm, Cm, hp, dy)
    return dx, ddt, dac + dacr.reshape(G, hb, S).transpose(0, 2, 1), db, dc


@jax.custom_vjp
def ssd_scan(x, dt, ac, Bm, Cm):
    return ssd_fwd(x, dt, ac, Bm, Cm)[0]


def _ssd_scan_fwd(x, dt, ac, Bm, Cm):
    y, hp = ssd_fwd(x, dt, ac, Bm, Cm)
    return y, (x, dt, ac, Bm, Cm, hp)


def _ssd_scan_bwd(res, dy):
    return ssd_bwd(*res, dy)


ssd_scan.defvjp(_ssd_scan_fwd, _ssd_scan_bwd)


def _vec(v):
    return v.reshape(1, -1)


def _ffn(x, gain, m3, w_gu, w_down, x_gu, x_down):
    h, = rowwise_op(_f_modulate, "modulate", 1, out_dtype=BF16)(x, _vec(gain), _vec(m3[0]), _vec(m3[1]))
    gu = mm_op("ffn_gu", out_dtype=BF16, stack=N_CHIPS)(h, w_gu, x_gu)
    a, = rowwise_op(_f_swiglu, "swiglu", 1, out_dtype=BF16)(gu)
    y = mm_op("ffn_down")(a, w_down, x_down)
    return rowwise_op(_f_resid(0.5), "resid_half", 2)(x, y, _vec(m3[2]))[0]


def _rope_tables(positions):
    inv = 1.0 / (ROPE_THETA ** (jnp.arange(0, QK_ROPE, 2, dtype=F32) / QK_ROPE))
    ang = positions.astype(F32)[:, None] * inv
    return jnp.cos(ang), jnp.sin(ang)


def _rope(x, cos, sin):
    x1, x2 = jnp.split(x, 2, axis=-1)
    cos, sin = cos[:, None], sin[:, None]
    return jnp.concatenate([x1 * cos - x2 * sin, x1 * sin + x2 * cos], axis=-1)


def _mla(h, cos, sin, P, W, X, j):
    S = h.shape[0]
    lat = mm_op("mla_a")(h, W["mla_w_a"][j], X["mla_w_a"][j])
    q_lat, kv_lat, k_rope = lat[:, :Q_LORA], lat[:, Q_LORA:Q_LORA + KV_LORA], lat[:, Q_LORA + KV_LORA:]
    qn, = rowwise_op(_f_rms, "rms_lat", 1, out_dtype=BF16)(q_lat, _vec(P["mla_q_a_gain"][j]))
    kvn, = rowwise_op(_f_rms, "rms_lat", 1, out_dtype=BF16)(kv_lat, _vec(P["mla_kv_a_gain"][j]))
    q = mm_op("mla_qb", stack=N_CHIPS)(qn, W["mla_w_qb"][j], X["mla_w_qb"][j]).reshape(S, MLA_HEADS, QK_HEAD)
    kv = mm_op("mla_kvb", stack=N_CHIPS)(kvn, W["mla_w_kvb"][j], X["mla_w_kvb"][j]).reshape(S, MLA_HEADS, QK_NOPE + V_HEAD)
    k_nope, v = kv[..., :QK_NOPE], kv[..., QK_NOPE:]
    k = jnp.concatenate([k_nope, jnp.broadcast_to(k_rope[:, None, :], (S, MLA_HEADS, QK_ROPE))], axis=-1)
    q = _rms(q, P["mla_q_gain"][j])
    k = _rms(k, P["mla_k_gain"][j])
    q = jnp.concatenate([q[..., :QK_NOPE], _rope(q[..., QK_NOPE:], cos, sin)], axis=-1)
    k = jnp.concatenate([k[..., :QK_NOPE], _rope(k[..., QK_NOPE:], cos, sin)], axis=-1)
    o = attention(q.transpose(1, 0, 2), k.transpose(1, 0, 2), v.transpose(1, 0, 2))
    o = o.transpose(1, 0, 2).reshape(S, MLA_HEADS * V_HEAD).astype(BF16)
    return mm_op("mla_o")(o, W["mla_w_o"][j], X["mla_w_o"][j])


def _ssd(h, P, W, X, j):
    S = h.shape[0]
    z = mm_op("ssd_in_z")(h, W["ssd_w_z"][j], X["ssd_w_z"][j])
    xbc = mm_op("ssd_in_xbc")(h, W["ssd_w_xbc"][j], X["ssd_w_xbc"][j])
    dtr = mm_op("ssd_in_dt")(h, W["ssd_w_dt"][j], X["ssd_w_dt"][j])
    cw, cb = P["ssd_conv_w"][j], P["ssd_conv_b"][j]
    up = jnp.pad(xbc, ((CONV_WIDTH - 1, 0), (0, 0)))
    conv = sum(cw[t][None, :] * up[t:t + S] for t in range(CONV_WIDTH)) + cb[None, :]
    xbc = _silu(conv)
    xs = xbc[:, :D_INNER].reshape(S, SSD_HEADS, SSD_HEAD_DIM).transpose(1, 0, 2)
    Bm = xbc[:, D_INNER:D_INNER + SSD_GROUPS * SSD_STATE].reshape(S, SSD_GROUPS, SSD_STATE).transpose(1, 0, 2)
    Cm = xbc[:, D_INNER + SSD_GROUPS * SSD_STATE:].reshape(S, SSD_GROUPS, SSD_STATE).transpose(1, 0, 2)
    dt = jax.nn.softplus(dtr + P["ssd_dt_bias"][j][None, :])
    A = -jnp.exp(P["ssd_a_log"][j])
    a = (dt * A[None, :]).reshape(S // CHUNK, CHUNK, SSD_HEADS)
    ac = jnp.cumsum(a, axis=1).reshape(S, SSD_HEADS)
    by_group = lambda t: t.reshape(S, SSD_GROUPS, SSD_HEADS // SSD_GROUPS).transpose(1, 0, 2)
    y = ssd_scan(xs, by_group(dt), by_group(ac), Bm, Cm)
    y = y + P["ssd_d"][j][:, None, None] * xs
    y = y.transpose(1, 0, 2).reshape(S, D_INNER)
    g, = rowwise_op(_f_gated_norm, "gated_norm", 2, out_dtype=BF16)(y, z, _vec(P["ssd_norm_gain"][j]))
    return mm_op("ssd_out")(g, W["ssd_w_out"][j], X["ssd_w_out"][j])


def trunk(x, mods, P, W, X, positions):
    cos, sin = _rope_tables(positions)
    for i in range(DEPTH):
        m, j = mods[i], i // 2
        x = _ffn(x, P["norm_gain"][i, 0], m[0], W["ffn_w_gu"][i][0], W["ffn_w_down"][i][0],
                 X["ffn_w_gu"][i][0], X["ffn_w_down"][i][0])
        h, = rowwise_op(_f_modulate, "modulate", 1, out_dtype=BF16)(x, _vec(P["norm_gain"][i, 1]), _vec(m[1, 0]), _vec(m[1, 1]))
        y = _mla(h, cos, sin, P, W, X, j) if i % 2 == 0 else _ssd(h, P, W, X, j)
        x, = rowwise_op(_f_resid(1.0), "resid_full", 2)(x, y, _vec(m[1, 2]))
        x = _ffn(x, P["norm_gain"][i, 2], m[2], W["ffn_w_gu"][i][1], W["ffn_w_down"][i][1],
                 X["ffn_w_gu"][i][1], X["ffn_w_down"][i][1])
    return x


def stand_ins(W):
    def one(name, w):
        if name in ("ffn_w_gu", "mla_w_qb", "mla_w_kvb"):
            return jnp.zeros((N_CHIPS, w.shape[0], w.shape[1] // N_CHIPS), F32)
        return jnp.zeros(w.shape, F32)
    return {n: jax.tree.map(lambda w, n=n: one(n, w), W[n]) for n in W}


def loss_head(y, target):
    S, D = y.shape
    tm = _row_tile(S, [D])

    def body(y_ref, t_ref, dy_ref, l_ref):
        d = y_ref[...] - t_ref[...]
        dy_ref[...] = d * (1.0 / D)
        part = jnp.sum(d * d, axis=0, keepdims=True) * (0.5 / D)

        @pl.when(pl.program_id(0) == 0)
        def _():
            l_ref[...] = part

        @pl.when(pl.program_id(0) > 0)
        def _():
            l_ref[...] += part

    return pl.pallas_call(
        body, name="loss_head", grid=(S // tm,),
        in_specs=[pl.BlockSpec((tm, D), lambda i: (i, 0))] * 2,
        out_specs=(pl.BlockSpec((tm, D), lambda i: (i, 0)), pl.BlockSpec((1, D), lambda i: (0, 0))),
        out_shape=(jax.ShapeDtypeStruct((S, D), F32), jax.ShapeDtypeStruct((1, D), F32)),
        compiler_params=_params(("arbitrary",)),
    )(y, target)


def _stream_rows(R, C):
    return _div_tile(R, max(16, (1 << 19) // C // 16 * 16), 16)


def adamw(w, m, v, g):
    R, C = w.shape
    tr = _stream_rows(R, C)
    c1 = 1.0 / (1.0 - ADAM_B1 ** ADAM_STEP)
    c2 = 1.0 / (1.0 - ADAM_B2 ** ADAM_STEP)

    def body(w_ref, m_ref, v_ref, g_ref, d_ref, nm_ref, nv_ref):
        gv = g_ref[...]
        nm = ADAM_B1 * m_ref[...] + (1.0 - ADAM_B1) * gv
        nv = ADAM_B2 * v_ref[...] + (1.0 - ADAM_B2) * (gv * gv)
        d_ref[...] = -ADAM_LR * ((nm * c1) / (jnp.sqrt(nv * c2) + ADAM_EPS) + ADAM_WD * w_ref[...])
        nm_ref[...] = nm
        nv_ref[...] = nv

    spec = pl.BlockSpec((tr, C), lambda i: (i, 0))
    return pl.pallas_call(
        body, name="adamw", grid=(R // tr,), in_specs=[spec] * 4, out_specs=(spec,) * 3,
        out_shape=(jax.ShapeDtypeStruct((R, C), F32),) * 3, compiler_params=_params(("parallel",)),
    )(w, m, v, g)


def sum_parts(parts, name, out_dtype=F32):
    R, C = parts[0].shape
    tr = _stream_rows(R, C)
    n = len(parts)

    def body(*refs):
        acc = refs[0][...].astype(F32)
        for r in refs[1:n]:
            acc = acc + r[...].astype(F32)
        refs[n][...] = acc.astype(out_dtype)

    spec = pl.BlockSpec((tr, C), lambda i: (i, 0))
    return pl.pallas_call(
        body, name=name, grid=(R // tr,), in_specs=[spec] * n, out_specs=spec,
        out_shape=jax.ShapeDtypeStruct((R, C), out_dtype), compiler_params=_params(("parallel",)),
    )(*parts)


def sum_own_half(full, theirs, c, name, out_dtype):
    n, R, C = full.shape
    h = R // 2
    tr = _stream_rows(h, C)
    nb = h // tr

    def body(c_ref, a_ref, b_ref, o_ref):
        o_ref[...] = (a_ref[...].astype(F32) + b_ref[...].astype(F32)).astype(out_dtype)

    return pl.pallas_call(
        body, name=name,
        grid_spec=pltpu.PrefetchScalarGridSpec(
            num_scalar_prefetch=1, grid=(n, nb),
            in_specs=[pl.BlockSpec((None, tr, C), lambda p, i, cr: (p, cr[0] * nb + i, 0)),
                      pl.BlockSpec((None, tr, C), lambda p, i, cr: (p, i, 0))],
            out_specs=pl.BlockSpec((None, tr, C), lambda p, i, cr: (p, i, 0))),
        out_shape=jax.ShapeDtypeStruct((n, h, C), out_dtype), compiler_params=_params(("parallel", "parallel")),
    )(jnp.reshape(c, (1,)).astype(jnp.int32), full, theirs)


def sum_slots(a, name, out_dtype=F32):
    n, R, C = a.shape
    tr = _stream_rows(R, C)

    def body(*refs):
        acc = refs[0][...].astype(F32)
        for r in refs[1:n]:
            acc = acc + r[...].astype(F32)
        refs[n][...] = acc.astype(out_dtype)

    return pl.pallas_call(
        body, name=name, grid=(R // tr,),
        in_specs=[pl.BlockSpec((None, tr, C), lambda i, p=p: (p, i, 0)) for p in range(n)],
        out_specs=pl.BlockSpec((tr, C), lambda i: (i, 0)),
        out_shape=jax.ShapeDtypeStruct((R, C), out_dtype), compiler_params=_params(("parallel",)),
    )(*([a] * n))


def _coords():
    return lax.axis_index("x"), lax.axis_index("y"), lax.axis_index("c")


def _other_chips(x, y):
    return [(1 - x, y), (x, 1 - y), (1 - x, 1 - y)]


def _hbm_call(body, name, ins, out_shapes, n_sems):
    return pl.pallas_call(
        body, name=name, out_shape=tuple(out_shapes),
        in_specs=[pl.BlockSpec(memory_space=pl.ANY)] * len(ins),
        out_specs=tuple(pl.BlockSpec(memory_space=pl.ANY) for _ in out_shapes),
        scratch_shapes=[pltpu.SemaphoreType.DMA((n_sems,)), pltpu.SemaphoreType.DMA((n_sems,))],
    )(*ins)


def allgather_small(v, name):
    m_per, n = v.shape

    def body(x_ref, out_ref, send_sems, recv_sems, local_sem):
        x, y, c = _coords()
        me, sibling = (x, y, c), (x, y, 1 - c)
        chips = _other_chips(x, y)

        def rows(px, py, pc):
            return out_ref.at[pl.ds((4 * px + 2 * py + pc) * m_per, m_per), :]

        def copy(k, block, to, src=None):
            return pltpu.make_async_remote_copy(
                src_ref=rows(*block) if src is None else src, dst_ref=rows(*block),
                send_sem=send_sems.at[k], recv_sem=recv_sems.at[k], device_id=to, device_id_type=MESH)

        mine = pltpu.make_async_copy(x_ref, rows(*me), local_sem)
        mine.start()
        first = [copy(0, me, sibling, src=x_ref)]
        first += [copy(1 + j, me, (*chip, c), src=x_ref) for j, chip in enumerate(chips)]
        for cp in first:
            cp.start()
        passed = [copy(4 + j, (*chip, c), sibling) for j, chip in enumerate(chips)]
        for j, chip in enumerate(chips):
            copy(1 + j, (*chip, c), me).wait_recv()
            passed[j].start()
        copy(0, sibling, me).wait_recv()
        for j, chip in enumerate(chips):
            copy(4 + j, (*chip, 1 - c), me).wait_recv()
        for cp in first + passed:
            cp.wait_send()
        mine.wait()

    return pl.pallas_call(
        body, name=name, out_shape=jax.ShapeDtypeStruct((N_DEV * m_per, n), v.dtype),
        in_specs=[pl.BlockSpec(memory_space=pltpu.VMEM)], out_specs=pl.BlockSpec(memory_space=pltpu.VMEM),
        scratch_shapes=[pltpu.SemaphoreType.DMA((7,)), pltpu.SemaphoreType.DMA((7,)), pltpu.SemaphoreType.DMA],
        compiler_params=pltpu.CompilerParams(vmem_limit_bytes=V7X_VMEM_LIMIT),
    )(v)


def allgather_chips(arrs, name):
    n = len(arrs)
    halves = [a.shape[0] // 2 for a in arrs]

    def body(*refs):
        xs, outs = refs[:n], refs[n:2 * n]
        send_sems, recv_sems = refs[2 * n:]
        x, y, c = _coords()
        me, sibling, chips = 2 * x + y, (x, y, 1 - c), _other_chips(x, y)

        def half(ref, cc, i):
            return ref.at[pl.ds(cc * halves[i], halves[i]), :]

        def copy(i, k, src, dst, to):
            return pltpu.make_async_remote_copy(src_ref=src, dst_ref=dst, send_sem=send_sems.at[6 * i + k],
                                                recv_sem=recv_sems.at[6 * i + k], device_id=to, device_id_type=MESH)

        sends = [copy(i, k, half(xs[i], c, i), half(outs[i].at[me], c, i), (*chip, c))
                 for k, chip in enumerate(chips) for i in range(n)]
        for cp in sends:
            cp.start()
        passed = []
        for k, (px, py) in enumerate(chips):
            for i in range(n):
                landed = half(outs[i].at[2 * px + py], c, i)
                copy(i, k, landed, landed, (px, py, c)).wait_recv()
                passed.append(copy(i, 3 + k, landed, landed, sibling))
                passed[-1].start()
        for k, (px, py) in enumerate(chips):
            for i in range(n):
                theirs = half(outs[i].at[2 * px + py], 1 - c, i)
                copy(i, 3 + k, theirs, theirs, sibling).wait_recv()
        for cp in sends + passed:
            cp.wait_send()

    return _hbm_call(body, name, arrs, [jax.ShapeDtypeStruct((N_CHIPS,) + a.shape, a.dtype) for a in arrs], 6 * n)


def pair_exchange(arrs, name):
    n = len(arrs)
    halves = [a.shape[1] // 2 for a in arrs]

    def body(*refs):
        xs, theirs = refs[:n], refs[n:2 * n]
        send_sems, recv_sems = refs[2 * n:]
        x, y, c = _coords()
        cps = [pltpu.make_async_remote_copy(
            src_ref=xs[i].at[:, pl.ds((1 - c) * halves[i], halves[i]), :], dst_ref=theirs[i],
            send_sem=send_sems.at[i], recv_sem=recv_sems.at[i], device_id=(x, y, 1 - c), device_id_type=MESH)
            for i in range(n)]
        for cp in cps:
            cp.start()
        for cp in cps:
            cp.wait()

    shapes = [jax.ShapeDtypeStruct((a.shape[0], a.shape[1] // 2, a.shape[2]), a.dtype) for a in arrs]
    return _hbm_call(body, name, arrs, shapes, n)


def scatter_chips(arrs, name):
    n = len(arrs)

    def body(*refs):
        xs, outs = refs[:n], refs[n:2 * n]
        send_sems, recv_sems = refs[2 * n:]
        x, y, c = _coords()
        me, chips = 2 * x + y, _other_chips(x, y)

        def copy(i, k, src_slot, dst_slot, to):
            return pltpu.make_async_remote_copy(
                src_ref=xs[i].at[src_slot], dst_ref=outs[i].at[dst_slot], send_sem=send_sems.at[3 * i + k],
                recv_sem=recv_sems.at[3 * i + k], device_id=to, device_id_type=MESH)

        sends = [copy(i, k, 2 * px + py, me, (px, py, c)) for k, (px, py) in enumerate(chips) for i in range(n)]
        for cp in sends:
            cp.start()
        for k, (px, py) in enumerate(chips):
            for i in range(n):
                copy(i, k, me, 2 * px + py, (px, py, c)).wait_recv()
        for cp in sends:
            cp.wait_send()

    return _hbm_call(body, name, arrs, [jax.ShapeDtypeStruct(a.shape, a.dtype) for a in arrs], 3 * n)


def pair_allgather(arrs, name):
    n = len(arrs)

    def body(*refs):
        xs, outs = refs[:n], refs[n:2 * n]
        send_sems, recv_sems = refs[2 * n:]
        x, y, c = _coords()
        cps = [pltpu.make_async_remote_copy(
            src_ref=xs[i], dst_ref=outs[i].at[pl.ds(c * xs[i].shape[0], xs[i].shape[0]), :], send_sem=send_sems.at[i],
            recv_sem=recv_sems.at[i], device_id=(x, y, 1 - c), device_id_type=MESH) for i in range(n)]
        for cp in cps:
            cp.start()
        for cp in cps:
            cp.wait()

    return _hbm_call(body, name, arrs, [jax.ShapeDtypeStruct((2 * a.shape[0], a.shape[1]), a.dtype) for a in arrs], n)


GROUPS = (("ffn_w_gu",), ("ffn_w_down", "mla_w_o", "ssd_w_out"), ("mla_w_a",), ("mla_w_qb",), ("mla_w_kvb",), ("ssd_w_in",))


def _pad_rows(a, mult):
    r = (-a.shape[0]) % mult
    return a if r == 0 else jnp.concatenate([a, jnp.zeros((r, a.shape[1]), a.dtype)], axis=0)


def _rows2d(a):
    return a.reshape(-1, a.shape[-1])


def _unstack(st, axis):
    full = jnp.moveaxis(st, 0, axis)
    sh = list(full.shape)
    sh[axis:axis + 2] = [sh[axis] * sh[axis + 1]]
    return full.reshape(sh)


def _stack(full, axis):
    sh = list(full.shape)
    sh[axis:axis + 1] = [N_CHIPS, sh[axis] // N_CHIPS]
    return jnp.moveaxis(full.reshape(sh), axis, 0)


def kernel(x, c, positions, norm_gain, ada_w, ada_b, ffn_w_gu, ffn_w_down, mla_w_a, mla_q_a_gain, mla_kv_a_gain, mla_w_qb, mla_w_kvb, mla_q_gain, mla_k_gain, mla_w_o, ssd_w_in, ssd_conv_w, ssd_conv_b, ssd_dt_bias, ssd_a_log, ssd_d, ssd_norm_gain, ssd_w_out, loss_target, m_norm_gain, m_ada_w, m_ada_b, m_ffn_w_gu, m_ffn_w_down, m_mla_w_a, m_mla_q_a_gain, m_mla_kv_a_gain, m_mla_w_qb, m_mla_w_kvb, m_mla_q_gain, m_mla_k_gain, m_mla_w_o, m_ssd_w_in, m_ssd_conv_w, m_ssd_conv_b, m_ssd_dt_bias, m_ssd_a_log, m_ssd_d, m_ssd_norm_gain, m_ssd_w_out, v_norm_gain, v_ada_w, v_ada_b, v_ffn_w_gu, v_ffn_w_down, v_mla_w_a, v_mla_q_a_gain, v_mla_kv_a_gain, v_mla_w_qb, v_mla_w_kvb, v_mla_q_gain, v_mla_k_gain, v_mla_w_o, v_ssd_w_in, v_ssd_conv_w, v_ssd_conv_b, v_ssd_dt_bias, v_ssd_a_log, v_ssd_d, v_ssd_norm_gain, v_ssd_w_out):
    w_in = dict(norm_gain=norm_gain, ada_w=ada_w, ada_b=ada_b, ffn_w_gu=ffn_w_gu, ffn_w_down=ffn_w_down, mla_w_a=mla_w_a, mla_q_a_gain=mla_q_a_gain, mla_kv_a_gain=mla_kv_a_gain, mla_w_qb=mla_w_qb, mla_w_kvb=mla_w_kvb, mla_q_gain=mla_q_gain, mla_k_gain=mla_k_gain, mla_w_o=mla_w_o, ssd_w_in=ssd_w_in, ssd_conv_w=ssd_conv_w, ssd_conv_b=ssd_conv_b, ssd_dt_bias=ssd_dt_bias, ssd_a_log=ssd_a_log, ssd_d=ssd_d, ssd_norm_gain=ssd_norm_gain, ssd_w_out=ssd_w_out)
    m_in = dict(norm_gain=m_norm_gain, ada_w=m_ada_w, ada_b=m_ada_b, ffn_w_gu=m_ffn_w_gu, ffn_w_down=m_ffn_w_down, mla_w_a=m_mla_w_a, mla_q_a_gain=m_mla_q_a_gain, mla_kv_a_gain=m_mla_kv_a_gain, mla_w_qb=m_mla_w_qb, mla_w_kvb=m_mla_w_kvb, mla_q_gain=m_mla_q_gain, mla_k_gain=m_mla_k_gain, mla_w_o=m_mla_w_o, ssd_w_in=m_ssd_w_in, ssd_conv_w=m_ssd_conv_w, ssd_conv_b=m_ssd_conv_b, ssd_dt_bias=m_ssd_dt_bias, ssd_a_log=m_ssd_a_log, ssd_d=m_ssd_d, ssd_norm_gain=m_ssd_norm_gain, ssd_w_out=m_ssd_w_out)
    v_in = dict(norm_gain=v_norm_gain, ada_w=v_ada_w, ada_b=v_ada_b, ffn_w_gu=v_ffn_w_gu, ffn_w_down=v_ffn_w_down, mla_w_a=v_mla_w_a, mla_q_a_gain=v_mla_q_a_gain, mla_kv_a_gain=v_mla_kv_a_gain, mla_w_qb=v_mla_w_qb, mla_w_kvb=v_mla_w_kvb, mla_q_gain=v_mla_q_gain, mla_k_gain=v_mla_k_gain, mla_w_o=v_mla_w_o, ssd_w_in=v_ssd_w_in, ssd_conv_w=v_ssd_conv_w, ssd_conv_b=v_ssd_conv_b, ssd_dt_bias=v_ssd_dt_bias, ssd_a_log=v_ssd_a_log, ssd_d=v_ssd_d, ssd_norm_gain=v_ssd_norm_gain, ssd_w_out=v_ssd_w_out)
    names = list(w_in)
    xi, yi, ci = _coords()
    chip = 2 * xi + yi
    batch = 4 * xi + 2 * yi + ci
    x2, target = x[0], loss_target[0]

    small_sharded = ("norm_gain", "ssd_conv_w", "ssd_conv_b", "ssd_norm_gain")
    pack0 = jnp.concatenate([c.reshape(-1)] + [w_in[n].reshape(-1) for n in small_sharded])
    pack0 = _pad_rows(pack0.reshape(-1, 128), 8)
    g0 = allgather_small(pack0, "gather_small").reshape(N_DEV, -1)
    c_all = g0[:, :D_MODEL]
    P, off = {}, D_MODEL
    for n in small_sharded:
        sz = w_in[n].size
        st = g0[0::2, off:off + sz].reshape((N_CHIPS,) + w_in[n].shape)
        P[n] = _unstack(st, w_in[n].ndim - 1)
        off += sz
    for n in ("mla_q_a_gain", "mla_kv_a_gain", "mla_q_gain", "mla_k_gain", "ssd_dt_bias", "ssd_a_log", "ssd_d"):
        P[n] = w_in[n]

    sc = _silu(c_all)
    n_ada = ada_w.shape[2]
    b_sh = lax.dynamic_slice_in_dim(ada_b, chip * n_ada, n_ada, axis=1)
    mods_sh = jnp.stack([matmul(sc, ada_w[l], "nn", "ada_fwd") for l in range(DEPTH)]) + b_sh[:, None, :]
    g1 = allgather_small(mods_sh.reshape(-1, 128), "gather_mods").reshape(N_DEV, DEPTH, N_DEV, n_ada)
    mods = lax.dynamic_index_in_dim(g1[0::2], batch, axis=2, keepdims=False)
    mods = mods.transpose(1, 0, 2).reshape(DEPTH, 3, 3, D_MODEL)

    shard_groups = [jnp.concatenate([_rows2d(w_in[n]).astype(BF16) for n in grp], axis=0) for grp in GROUPS]
    gathered = allgather_chips(shard_groups, "gather_weights")
    gathered = [lax.dynamic_update_slice(g, s[None], (chip, 0, 0)) for g, s in zip(gathered, shard_groups)]
    G = {}
    for grp, arr in zip(GROUPS, gathered):
        off = 0
        for n in grp:
            rows = w_in[n].size // w_in[n].shape[-1]
            G[n] = arr[:, off:off + rows].reshape((N_CHIPS,) + w_in[n].shape)
            off += rows
    W = {
        "ffn_w_gu": [[_unstack(G["ffn_w_gu"][:, i, t], 1) for t in range(2)] for i in range(DEPTH)],
        "ffn_w_down": [[_unstack(G["ffn_w_down"][:, i, t], 0) for t in range(2)] for i in range(DEPTH)],
        "mla_w_a": [_unstack(G["mla_w_a"][:, j], 0) for j in range(2)],
        "mla_w_qb": [_unstack(G["mla_w_qb"][:, j], 1) for j in range(2)],
        "mla_w_kvb": [_unstack(G["mla_w_kvb"][:, j], 1) for j in range(2)],
        "mla_w_o": [_unstack(G["mla_w_o"][:, j], 0) for j in range(2)],
        "ssd_w_out": [_unstack(G["ssd_w_out"][:, j], 0) for j in range(2)],
    }
    w_in_full = [_unstack(G["ssd_w_in"][:, j], 1) for j in range(2)]
    W["ssd_w_z"] = [w[:, :D_INNER] for w in w_in_full]
    W["ssd_w_xbc"] = [w[:, D_INNER:D_INNER + CONV_DIM] for w in w_in_full]
    W["ssd_w_dt"] = [w[:, D_INNER + CONV_DIM:] for w in w_in_full]
    X = stand_ins(W)

    pos = positions[0]
    y, vjp = jax.vjp(lambda a, b, p_, x_: trunk(a, b, p_, W, x_, pos), x2, mods, P, X)
    dy, loss_cols = loss_head(y, target)
    dx, dmods, dP, dX = vjp(dy)
    loss = lax.psum(jnp.sum(loss_cols), ("x", "y", "c"))

    small_names = ("norm_gain", "ssd_conv_w", "ssd_conv_b", "ssd_norm_gain", "mla_q_a_gain", "mla_kv_a_gain",
                   "mla_q_gain", "mla_k_gain", "ssd_dt_bias", "ssd_a_log", "ssd_d")
    pack1 = jnp.concatenate([dmods.reshape(-1)] + [dP[n].reshape(-1) for n in small_names])
    pack1 = _pad_rows(jnp.pad(pack1, (0, (-pack1.size) % 128)).reshape(-1, 128), 8)
    rows1 = pack1.shape[0]
    g2 = allgather_small(pack1, "gather_small_grads")
    tot = sum_slots(g2.reshape(N_DEV, rows1, 128), "sum_small_grads").reshape(-1)
    n_mod = DEPTH * 9 * D_MODEL
    grads = {"ada_b": tot[:n_mod].reshape(DEPTH, 9 * D_MODEL)}
    off = n_mod
    for n in small_names:
        sz = dP[n].size
        full = tot[off:off + sz].reshape(dP[n].shape)
        off += sz
        if n in small_sharded:
            k = w_in[n].shape[-1]
            full = lax.dynamic_slice_in_dim(full, chip * k, k, axis=full.ndim - 1)
        grads[n] = full
    dmods_all = g2.reshape(N_DEV, -1)[:, :n_mod].reshape(N_DEV, DEPTH, 9 * D_MODEL)
    dm_sh = lax.dynamic_slice_in_dim(dmods_all, chip * n_ada, n_ada, axis=2)
    grads["ada_w"] = jnp.stack([matmul(sc, dm_sh[:, l], "tn", "ada_dw") for l in range(DEPTH)])

    w_in_g = [jnp.concatenate([dX["ssd_w_z"][j], dX["ssd_w_xbc"][j], dX["ssd_w_dt"][j]], axis=1) for j in range(2)]
    per_name = {
        "ffn_w_gu": [dX["ffn_w_gu"][i][t] for i in range(DEPTH) for t in range(2)],
        "ffn_w_down": [dX["ffn_w_down"][i][t].reshape(N_CHIPS, -1, D_MODEL) for i in range(DEPTH) for t in range(2)],
        "mla_w_a": [g.reshape(N_CHIPS, -1, g.shape[-1]) for g in dX["mla_w_a"]],
        "mla_w_qb": dX["mla_w_qb"], "mla_w_kvb": dX["mla_w_kvb"],
        "mla_w_o": [g.reshape(N_CHIPS, -1, D_MODEL) for g in dX["mla_w_o"]],
        "ssd_w_out": [g.reshape(N_CHIPS, -1, D_MODEL) for g in dX["ssd_w_out"]],
        "ssd_w_in": [_stack(g, 1) for g in w_in_g],
    }
    grad_groups = [jnp.concatenate([p for n in grp for p in per_name[n]], axis=1) for grp in GROUPS]
    theirs = pair_exchange(grad_groups, "grads_to_sibling")
    pair = [sum_own_half(a, b, ci, "sum_pair", BF16) for a, b in zip(grad_groups, theirs)]
    landed = scatter_chips(pair, "grads_to_chips")
    landed = [lax.dynamic_update_slice(a, lax.dynamic_slice_in_dim(p, chip, 1, axis=0), (chip, 0, 0)) for a, p in zip(landed, pair)]
    half_sums = [sum_slots(a, "sum_chips") for a in landed]
    totals = pair_allgather(half_sums, "grad_halves_swap")
    totals = [lax.dynamic_update_slice(t, h, (ci * h.shape[0], 0)) for t, h in zip(totals, half_sums)]
    for grp, arr in zip(GROUPS, totals):
        off = 0
        for n in grp:
            rows = w_in[n].size // w_in[n].shape[-1]
            grads[n] = arr[off:off + rows].reshape(w_in[n].shape)
            off += rows

    deltas, new_m, new_v = {}, {}, {}
    for n in names:
        w = w_in[n]
        d, nm, nv = adamw(_rows2d(w), _rows2d(m_in[n]), _rows2d(v_in[n]), _rows2d(grads[n]))
        deltas[n], new_m[n], new_v[n] = d.reshape(w.shape), nm.reshape(w.shape), nv.reshape(w.shape)

    return (loss, dx[None], *[grads[n] for n in names], *[deltas[n] for n in names],
            *[new_m[n] for n in names], *[new_v[n] for n in names])
```

```python
import numpy as np

import jax
import jax.numpy as jnp
from jax import lax
from jax.experimental import pallas as pl
from jax.experimental.pallas import tpu as pltpu

F32 = jnp.float32
BF16 = jnp.bfloat16
MESH = pl.DeviceIdType.MESH

D_MODEL = 1024
DEPTH = 4
EPS = 1e-6
D_FF = 2816
MLA_HEADS = 16
Q_LORA = 384
KV_LORA = 256
QK_NOPE = 64
QK_ROPE = 32
QK_HEAD = QK_NOPE + QK_ROPE
V_HEAD = 64
ROPE_THETA = 10000.0
D_INNER = 2048
SSD_HEAD_DIM = 64
SSD_HEADS = 32
SSD_GROUPS = 4
SSD_STATE = 128
CONV_WIDTH = 4
CHUNK = 128
CONV_DIM = D_INNER + 2 * SSD_GROUPS * SSD_STATE
ADAM_LR = 0.001
ADAM_B1 = 0.9
ADAM_B2 = 0.999
ADAM_EPS = 1e-08
ADAM_WD = 0.01
ADAM_STEP = 10

N_CHIPS = 4
N_DEV = 8
V7X_VMEM_LIMIT = 56 * 1024 * 1024
ATTN_HEADS_PER_STEP = 4
LANES = 128
LOG2E = 1.4426950408889634


def _params(sem=None):
    return pltpu.CompilerParams(dimension_semantics=sem, vmem_limit_bytes=V7X_VMEM_LIMIT)


def _div_tile(n, pref, quantum):
    if n <= pref:
        return n
    t = (pref // quantum) * quantum
    while t >= quantum:
        if n % t == 0:
            return t
        t -= quantum
    return n


def matmul(a, b, mode, name, out_dtype=F32, stack=0, tm=1024, tn=1024, tk=2816):
    if mode == "nn":
        (M, K), (K2, N) = a.shape, b.shape
    elif mode == "nt":
        (M, K), (N, K2) = a.shape, b.shape
    else:
        (K, M), (K2, N) = a.shape, b.shape
    assert K == K2, (a.shape, b.shape, mode)
    tm = _div_tile(M, tm, 128 if mode == "tn" else 16)
    tn = _div_tile(N // stack if stack else N, tn, 128)
    tk = _div_tile(K, tk, 128)
    nk = K // tk
    if mode == "nn":
        a_spec = pl.BlockSpec((tm, tk), lambda i, j, k: (i, k))
        b_spec = pl.BlockSpec((tk, tn), lambda i, j, k: (k, j))
        dims = (((1,), (0,)), ((), ()))
    elif mode == "nt":
        a_spec = pl.BlockSpec((tm, tk), lambda i, j, k: (i, k))
        b_spec = pl.BlockSpec((tn, tk), lambda i, j, k: (j, k))
        dims = (((1,), (1,)), ((), ()))
    else:
        a_spec = pl.BlockSpec((tk, tm), lambda i, j, k: (k, i))
        b_spec = pl.BlockSpec((tk, tn), lambda i, j, k: (k, j))
        dims = (((0,), (0,)), ((), ()))
    if stack:
        nb = N // stack // tn
        out_spec = pl.BlockSpec((None, tm, tn), lambda i, j, k: (j // nb, i, j % nb))
        out_shape = jax.ShapeDtypeStruct((stack, M, N // stack), out_dtype)
    else:
        out_spec = pl.BlockSpec((tm, tn), lambda i, j, k: (i, j))
        out_shape = jax.ShapeDtypeStruct((M, N), out_dtype)
    use_acc = nk > 1 and out_dtype != F32

    def body(a_ref, b_ref, o_ref, *acc):
        p = lax.dot_general(a_ref[...].astype(BF16), b_ref[...].astype(BF16), dims, preferred_element_type=F32)
        if nk == 1:
            o_ref[...] = p.astype(out_dtype)
            return
        acc_ref = acc[0] if use_acc else o_ref
        k = pl.program_id(2)

        @pl.when(k == 0)
        def _():
            acc_ref[...] = p

        @pl.when(k > 0)
        def _():
            acc_ref[...] += p

        if use_acc:
            @pl.when(k == nk - 1)
            def _():
                o_ref[...] = acc_ref[...].astype(out_dtype)

    return pl.pallas_call(
        body, name=name, grid=(M // tm, N // tn, nk), in_specs=[a_spec, b_spec], out_specs=out_spec, out_shape=out_shape,
        scratch_shapes=[pltpu.VMEM((tm, tn), F32)] if use_acc else [],
        compiler_params=_params(("parallel", "parallel", "arbitrary")),
    )(a, b)


def mm_op(name, out_dtype=F32, stack=0):
    @jax.custom_vjp
    def op(a, w, wp):
        return matmul(a, w, "nn", name + "_fwd", out_dtype=out_dtype)

    def fwd(a, w, wp):
        return op(a, w, wp), (a, w)

    def bwd(res, g):
        a, w = res
        return (matmul(g, w, "nt", name + "_dx", out_dtype=a.dtype), jnp.zeros_like(w),
                matmul(a, g, "tn", name + "_dw", stack=stack, tn=1408, tk=1024))

    op.defvjp(fwd, bwd)
    return op


def _row_tile(rows, widths):
    w = max(widths)
    t = 128 if w > 4096 else (256 if w > 1024 else 512)
    return _div_tile(rows, t, 16)


def _row_spec(r, tm):
    nb = r.shape[0] // tm
    return pl.BlockSpec((tm, r.shape[1]), lambda i: (i % nb, 0))


def _rowwise_fwd(f, rows, vecs, name, out_dtype):
    n_r, n_v = len(rows), len(vecs)
    S = rows[0].shape[0]
    tm = _row_tile(min(r.shape[0] for r in rows), [r.shape[1] for r in rows])
    outs = jax.eval_shape(f, *[jax.ShapeDtypeStruct((tm, r.shape[1]), F32) for r in rows], *vecs)

    def body(*refs):
        res = f(*[r[...].astype(F32) for r in refs[: n_r + n_v]])
        for o, r in zip(refs[n_r + n_v:], res):
            o[...] = r.astype(out_dtype)

    return pl.pallas_call(
        body, name=name, grid=(S // tm,),
        in_specs=[_row_spec(r, tm) for r in rows] + [pl.BlockSpec(v.shape, lambda i: (0, 0)) for v in vecs],
        out_specs=tuple(pl.BlockSpec((tm, o.shape[1]), lambda i: (i, 0)) for o in outs),
        out_shape=tuple(jax.ShapeDtypeStruct((S, o.shape[1]), out_dtype) for o in outs),
        compiler_params=_params(("parallel",)),
    )(*rows, *vecs)


def _rowwise_bwd(f, rows, vecs, douts, diff_rows, n_const, name):
    n_r, n_o = len(rows), len(douts)
    consts, vecs = vecs[len(vecs) - n_const:], vecs[:len(vecs) - n_const]
    n_v = len(vecs)
    S = rows[0].shape[0]
    tm = _row_tile(min(r.shape[0] for r in rows), [r.shape[1] for r in rows] + [d.shape[1] for d in douts])
    d_idx = [i for i in range(n_r) if diff_rows[i]]

    def body(*refs):
        row_v = [r[...].astype(F32) for r in refs[:n_r]]
        vec_v = [r[...] for r in refs[n_r:n_r + n_v]]
        const_v = [r[...] for r in refs[n_r + n_v:n_r + n_v + n_const]]
        dout_v = tuple(r[...].astype(F32) for r in refs[n_r + n_v + n_const:n_r + n_v + n_const + n_o])
        out_refs = refs[n_r + n_v + n_const + n_o:]

        def g(*args):
            full = list(row_v)
            for j, i in enumerate(d_idx):
                full[i] = args[j]
            return f(*full, *args[len(d_idx):], *const_v)

        _, vjp = jax.vjp(g, *[row_v[i] for i in d_idx], *vec_v)
        grads = vjp(dout_v)
        for j in range(len(d_idx)):
            out_refs[j][...] = grads[j].astype(out_refs[j].dtype)
        step = pl.program_id(0)
        for j in range(n_v):
            gv, o = grads[len(d_idx) + j], out_refs[len(d_idx) + j]

            @pl.when(step == 0)
            def _(gv=gv, o=o):
                o[...] = gv

            @pl.when(step > 0)
            def _(gv=gv, o=o):
                o[...] += gv

    res = pl.pallas_call(
        body, name=name, grid=(S // tm,),
        in_specs=[_row_spec(r, tm) for r in rows] + [pl.BlockSpec(v.shape, lambda i: (0, 0)) for v in vecs + consts]
        + [pl.BlockSpec((tm, d.shape[1]), lambda i: (i, 0)) for d in douts],
        out_specs=tuple([pl.BlockSpec((tm, rows[i].shape[1]), lambda i_: (i_, 0)) for i in d_idx]
                        + [pl.BlockSpec(v.shape, lambda i: (0, 0)) for v in vecs]),
        out_shape=tuple([jax.ShapeDtypeStruct(rows[i].shape, rows[i].dtype) for i in d_idx]
                        + [jax.ShapeDtypeStruct(v.shape, F32) for v in vecs]),
        compiler_params=_params(("arbitrary",)),
    )(*rows, *vecs, *consts, *douts)
    drows = [None] * n_r
    for j, i in enumerate(d_idx):
        drows[i] = res[j]
    for i in range(n_r):
        if drows[i] is None:
            drows[i] = jnp.zeros_like(rows[i])
    return tuple(drows) + tuple(res[len(d_idx):]) + tuple(jnp.zeros_like(k) for k in consts)


def rowwise_op(f, name, n_rows, diff_rows=None, out_dtype=F32, n_const=0):
    diff = tuple(diff_rows) if diff_rows is not None else (True,) * n_rows

    @jax.custom_vjp
    def op(*args):
        return _rowwise_fwd(f, args[:n_rows], args[n_rows:], name + "_fwd", out_dtype)

    def fwd(*args):
        return op(*args), args

    def bwd(args, douts):
        return _rowwise_bwd(f, args[:n_rows], args[n_rows:], douts, diff, n_const, name + "_bwd")

    op.defvjp(fwd, bwd)
    return op


def _rms(x, gain):
    return x * lax.rsqrt(jnp.mean(x * x, axis=-1, keepdims=True) + EPS) * gain


def _silu(x):
    return x * jax.nn.sigmoid(x)


def _f_modulate(x, gain, shift, scale):
    return (_rms(x, gain) * (1.0 + scale) + shift,)


def _f_rms(x, gain):
    return (_rms(x, gain),)


def _f_swiglu(gu):
    n = gu.shape[1] // 2
    return (_silu(gu[:, :n]) * gu[:, n:],)


def _f_resid(coef):
    def f(x, y, gate):
        return (x + (coef * gate) * y,)
    return f


def _f_head_rope(x, cosf, sinf, gain, swap):
    y = _rms(x, gain)
    return (y * cosf + jnp.dot(y, swap, precision=lax.Precision.HIGHEST, preferred_element_type=F32) * sinf,)


def _rope_swap():
    m = np.zeros((QK_HEAD, QK_HEAD), np.float32)
    half = QK_ROPE // 2
    for i in range(half):
        m[QK_NOPE + half + i, QK_NOPE + i] = -1.0
        m[QK_NOPE + i, QK_NOPE + half + i] = 1.0
    return jnp.asarray(m)


def _f_gated_norm(y, z, gain):
    g = y * _silu(z)
    n = g.shape[1] // SSD_GROUPS
    return (jnp.concatenate([_rms(g[:, i * n:(i + 1) * n], gain[:, i * n:(i + 1) * n]) for i in range(SSD_GROUPS)], axis=1),)


HALO = 8


def _conv_taps(ext, w, rows, off):
    acc = None
    for k in range(CONV_WIDTH):
        term = w[k:k + 1, :] * pltpu.roll(ext, CONV_WIDTH - 1 - k, 0)[off:off + rows]
        acc = term if acc is None else acc + term
    return acc


def _conv_specs(S, tm, tc):
    tile = pl.BlockSpec((tm, tc), lambda j, i: (i, j))
    prev = pl.BlockSpec((HALO, tc), lambda j, i: (jnp.maximum(i * (tm // HALO) - 1, 0), j))
    nxt = pl.BlockSpec((HALO, tc), lambda j, i: (jnp.minimum((i + 1) * (tm // HALO), S // HALO - 1), j))
    wspec = pl.BlockSpec((CONV_WIDTH, tc), lambda j, i: (0, j))
    bspec = pl.BlockSpec((1, tc), lambda j, i: (0, j))
    return tile, prev, nxt, wspec, bspec


def conv_silu_fwd(u, w, b, tm=256, tc=1024):
    S, C = u.shape

    def body(u_ref, p_ref, w_ref, b_ref, o_ref):
        prev = jnp.where(pl.program_id(1) == 0, 0.0, p_ref[...])
        ext = jnp.concatenate([prev, u_ref[...]], axis=0)
        conv = _conv_taps(ext, w_ref[...], tm, HALO) + b_ref[...]
        o_ref[...] = conv * jax.nn.sigmoid(conv)

    tile, prev, _, wspec, bspec = _conv_specs(S, tm, tc)
    return pl.pallas_call(
        body, name="conv_silu_fwd", grid=(C // tc, S // tm), in_specs=[tile, prev, wspec, bspec], out_specs=tile,
        out_shape=jax.ShapeDtypeStruct((S, C), F32), compiler_params=_params(("parallel", "arbitrary")),
    )(u, u, w, b)


def conv_silu_bwd(u, w, b, dout, tm=256, tc=1024):
    S, C = u.shape
    n = S // tm
    ext_rows = tm + HALO

    def body(u_ref, p_ref, n_ref, g_ref, gn_ref, w_ref, b_ref, du_ref, dw_ref, db_ref):
        i = pl.program_id(1)
        wv = w_ref[...]
        prev = jnp.where(i == 0, 0.0, p_ref[...])
        ext = jnp.concatenate([prev, u_ref[...], n_ref[...]], axis=0)
        conv = _conv_taps(ext, wv, ext_rows, HALO) + b_ref[...]
        g_ext = jnp.concatenate([g_ref[...], jnp.where(i == n - 1, 0.0, gn_ref[...])], axis=0)
        sg = jax.nn.sigmoid(conv)
        dconv = g_ext * (sg * (1.0 + conv * (1.0 - sg)))
        du = None
        for k in range(CONV_WIDTH):
            s = CONV_WIDTH - 1 - k
            term = wv[k:k + 1, :] * pltpu.roll(dconv, (ext_rows - s) % ext_rows, 0)[:tm]
            du = term if du is None else du + term
        du_ref[...] = du
        dc = dconv[:tm]
        dw = jnp.concatenate([jnp.sum(dc * pltpu.roll(ext, CONV_WIDTH - 1 - k, 0)[HALO:HALO + tm], axis=0, keepdims=True)
                              for k in range(CONV_WIDTH)], axis=0)
        dbv = jnp.sum(dc, axis=0, keepdims=True)

        @pl.when(i == 0)
        def _():
            dw_ref[...] = dw
            db_ref[...] = dbv

        @pl.when(i > 0)
        def _():
            dw_ref[...] += dw
            db_ref[...] += dbv

    tile, prev, nxt, wspec, bspec = _conv_specs(S, tm, tc)
    return pl.pallas_call(
        body, name="conv_silu_bwd", grid=(C // tc, n), in_specs=[tile, prev, nxt, tile, nxt, wspec, bspec],
        out_specs=(tile, wspec, bspec),
        out_shape=(jax.ShapeDtypeStruct((S, C), F32), jax.ShapeDtypeStruct((CONV_WIDTH, C), F32),
                   jax.ShapeDtypeStruct((1, C), F32)),
        compiler_params=_params(("parallel", "arbitrary")),
    )(u, u, u, dout, dout, w, b)


@jax.custom_vjp
def conv_silu(u, w, b):
    return conv_silu_fwd(u, w, b)


def _conv_silu_fwd(u, w, b):
    return conv_silu_fwd(u, w, b), (u, w, b)


def _conv_silu_bwd(res, dout):
    return conv_silu_bwd(*res, dout)


conv_silu.defvjp(_conv_silu_fwd, _conv_silu_bwd)


_NT = (((1,), (1,)), ((), ()))


def _dot(a, b):
    return jnp.dot(a.astype(BF16), b.astype(BF16), preferred_element_type=F32)


def _dot_nt(a, b):
    return lax.dot_general(a.astype(BF16), b.astype(BF16), _NT, preferred_element_type=F32)


def _attn_tile(S):
    return _div_tile(S, 512, 128)


def _causal(t, transposed=False):
    r = lax.broadcasted_iota(jnp.int32, (t, t), 0)
    c = lax.broadcasted_iota(jnp.int32, (t, t), 1)
    return r <= c if transposed else r >= c


def _tri_tables(n, by_key):
    if by_key:
        pairs = [(i, j) for j in range(n) for i in range(j, n)]
    else:
        pairs = [(i, j) for i in range(n) for j in range(i + 1)]
    return (jnp.asarray(np.array([p[0] for p in pairs], np.int32)), jnp.asarray(np.array([p[1] for p in pairs], np.int32)))


def attn_fwd(q, k, v):
    H, S, dk = q.shape
    dv = v.shape[-1]
    t, hb = _attn_tile(S), ATTN_HEADS_PER_STEP
    n = S // t
    scale = dk ** -0.5
    qi_tab, kj_tab = _tri_tables(n, by_key=False)

    def body(qi_ref, kj_ref, q_ref, k_ref, v_ref, o_ref, lse_ref, m_s, l_s, acc_s):
        qi, kj = qi_ref[pl.program_id(1)], kj_ref[pl.program_id(1)]

        @pl.when(kj == 0)
        def _():
            m_s[...] = jnp.full(m_s.shape, -jnp.inf, F32)
            l_s[...] = jnp.zeros(l_s.shape, F32)
            acc_s[...] = jnp.zeros(acc_s.shape, F32)

        def step(masked):
            ss = [_dot_nt(q_ref[j], k_ref[j]) for j in range(hb)]
            new = []
            for j in range(hb):
                s = ss[j] * (scale * LOG2E)
                if masked:
                    s = jnp.where(_causal(t), s, -jnp.inf)
                m_old = m_s[j]
                m_new = jnp.maximum(m_old, jnp.max(s, axis=-1, keepdims=True))
                alpha = jnp.exp2(m_old - m_new)
                p = jnp.exp2(s - jnp.tile(m_new, (1, t // LANES)))
                new.append((m_new, alpha * l_s[j] + jnp.sum(p, axis=-1, keepdims=True),
                            alpha[:, :dv] * acc_s[j] + _dot(p, v_ref[j])))
            for j in range(hb):
                m_s[j], l_s[j], acc_s[j] = new[j]

        @pl.when(kj < qi)
        def _():
            step(False)

        @pl.when(kj == qi)
        def _():
            step(True)
            l = l_s[...]
            o_ref[...] = acc_s[...] / l[:, :, :dv]
            lse_ref[...] = m_s[...] + jnp.log2(l)

    qmap = lambda h, s, qi, kj: (h, qi[s], 0)
    kmap = lambda h, s, qi, kj: (h, kj[s], 0)
    return pl.pallas_call(
        body, name="attn_fwd",
        grid_spec=pltpu.PrefetchScalarGridSpec(
            num_scalar_prefetch=2, grid=(H // hb, qi_tab.shape[0]),
            in_specs=[pl.BlockSpec((hb, t, dk), qmap), pl.BlockSpec((hb, t, dk), kmap), pl.BlockSpec((hb, t, dv), kmap)],
            out_specs=(pl.BlockSpec((hb, t, dv), qmap), pl.BlockSpec((hb, t, LANES), qmap)),
            scratch_shapes=[pltpu.VMEM((hb, t, LANES), F32), pltpu.VMEM((hb, t, LANES), F32), pltpu.VMEM((hb, t, dv), F32)]),
        out_shape=(jax.ShapeDtypeStruct((H, S, dv), F32), jax.ShapeDtypeStruct((H, S, LANES), F32)),
        compiler_params=_params(("parallel", "arbitrary")),
    )(qi_tab, kj_tab, q, k, v)


def attn_bwd(q, k, v, o, lse, do):
    H, S, dk = q.shape
    dv = v.shape[-1]
    t, hb = _attn_tile(S), ATTN_HEADS_PER_STEP
    n = S // t
    scale = dk ** -0.5
    qi_tab, kj_tab = _tri_tables(n, by_key=False)

    def dq_body(qi_ref, kj_ref, q_ref, k_ref, v_ref, o_ref, do_ref, lse_ref, dq_ref, delta_ref):
        qi, kj = qi_ref[pl.program_id(1)], kj_ref[pl.program_id(1)]

        @pl.when(kj == 0)
        def _():
            d = jnp.sum(do_ref[...] * o_ref[...], axis=-1, keepdims=True)
            delta_ref[...] = jnp.broadcast_to(d, delta_ref.shape)
            dq_ref[...] = jnp.zeros((hb, t, dk), F32)

        def step(masked):
            ss = [_dot_nt(q_ref[j], k_ref[j]) for j in range(hb)]
            dps = [_dot_nt(do_ref[j], v_ref[j]) for j in range(hb)]
            for j in range(hb):
                p = jnp.exp2(ss[j] * (scale * LOG2E) - jnp.tile(lse_ref[j], (1, t // LANES)))
                if masked:
                    p = jnp.where(_causal(t), p, 0.0)
                ds = p * (dps[j] - jnp.tile(delta_ref[j], (1, t // LANES)))
                dq_ref[j] += _dot(ds, k_ref[j]) * scale

        @pl.when(kj < qi)
        def _():
            step(False)

        @pl.when(kj == qi)
        def _():
            step(True)

    qmap = lambda h, s, qi, kj: (h, qi[s], 0)
    kmap = lambda h, s, qi, kj: (h, kj[s], 0)
    dq, delta = pl.pallas_call(
        dq_body, name="attn_bwd_dq",
        grid_spec=pltpu.PrefetchScalarGridSpec(
            num_scalar_prefetch=2, grid=(H // hb, qi_tab.shape[0]),
            in_specs=[pl.BlockSpec((hb, t, dk), qmap), pl.BlockSpec((hb, t, dk), kmap), pl.BlockSpec((hb, t, dv), kmap),
                      pl.BlockSpec((hb, t, dv), qmap), pl.BlockSpec((hb, t, dv), qmap), pl.BlockSpec((hb, t, LANES), qmap)],
            out_specs=(pl.BlockSpec((hb, t, dk), qmap), pl.BlockSpec((hb, t, LANES), qmap))),
        out_shape=(jax.ShapeDtypeStruct((H, S, dk), F32), jax.ShapeDtypeStruct((H, S, LANES), F32)),
        compiler_params=_params(("parallel", "arbitrary")),
    )(qi_tab, kj_tab, q, k, v, o, do, lse)

    lse_r = lse[:, :, 0].reshape(H, 1, S)
    delta_r = delta[:, :, 0].reshape(H, 1, S)
    do_b = do.astype(BF16)
    qi_tab2, kj_tab2 = _tri_tables(n, by_key=True)

    def dkv_body(qi_ref, kj_ref, q_ref, k_ref, v_ref, do_ref, lse_ref, delta_ref, dk_ref, dv_ref):
        qi, kj = qi_ref[pl.program_id(1)], kj_ref[pl.program_id(1)]

        def step(masked):
            sts = [_dot_nt(k_ref[j], q_ref[j]) for j in range(hb)]
            dpts = [_dot_nt(v_ref[j], do_ref[j]) for j in range(hb)]
            for j in range(hb):
                pt = jnp.exp2(sts[j] * (scale * LOG2E) - lse_ref[j])
                if masked:
                    pt = jnp.where(_causal(t, transposed=True), pt, 0.0)
                dvj = _dot(pt, do_ref[j])
                dst = pt * (dpts[j] - delta_ref[j])
                dkj = _dot(dst, q_ref[j]) * scale
                if masked:
                    dv_ref[j] = dvj
                    dk_ref[j] = dkj
                else:
                    dv_ref[j] += dvj
                    dk_ref[j] += dkj

        @pl.when(qi == kj)
        def _():
            step(True)

        @pl.when(qi > kj)
        def _():
            step(False)

    qmap2 = lambda h, s, qi, kj: (h, qi[s], 0)
    kmap2 = lambda h, s, qi, kj: (h, kj[s], 0)
    rowq = lambda h, s, qi, kj: (h, 0, qi[s])
    dk_, dv_ = pl.pallas_call(
        dkv_body, name="attn_bwd_dkv",
        grid_spec=pltpu.PrefetchScalarGridSpec(
            num_scalar_prefetch=2, grid=(H // hb, qi_tab2.shape[0]),
            in_specs=[pl.BlockSpec((hb, t, dk), qmap2), pl.BlockSpec((hb, t, dk), kmap2), pl.BlockSpec((hb, t, dv), kmap2),
                      pl.BlockSpec((hb, t, dv), qmap2), pl.BlockSpec((hb, 1, t), rowq), pl.BlockSpec((hb, 1, t), rowq)],
            out_specs=(pl.BlockSpec((hb, t, dk), kmap2), pl.BlockSpec((hb, t, dv), kmap2))),
        out_shape=(jax.ShapeDtypeStruct((H, S, dk), F32), jax.ShapeDtypeStruct((H, S, dv), F32)),
        compiler_params=_params(("parallel", "arbitrary")),
    )(qi_tab2, kj_tab2, q, k, v, do_b, lse_r, delta_r)
    return dq, dk_, dv_


@jax.custom_vjp
def attention(q, k, v):
    return attn_fwd(q.astype(BF16), k.astype(BF16), v.astype(BF16))[0]


def _attention_fwd(q, k, v):
    qb, kb, vb = q.astype(BF16), k.astype(BF16), v.astype(BF16)
    o, lse = attn_fwd(qb, kb, vb)
    return o, (qb, kb, vb, o, lse)


def _attention_bwd(res, do):
    return attn_bwd(*res, do)


attention.defvjp(_attention_fwd, _attention_bwd)


def _ssd_specs(hb, L, P, N, order):
    xs = pl.BlockSpec((hb, L, P), lambda g, c: (g, order(c), 0))
    col = pl.BlockSpec((None, L, hb), lambda g, c: (g, order(c), 0))
    row = pl.BlockSpec((hb, 1, L), lambda g, c: (g, 0, order(c)))
    bc = pl.BlockSpec((None, L, N), lambda g, c: (g, order(c), 0))
    st = pl.BlockSpec((hb, None, N, P), lambda g, c: (g, order(c), 0, 0))
    return xs, col, row, bc, st


def _ssd_rows(cols, H):
    G, S, hb = cols.shape
    return cols.transpose(0, 2, 1).reshape(H, 1, S)


def ssd_fwd(x, dt, ac, Bm, Cm):
    H, S, P = x.shape
    G, _, N = Bm.shape
    hb, L = H // G, CHUNK
    nc = S // L
    acr = _ssd_rows(ac, H)

    def body(x_ref, dt_ref, ac_ref, acr_ref, b_ref, c_ref, y_ref, hp_ref, h_s):
        @pl.when(pl.program_id(1) == 0)
        def _():
            h_s[...] = jnp.zeros((hb, N, P), F32)

        Bv, Cv = b_ref[...], c_ref[...]
        cb = _dot_nt(Cv, Bv)
        bt = Bv.T
        mask = _causal(L)
        ac_all, dt_all = ac_ref[...], dt_ref[...]
        for j in range(hb):
            a = jnp.broadcast_to(ac_all[:, j:j + 1], (L, L))
            dtv = jnp.broadcast_to(dt_all[:, j:j + 1], (L, L))
            lm = jnp.exp(jnp.where(mask, a - acr_ref[j], -jnp.inf))
            xdt = x_ref[j] * dtv[:, :P]
            h = h_s[j]
            hp_ref[j] = h
            y_ref[j] = _dot(cb * lm, xdt) + jnp.exp(a)[:, :P] * _dot(Cv, h)
            al = a[L - 1:L, :]
            h_s[j] = jnp.exp(al)[:, :P] * h + _dot(bt, xdt * jnp.exp(al - a)[:, :P])

    xs, col, row, bc, st = _ssd_specs(hb, L, P, N, lambda c: c)
    return pl.pallas_call(
        body, name="ssd_fwd", grid=(G, nc), in_specs=[xs, col, col, row, bc, bc], out_specs=(xs, st),
        out_shape=(jax.ShapeDtypeStruct((H, S, P), F32), jax.ShapeDtypeStruct((H, nc, N, P), F32)),
        scratch_shapes=[pltpu.VMEM((hb, N, P), F32)],
        compiler_params=_params(("parallel", "arbitrary")),
    )(x, dt, ac, acr, Bm, Cm)


def ssd_bwd(x, dt, ac, Bm, Cm, hp, dy):
    H, S, P = x.shape
    G, _, N = Bm.shape
    hb, L = H // G, CHUNK
    nc = S // L
    acr = _ssd_rows(ac, H)

    def body(x_ref, dt_ref, ac_ref, acr_ref, b_ref, c_ref, hp_ref, dy_ref,
             dx_ref, ddt_ref, dac_ref, dacr_ref, db_ref, dc_ref, dh_s):
        @pl.when(pl.program_id(1) == 0)
        def _():
            dh_s[...] = jnp.zeros((hb, N, P), F32)

        Bv, Cv = b_ref[...], c_ref[...]
        cb = _dot_nt(Cv, Bv)
        cbt = _dot_nt(Bv, Cv)
        ct = Cv.T
        mask, maskt = _causal(L), _causal(L, transposed=True)
        last = lax.broadcasted_iota(jnp.int32, (L, 1), 0) == L - 1
        lane = lax.broadcasted_iota(jnp.int32, (L, hb), 1)
        db = jnp.zeros((L, N), F32)
        dc = jnp.zeros((L, N), F32)
        dac_all = jnp.zeros((L, hb), F32)
        ddt_all = jnp.zeros((L, hb), F32)
        ac_all, dt_all = ac_ref[...], dt_ref[...]
        for j in range(hb):
            ar, xv, g, h, dh = acr_ref[j], x_ref[j], dy_ref[j], hp_ref[j], dh_s[j]
            a = jnp.broadcast_to(ac_all[:, j:j + 1], (L, L))
            dtv = jnp.broadcast_to(dt_all[:, j:j + 1], (L, L))
            lm = jnp.exp(jnp.where(mask, a - ar, -jnp.inf))
            lmt = jnp.exp(jnp.where(maskt, ar - a, -jnp.inf))
            xdt = xv * dtv[:, :P]
            e = jnp.exp(a)
            al = a[L - 1:L, :]
            dte = jnp.exp(al - a)
            el = jnp.exp(al)
            dcb = _dot_nt(g, xdt) * lm
            dcbt = _dot_nt(xdt, g) * lmt
            dseg = dcb * cb
            ch = _dot(Cv, h)
            bdh = _dot(Bv, dh)
            dxdt = _dot(cbt * lmt, g) + dte[:, :P] * bdh
            dc += _dot(dcb, Bv) + e * _dot_nt(g, h)
            db += _dot(dcbt, Cv) + _dot_nt(xdt * dte[:, :P], dh)
            d_e = jnp.sum(g * ch, axis=-1, keepdims=True)
            d_dte = jnp.sum(xdt * bdh, axis=-1, keepdims=True)
            d_el = jnp.sum(h * dh, keepdims=True)
            d_al = jnp.sum(d_dte * dte[:, :1], keepdims=True) + d_el * el[:, :1]
            dac_j = (jnp.sum(dseg, axis=-1, keepdims=True) + d_e * e[:, :1] - d_dte * dte[:, :1]
                     + jnp.where(last, d_al, 0.0))
            dac_all = jnp.where(lane == j, dac_j, dac_all)
            dacr_ref[j] = -jnp.sum(dseg, axis=0, keepdims=True)
            dx_ref[j] = dxdt * dtv[:, :P]
            ddt_all = jnp.where(lane == j, jnp.sum(dxdt * xv, axis=-1, keepdims=True), ddt_all)
            dh_s[j] = el[:, :P] * dh + _dot(ct, e[:, :P] * g)
        dac_ref[...] = dac_all
        ddt_ref[...] = ddt_all
        db_ref[...] = db
        dc_ref[...] = dc

    xs, col, row, bc, st = _ssd_specs(hb, L, P, N, lambda c: nc - 1 - c)
    dx, ddt, dac, dacr, db, dc = pl.pallas_call(
        body, name="ssd_bwd", grid=(G, nc), in_specs=[xs, col, col, row, bc, bc, st, xs],
        out_specs=(xs, col, col, row, bc, bc),
        out_shape=(jax.ShapeDtypeStruct((H, S, P), F32), jax.ShapeDtypeStruct((G, S, hb), F32),
                   jax.ShapeDtypeStruct((G, S, hb), F32), jax.ShapeDtypeStruct((H, 1, S), F32),
                   jax.ShapeDtypeStruct((G, S, N), F32), jax.ShapeDtypeStruct((G, S, N), F32)),
        scratch_shapes=[pltpu.VMEM((hb, N, P), F32)],
        compiler_params=_params(("parallel", "arbitrary")),
    )(x, dt, ac, acr, Bm, Cm, hp, dy)
    return dx, ddt, dac + dacr.reshape(G, hb, S).transpose(0, 2, 1), db, dc


@jax.custom_vjp
def ssd_scan(x, dt, ac, Bm, Cm):
    return ssd_fwd(x, dt, ac, Bm, Cm)[0]


def _ssd_scan_fwd(x, dt, ac, Bm, Cm):
    y, hp = ssd_fwd(x, dt, ac, Bm, Cm)
    return y, (x, dt, ac, Bm, Cm, hp)


def _ssd_scan_bwd(res, dy):
    return ssd_bwd(*res, dy)


ssd_scan.defvjp(_ssd_scan_fwd, _ssd_scan_bwd)


def _vec(v):
    return v.reshape(1, -1)


def _ffn(x, gain, m3, w_gu, w_down, x_gu, x_down):
    h, = rowwise_op(_f_modulate, "modulate", 1, out_dtype=BF16)(x, _vec(gain), _vec(m3[0]), _vec(m3[1]))
    gu = mm_op("ffn_gu", out_dtype=BF16, stack=N_CHIPS)(h, w_gu, x_gu)
    a, = rowwise_op(_f_swiglu, "swiglu", 1, out_dtype=BF16)(gu)
    y = mm_op("ffn_down")(a, w_down, x_down)
    return rowwise_op(_f_resid(0.5), "resid_half", 2)(x, y, _vec(m3[2]))[0]


def _rope_tables(positions):
    inv = 1.0 / (ROPE_THETA ** (jnp.arange(0, QK_ROPE, 2, dtype=F32) / QK_ROPE))
    ang = positions.astype(F32)[:, None] * inv
    S = positions.shape[0]
    cosf = jnp.concatenate([jnp.ones((S, QK_NOPE), F32), jnp.cos(ang), jnp.cos(ang)], axis=1)
    sinf = jnp.concatenate([jnp.zeros((S, QK_NOPE), F32), jnp.sin(ang), jnp.sin(ang)], axis=1)
    return cosf, sinf


def _heads_first(t):
    return t.transpose(1, 0, 2).reshape(-1, t.shape[-1])


def _mla(h, cos, sin, P, W, X, j):
    S = h.shape[0]
    lat = mm_op("mla_a")(h, W["mla_w_a"][j], X["mla_w_a"][j])
    q_lat, kv_lat, k_rope = lat[:, :Q_LORA], lat[:, Q_LORA:Q_LORA + KV_LORA], lat[:, Q_LORA + KV_LORA:]
    qn, = rowwise_op(_f_rms, "rms_lat", 1, out_dtype=BF16)(q_lat, _vec(P["mla_q_a_gain"][j]))
    kvn, = rowwise_op(_f_rms, "rms_lat", 1, out_dtype=BF16)(kv_lat, _vec(P["mla_kv_a_gain"][j]))
    q = mm_op("mla_qb", stack=N_CHIPS)(qn, W["mla_w_qb"][j], X["mla_w_qb"][j]).reshape(S, MLA_HEADS, QK_HEAD)
    kv = mm_op("mla_kvb", stack=N_CHIPS)(kvn, W["mla_w_kvb"][j], X["mla_w_kvb"][j]).reshape(S, MLA_HEADS, QK_NOPE + V_HEAD)
    k_nope, v = kv[..., :QK_NOPE], kv[..., QK_NOPE:]
    k = jnp.concatenate([k_nope, jnp.broadcast_to(k_rope[:, None, :], (S, MLA_HEADS, QK_ROPE))], axis=-1)
    head_rope = rowwise_op(_f_head_rope, "head_rope", 3, diff_rows=(True, False, False), n_const=1)
    swap = _rope_swap()
    q, = head_rope(_heads_first(q), cos, sin, _vec(P["mla_q_gain"][j]), swap)
    k, = head_rope(_heads_first(k), cos, sin, _vec(P["mla_k_gain"][j]), swap)
    o = attention(q.reshape(MLA_HEADS, S, QK_HEAD), k.reshape(MLA_HEADS, S, QK_HEAD), v.transpose(1, 0, 2))
    o = o.transpose(1, 0, 2).reshape(S, MLA_HEADS * V_HEAD).astype(BF16)
    return mm_op("mla_o")(o, W["mla_w_o"][j], X["mla_w_o"][j])


def _ssd(h, P, W, X, j):
    S = h.shape[0]
    z = mm_op("ssd_in_z")(h, W["ssd_w_z"][j], X["ssd_w_z"][j])
    xbc = mm_op("ssd_in_xbc")(h, W["ssd_w_xbc"][j], X["ssd_w_xbc"][j])
    dtr = mm_op("ssd_in_dt")(h, W["ssd_w_dt"][j], X["ssd_w_dt"][j])
    xbc = conv_silu(xbc, P["ssd_conv_w"][j], _vec(P["ssd_conv_b"][j]))
    xs =xbc[:, :D_INNER].reshape(S, SSD_HEADS, SSD_HEAD_DIM).transpose(1, 0, 2)
    Bm = xbc[:, D_INNER:D_INNER + SSD_GROUPS * SSD_STATE].reshape(S, SSD_GROUPS, SSD_STATE).transpose(1, 0, 2)
    Cm = xbc[:, D_INNER + SSD_GROUPS * SSD_STATE:].reshape(S, SSD_GROUPS, SSD_STATE).transpose(1, 0, 2)
    dt = jax.nn.softplus(dtr + P["ssd_dt_bias"][j][None, :])
    A = -jnp.exp(P["ssd_a_log"][j])
    a = (dt * A[None, :]).reshape(S // CHUNK, CHUNK, SSD_HEADS)
    ac = jnp.cumsum(a, axis=1).reshape(S, SSD_HEADS)
    by_group = lambda t: t.reshape(S, SSD_GROUPS, SSD_HEADS // SSD_GROUPS).transpose(1, 0, 2)
    y = ssd_scan(xs, by_group(dt), by_group(ac), Bm, Cm)
    y = y + P["ssd_d"][j][:, None, None] * xs
    y = y.transpose(1, 0, 2).reshape(S, D_INNER)
    g, = rowwise_op(_f_gated_norm, "gated_norm", 2, out_dtype=BF16)(y, z, _vec(P["ssd_norm_gain"][j]))
    return mm_op("ssd_out")(g, W["ssd_w_out"][j], X["ssd_w_out"][j])


def trunk(x, mods, P, W, X, positions):
    cos, sin = _rope_tables(positions)
    for i in range(DEPTH):
        m, j = mods[i], i // 2
        x = _ffn(x, P["norm_gain"][i, 0], m[0], W["ffn_w_gu"][i][0], W["ffn_w_down"][i][0],
                 X["ffn_w_gu"][i][0], X["ffn_w_down"][i][0])
        h, = rowwise_op(_f_modulate, "modulate", 1, out_dtype=BF16)(x, _vec(P["norm_gain"][i, 1]), _vec(m[1, 0]), _vec(m[1, 1]))
        y = _mla(h, cos, sin, P, W, X, j) if i % 2 == 0 else _ssd(h, P, W, X, j)
        x, = rowwise_op(_f_resid(1.0), "resid_full", 2)(x, y, _vec(m[1, 2]))
        x = _ffn(x, P["norm_gain"][i, 2], m[2], W["ffn_w_gu"][i][1], W["ffn_w_down"][i][1],
                 X["ffn_w_gu"][i][1], X["ffn_w_down"][i][1])
    return x


def stand_ins(W):
    def one(name, w):
        if name in ("ffn_w_gu", "mla_w_qb", "mla_w_kvb"):
            return jnp.zeros((N_CHIPS, w.shape[0], w.shape[1] // N_CHIPS), F32)
        return jnp.zeros(w.shape, F32)
    return {n: jax.tree.map(lambda w, n=n: one(n, w), W[n]) for n in W}


def loss_head(y, target):
    S, D = y.shape
    tm = _row_tile(S, [D])

    def body(y_ref, t_ref, dy_ref, l_ref):
        d = y_ref[...] - t_ref[...]
        dy_ref[...] = d * (1.0 / D)
        part = jnp.sum(d * d, axis=0, keepdims=True) * (0.5 / D)

        @pl.when(pl.program_id(0) == 0)
        def _():
            l_ref[...] = part

        @pl.when(pl.program_id(0) > 0)
        def _():
            l_ref[...] += part

    return pl.pallas_call(
        body, name="loss_head", grid=(S // tm,),
        in_specs=[pl.BlockSpec((tm, D), lambda i: (i, 0))] * 2,
        out_specs=(pl.BlockSpec((tm, D), lambda i: (i, 0)), pl.BlockSpec((1, D), lambda i: (0, 0))),
        out_shape=(jax.ShapeDtypeStruct((S, D), F32), jax.ShapeDtypeStruct((1, D), F32)),
        compiler_params=_params(("arbitrary",)),
    )(y, target)


def _stream_rows(R, C):
    return _div_tile(R, max(16, (1 << 19) // C // 16 * 16), 16)


def adamw(w, m, v, g):
    R, C = w.shape
    tr = _stream_rows(R, C)
    c1 = 1.0 / (1.0 - ADAM_B1 ** ADAM_STEP)
    c2 = 1.0 / (1.0 - ADAM_B2 ** ADAM_STEP)

    def body(w_ref, m_ref, v_ref, g_ref, d_ref, nm_ref, nv_ref):
        gv = g_ref[...]
        nm = ADAM_B1 * m_ref[...] + (1.0 - ADAM_B1) * gv
        nv = ADAM_B2 * v_ref[...] + (1.0 - ADAM_B2) * (gv * gv)
        d_ref[...] = -ADAM_LR * ((nm * c1) / (jnp.sqrt(nv * c2) + ADAM_EPS) + ADAM_WD * w_ref[...])
        nm_ref[...] = nm
        nv_ref[...] = nv

    spec = pl.BlockSpec((tr, C), lambda i: (i, 0))
    return pl.pallas_call(
        body, name="adamw", grid=(R // tr,), in_specs=[spec] * 4, out_specs=(spec,) * 3,
        out_shape=(jax.ShapeDtypeStruct((R, C), F32),) * 3, compiler_params=_params(("parallel",)),
    )(w, m, v, g)


def sum_parts(parts, name, out_dtype=F32):
    R, C = parts[0].shape
    tr = _stream_rows(R, C)
    n = len(parts)

    def body(*refs):
        acc = refs[0][...].astype(F32)
        for r in refs[1:n]:
            acc = acc + r[...].astype(F32)
        refs[n][...] = acc.astype(out_dtype)

    spec = pl.BlockSpec((tr, C), lambda i: (i, 0))
    return pl.pallas_call(
        body, name=name, grid=(R // tr,), in_specs=[spec] * n, out_specs=spec,
        out_shape=jax.ShapeDtypeStruct((R, C), out_dtype), compiler_params=_params(("parallel",)),
    )(*parts)


def sum_own_half(full, theirs, c, name, out_dtype):
    n, R, C = full.shape
    h = R // 2
    tr = _stream_rows(h, C)
    nb = h // tr

    def body(c_ref, a_ref, b_ref, o_ref):
        o_ref[...] = (a_ref[...].astype(F32) + b_ref[...].astype(F32)).astype(out_dtype)

    return pl.pallas_call(
        body, name=name,
        grid_spec=pltpu.PrefetchScalarGridSpec(
            num_scalar_prefetch=1, grid=(n, nb),
            in_specs=[pl.BlockSpec((None, tr, C), lambda p, i, cr: (p, cr[0] * nb + i, 0)),
                      pl.BlockSpec((None, tr, C), lambda p, i, cr: (p, i, 0))],
            out_specs=pl.BlockSpec((None, tr, C), lambda p, i, cr: (p, i, 0))),
        out_shape=jax.ShapeDtypeStruct((n, h, C), out_dtype), compiler_params=_params(("parallel", "parallel")),
    )(jnp.reshape(c, (1,)).astype(jnp.int32), full, theirs)


def sum_slots(a, name, out_dtype=F32):
    n, R, C = a.shape
    tr = _stream_rows(R, C)

    def body(*refs):
        acc = refs[0][...].astype(F32)
        for r in refs[1:n]:
            acc = acc + r[...].astype(F32)
        refs[n][...] = acc.astype(out_dtype)

    return pl.pallas_call(
        body, name=name, grid=(R // tr,),
        in_specs=[pl.BlockSpec((None, tr, C), lambda i, p=p: (p, i, 0)) for p in range(n)],
        out_specs=pl.BlockSpec((tr, C), lambda i: (i, 0)),
        out_shape=jax.ShapeDtypeStruct((R, C), out_dtype), compiler_params=_params(("parallel",)),
    )(*([a] * n))


def _coords():
    return lax.axis_index("x"), lax.axis_index("y"), lax.axis_index("c")


def _other_chips(x, y):
    return [(1 - x, y), (x, 1 - y), (1 - x, 1 - y)]


def _hbm_call(body, name, ins, out_shapes, n_sems):
    return pl.pallas_call(
        body, name=name, out_shape=tuple(out_shapes),
        in_specs=[pl.BlockSpec(memory_space=pl.ANY)] * len(ins),
        out_specs=tuple(pl.BlockSpec(memory_space=pl.ANY) for _ in out_shapes),
        scratch_shapes=[pltpu.SemaphoreType.DMA((n_sems,)), pltpu.SemaphoreType.DMA((n_sems,))],
    )(*ins)


def allgather_small(v, name):
    m_per, n = v.shape

    def body(x_ref, out_ref, send_sems, recv_sems, local_sem):
        x, y, c = _coords()
        me, sibling = (x, y, c), (x, y, 1 - c)
        chips = _other_chips(x, y)

        def rows(px, py, pc):
            return out_ref.at[pl.ds((4 * px + 2 * py + pc) * m_per, m_per), :]

        def copy(k, block, to, src=None):
            return pltpu.make_async_remote_copy(
                src_ref=rows(*block) if src is None else src, dst_ref=rows(*block),
                send_sem=send_sems.at[k], recv_sem=recv_sems.at[k], device_id=to, device_id_type=MESH)

        mine = pltpu.make_async_copy(x_ref, rows(*me), local_sem)
        mine.start()
        first = [copy(0, me, sibling, src=x_ref)]
        first += [copy(1 + j, me, (*chip, c), src=x_ref) for j, chip in enumerate(chips)]
        for cp in first:
            cp.start()
        passed = [copy(4 + j, (*chip, c), sibling) for j, chip in enumerate(chips)]
        for j, chip in enumerate(chips):
            copy(1 + j, (*chip, c), me).wait_recv()
            passed[j].start()
        copy(0, sibling, me).wait_recv()
        for j, chip in enumerate(chips):
            copy(4 + j, (*chip, 1 - c), me).wait_recv()
        for cp in first + passed:
            cp.wait_send()
        mine.wait()

    return pl.pallas_call(
        body, name=name, out_shape=jax.ShapeDtypeStruct((N_DEV * m_per, n), v.dtype),
        in_specs=[pl.BlockSpec(memory_space=pltpu.VMEM)], out_specs=pl.BlockSpec(memory_space=pltpu.VMEM),
        scratch_shapes=[pltpu.SemaphoreType.DMA((7,)), pltpu.SemaphoreType.DMA((7,)), pltpu.SemaphoreType.DMA],
        compiler_params=pltpu.CompilerParams(vmem_limit_bytes=V7X_VMEM_LIMIT),
    )(v)


def allgather_chips(arrs, name):
    n = len(arrs)
    halves = [a.shape[0] // 2 for a in arrs]

    def body(*refs):
        xs, outs = refs[:n], refs[n:2 * n]
        send_sems, recv_sems = refs[2 * n:]
        x, y, c = _coords()
        me, sibling, chips = 2 * x + y, (x, y, 1 - c), _other_chips(x, y)

        def half(ref, cc, i):
            return ref.at[pl.ds(cc * halves[i], halves[i]), :]

        def copy(i, k, src, dst, to):
            return pltpu.make_async_remote_copy(src_ref=src, dst_ref=dst, send_sem=send_sems.at[6 * i + k],
                                                recv_sem=recv_sems.at[6 * i + k], device_id=to, device_id_type=MESH)

        sends = [copy(i, k, half(xs[i], c, i), half(outs[i].at[me], c, i), (*chip, c))
                 for k, chip in enumerate(chips) for i in range(n)]
        for cp in sends:
            cp.start()
        passed = []
        for k, (px, py) in enumerate(chips):
            for i in range(n):
                landed = half(outs[i].at[2 * px + py], c, i)
                copy(i, k, landed, landed, (px, py, c)).wait_recv()
                passed.append(copy(i, 3 + k, landed, landed, sibling))
                passed[-1].start()
        for k, (px, py) in enumerate(chips):
            for i in range(n):
                theirs = half(outs[i].at[2 * px + py], 1 - c, i)
                copy(i, 3 + k, theirs, theirs, sibling).wait_recv()
        for cp in sends + passed:
            cp.wait_send()

    return _hbm_call(body, name, arrs, [jax.ShapeDtypeStruct((N_CHIPS,) + a.shape, a.dtype) for a in arrs], 6 * n)


def pair_exchange(arrs, name):
    n = len(arrs)
    halves = [a.shape[1] // 2 for a in arrs]

    def body(*refs):
        xs, theirs = refs[:n], refs[n:2 * n]
        send_sems, recv_sems = refs[2 * n:]
        x, y, c = _coords()
        cps = [pltpu.make_async_remote_copy(
            src_ref=xs[i].at[:, pl.ds((1 - c) * halves[i], halves[i]), :], dst_ref=theirs[i],
            send_sem=send_sems.at[i], recv_sem=recv_sems.at[i], device_id=(x, y, 1 - c), device_id_type=MESH)
            for i in range(n)]
        for cp in cps:
            cp.start()
        for cp in cps:
            cp.wait()

    shapes = [jax.ShapeDtypeStruct((a.shape[0], a.shape[1] // 2, a.shape[2]), a.dtype) for a in arrs]
    return _hbm_call(body, name, arrs, shapes, n)


def scatter_chips(arrs, name):
    n = len(arrs)

    def body(*refs):
        xs, outs = refs[:n], refs[n:2 * n]
        send_sems, recv_sems = refs[2 * n:]
        x, y, c = _coords()
        me, chips = 2 * x + y, _other_chips(x, y)

        def copy(i, k, src_slot, dst_slot, to):
            return pltpu.make_async_remote_copy(
                src_ref=xs[i].at[src_slot], dst_ref=outs[i].at[dst_slot], send_sem=send_sems.at[3 * i + k],
                recv_sem=recv_sems.at[3 * i + k], device_id=to, device_id_type=MESH)

        sends = [copy(i, k, 2 * px + py, me, (px, py, c)) for k, (px, py) in enumerate(chips) for i in range(n)]
        for cp in sends:
            cp.start()
        for k, (px, py) in enumerate(chips):
            for i in range(n):
                copy(i, k, me, 2 * px + py, (px, py, c)).wait_recv()
        for cp in sends:
            cp.wait_send()

    return _hbm_call(body, name, arrs, [jax.ShapeDtypeStruct(a.shape, a.dtype) for a in arrs], 3 * n)


def pair_allgather(arrs, name):
    n = len(arrs)

    def body(*refs):
        xs, outs = refs[:n], refs[n:2 * n]
        send_sems, recv_sems = refs[2 * n:]
        x, y, c = _coords()
        cps = [pltpu.make_async_remote_copy(
            src_ref=xs[i], dst_ref=outs[i].at[pl.ds(c * xs[i].shape[0], xs[i].shape[0]), :], send_sem=send_sems.at[i],
            recv_sem=recv_sems.at[i], device_id=(x, y, 1 - c), device_id_type=MESH) for i in range(n)]
        for cp in cps:
            cp.start()
        for cp in cps:
            cp.wait()

    return _hbm_call(body, name, arrs, [jax.ShapeDtypeStruct((2 * a.shape[0], a.shape[1]), a.dtype) for a in arrs], n)


GROUPS = (("ffn_w_gu",), ("ffn_w_down", "mla_w_o", "ssd_w_out"), ("mla_w_a",), ("mla_w_qb",), ("mla_w_kvb",), ("ssd_w_in",))


def _pad_rows(a, mult):
    r = (-a.shape[0]) % mult
    return a if r == 0 else jnp.concatenate([a, jnp.zeros((r, a.shape[1]), a.dtype)], axis=0)


def _rows2d(a):
    return a.reshape(-1, a.shape[-1])


def _unstack(st, axis):
    full = jnp.moveaxis(st, 0, axis)
    sh = list(full.shape)
    sh[axis:axis + 2] = [sh[axis] * sh[axis + 1]]
    return full.reshape(sh)


def _stack(full, axis):
    sh = list(full.shape)
    sh[axis:axis + 1] = [N_CHIPS, sh[axis] // N_CHIPS]
    return jnp.moveaxis(full.reshape(sh), axis, 0)


def kernel(x, c, positions, norm_gain, ada_w, ada_b, ffn_w_gu, ffn_w_down, mla_w_a, mla_q_a_gain, mla_kv_a_gain, mla_w_qb, mla_w_kvb, mla_q_gain, mla_k_gain, mla_w_o, ssd_w_in, ssd_conv_w, ssd_conv_b, ssd_dt_bias, ssd_a_log, ssd_d, ssd_norm_gain, ssd_w_out, loss_target, m_norm_gain, m_ada_w, m_ada_b, m_ffn_w_gu, m_ffn_w_down, m_mla_w_a, m_mla_q_a_gain, m_mla_kv_a_gain, m_mla_w_qb, m_mla_w_kvb, m_mla_q_gain, m_mla_k_gain, m_mla_w_o, m_ssd_w_in, m_ssd_conv_w, m_ssd_conv_b, m_ssd_dt_bias, m_ssd_a_log, m_ssd_d, m_ssd_norm_gain, m_ssd_w_out, v_norm_gain, v_ada_w, v_ada_b, v_ffn_w_gu, v_ffn_w_down, v_mla_w_a, v_mla_q_a_gain, v_mla_kv_a_gain, v_mla_w_qb, v_mla_w_kvb, v_mla_q_gain, v_mla_k_gain, v_mla_w_o, v_ssd_w_in, v_ssd_conv_w, v_ssd_conv_b, v_ssd_dt_bias, v_ssd_a_log, v_ssd_d, v_ssd_norm_gain, v_ssd_w_out):
    w_in = dict(norm_gain=norm_gain, ada_w=ada_w, ada_b=ada_b, ffn_w_gu=ffn_w_gu, ffn_w_down=ffn_w_down, mla_w_a=mla_w_a, mla_q_a_gain=mla_q_a_gain, mla_kv_a_gain=mla_kv_a_gain, mla_w_qb=mla_w_qb, mla_w_kvb=mla_w_kvb, mla_q_gain=mla_q_gain, mla_k_gain=mla_k_gain, mla_w_o=mla_w_o, ssd_w_in=ssd_w_in, ssd_conv_w=ssd_conv_w, ssd_conv_b=ssd_conv_b, ssd_dt_bias=ssd_dt_bias, ssd_a_log=ssd_a_log, ssd_d=ssd_d, ssd_norm_gain=ssd_norm_gain, ssd_w_out=ssd_w_out)
    m_in = dict(norm_gain=m_norm_gain, ada_w=m_ada_w, ada_b=m_ada_b, ffn_w_gu=m_ffn_w_gu, ffn_w_down=m_ffn_w_down, mla_w_a=m_mla_w_a, mla_q_a_gain=m_mla_q_a_gain, mla_kv_a_gain=m_mla_kv_a_gain, mla_w_qb=m_mla_w_qb, mla_w_kvb=m_mla_w_kvb, mla_q_gain=m_mla_q_gain, mla_k_gain=m_mla_k_gain, mla_w_o=m_mla_w_o, ssd_w_in=m_ssd_w_in, ssd_conv_w=m_ssd_conv_w, ssd_conv_b=m_ssd_conv_b, ssd_dt_bias=m_ssd_dt_bias, ssd_a_log=m_ssd_a_log, ssd_d=m_ssd_d, ssd_norm_gain=m_ssd_norm_gain, ssd_w_out=m_ssd_w_out)
    v_in = dict(norm_gain=v_norm_gain, ada_w=v_ada_w, ada_b=v_ada_b, ffn_w_gu=v_ffn_w_gu, ffn_w_down=v_ffn_w_down, mla_w_a=v_mla_w_a, mla_q_a_gain=v_mla_q_a_gain, mla_kv_a_gain=v_mla_kv_a_gain, mla_w_qb=v_mla_w_qb, mla_w_kvb=v_mla_w_kvb, mla_q_gain=v_mla_q_gain, mla_k_gain=v_mla_k_gain, mla_w_o=v_mla_w_o, ssd_w_in=v_ssd_w_in, ssd_conv_w=v_ssd_conv_w, ssd_conv_b=v_ssd_conv_b, ssd_dt_bias=v_ssd_dt_bias, ssd_a_log=v_ssd_a_log, ssd_d=v_ssd_d, ssd_norm_gain=v_ssd_norm_gain, ssd_w_out=v_ssd_w_out)
    names = list(w_in)
    xi, yi, ci = _coords()
    chip = 2 * xi + yi
    batch = 4 * xi + 2 * yi + ci
    x2, target = x[0], loss_target[0]

    small_sharded = ("norm_gain", "ssd_conv_w", "ssd_conv_b", "ssd_norm_gain")
    pack0 = jnp.concatenate([c.reshape(-1)] + [w_in[n].reshape(-1) for n in small_sharded])
    pack0 = _pad_rows(pack0.reshape(-1, 128), 8)
    g0 = allgather_small(pack0, "gather_small").reshape(N_DEV, -1)
    c_all = g0[:, :D_MODEL]
    P, off = {}, D_MODEL
    for n in small_sharded:
        sz = w_in[n].size
        st = g0[0::2, off:off + sz].reshape((N_CHIPS,) + w_in[n].shape)
        P[n] = _unstack(st, w_in[n].ndim - 1)
        off += sz
    for n in ("mla_q_a_gain", "mla_kv_a_gain", "mla_q_gain", "mla_k_gain", "ssd_dt_bias", "ssd_a_log", "ssd_d"):
        P[n] = w_in[n]

    sc = _silu(c_all)
    n_ada = ada_w.shape[2]
    b_sh = lax.dynamic_slice_in_dim(ada_b, chip * n_ada, n_ada, axis=1)
    mods_sh = jnp.stack([matmul(sc, ada_w[l], "nn", "ada_fwd") for l in range(DEPTH)]) + b_sh[:, None, :]
    g1 = allgather_small(mods_sh.reshape(-1, 128), "gather_mods").reshape(N_DEV, DEPTH, N_DEV, n_ada)
    mods = lax.dynamic_index_in_dim(g1[0::2], batch, axis=2, keepdims=False)
    mods = mods.transpose(1, 0, 2).reshape(DEPTH, 3, 3, D_MODEL)

    shard_groups = [jnp.concatenate([_rows2d(w_in[n]).astype(BF16) for n in grp], axis=0) for grp in GROUPS]
    gathered = allgather_chips(shard_groups, "gather_weights")
    gathered = [lax.dynamic_update_slice(g, s[None], (chip, 0, 0)) for g, s in zip(gathered, shard_groups)]
    G = {}
    for grp, arr in zip(GROUPS, gathered):
        off = 0
        for n in grp:
            rows = w_in[n].size // w_in[n].shape[-1]
            G[n] = arr[:, off:off + rows].reshape((N_CHIPS,) + w_in[n].shape)
            off += rows
    W = {
        "ffn_w_gu": [[_unstack(G["ffn_w_gu"][:, i, t], 1) for t in range(2)] for i in range(DEPTH)],
        "ffn_w_down": [[_unstack(G["ffn_w_down"][:, i, t], 0) for t in range(2)] for i in range(DEPTH)],
        "mla_w_a": [_unstack(G["mla_w_a"][:, j], 0) for j in range(2)],
        "mla_w_qb": [_unstack(G["mla_w_qb"][:, j], 1) for j in range(2)],
        "mla_w_kvb": [_unstack(G["mla_w_kvb"][:, j], 1) for j in range(2)],
        "mla_w_o": [_unstack(G["mla_w_o"][:, j], 0) for j in range(2)],
        "ssd_w_out": [_unstack(G["ssd_w_out"][:, j], 0) for j in range(2)],
    }
    w_in_full = [_unstack(G["ssd_w_in"][:, j], 1) for j in range(2)]
    W["ssd_w_z"] = [w[:, :D_INNER] for w in w_in_full]
    W["ssd_w_xbc"] = [w[:, D_INNER:D_INNER + CONV_DIM] for w in w_in_full]
    W["ssd_w_dt"] = [w[:, D_INNER + CONV_DIM:] for w in w_in_full]
    X = stand_ins(W)

    pos = positions[0]
    y, vjp = jax.vjp(lambda a, b, p_, x_: trunk(a, b, p_, W, x_, pos), x2, mods, P, X)
    dy, loss_cols = loss_head(y, target)
    dx, dmods, dP, dX = vjp(dy)
    loss = lax.psum(jnp.sum(loss_cols), ("x", "y", "c"))

    small_names = ("norm_gain", "ssd_conv_w", "ssd_conv_b", "ssd_norm_gain", "mla_q_a_gain", "mla_kv_a_gain",
                   "mla_q_gain", "mla_k_gain", "ssd_dt_bias", "ssd_a_log", "ssd_d")
    pack1 = jnp.concatenate([dmods.reshape(-1)] + [dP[n].reshape(-1) for n in small_names])
    pack1 = _pad_rows(jnp.pad(pack1, (0, (-pack1.size) % 128)).reshape(-1, 128), 8)
    rows1 = pack1.shape[0]
    g2 = allgather_small(pack1, "gather_small_grads")
    tot = sum_slots(g2.reshape(N_DEV, rows1, 128), "sum_small_grads").reshape(-1)
    n_mod = DEPTH * 9 * D_MODEL
    grads = {"ada_b": tot[:n_mod].reshape(DEPTH, 9 * D_MODEL)}
    off = n_mod
    for n in small_names:
        sz = dP[n].size
        full = tot[off:off + sz].reshape(dP[n].shape)
        off += sz
        if n in small_sharded:
            k = w_in[n].shape[-1]
            full = lax.dynamic_slice_in_dim(full, chip * k, k, axis=full.ndim - 1)
        grads[n] = full
    dmods_all = g2.reshape(N_DEV, -1)[:, :n_mod].reshape(N_DEV, DEPTH, 9 * D_MODEL)
    dm_sh = lax.dynamic_slice_in_dim(dmods_all, chip * n_ada, n_ada, axis=2)
    grads["ada_w"] = jnp.stack([matmul(sc, dm_sh[:, l], "tn", "ada_dw") for l in range(DEPTH)])

    w_in_g = [jnp.concatenate([dX["ssd_w_z"][j], dX["ssd_w_xbc"][j], dX["ssd_w_dt"][j]], axis=1) for j in range(2)]
    per_name = {
        "ffn_w_gu": [dX["ffn_w_gu"][i][t] for i in range(DEPTH) for t in range(2)],
        "ffn_w_down": [dX["ffn_w_down"][i][t].reshape(N_CHIPS, -1, D_MODEL) for i in range(DEPTH) for t in range(2)],
        "mla_w_a": [g.reshape(N_CHIPS, -1, g.shape[-1]) for g in dX["mla_w_a"]],
        "mla_w_qb": dX["mla_w_qb"], "mla_w_kvb": dX["mla_w_kvb"],
        "mla_w_o": [g.reshape(N_CHIPS, -1, D_MODEL) for g in dX["mla_w_o"]],
        "ssd_w_out": [g.reshape(N_CHIPS, -1, D_MODEL) for g in dX["ssd_w_out"]],
        "ssd_w_in": [_stack(g, 1) for g in w_in_g],
    }
    grad_groups = [jnp.concatenate([p for n in grp for p in per_name[n]], axis=1) for grp in GROUPS]
    theirs = pair_exchange(grad_groups, "grads_to_sibling")
    pair = [sum_own_half(a, b, ci, "sum_pair", BF16) for a, b in zip(grad_groups, theirs)]
    landed = scatter_chips(pair, "grads_to_chips")
    landed = [lax.dynamic_update_slice(a, lax.dynamic_slice_in_dim(p, chip, 1, axis=0), (chip, 0, 0)) for a, p in zip(landed, pair)]
    half_sums = [sum_slots(a, "sum_chips") for a in landed]
    totals = pair_allgather(half_sums, "grad_halves_swap")
    totals = [lax.dynamic_update_slice(t, h, (ci * h.shape[0], 0)) for t, h in zip(totals, half_sums)]
    for grp, arr in zip(GROUPS, totals):
        off = 0
        for n in grp:
            rows = w_in[n].size // w_in[n].shape[-1]
            grads[n] = arr[off:off + rows].reshape(w_in[n].shape)
            off += rows

    deltas, new_m, new_v = {}, {}, {}
    for n in names:
        w = w_in[n]
        d, nm, nv = adamw(_rows2d(w), _rows2d(m_in[n]), _rows2d(v_in[n]), _rows2d(grads[n]))
        deltas[n], new_m[n], new_v[n] = d.reshape(w.shape), nm.reshape(w.shape), nv.reshape(w.shape)

    return (loss, dx[None], *[grads[n] for n in names], *[deltas[n] for n in names],
            *[new_m[n] for n in names], *[new_v[n] for n in names])
```

```python
import numpy as np

import jax
import jax.numpy as jnp
from jax import lax
from jax.experimental import pallas as pl
from jax.experimental.pallas import tpu as pltpu

F32 = jnp.float32
BF16 = jnp.bfloat16
MESH = pl.DeviceIdType.MESH

D_MODEL = 1024
DEPTH = 4
EPS = 1e-6
D_FF = 2816
MLA_HEADS = 16
Q_LORA = 384
KV_LORA = 256
QK_NOPE = 64
QK_ROPE = 32
QK_HEAD = QK_NOPE + QK_ROPE
V_HEAD = 64
ROPE_THETA = 10000.0
D_INNER = 2048
SSD_HEAD_DIM = 64
SSD_HEADS = 32
SSD_GROUPS = 4
SSD_STATE = 128
CONV_WIDTH = 4
CHUNK = 128
CONV_DIM = D_INNER + 2 * SSD_GROUPS * SSD_STATE
ADAM_LR = 0.001
ADAM_B1 = 0.9
ADAM_B2 = 0.999
ADAM_EPS = 1e-08
ADAM_WD = 0.01
ADAM_STEP = 10

N_CHIPS = 4
N_DEV = 8
V7X_VMEM_LIMIT = 56 * 1024 * 1024
ATTN_HEADS_PER_STEP = 4
LANES = 128
LOG2E = 1.4426950408889634


def _params(sem=None):
    return pltpu.CompilerParams(dimension_semantics=sem, vmem_limit_bytes=V7X_VMEM_LIMIT)


def _div_tile(n, pref, quantum):
    if n <= pref:
        return n
    t = (pref // quantum) * quantum
    while t >= quantum:
        if n % t == 0:
            return t
        t -= quantum
    return n


def matmul(a, b, mode, name, out_dtype=F32, stack=0, b_slot=None, tm=1024, tn=1408, tk=2816):
    if b_slot is not None:
        (roff, rows), (nsl, _, n) = b_slot, b.shape
        (M, K) = a.shape
        K2, N = (rows, nsl * n) if mode == "nn" else (nsl * n, rows)
    elif mode == "nn":
        (M, K), (K2, N) = a.shape, b.shape
    elif mode == "nt":
        (M, K), (N, K2) = a.shape, b.shape
    else:
        (K, M), (K2, N) = a.shape, b.shape
    assert K == K2, (a.shape, b.shape, mode)
    tm = _div_tile(M, tm, 128 if mode == "tn" else 16)
    tn = _div_tile(N // stack if stack else (n if b_slot and mode == "nn" else N), tn, 128)
    tk = _div_tile(n if b_slot and mode == "nt" else K, tk, 128)
    nk = K // tk
    if mode == "nn":
        a_spec = pl.BlockSpec((tm, tk), lambda i, j, k: (i, k))
        b_spec = pl.BlockSpec((tk, tn), lambda i, j, k: (k, j))
        dims = (((1,), (0,)), ((), ()))
        if b_slot:
            assert roff % tk == 0
            nbn, rb = n // tn, roff // tk
            b_spec = pl.BlockSpec((None, tk, tn), lambda i, j, k: (j // nbn, rb + k, j % nbn))
    elif mode == "nt":
        a_spec = pl.BlockSpec((tm, tk), lambda i, j, k: (i, k))
        b_spec = pl.BlockSpec((tn, tk), lambda i, j, k: (j, k))
        dims = (((1,), (1,)), ((), ()))
        if b_slot:
            assert roff % tn == 0
            nbk, rb = n // tk, roff // tn
            b_spec = pl.BlockSpec((None, tn, tk), lambda i, j, k: (k // nbk, rb + j, k % nbk))
    else:
        a_spec = pl.BlockSpec((tk, tm), lambda i, j, k: (k, i))
        b_spec = pl.BlockSpec((tk, tn), lambda i, j, k: (k, j))
        dims = (((0,), (0,)), ((), ()))
    if stack:
        nb = N // stack // tn
        out_spec = pl.BlockSpec((None, tm, tn), lambda i, j, k: (j // nb, i, j % nb))
        out_shape = jax.ShapeDtypeStruct((stack, M, N // stack), out_dtype)
    else:
        out_spec = pl.BlockSpec((tm, tn), lambda i, j, k: (i, j))
        out_shape = jax.ShapeDtypeStruct((M, N), out_dtype)
    use_acc = nk > 1 and out_dtype != F32

    def body(a_ref, b_ref, o_ref, *acc):
        p = lax.dot_general(a_ref[...].astype(BF16), b_ref[...].astype(BF16), dims, preferred_element_type=F32)
        if nk == 1:
            o_ref[...] = p.astype(out_dtype)
            return
        acc_ref = acc[0] if use_acc else o_ref
        k = pl.program_id(2)

        @pl.when(k == 0)
        def _():
            acc_ref[...] = p

        @pl.when(k > 0)
        def _():
            acc_ref[...] += p

        if use_acc:
            @pl.when(k == nk - 1)
            def _():
                o_ref[...] = acc_ref[...].astype(out_dtype)

    return pl.pallas_call(
        body, name=name, grid=(M // tm, N // tn, nk), in_specs=[a_spec, b_spec], out_specs=out_spec, out_shape=out_shape,
        scratch_shapes=[pltpu.VMEM((tm, tn), F32)] if use_acc else [],
        compiler_params=_params(("parallel", "parallel", "arbitrary")),
    )(a, b)


def mm_op(name, out_dtype=F32, stack=0, b_slot=None):
    @jax.custom_vjp
    def op(a, w, wp):
        return matmul(a, w, "nn", name + "_fwd", out_dtype=out_dtype, b_slot=b_slot)

    def fwd(a, w, wp):
        return op(a, w, wp), (a, w)

    def bwd(res, g):
        a, w = res
        return (matmul(g, w, "nt", name + "_dx", out_dtype=a.dtype, b_slot=b_slot), jnp.zeros_like(w),
                matmul(a, g, "tn", name + "_dw", stack=stack, tn=1408, tk=1024))

    op.defvjp(fwd, bwd)
    return op


def _row_tile(rows, widths):
    w = max(widths)
    t = 128 if w > 4096 else (256 if w > 1024 else (512 if w > 128 else 2048))
    return _div_tile(rows, t, 16)


def _row_spec(r, tm):
    nb = r.shape[0] // tm
    return pl.BlockSpec((tm, r.shape[1]), lambda i: (i % nb, 0))


def _rowwise_fwd(f, rows, vecs, name, out_dtype):
    n_r, n_v = len(rows), len(vecs)
    S = rows[0].shape[0]
    tm = _row_tile(min(r.shape[0] for r in rows), [r.shape[1] for r in rows])
    outs = jax.eval_shape(f, *[jax.ShapeDtypeStruct((tm, r.shape[1]), F32) for r in rows], *vecs)

    def body(*refs):
        res = f(*[r[...].astype(F32) for r in refs[: n_r + n_v]])
        for o, r in zip(refs[n_r + n_v:], res):
            o[...] = r.astype(out_dtype)

    return pl.pallas_call(
        body, name=name, grid=(S // tm,),
        in_specs=[_row_spec(r, tm) for r in rows] + [pl.BlockSpec(v.shape, lambda i: (0, 0)) for v in vecs],
        out_specs=tuple(pl.BlockSpec((tm, o.shape[1]), lambda i: (i, 0)) for o in outs),
        out_shape=tuple(jax.ShapeDtypeStruct((S, o.shape[1]), out_dtype) for o in outs),
        compiler_params=_params(("parallel",)),
    )(*rows, *vecs)


def _rowwise_bwd(f, rows, vecs, douts, diff_rows, n_const, name):
    n_r, n_o = len(rows), len(douts)
    consts, vecs = vecs[len(vecs) - n_const:], vecs[:len(vecs) - n_const]
    n_v = len(vecs)
    S = rows[0].shape[0]
    tm = _row_tile(min(r.shape[0] for r in rows), [r.shape[1] for r in rows] + [d.shape[1] for d in douts])
    d_idx = [i for i in range(n_r) if diff_rows[i]]

    def body(*refs):
        row_v = [r[...].astype(F32) for r in refs[:n_r]]
        vec_v = [r[...] for r in refs[n_r:n_r + n_v]]
        const_v = [r[...] for r in refs[n_r + n_v:n_r + n_v + n_const]]
        dout_v = tuple(r[...].astype(F32) for r in refs[n_r + n_v + n_const:n_r + n_v + n_const + n_o])
        out_refs = refs[n_r + n_v + n_const + n_o:]

        def g(*args):
            full = list(row_v)
            for j, i in enumerate(d_idx):
                full[i] = args[j]
            return f(*full, *args[len(d_idx):], *const_v)

        _, vjp = jax.vjp(g, *[row_v[i] for i in d_idx], *vec_v)
        grads = vjp(dout_v)
        for j in range(len(d_idx)):
            out_refs[j][...] = grads[j].astype(out_refs[j].dtype)
        step = pl.program_id(0)
        for j in range(n_v):
            gv, o = grads[len(d_idx) + j], out_refs[len(d_idx) + j]

            @pl.when(step == 0)
            def _(gv=gv, o=o):
                o[...] = gv

            @pl.when(step > 0)
            def _(gv=gv, o=o):
                o[...] += gv

    res = pl.pallas_call(
        body, name=name, grid=(S // tm,),
        in_specs=[_row_spec(r, tm) for r in rows] + [pl.BlockSpec(v.shape, lambda i: (0, 0)) for v in vecs + consts]
        + [pl.BlockSpec((tm, d.shape[1]), lambda i: (i, 0)) for d in douts],
        out_specs=tuple([pl.BlockSpec((tm, rows[i].shape[1]), lambda i_: (i_, 0)) for i in d_idx]
                        + [pl.BlockSpec(v.shape, lambda i: (0, 0)) for v in vecs]),
        out_shape=tuple([jax.ShapeDtypeStruct(rows[i].shape, rows[i].dtype) for i in d_idx]
                        + [jax.ShapeDtypeStruct(v.shape, F32) for v in vecs]),
        compiler_params=_params(("arbitrary",)),
    )(*rows, *vecs, *consts, *douts)
    drows = [None] * n_r
    for j, i in enumerate(d_idx):
        drows[i] = res[j]
    for i in range(n_r):
        if drows[i] is None:
            drows[i] = jnp.zeros_like(rows[i])
    return tuple(drows) + tuple(res[len(d_idx):]) + tuple(jnp.zeros_like(k) for k in consts)


def rowwise_op(f, name, n_rows, diff_rows=None, out_dtype=F32, n_const=0):
    diff = tuple(diff_rows) if diff_rows is not None else (True,) * n_rows

    @jax.custom_vjp
    def op(*args):
        return _rowwise_fwd(f, args[:n_rows], args[n_rows:], name + "_fwd", out_dtype)

    def fwd(*args):
        return op(*args), args

    def bwd(args, douts):
        return _rowwise_bwd(f, args[:n_rows], args[n_rows:], douts, diff, n_const, name + "_bwd")

    op.defvjp(fwd, bwd)
    return op


def _rms(x, gain):
    return x * lax.rsqrt(jnp.mean(x * x, axis=-1, keepdims=True) + EPS) * gain


def _silu(x):
    return x * jax.nn.sigmoid(x)


def _f_modulate(x, gain, shift, scale):
    return (_rms(x, gain) * (1.0 + scale) + shift,)


def _f_rms(x, gain):
    return (_rms(x, gain),)


def _f_swiglu(gu):
    n = gu.shape[1] // 2
    return (_silu(gu[:, :n]) * gu[:, n:],)


def _f_resid(coef):
    def f(x, y, gate):
        return (x + (coef * gate) * y,)
    return f


def _f_head_rope(x, cosf, sinf, gain, swap):
    y = _rms(x, gain)
    return (y * cosf + jnp.dot(y, swap, precision=lax.Precision.HIGHEST, preferred_element_type=F32) * sinf,)


def _rope_swap():
    m = np.zeros((QK_HEAD, QK_HEAD), np.float32)
    half = QK_ROPE // 2
    for i in range(half):
        m[QK_NOPE + half + i, QK_NOPE + i] = -1.0
        m[QK_NOPE + i, QK_NOPE + half + i] = 1.0
    return jnp.asarray(m)


def _f_gated_norm(y, z, gain):
    g = y * _silu(z)
    n = g.shape[1] // SSD_GROUPS
    return (jnp.concatenate([_rms(g[:, i * n:(i + 1) * n], gain[:, i * n:(i + 1) * n]) for i in range(SSD_GROUPS)], axis=1),)


HALO = 8


def _conv_taps(ext, w, rows, off):
    acc = None
    for k in range(CONV_WIDTH):
        term = w[k:k + 1, :] * pltpu.roll(ext, CONV_WIDTH - 1 - k, 0)[off:off + rows]
        acc = term if acc is None else acc + term
    return acc


def _conv_specs(S, tm, tc):
    tile = pl.BlockSpec((tm, tc), lambda j, i: (i, j))
    prev = pl.BlockSpec((HALO, tc), lambda j, i: (jnp.maximum(i * (tm // HALO) - 1, 0), j))
    nxt = pl.BlockSpec((HALO, tc), lambda j, i: (jnp.minimum((i + 1) * (tm // HALO), S // HALO - 1), j))
    wspec = pl.BlockSpec((CONV_WIDTH, tc), lambda j, i: (0, j))
    bspec = pl.BlockSpec((1, tc), lambda j, i: (0, j))
    return tile, prev, nxt, wspec, bspec


def conv_silu_fwd(u, w, b, tm=512, tc=1024):
    S, C = u.shape
    tm = _div_tile(S, tm, HALO)

    def body(u_ref, p_ref, w_ref, b_ref, o_ref):
        prev = jnp.where(pl.program_id(1) == 0, 0.0, p_ref[...])
        ext = jnp.concatenate([prev, u_ref[...]], axis=0)
        conv = _conv_taps(ext, w_ref[...], tm, HALO) + b_ref[...]
        o_ref[...] = conv * jax.nn.sigmoid(conv)

    tile, prev, _, wspec, bspec = _conv_specs(S, tm, tc)
    return pl.pallas_call(
        body, name="conv_silu_fwd", grid=(C // tc, S // tm), in_specs=[tile, prev, wspec, bspec], out_specs=tile,
        out_shape=jax.ShapeDtypeStruct((S, C), F32), compiler_params=_params(("parallel", "arbitrary")),
    )(u, u, w, b)


def conv_silu_bwd(u, w, b, dout, tm=512, tc=1024):
    S, C = u.shape
    tm = _div_tile(S, tm, HALO)
    n = S // tm
    ext_rows = tm + HALO

    def body(u_ref, p_ref, n_ref, g_ref, gn_ref, w_ref, b_ref, du_ref, dw_ref, db_ref):
        i = pl.program_id(1)
        wv = w_ref[...]
        prev = jnp.where(i == 0, 0.0, p_ref[...])
        ext = jnp.concatenate([prev, u_ref[...], n_ref[...]], axis=0)
        conv = _conv_taps(ext, wv, ext_rows, HALO) + b_ref[...]
        g_ext = jnp.concatenate([g_ref[...], jnp.where(i == n - 1, 0.0, gn_ref[...])], axis=0)
        sg = jax.nn.sigmoid(conv)
        dconv = g_ext * (sg * (1.0 + conv * (1.0 - sg)))
        du = None
        for k in range(CONV_WIDTH):
            s = CONV_WIDTH - 1 - k
            term = wv[k:k + 1, :] * pltpu.roll(dconv, (ext_rows - s) % ext_rows, 0)[:tm]
            du = term if du is None else du + term
        du_ref[...] = du
        dc = dconv[:tm]
        dw = jnp.concatenate([jnp.sum(dc * pltpu.roll(ext, CONV_WIDTH - 1 - k, 0)[HALO:HALO + tm], axis=0, keepdims=True)
                              for k in range(CONV_WIDTH)], axis=0)
        dbv = jnp.sum(dc, axis=0, keepdims=True)

        @pl.when(i == 0)
        def _():
            dw_ref[...] = dw
            db_ref[...] = dbv

        @pl.when(i > 0)
        def _():
            dw_ref[...] += dw
            db_ref[...] += dbv

    tile, prev, nxt, wspec, bspec = _conv_specs(S, tm, tc)
    return pl.pallas_call(
        body, name="conv_silu_bwd", grid=(C // tc, n), in_specs=[tile, prev, nxt, tile, nxt, wspec, bspec],
        out_specs=(tile, wspec, bspec),
        out_shape=(jax.ShapeDtypeStruct((S, C), F32), jax.ShapeDtypeStruct((CONV_WIDTH, C), F32),
                   jax.ShapeDtypeStruct((1, C), F32)),
        compiler_params=_params(("parallel", "arbitrary")),
    )(u, u, u, dout, dout, w, b)


@jax.custom_vjp
def conv_silu(u, w, b):
    return conv_silu_fwd(u, w, b)


def _conv_silu_fwd(u, w, b):
    return conv_silu_fwd(u, w, b), (u, w, b)


def _conv_silu_bwd(res, dout):
    return conv_silu_bwd(*res, dout)


conv_silu.defvjp(_conv_silu_fwd, _conv_silu_bwd)


_NT = (((1,), (1,)), ((), ()))


def _dot(a, b):
    return jnp.dot(a.astype(BF16), b.astype(BF16), preferred_element_type=F32)


def _dot_nt(a, b):
    return lax.dot_general(a.astype(BF16), b.astype(BF16), _NT, preferred_element_type=F32)


def _attn_tile(S):
    return _div_tile(S, 512, 128)


def _causal(t, transposed=False):
    r = lax.broadcasted_iota(jnp.int32, (t, t), 0)
    c = lax.broadcasted_iota(jnp.int32, (t, t), 1)
    return r <= c if transposed else r >= c


def _tri_tables(n, by_key):
    if by_key:
        pairs = [(i, j) for j in range(n) for i in range(j, n)]
    else:
        pairs = [(i, j) for i in range(n) for j in range(i + 1)]
    return (jnp.asarray(np.array([p[0] for p in pairs], np.int32)), jnp.asarray(np.array([p[1] for p in pairs], np.int32)))


def attn_fwd(q, k, v):
    H, S, dk = q.shape
    dv = v.shape[-1]
    t, hb = _attn_tile(S), ATTN_HEADS_PER_STEP
    n = S // t
    scale = dk ** -0.5
    qi_tab, kj_tab = _tri_tables(n, by_key=False)

    def body(qi_ref, kj_ref, q_ref, k_ref, v_ref, o_ref, lse_ref, m_s, l_s, acc_s):
        qi, kj = qi_ref[pl.program_id(1)], kj_ref[pl.program_id(1)]

        @pl.when(kj == 0)
        def _():
            m_s[...] = jnp.full(m_s.shape, -jnp.inf, F32)
            l_s[...] = jnp.zeros(l_s.shape, F32)
            acc_s[...] = jnp.zeros(acc_s.shape, F32)

        def step(masked):
            ss = [_dot_nt(q_ref[j], k_ref[j]) for j in range(hb)]
            new = []
            for j in range(hb):
                s = ss[j] * (scale * LOG2E)
                if masked:
                    s = jnp.where(_causal(t), s, -jnp.inf)
                m_old = m_s[j]
                m_new = jnp.maximum(m_old, jnp.max(s, axis=-1, keepdims=True))
                alpha = jnp.exp2(m_old - m_new)
                p = jnp.exp2(s - jnp.tile(m_new, (1, t // LANES)))
                new.append((m_new, alpha * l_s[j] + jnp.sum(p, axis=-1, keepdims=True),
                            alpha[:, :dv] * acc_s[j] + _dot(p, v_ref[j])))
            for j in range(hb):
                m_s[j], l_s[j], acc_s[j] = new[j]

        @pl.when(kj < qi)
        def _():
            step(False)

        @pl.when(kj == qi)
        def _():
            step(True)
            l = l_s[...]
            o_ref[...] = acc_s[...] / l[:, :, :dv]
            lse_ref[...] = m_s[...] + jnp.log2(l)

    qmap = lambda h, s, qi, kj: (h, qi[s], 0)
    kmap = lambda h, s, qi, kj: (h, kj[s], 0)
    return pl.pallas_call(
        body, name="attn_fwd",
        grid_spec=pltpu.PrefetchScalarGridSpec(
            num_scalar_prefetch=2, grid=(H // hb, qi_tab.shape[0]),
            in_specs=[pl.BlockSpec((hb, t, dk), qmap), pl.BlockSpec((hb, t, dk), kmap), pl.BlockSpec((hb, t, dv), kmap)],
            out_specs=(pl.BlockSpec((hb, t, dv), qmap), pl.BlockSpec((hb, t, LANES), qmap)),
            scratch_shapes=[pltpu.VMEM((hb, t, LANES), F32), pltpu.VMEM((hb, t, LANES), F32), pltpu.VMEM((hb, t, dv), F32)]),
        out_shape=(jax.ShapeDtypeStruct((H, S, dv), F32), jax.ShapeDtypeStruct((H, S, LANES), F32)),
        compiler_params=_params(("parallel", "arbitrary")),
    )(qi_tab, kj_tab, q, k, v)


def attn_bwd(q, k, v, o, lse, do):
    H, S, dk = q.shape
    dv = v.shape[-1]
    t, hb = _attn_tile(S), ATTN_HEADS_PER_STEP
    n = S // t
    scale = dk ** -0.5
    qi_tab, kj_tab = _tri_tables(n, by_key=False)

    def dq_body(qi_ref, kj_ref, q_ref, k_ref, v_ref, o_ref, do_ref, lse_ref, dq_ref, delta_ref):
        qi, kj = qi_ref[pl.program_id(1)], kj_ref[pl.program_id(1)]

        @pl.when(kj == 0)
        def _():
            d = jnp.sum(do_ref[...] * o_ref[...], axis=-1, keepdims=True)
            delta_ref[...] = jnp.broadcast_to(d, delta_ref.shape)
            dq_ref[...] = jnp.zeros((hb, t, dk), F32)

        def step(masked):
            ss = [_dot_nt(q_ref[j], k_ref[j]) for j in range(hb)]
            dps = [_dot_nt(do_ref[j], v_ref[j]) for j in range(hb)]
            for j in range(hb):
                p = jnp.exp2(ss[j] * (scale * LOG2E) - jnp.tile(lse_ref[j], (1, t // LANES)))
                if masked:
                    p = jnp.where(_causal(t), p, 0.0)
                ds = p * (dps[j] - jnp.tile(delta_ref[j], (1, t // LANES)))
                dq_ref[j] += _dot(ds, k_ref[j]) * scale

        @pl.when(kj < qi)
        def _():
            step(False)

        @pl.when(kj == qi)
        def _():
            step(True)

    qmap = lambda h, s, qi, kj: (h, qi[s], 0)
    kmap = lambda h, s, qi, kj: (h, kj[s], 0)
    dq, delta = pl.pallas_call(
        dq_body, name="attn_bwd_dq",
        grid_spec=pltpu.PrefetchScalarGridSpec(
            num_scalar_prefetch=2, grid=(H // hb, qi_tab.shape[0]),
            in_specs=[pl.BlockSpec((hb, t, dk), qmap), pl.BlockSpec((hb, t, dk), kmap), pl.BlockSpec((hb, t, dv), kmap),
                      pl.BlockSpec((hb, t, dv), qmap), pl.BlockSpec((hb, t, dv), qmap), pl.BlockSpec((hb, t, LANES), qmap)],
            out_specs=(pl.BlockSpec((hb, t, dk), qmap), pl.BlockSpec((hb, t, LANES), qmap))),
        out_shape=(jax.ShapeDtypeStruct((H, S, dk), F32), jax.ShapeDtypeStruct((H, S, LANES), F32)),
        compiler_params=_params(("parallel", "arbitrary")),
    )(qi_tab, kj_tab, q, k, v, o, do, lse)

    lse_r = lse[:, :, 0].reshape(H, 1, S)
    delta_r = delta[:, :, 0].reshape(H, 1, S)
    do_b = do.astype(BF16)
    qi_tab2, kj_tab2 = _tri_tables(n, by_key=True)

    def dkv_body(qi_ref, kj_ref, q_ref, k_ref, v_ref, do_ref, lse_ref, delta_ref, dk_ref, dv_ref):
        qi, kj = qi_ref[pl.program_id(1)], kj_ref[pl.program_id(1)]

        def step(masked):
            sts = [_dot_nt(k_ref[j], q_ref[j]) for j in range(hb)]
            dpts = [_dot_nt(v_ref[j], do_ref[j]) for j in range(hb)]
            for j in range(hb):
                pt = jnp.exp2(sts[j] * (scale * LOG2E) - lse_ref[j])
                if masked:
                    pt = jnp.where(_causal(t, transposed=True), pt, 0.0)
                dvj = _dot(pt, do_ref[j])
                dst = pt * (dpts[j] - delta_ref[j])
                dkj = _dot(dst, q_ref[j]) * scale
                if masked:
                    dv_ref[j] = dvj
                    dk_ref[j] = dkj
                else:
                    dv_ref[j] += dvj
                    dk_ref[j] += dkj

        @pl.when(qi == kj)
        def _():
            step(True)

        @pl.when(qi > kj)
        def _():
            step(False)

    qmap2 = lambda h, s, qi, kj: (h, qi[s], 0)
    kmap2 = lambda h, s, qi, kj: (h, kj[s], 0)
    rowq = lambda h, s, qi, kj: (h, 0, qi[s])
    dk_, dv_ = pl.pallas_call(
        dkv_body, name="attn_bwd_dkv",
        grid_spec=pltpu.PrefetchScalarGridSpec(
            num_scalar_prefetch=2, grid=(H // hb, qi_tab2.shape[0]),
            in_specs=[pl.BlockSpec((hb, t, dk), qmap2), pl.BlockSpec((hb, t, dk), kmap2), pl.BlockSpec((hb, t, dv), kmap2),
                      pl.BlockSpec((hb, t, dv), qmap2), pl.BlockSpec((hb, 1, t), rowq), pl.BlockSpec((hb, 1, t), rowq)],
            out_specs=(pl.BlockSpec((hb, t, dk), kmap2), pl.BlockSpec((hb, t, dv), kmap2))),
        out_shape=(jax.ShapeDtypeStruct((H, S, dk), F32), jax.ShapeDtypeStruct((H, S, dv), F32)),
        compiler_params=_params(("parallel", "arbitrary")),
    )(qi_tab2, kj_tab2, q, k, v, do_b, lse_r, delta_r)
    return dq, dk_, dv_


@jax.custom_vjp
def attention(q, k, v):
    return attn_fwd(q.astype(BF16), k.astype(BF16), v.astype(BF16))[0]


def _attention_fwd(q, k, v):
    qb, kb, vb = q.astype(BF16), k.astype(BF16), v.astype(BF16)
    o, lse = attn_fwd(qb, kb, vb)
    return o, (qb, kb, vb, o, lse)


def _attention_bwd(res, do):
    return attn_bwd(*res, do)


attention.defvjp(_attention_fwd, _attention_bwd)


def _ssd_specs(hb, L, P, N, order):
    xs = pl.BlockSpec((hb, L, P), lambda g, c: (g, order(c), 0))
    col = pl.BlockSpec((None, L, hb), lambda g, c: (g, order(c), 0))
    row = pl.BlockSpec((hb, 1, L), lambda g, c: (g, 0, order(c)))
    bc = pl.BlockSpec((None, L, N), lambda g, c: (g, order(c), 0))
    st = pl.BlockSpec((hb, None, N, P), lambda g, c: (g, order(c), 0, 0))
    return xs, col, row, bc, st


def _ssd_rows(cols, H):
    G, S, hb = cols.shape
    return cols.transpose(0, 2, 1).reshape(H, 1, S)


def ssd_fwd(x, dt, ac, Bm, Cm):
    H, S, P = x.shape
    G, _, N = Bm.shape
    hb, L = H // G, CHUNK
    nc = S // L
    acr = _ssd_rows(ac, H)

    def body(x_ref, dt_ref, ac_ref, acr_ref, b_ref, c_ref, y_ref, hp_ref, h_s):
        @pl.when(pl.program_id(1) == 0)
        def _():
            h_s[...] = jnp.zeros((hb, N, P), F32)

        Bv, Cv = b_ref[...], c_ref[...]
        cb = _dot_nt(Cv, Bv)
        bt = Bv.T
        mask = _causal(L)
        ac_all, dt_all = ac_ref[...], dt_ref[...]
        for j in range(hb):
            a = jnp.broadcast_to(ac_all[:, j:j + 1], (L, L))
            dtv = jnp.broadcast_to(dt_all[:, j:j + 1], (L, L))
            lm = jnp.exp(jnp.where(mask, a - acr_ref[j], -jnp.inf))
            xdt = x_ref[j] * dtv[:, :P]
            h = h_s[j]
            hp_ref[j] = h
            y_ref[j] = _dot(cb * lm, xdt) + jnp.exp(a)[:, :P] * _dot(Cv, h)
            al = a[L - 1:L, :]
            h_s[j] = jnp.exp(al)[:, :P] * h + _dot(bt, xdt * jnp.exp(al - a)[:, :P])

    xs, col, row, bc, st = _ssd_specs(hb, L, P, N, lambda c: c)
    return pl.pallas_call(
        body, name="ssd_fwd", grid=(G, nc), in_specs=[xs, col, col, row, bc, bc], out_specs=(xs, st),
        out_shape=(jax.ShapeDtypeStruct((H, S, P), F32), jax.ShapeDtypeStruct((H, nc, N, P), F32)),
        scratch_shapes=[pltpu.VMEM((hb, N, P), F32)],
        compiler_params=_params(("parallel", "arbitrary")),
    )(x, dt, ac, acr, Bm, Cm)


def ssd_bwd(x, dt, ac, Bm, Cm, hp, dy):
    H, S, P = x.shape
    G, _, N = Bm.shape
    hb, L = H // G, CHUNK
    nc = S // L
    acr = _ssd_rows(ac, H)

    def body(x_ref, dt_ref, ac_ref, acr_ref, b_ref, c_ref, hp_ref, dy_ref,
             dx_ref, ddt_ref, dac_ref, dacr_ref, db_ref, dc_ref, dh_s):
        @pl.when(pl.program_id(1) == 0)
        def _():
            dh_s[...] = jnp.zeros((hb, N, P), F32)

        Bv, Cv = b_ref[...], c_ref[...]
        cb = _dot_nt(Cv, Bv)
        cbt = _dot_nt(Bv, Cv)
        ct = Cv.T
        mask, maskt = _causal(L), _causal(L, transposed=True)
        last = lax.broadcasted_iota(jnp.int32, (L, 1), 0) == L - 1
        lane = lax.broadcasted_iota(jnp.int32, (L, hb), 1)
        db = jnp.zeros((L, N), F32)
        dc = jnp.zeros((L, N), F32)
        dac_all = jnp.zeros((L, hb), F32)
        ddt_all = jnp.zeros((L, hb), F32)
        ac_all, dt_all = ac_ref[...], dt_ref[...]
        for j in range(hb):
            ar, xv, g, h, dh = acr_ref[j], x_ref[j], dy_ref[j], hp_ref[j], dh_s[j]
            a = jnp.broadcast_to(ac_all[:, j:j + 1], (L, L))
            dtv = jnp.broadcast_to(dt_all[:, j:j + 1], (L, L))
            lm = jnp.exp(jnp.where(mask, a - ar, -jnp.inf))
            lmt = jnp.exp(jnp.where(maskt, ar - a, -jnp.inf))
            xdt = xv * dtv[:, :P]
            e = jnp.exp(a)
            al = a[L - 1:L, :]
            dte = jnp.exp(al - a)
            el = jnp.exp(al)
            dcb = _dot_nt(g, xdt) * lm
            dcbt = _dot_nt(xdt, g) * lmt
            dseg = dcb * cb
            ch = _dot(Cv, h)
            bdh = _dot(Bv, dh)
            dxdt = _dot(cbt * lmt, g) + dte[:, :P] * bdh
            dc += _dot(dcb, Bv) + e * _dot_nt(g, h)
            db += _dot(dcbt, Cv) + _dot_nt(xdt * dte[:, :P], dh)
            d_e = jnp.sum(g * ch, axis=-1, keepdims=True)
            d_dte = jnp.sum(xdt * bdh, axis=-1, keepdims=True)
            d_el = jnp.sum(h * dh, keepdims=True)
            d_al = jnp.sum(d_dte * dte[:, :1], keepdims=True) + d_el * el[:, :1]
            dac_j = (jnp.sum(dseg, axis=-1, keepdims=True) + d_e * e[:, :1] - d_dte * dte[:, :1]
                     + jnp.where(last, d_al, 0.0))
            dac_all = jnp.where(lane == j, dac_j, dac_all)
            dacr_ref[j] = -jnp.sum(dseg, axis=0, keepdims=True)
            dx_ref[j] = dxdt * dtv[:, :P]
            ddt_all = jnp.where(lane == j, jnp.sum(dxdt * xv, axis=-1, keepdims=True), ddt_all)
            dh_s[j] = el[:, :P] * dh + _dot(ct, e[:, :P] * g)
        dac_ref[...] = dac_all
        ddt_ref[...] = ddt_all
        db_ref[...] = db
        dc_ref[...] = dc

    xs, col, row, bc, st = _ssd_specs(hb, L, P, N, lambda c: nc - 1 - c)
    dx, ddt, dac, dacr, db, dc = pl.pallas_call(
        body, name="ssd_bwd", grid=(G, nc), in_specs=[xs, col, col, row, bc, bc, st, xs],
        out_specs=(xs, col, col, row, bc, bc),
        out_shape=(jax.ShapeDtypeStruct((H, S, P), F32), jax.ShapeDtypeStruct((G, S, hb), F32),
                   jax.ShapeDtypeStruct((G, S, hb), F32), jax.ShapeDtypeStruct((H, 1, S), F32),
                   jax.ShapeDtypeStruct((G, S, N), F32), jax.ShapeDtypeStruct((G, S, N), F32)),
        scratch_shapes=[pltpu.VMEM((hb, N, P), F32)],
        compiler_params=_params(("parallel", "arbitrary")),
    )(x, dt, ac, acr, Bm, Cm, hp, dy)
    return dx, ddt, dac + dacr.reshape(G, hb, S).transpose(0, 2, 1), db, dc


@jax.custom_vjp
def ssd_scan(x, dt, ac, Bm, Cm):
    return ssd_fwd(x, dt, ac, Bm, Cm)[0]


def _ssd_scan_fwd(x, dt, ac, Bm, Cm):
    y, hp = ssd_fwd(x, dt, ac, Bm, Cm)
    return y, (x, dt, ac, Bm, Cm, hp)


def _ssd_scan_bwd(res, dy):
    return ssd_bwd(*res, dy)


ssd_scan.defvjp(_ssd_scan_fwd, _ssd_scan_bwd)


def _vec(v):
    return v.reshape(1, -1)


def _ffn(x, gain, m3, w_gu, gu_row, w_down, x_gu, x_down):
    h, = rowwise_op(_f_modulate, "modulate", 1, out_dtype=BF16)(x, _vec(gain), _vec(m3[0]), _vec(m3[1]))
    gu = mm_op("ffn_gu", out_dtype=BF16, stack=N_CHIPS, b_slot=(gu_row, D_MODEL))(h, w_gu, x_gu)
    a, = rowwise_op(_f_swiglu, "swiglu", 1, out_dtype=BF16)(gu)
    y = mm_op("ffn_down")(a, w_down, x_down)
    return rowwise_op(_f_resid(0.5), "resid_half", 2)(x, y, _vec(m3[2]))[0]


def _rope_tables(positions):
    inv = 1.0 / (ROPE_THETA ** (jnp.arange(0, QK_ROPE, 2, dtype=F32) / QK_ROPE))
    ang = positions.astype(F32)[:, None] * inv
    S = positions.shape[0]
    cosf = jnp.concatenate([jnp.ones((S, QK_NOPE), F32), jnp.cos(ang), jnp.cos(ang)], axis=1)
    sinf = jnp.concatenate([jnp.zeros((S, QK_NOPE), F32), jnp.sin(ang), jnp.sin(ang)], axis=1)
    return cosf, sinf


def _heads_first(t):
    return t.transpose(1, 0, 2).reshape(-1, t.shape[-1])


def _mla(h, cos, sin, P, W, X, j):
    S = h.shape[0]
    lat = mm_op("mla_a")(h, W["mla_w_a"][j], X["mla_w_a"][j])
    q_lat, kv_lat, k_rope = lat[:, :Q_LORA], lat[:, Q_LORA:Q_LORA + KV_LORA], lat[:, Q_LORA + KV_LORA:]
    qn, = rowwise_op(_f_rms, "rms_lat", 1, out_dtype=BF16)(q_lat, _vec(P["mla_q_a_gain"][j]))
    kvn, = rowwise_op(_f_rms, "rms_lat", 1, out_dtype=BF16)(kv_lat, _vec(P["mla_kv_a_gain"][j]))
    q = mm_op("mla_qb", stack=N_CHIPS)(qn, W["mla_w_qb"][j], X["mla_w_qb"][j]).reshape(S, MLA_HEADS, QK_HEAD)
    kv = mm_op("mla_kvb", stack=N_CHIPS)(kvn, W["mla_w_kvb"][j], X["mla_w_kvb"][j]).reshape(S, MLA_HEADS, QK_NOPE + V_HEAD)
    k_nope, v = kv[..., :QK_NOPE], kv[..., QK_NOPE:]
    k = jnp.concatenate([k_nope, jnp.broadcast_to(k_rope[:, None, :], (S, MLA_HEADS, QK_ROPE))], axis=-1)
    head_rope = rowwise_op(_f_head_rope, "head_rope", 3, diff_rows=(True, False, False), n_const=1)
    swap = _rope_swap()
    q, = head_rope(_heads_first(q), cos, sin, _vec(P["mla_q_gain"][j]), swap)
    k, = head_rope(_heads_first(k), cos, sin, _vec(P["mla_k_gain"][j]), swap)
    o = attention(q.reshape(MLA_HEADS, S, QK_HEAD), k.reshape(MLA_HEADS, S, QK_HEAD), v.transpose(1, 0, 2))
    o = o.transpose(1, 0, 2).reshape(S, MLA_HEADS * V_HEAD).astype(BF16)
    return mm_op("mla_o")(o, W["mla_w_o"][j], X["mla_w_o"][j])


def _ssd(h, P, W, X, j):
    S = h.shape[0]
    z = mm_op("ssd_in_z")(h, W["ssd_w_z"][j], X["ssd_w_z"][j])
    xbc = mm_op("ssd_in_xbc")(h, W["ssd_w_xbc"][j], X["ssd_w_xbc"][j])
    dtr = mm_op("ssd_in_dt")(h, W["ssd_w_dt"][j], X["ssd_w_dt"][j])
    xbc = conv_silu(xbc, P["ssd_conv_w"][j], _vec(P["ssd_conv_b"][j]))
    xs =xbc[:, :D_INNER].reshape(S, SSD_HEADS, SSD_HEAD_DIM).transpose(1, 0, 2)
    Bm = xbc[:, D_INNER:D_INNER + SSD_GROUPS * SSD_STATE].reshape(S, SSD_GROUPS, SSD_STATE).transpose(1, 0, 2)
    Cm = xbc[:, D_INNER + SSD_GROUPS * SSD_STATE:].reshape(S, SSD_GROUPS, SSD_STATE).transpose(1, 0, 2)
    dt = jax.nn.softplus(dtr + P["ssd_dt_bias"][j][None, :])
    A = -jnp.exp(P["ssd_a_log"][j])
    a = (dt * A[None, :]).reshape(S // CHUNK, CHUNK, SSD_HEADS)
    ac = jnp.cumsum(a, axis=1).reshape(S, SSD_HEADS)
    by_group = lambda t: t.reshape(S, SSD_GROUPS, SSD_HEADS // SSD_GROUPS).transpose(1, 0, 2)
    y = ssd_scan(xs, by_group(dt), by_group(ac), Bm, Cm)
    y = y + P["ssd_d"][j][:, None, None] * xs
    y = y.transpose(1, 0, 2).reshape(S, D_INNER)
    g, = rowwise_op(_f_gated_norm, "gated_norm", 2, out_dtype=BF16)(y, z, _vec(P["ssd_norm_gain"][j]))
    return mm_op("ssd_out")(g, W["ssd_w_out"][j], X["ssd_w_out"][j])


def trunk(x, mods, P, W, X, positions):
    cos, sin = _rope_tables(positions)
    for i in range(DEPTH):
        m, j = mods[i], i // 2
        x = _ffn(x, P["norm_gain"][i, 0], m[0], W["ffn_w_gu"], (2 * i) * D_MODEL, W["ffn_w_down"][i][0],
                 X["ffn_w_gu"][i][0], X["ffn_w_down"][i][0])
        h, = rowwise_op(_f_modulate, "modulate", 1, out_dtype=BF16)(x, _vec(P["norm_gain"][i, 1]), _vec(m[1, 0]), _vec(m[1, 1]))
        y = _mla(h, cos, sin, P, W, X, j) if i % 2 == 0 else _ssd(h, P, W, X, j)
        x, = rowwise_op(_f_resid(1.0), "resid_full", 2)(x, y, _vec(m[1, 2]))
        x = _ffn(x, P["norm_gain"][i, 2], m[2], W["ffn_w_gu"], (2 * i + 1) * D_MODEL, W["ffn_w_down"][i][1],
                 X["ffn_w_gu"][i][1], X["ffn_w_down"][i][1])
    return x


def stand_ins(W):
    def one(name, w):
        if name in ("mla_w_qb", "mla_w_kvb"):
            return jnp.zeros((N_CHIPS, w.shape[0], w.shape[1] // N_CHIPS), F32)
        return jnp.zeros(w.shape, F32)
    X = {n: jax.tree.map(lambda w, n=n: one(n, w), W[n]) for n in W if n != "ffn_w_gu"}
    g = W["ffn_w_gu"]
    X["ffn_w_gu"] = [[jnp.zeros((g.shape[0], D_MODEL, g.shape[2]), F32) for _ in range(2)] for _ in range(DEPTH)]
    return X


def loss_head(y, target):
    S, D = y.shape
    tm = _row_tile(S, [D])

    def body(y_ref, t_ref, dy_ref, l_ref):
        d = y_ref[...] - t_ref[...]
        dy_ref[...] = d * (1.0 / D)
        part = jnp.sum(d * d, axis=0, keepdims=True) * (0.5 / D)

        @pl.when(pl.program_id(0) == 0)
        def _():
            l_ref[...] = part

        @pl.when(pl.program_id(0) > 0)
        def _():
            l_ref[...] += part

    return pl.pallas_call(
        body, name="loss_head", grid=(S // tm,),
        in_specs=[pl.BlockSpec((tm, D), lambda i: (i, 0))] * 2,
        out_specs=(pl.BlockSpec((tm, D), lambda i: (i, 0)), pl.BlockSpec((1, D), lambda i: (0, 0))),
        out_shape=(jax.ShapeDtypeStruct((S, D), F32), jax.ShapeDtypeStruct((1, D), F32)),
        compiler_params=_params(("arbitrary",)),
    )(y, target)


def _stream_rows(R, C):
    return _div_tile(R, max(16, (1 << 19) // C // 16 * 16), 16)


def adamw(w, m, v, g):
    R, C = w.shape
    tr = _stream_rows(R, C)
    c1 = 1.0 / (1.0 - ADAM_B1 ** ADAM_STEP)
    c2 = 1.0 / (1.0 - ADAM_B2 ** ADAM_STEP)

    def body(w_ref, m_ref, v_ref, g_ref, d_ref, nm_ref, nv_ref):
        gv = g_ref[...]
        nm = ADAM_B1 * m_ref[...] + (1.0 - ADAM_B1) * gv
        nv = ADAM_B2 * v_ref[...] + (1.0 - ADAM_B2) * (gv * gv)
        d_ref[...] = -ADAM_LR * ((nm * c1) / (jnp.sqrt(nv * c2) + ADAM_EPS) + ADAM_WD * w_ref[...])
        nm_ref[...] = nm
        nv_ref[...] = nv

    spec = pl.BlockSpec((tr, C), lambda i: (i, 0))
    return pl.pallas_call(
        body, name="adamw", grid=(R // tr,), in_specs=[spec] * 4, out_specs=(spec,) * 3,
        out_shape=(jax.ShapeDtypeStruct((R, C), F32),) * 3, compiler_params=_params(("parallel",)),
    )(w, m, v, g)


def sum_parts(parts, name, out_dtype=F32):
    R, C = parts[0].shape
    tr = _stream_rows(R, C)
    n = len(parts)

    def body(*refs):
        acc = refs[0][...].astype(F32)
        for r in refs[1:n]:
            acc = acc + r[...].astype(F32)
        refs[n][...] = acc.astype(out_dtype)

    spec = pl.BlockSpec((tr, C), lambda i: (i, 0))
    return pl.pallas_call(
        body, name=name, grid=(R // tr,), in_specs=[spec] * n, out_specs=spec,
        out_shape=jax.ShapeDtypeStruct((R, C), out_dtype), compiler_params=_params(("parallel",)),
    )(*parts)


def sum_own_half(full, theirs, c, name, out_dtype):
    n, R, C = full.shape
    h = R // 2
    tr = _stream_rows(h, C)
    nb = h // tr

    def body(c_ref, a_ref, b_ref, o_ref):
        o_ref[...] = (a_ref[...].astype(F32) + b_ref[...].astype(F32)).astype(out_dtype)

    return pl.pallas_call(
        body, name=name,
        grid_spec=pltpu.PrefetchScalarGridSpec(
            num_scalar_prefetch=1, grid=(n, nb),
            in_specs=[pl.BlockSpec((None, tr, C), lambda p, i, cr: (p, cr[0] * nb + i, 0)),
                      pl.BlockSpec((None, tr, C), lambda p, i, cr: (p, i, 0))],
            out_specs=pl.BlockSpec((None, tr, C), lambda p, i, cr: (p, i, 0))),
        out_shape=jax.ShapeDtypeStruct((n, h, C), out_dtype), compiler_params=_params(("parallel", "parallel")),
    )(jnp.reshape(c, (1,)).astype(jnp.int32), full, theirs)


def sum_slots(a, name, out_dtype=F32):
    n, R, C = a.shape
    tr = _stream_rows(R, C)

    def body(*refs):
        acc = refs[0][...].astype(F32)
        for r in refs[1:n]:
            acc = acc + r[...].astype(F32)
        refs[n][...] = acc.astype(out_dtype)

    return pl.pallas_call(
        body, name=name, grid=(R // tr,),
        in_specs=[pl.BlockSpec((None, tr, C), lambda i, p=p: (p, i, 0)) for p in range(n)],
        out_specs=pl.BlockSpec((tr, C), lambda i: (i, 0)),
        out_shape=jax.ShapeDtypeStruct((R, C), out_dtype), compiler_params=_params(("parallel",)),
    )(*([a] * n))


def _coords():
    return lax.axis_index("x"), lax.axis_index("y"), lax.axis_index("c")


def _other_chips(x, y):
    return [(1 - x, y), (x, 1 - y), (1 - x, 1 - y)]


def _hbm_call(body, name, ins, out_shapes, n_sems):
    return pl.pallas_call(
        body, name=name, out_shape=tuple(out_shapes),
        in_specs=[pl.BlockSpec(memory_space=pl.ANY)] * len(ins),
        out_specs=tuple(pl.BlockSpec(memory_space=pl.ANY) for _ in out_shapes),
        scratch_shapes=[pltpu.SemaphoreType.DMA((n_sems,)), pltpu.SemaphoreType.DMA((n_sems,))],
    )(*ins)


def allgather_small(v, name):
    m_per, n = v.shape

    def body(x_ref, out_ref, send_sems, recv_sems, local_sem):
        x, y, c = _coords()
        me, sibling = (x, y, c), (x, y, 1 - c)
        chips = _other_chips(x, y)

        def rows(px, py, pc):
            return out_ref.at[pl.ds((4 * px + 2 * py + pc) * m_per, m_per), :]

        def copy(k, block, to, src=None):
            return pltpu.make_async_remote_copy(
                src_ref=rows(*block) if src is None else src, dst_ref=rows(*block),
                send_sem=send_sems.at[k], recv_sem=recv_sems.at[k], device_id=to, device_id_type=MESH)

        mine = pltpu.make_async_copy(x_ref, rows(*me), local_sem)
        mine.start()
        first = [copy(0, me, sibling, src=x_ref)]
        first += [copy(1 + j, me, (*chip, c), src=x_ref) for j, chip in enumerate(chips)]
        for cp in first:
            cp.start()
        passed = [copy(4 + j, (*chip, c), sibling) for j, chip in enumerate(chips)]
        for j, chip in enumerate(chips):
            copy(1 + j, (*chip, c), me).wait_recv()
            passed[j].start()
        copy(0, sibling, me).wait_recv()
        for j, chip in enumerate(chips):
            copy(4 + j, (*chip, 1 - c), me).wait_recv()
        for cp in first + passed:
            cp.wait_send()
        mine.wait()

    return pl.pallas_call(
        body, name=name, out_shape=jax.ShapeDtypeStruct((N_DEV * m_per, n), v.dtype),
        in_specs=[pl.BlockSpec(memory_space=pltpu.VMEM)], out_specs=pl.BlockSpec(memory_space=pltpu.VMEM),
        scratch_shapes=[pltpu.SemaphoreType.DMA((7,)), pltpu.SemaphoreType.DMA((7,)), pltpu.SemaphoreType.DMA],
        compiler_params=pltpu.CompilerParams(vmem_limit_bytes=V7X_VMEM_LIMIT),
    )(v)


def allgather_chips(arrs, name):
    n = len(arrs)
    halves = [a.shape[0] // 2 for a in arrs]

    def body(*refs):
        xs, outs = refs[:n], refs[n:2 * n]
        send_sems, recv_sems = refs[2 * n:]
        x, y, c = _coords()
        me, sibling, chips = 2 * x + y, (x, y, 1 - c), _other_chips(x, y)

        def half(ref, cc, i):
            return ref.at[pl.ds(cc * halves[i], halves[i]), :]

        def copy(i, k, src, dst, to):
            return pltpu.make_async_remote_copy(src_ref=src, dst_ref=dst, send_sem=send_sems.at[6 * i + k],
                                                recv_sem=recv_sems.at[6 * i + k], device_id=to, device_id_type=MESH)

        sends = [copy(i, k, half(xs[i], c, i), half(outs[i].at[me], c, i), (*chip, c))
                 for k, chip in enumerate(chips) for i in range(n)]
        for cp in sends:
            cp.start()
        passed = []
        for k, (px, py) in enumerate(chips):
            for i in range(n):
                landed = half(outs[i].at[2 * px + py], c, i)
                copy(i, k, landed, landed, (px, py, c)).wait_recv()
                passed.append(copy(i, 3 + k, landed, landed, sibling))
                passed[-1].start()
        for k, (px, py) in enumerate(chips):
            for i in range(n):
                theirs = half(outs[i].at[2 * px + py], 1 - c, i)
                copy(i, 3 + k, theirs, theirs, sibling).wait_recv()
        for cp in sends + passed:
            cp.wait_send()

    return _hbm_call(body, name, arrs, [jax.ShapeDtypeStruct((N_CHIPS,) + a.shape, a.dtype) for a in arrs], 6 * n)


def pair_exchange(arrs, name):
    n = len(arrs)
    halves = [a.shape[1] // 2 for a in arrs]

    def body(*refs):
        xs, theirs = refs[:n], refs[n:2 * n]
        send_sems, recv_sems = refs[2 * n:]
        x, y, c = _coords()
        cps = [pltpu.make_async_remote_copy(
            src_ref=xs[i].at[:, pl.ds((1 - c) * halves[i], halves[i]), :], dst_ref=theirs[i],
            send_sem=send_sems.at[i], recv_sem=recv_sems.at[i], device_id=(x, y, 1 - c), device_id_type=MESH)
            for i in range(n)]
        for cp in cps:
            cp.start()
        for cp in cps:
            cp.wait()

    shapes = [jax.ShapeDtypeStruct((a.shape[0], a.shape[1] // 2, a.shape[2]), a.dtype) for a in arrs]
    return _hbm_call(body, name, arrs, shapes, n)


def scatter_chips(arrs, name):
    n = len(arrs)

    def body(*refs):
        xs, outs = refs[:n], refs[n:2 * n]
        send_sems, recv_sems = refs[2 * n:]
        x, y, c = _coords()
        me, chips = 2 * x + y, _other_chips(x, y)

        def copy(i, k, src_slot, dst_slot, to):
            return pltpu.make_async_remote_copy(
                src_ref=xs[i].at[src_slot], dst_ref=outs[i].at[dst_slot], send_sem=send_sems.at[3 * i + k],
                recv_sem=recv_sems.at[3 * i + k], device_id=to, device_id_type=MESH)

        sends = [copy(i, k, 2 * px + py, me, (px, py, c)) for k, (px, py) in enumerate(chips) for i in range(n)]
        for cp in sends:
            cp.start()
        for k, (px, py) in enumerate(chips):
            for i in range(n):
                copy(i, k, me, 2 * px + py, (px, py, c)).wait_recv()
        for cp in sends:
            cp.wait_send()

    return _hbm_call(body, name, arrs, [jax.ShapeDtypeStruct(a.shape, a.dtype) for a in arrs], 3 * n)


def pair_allgather(arrs, name):
    n = len(arrs)

    def body(*refs):
        xs, outs = refs[:n], refs[n:2 * n]
        send_sems, recv_sems = refs[2 * n:]
        x, y, c = _coords()
        cps = [pltpu.make_async_remote_copy(
            src_ref=xs[i], dst_ref=outs[i].at[pl.ds(c * xs[i].shape[0], xs[i].shape[0]), :], send_sem=send_sems.at[i],
            recv_sem=recv_sems.at[i], device_id=(x, y, 1 - c), device_id_type=MESH) for i in range(n)]
        for cp in cps:
            cp.start()
        for cp in cps:
            cp.wait()

    return _hbm_call(body, name, arrs, [jax.ShapeDtypeStruct((2 * a.shape[0], a.shape[1]), a.dtype) for a in arrs], n)


GROUPS = (("ffn_w_gu",), ("ffn_w_down", "mla_w_o", "ssd_w_out"), ("mla_w_a",), ("mla_w_qb",), ("mla_w_kvb",), ("ssd_w_in",))


def _pad_rows(a, mult):
    r = (-a.shape[0]) % mult
    return a if r == 0 else jnp.concatenate([a, jnp.zeros((r, a.shape[1]), a.dtype)], axis=0)


def _rows2d(a):
    return a.reshape(-1, a.shape[-1])


def _unstack(st, axis):
    full = jnp.moveaxis(st, 0, axis)
    sh = list(full.shape)
    sh[axis:axis + 2] = [sh[axis] * sh[axis + 1]]
    return full.reshape(sh)


def _stack(full, axis):
    sh = list(full.shape)
    sh[axis:axis + 1] = [N_CHIPS, sh[axis] // N_CHIPS]
    return jnp.moveaxis(full.reshape(sh), axis, 0)


def kernel(x, c, positions, norm_gain, ada_w, ada_b, ffn_w_gu, ffn_w_down, mla_w_a, mla_q_a_gain, mla_kv_a_gain, mla_w_qb, mla_w_kvb, mla_q_gain, mla_k_gain, mla_w_o, ssd_w_in, ssd_conv_w, ssd_conv_b, ssd_dt_bias, ssd_a_log, ssd_d, ssd_norm_gain, ssd_w_out, loss_target, m_norm_gain, m_ada_w, m_ada_b, m_ffn_w_gu, m_ffn_w_down, m_mla_w_a, m_mla_q_a_gain, m_mla_kv_a_gain, m_mla_w_qb, m_mla_w_kvb, m_mla_q_gain, m_mla_k_gain, m_mla_w_o, m_ssd_w_in, m_ssd_conv_w, m_ssd_conv_b, m_ssd_dt_bias, m_ssd_a_log, m_ssd_d, m_ssd_norm_gain, m_ssd_w_out, v_norm_gain, v_ada_w, v_ada_b, v_ffn_w_gu, v_ffn_w_down, v_mla_w_a, v_mla_q_a_gain, v_mla_kv_a_gain, v_mla_w_qb, v_mla_w_kvb, v_mla_q_gain, v_mla_k_gain, v_mla_w_o, v_ssd_w_in, v_ssd_conv_w, v_ssd_conv_b, v_ssd_dt_bias, v_ssd_a_log, v_ssd_d, v_ssd_norm_gain, v_ssd_w_out):
    w_in = dict(norm_gain=norm_gain, ada_w=ada_w, ada_b=ada_b, ffn_w_gu=ffn_w_gu, ffn_w_down=ffn_w_down, mla_w_a=mla_w_a, mla_q_a_gain=mla_q_a_gain, mla_kv_a_gain=mla_kv_a_gain, mla_w_qb=mla_w_qb, mla_w_kvb=mla_w_kvb, mla_q_gain=mla_q_gain, mla_k_gain=mla_k_gain, mla_w_o=mla_w_o, ssd_w_in=ssd_w_in, ssd_conv_w=ssd_conv_w, ssd_conv_b=ssd_conv_b, ssd_dt_bias=ssd_dt_bias, ssd_a_log=ssd_a_log, ssd_d=ssd_d, ssd_norm_gain=ssd_norm_gain, ssd_w_out=ssd_w_out)
    m_in = dict(norm_gain=m_norm_gain, ada_w=m_ada_w, ada_b=m_ada_b, ffn_w_gu=m_ffn_w_gu, ffn_w_down=m_ffn_w_down, mla_w_a=m_mla_w_a, mla_q_a_gain=m_mla_q_a_gain, mla_kv_a_gain=m_mla_kv_a_gain, mla_w_qb=m_mla_w_qb, mla_w_kvb=m_mla_w_kvb, mla_q_gain=m_mla_q_gain, mla_k_gain=m_mla_k_gain, mla_w_o=m_mla_w_o, ssd_w_in=m_ssd_w_in, ssd_conv_w=m_ssd_conv_w, ssd_conv_b=m_ssd_conv_b, ssd_dt_bias=m_ssd_dt_bias, ssd_a_log=m_ssd_a_log, ssd_d=m_ssd_d, ssd_norm_gain=m_ssd_norm_gain, ssd_w_out=m_ssd_w_out)
    v_in = dict(norm_gain=v_norm_gain, ada_w=v_ada_w, ada_b=v_ada_b, ffn_w_gu=v_ffn_w_gu, ffn_w_down=v_ffn_w_down, mla_w_a=v_mla_w_a, mla_q_a_gain=v_mla_q_a_gain, mla_kv_a_gain=v_mla_kv_a_gain, mla_w_qb=v_mla_w_qb, mla_w_kvb=v_mla_w_kvb, mla_q_gain=v_mla_q_gain, mla_k_gain=v_mla_k_gain, mla_w_o=v_mla_w_o, ssd_w_in=v_ssd_w_in, ssd_conv_w=v_ssd_conv_w, ssd_conv_b=v_ssd_conv_b, ssd_dt_bias=v_ssd_dt_bias, ssd_a_log=v_ssd_a_log, ssd_d=v_ssd_d, ssd_norm_gain=v_ssd_norm_gain, ssd_w_out=v_ssd_w_out)
    names = list(w_in)
    xi, yi, ci = _coords()
    chip = 2 * xi + yi
    batch = 4 * xi + 2 * yi + ci
    x2, target = x[0], loss_target[0]

    small_sharded = ("norm_gain", "ssd_conv_w", "ssd_conv_b", "ssd_norm_gain")
    pack0 = jnp.concatenate([c.reshape(-1)] + [w_in[n].reshape(-1) for n in small_sharded])
    pack0 = _pad_rows(pack0.reshape(-1, 128), 8)
    g0 = allgather_small(pack0, "gather_small").reshape(N_DEV, -1)
    c_all = g0[:, :D_MODEL]
    P, off = {}, D_MODEL
    for n in small_sharded:
        sz = w_in[n].size
        st = g0[0::2, off:off + sz].reshape((N_CHIPS,) + w_in[n].shape)
        P[n] = _unstack(st, w_in[n].ndim - 1)
        off += sz
    for n in ("mla_q_a_gain", "mla_kv_a_gain", "mla_q_gain", "mla_k_gain", "ssd_dt_bias", "ssd_a_log", "ssd_d"):
        P[n] = w_in[n]

    sc = _silu(c_all)
    n_ada = ada_w.shape[2]
    b_sh = lax.dynamic_slice_in_dim(ada_b, chip * n_ada, n_ada, axis=1)
    mods_sh = jnp.stack([matmul(sc, ada_w[l], "nn", "ada_fwd") for l in range(DEPTH)]) + b_sh[:, None, :]
    g1 = allgather_small(mods_sh.reshape(-1, 128), "gather_mods").reshape(N_DEV, DEPTH, N_DEV, n_ada)
    mods = lax.dynamic_index_in_dim(g1[0::2], batch, axis=2, keepdims=False)
    mods = mods.transpose(1, 0, 2).reshape(DEPTH, 3, 3, D_MODEL)

    shard_groups = [jnp.concatenate([_rows2d(w_in[n]).astype(BF16) for n in grp], axis=0) for grp in GROUPS]
    gathered = allgather_chips(shard_groups, "gather_weights")
    gathered = [lax.dynamic_update_slice(g, s[None], (chip, 0, 0)) for g, s in zip(gathered, shard_groups)]
    G = {}
    for grp, arr in zip(GROUPS, gathered):
        off = 0
        for n in grp:
            rows = w_in[n].size // w_in[n].shape[-1]
            G[n] = arr[:, off:off + rows].reshape((N_CHIPS,) + w_in[n].shape)
            off += rows
    W = {
        "ffn_w_gu": gathered[0],
        "ffn_w_down": [[_unstack(G["ffn_w_down"][:, i, t], 0) for t in range(2)] for i in range(DEPTH)],
        "mla_w_a": [_unstack(G["mla_w_a"][:, j], 0) for j in range(2)],
        "mla_w_qb": [_unstack(G["mla_w_qb"][:, j], 1) for j in range(2)],
        "mla_w_kvb": [_unstack(G["mla_w_kvb"][:, j], 1) for j in range(2)],
        "mla_w_o": [_unstack(G["mla_w_o"][:, j], 0) for j in range(2)],
        "ssd_w_out": [_unstack(G["ssd_w_out"][:, j], 0) for j in range(2)],
    }
    w_in_full = [_unstack(G["ssd_w_in"][:, j], 1) for j in range(2)]
    W["ssd_w_z"] = [w[:, :D_INNER] for w in w_in_full]
    W["ssd_w_xbc"] = [w[:, D_INNER:D_INNER + CONV_DIM] for w in w_in_full]
    W["ssd_w_dt"] = [w[:, D_INNER + CONV_DIM:] for w in w_in_full]
    X = stand_ins(W)

    pos = positions[0]
    y, vjp = jax.vjp(lambda a, b, p_, x_: trunk(a, b, p_, W, x_, pos), x2, mods, P, X)
    dy, loss_cols = loss_head(y, target)
    dx, dmods, dP, dX = vjp(dy)
    loss = lax.psum(jnp.sum(loss_cols), ("x", "y", "c"))

    small_names = ("norm_gain", "ssd_conv_w", "ssd_conv_b", "ssd_norm_gain", "mla_q_a_gain", "mla_kv_a_gain",
                   "mla_q_gain", "mla_k_gain", "ssd_dt_bias", "ssd_a_log", "ssd_d")
    pack1 = jnp.concatenate([dmods.reshape(-1)] + [dP[n].reshape(-1) for n in small_names])
    pack1 = _pad_rows(jnp.pad(pack1, (0, (-pack1.size) % 128)).reshape(-1, 128), 8)
    rows1 = pack1.shape[0]
    g2 = allgather_small(pack1, "gather_small_grads")
    tot = sum_slots(g2.reshape(N_DEV, rows1, 128), "sum_small_grads").reshape(-1)
    n_mod = DEPTH * 9 * D_MODEL
    grads = {"ada_b": tot[:n_mod].reshape(DEPTH, 9 * D_MODEL)}
    off = n_mod
    for n in small_names:
        sz = dP[n].size
        full = tot[off:off + sz].reshape(dP[n].shape)
        off += sz
        if n in small_sharded:
            k = w_in[n].shape[-1]
            full = lax.dynamic_slice_in_dim(full, chip * k, k, axis=full.ndim - 1)
        grads[n] = full
    dmods_all = g2.reshape(N_DEV, -1)[:, :n_mod].reshape(N_DEV, DEPTH, 9 * D_MODEL)
    dm_sh = lax.dynamic_slice_in_dim(dmods_all, chip * n_ada, n_ada, axis=2)
    grads["ada_w"] = jnp.stack([matmul(sc, dm_sh[:, l], "tn", "ada_dw") for l in range(DEPTH)])

    w_in_g = [jnp.concatenate([dX["ssd_w_z"][j], dX["ssd_w_xbc"][j], dX["ssd_w_dt"][j]], axis=1) for j in range(2)]
    per_name = {
        "ffn_w_gu": [dX["ffn_w_gu"][i][t] for i in range(DEPTH) for t in range(2)],
        "ffn_w_down": [dX["ffn_w_down"][i][t].reshape(N_CHIPS, -1, D_MODEL) for i in range(DEPTH) for t in range(2)],
        "mla_w_a": [g.reshape(N_CHIPS, -1, g.shape[-1]) for g in dX["mla_w_a"]],
        "mla_w_qb": dX["mla_w_qb"], "mla_w_kvb": dX["mla_w_kvb"],
        "mla_w_o": [g.reshape(N_CHIPS, -1, D_MODEL) for g in dX["mla_w_o"]],
        "ssd_w_out": [g.reshape(N_CHIPS, -1, D_MODEL) for g in dX["ssd_w_out"]],
        "ssd_w_in": [_stack(g, 1) for g in w_in_g],
    }
    grad_groups = [jnp.concatenate([p for n in grp for p in per_name[n]], axis=1) for grp in GROUPS]
    theirs = pair_exchange(grad_groups, "grads_to_sibling")
    pair = [sum_own_half(a, b, ci, "sum_pair", BF16) for a, b in zip(grad_groups, theirs)]
    landed = scatter_chips(pair, "grads_to_chips")
    landed = [lax.dynamic_update_slice(a, lax.dynamic_slice_in_dim(p, chip, 1, axis=0), (chip, 0, 0)) for a, p in zip(landed, pair)]
    half_sums = [sum_slots(a, "sum_chips") for a in landed]
    totals = pair_allgather(half_sums, "grad_halves_swap")
    totals = [lax.dynamic_update_slice(t, h, (ci * h.shape[0], 0)) for t, h in zip(totals, half_sums)]
    for grp, arr in zip(GROUPS, totals):
        off = 0
        for n in grp:
            rows = w_in[n].size // w_in[n].shape[-1]
            grads[n] = arr[off:off + rows].reshape(w_in[n].shape)
            off += rows

    deltas, new_m, new_v = {}, {}, {}
    for n in names:
        w = w_in[n]
        d, nm, nv = adamw(_rows2d(w), _rows2d(m_in[n]), _rows2d(v_in[n]), _rows2d(grads[n]))
        deltas[n], new_m[n], new_v[n] = d.reshape(w.shape), nm.reshape(w.shape), nv.reshape(w.shape)

    return (loss, dx[None], *[grads[n] for n in names], *[deltas[n] for n in names],
            *[new_m[n] for n in names], *[new_v[n] for n in names])
```

```python
import numpy as np

import jax
import jax.numpy as jnp
from jax import lax
from jax.experimental import pallas as pl
from jax.experimental.pallas import tpu as pltpu

F32 = jnp.float32
BF16 = jnp.bfloat16
MESH = pl.DeviceIdType.MESH

D_MODEL = 1024
DEPTH = 4
EPS = 1e-6
D_FF = 2816
MLA_HEADS = 16
Q_LORA = 384
KV_LORA = 256
QK_NOPE = 64
QK_ROPE = 32
QK_HEAD = QK_NOPE + QK_ROPE
V_HEAD = 64
ROPE_THETA = 10000.0
D_INNER = 2048
SSD_HEAD_DIM = 64
SSD_HEADS = 32
SSD_GROUPS = 4
SSD_STATE = 128
CONV_WIDTH = 4
CHUNK = 128
CONV_DIM = D_INNER + 2 * SSD_GROUPS * SSD_STATE
ADAM_LR = 0.001
ADAM_B1 = 0.9
ADAM_B2 = 0.999
ADAM_EPS = 1e-08
ADAM_WD = 0.01
ADAM_STEP = 10

N_CHIPS = 4
N_DEV = 8
V7X_VMEM_LIMIT = 56 * 1024 * 1024
ATTN_HEADS_PER_STEP = 4
LANES = 128
LOG2E = 1.4426950408889634


def _params(sem=None):
    return pltpu.CompilerParams(dimension_semantics=sem, vmem_limit_bytes=V7X_VMEM_LIMIT)


def _div_tile(n, pref, quantum):
    if n <= pref:
        return n
    t = (pref // quantum) * quantum
    while t >= quantum:
        if n % t == 0:
            return t
        t -= quantum
    return n


def matmul(a, b, mode, name, out_dtype=F32, stack=0, b_slot=None, tm=1024, tn=1408, tk=2816):
    if b_slot is not None:
        (roff, rows), (nsl, _, n) = b_slot, b.shape
        (M, K) = a.shape
        K2, N = (rows, nsl * n) if mode == "nn" else (nsl * n, rows)
    elif mode == "nn":
        (M, K), (K2, N) = a.shape, b.shape
    elif mode == "nt":
        (M, K), (N, K2) = a.shape, b.shape
    else:
        (K, M), (K2, N) = a.shape, b.shape
    assert K == K2, (a.shape, b.shape, mode)
    tm = _div_tile(M, tm, 128 if mode == "tn" else 16)
    tn = _div_tile(N // stack if stack else (n if b_slot and mode == "nn" else N), tn, 128)
    tk = _div_tile(n if b_slot and mode == "nt" else K, tk, 128)
    nk = K // tk
    if mode == "nn":
        a_spec = pl.BlockSpec((tm, tk), lambda i, j, k: (i, k))
        b_spec = pl.BlockSpec((tk, tn), lambda i, j, k: (k, j))
        dims = (((1,), (0,)), ((), ()))
        if b_slot:
            assert roff % tk == 0
            nbn, rb = n // tn, roff // tk
            b_spec = pl.BlockSpec((None, tk, tn), lambda i, j, k: (j // nbn, rb + k, j % nbn))
    elif mode == "nt":
        a_spec = pl.BlockSpec((tm, tk), lambda i, j, k: (i, k))
        b_spec = pl.BlockSpec((tn, tk), lambda i, j, k: (j, k))
        dims = (((1,), (1,)), ((), ()))
        if b_slot:
            assert roff % tn == 0
            nbk, rb = n // tk, roff // tn
            b_spec = pl.BlockSpec((None, tn, tk), lambda i, j, k: (k // nbk, rb + j, k % nbk))
    else:
        a_spec = pl.BlockSpec((tk, tm), lambda i, j, k: (k, i))
        b_spec = pl.BlockSpec((tk, tn), lambda i, j, k: (k, j))
        dims = (((0,), (0,)), ((), ()))
    if stack:
        nb = N // stack // tn
        out_spec = pl.BlockSpec((None, tm, tn), lambda i, j, k: (j // nb, i, j % nb))
        out_shape = jax.ShapeDtypeStruct((stack, M, N // stack), out_dtype)
    else:
        out_spec = pl.BlockSpec((tm, tn), lambda i, j, k: (i, j))
        out_shape = jax.ShapeDtypeStruct((M, N), out_dtype)
    use_acc = nk > 1 and out_dtype != F32

    def body(a_ref, b_ref, o_ref, *acc):
        p = lax.dot_general(a_ref[...].astype(BF16), b_ref[...].astype(BF16), dims, preferred_element_type=F32)
        if nk == 1:
            o_ref[...] = p.astype(out_dtype)
            return
        acc_ref = acc[0] if use_acc else o_ref
        k = pl.program_id(2)

        @pl.when(k == 0)
        def _():
            acc_ref[...] = p

        @pl.when(k > 0)
        def _():
            acc_ref[...] += p

        if use_acc:
            @pl.when(k == nk - 1)
            def _():
                o_ref[...] = acc_ref[...].astype(out_dtype)

    return pl.pallas_call(
        body, name=name, grid=(M // tm, N // tn, nk), in_specs=[a_spec, b_spec], out_specs=out_spec, out_shape=out_shape,
        scratch_shapes=[pltpu.VMEM((tm, tn), F32)] if use_acc else [],
        compiler_params=_params(("parallel", "parallel", "arbitrary")),
    )(a, b)


def mm_op(name, out_dtype=F32, stack=0, b_slot=None):
    @jax.custom_vjp
    def op(a, w, wp):
        return matmul(a, w, "nn", name + "_fwd", out_dtype=out_dtype, b_slot=b_slot)

    def fwd(a, w, wp):
        return op(a, w, wp), (a, w)

    def bwd(res, g):
        a, w = res
        return (matmul(g, w, "nt", name + "_dx", out_dtype=a.dtype, b_slot=b_slot), jnp.zeros_like(w),
                matmul(a, g, "tn", name + "_dw", stack=stack, tn=1408, tk=1024))

    op.defvjp(fwd, bwd)
    return op


def _row_tile(rows, widths):
    w = max(widths)
    t = 128 if w > 4096 else (256 if w > 1024 else (512 if w > 128 else 2048))
    return _div_tile(rows, t, 16)


def _row_spec(r, tm):
    nb = r.shape[0] // tm
    return pl.BlockSpec((tm, r.shape[1]), lambda i: (i % nb, 0))


def _rowwise_fwd(f, rows, vecs, name, out_dtype):
    n_r, n_v = len(rows), len(vecs)
    S = rows[0].shape[0]
    tm = _row_tile(min(r.shape[0] for r in rows), [r.shape[1] for r in rows])
    outs = jax.eval_shape(f, *[jax.ShapeDtypeStruct((tm, r.shape[1]), F32) for r in rows], *vecs)

    def body(*refs):
        res = f(*[r[...].astype(F32) for r in refs[: n_r + n_v]])
        for o, r in zip(refs[n_r + n_v:], res):
            o[...] = r.astype(out_dtype)

    return pl.pallas_call(
        body, name=name, grid=(S // tm,),
        in_specs=[_row_spec(r, tm) for r in rows] + [pl.BlockSpec(v.shape, lambda i: (0, 0)) for v in vecs],
        out_specs=tuple(pl.BlockSpec((tm, o.shape[1]), lambda i: (i, 0)) for o in outs),
        out_shape=tuple(jax.ShapeDtypeStruct((S, o.shape[1]), out_dtype) for o in outs),
        compiler_params=_params(("parallel",)),
    )(*rows, *vecs)


def _rowwise_bwd(f, rows, vecs, douts, diff_rows, n_const, name):
    n_r, n_o = len(rows), len(douts)
    consts, vecs = vecs[len(vecs) - n_const:], vecs[:len(vecs) - n_const]
    n_v = len(vecs)
    S = rows[0].shape[0]
    tm = _row_tile(min(r.shape[0] for r in rows), [r.shape[1] for r in rows] + [d.shape[1] for d in douts])
    d_idx = [i for i in range(n_r) if diff_rows[i]]

    def body(*refs):
        row_v = [r[...].astype(F32) for r in refs[:n_r]]
        vec_v = [r[...] for r in refs[n_r:n_r + n_v]]
        const_v = [r[...] for r in refs[n_r + n_v:n_r + n_v + n_const]]
        dout_v = tuple(r[...].astype(F32) for r in refs[n_r + n_v + n_const:n_r + n_v + n_const + n_o])
        out_refs = refs[n_r + n_v + n_const + n_o:]

        def g(*args):
            full = list(row_v)
            for j, i in enumerate(d_idx):
                full[i] = args[j]
            return f(*full, *args[len(d_idx):], *const_v)

        _, vjp = jax.vjp(g, *[row_v[i] for i in d_idx], *vec_v)
        grads = vjp(dout_v)
        for j in range(len(d_idx)):
            out_refs[j][...] = grads[j].astype(out_refs[j].dtype)
        step = pl.program_id(0)
        for j in range(n_v):
            gv, o = grads[len(d_idx) + j], out_refs[len(d_idx) + j]

            @pl.when(step == 0)
            def _(gv=gv, o=o):
                o[...] = gv

            @pl.when(step > 0)
            def _(gv=gv, o=o):
                o[...] += gv

    res = pl.pallas_call(
        body, name=name, grid=(S // tm,),
        in_specs=[_row_spec(r, tm) for r in rows] + [pl.BlockSpec(v.shape, lambda i: (0, 0)) for v in vecs + consts]
        + [pl.BlockSpec((tm, d.shape[1]), lambda i: (i, 0)) for d in douts],
        out_specs=tuple([pl.BlockSpec((tm, rows[i].shape[1]), lambda i_: (i_, 0)) for i in d_idx]
                        + [pl.BlockSpec(v.shape, lambda i: (0, 0)) for v in vecs]),
        out_shape=tuple([jax.ShapeDtypeStruct(rows[i].shape, rows[i].dtype) for i in d_idx]
                        + [jax.ShapeDtypeStruct(v.shape, F32) for v in vecs]),
        compiler_params=_params(("arbitrary",)),
    )(*rows, *vecs, *consts, *douts)
    drows = [None] * n_r
    for j, i in enumerate(d_idx):
        drows[i] = res[j]
    for i in range(n_r):
        if drows[i] is None:
            drows[i] = jnp.zeros_like(rows[i])
    return tuple(drows) + tuple(res[len(d_idx):]) + tuple(jnp.zeros_like(k) for k in consts)


def rowwise_op(f, name, n_rows, diff_rows=None, out_dtype=F32, n_const=0):
    diff = tuple(diff_rows) if diff_rows is not None else (True,) * n_rows

    @jax.custom_vjp
    def op(*args):
        return _rowwise_fwd(f, args[:n_rows], args[n_rows:], name + "_fwd", out_dtype)

    def fwd(*args):
        return op(*args), args

    def bwd(args, douts):
        return _rowwise_bwd(f, args[:n_rows], args[n_rows:], douts, diff, n_const, name + "_bwd")

    op.defvjp(fwd, bwd)
    return op


def _rms(x, gain):
    return x * lax.rsqrt(jnp.mean(x * x, axis=-1, keepdims=True) + EPS) * gain


def _silu(x):
    return x * jax.nn.sigmoid(x)


def _f_modulate(x, gain, shift, scale):
    return (_rms(x, gain) * (1.0 + scale) + shift,)


def _f_rms(x, gain):
    return (_rms(x, gain),)


def _f_swiglu(gu):
    n = gu.shape[1] // 2
    return (_silu(gu[:, :n]) * gu[:, n:],)


def _f_resid(coef):
    def f(x, y, gate):
        return (x + (coef * gate) * y,)
    return f


def _f_head_rope(x, cosf, sinf, gain, swap):
    y = _rms(x, gain)
    return (y * cosf + jnp.dot(y, swap, precision=lax.Precision.HIGHEST, preferred_element_type=F32) * sinf,)


def _rope_swap():
    m = np.zeros((QK_HEAD, QK_HEAD), np.float32)
    half = QK_ROPE // 2
    for i in range(half):
        m[QK_NOPE + half + i, QK_NOPE + i] = -1.0
        m[QK_NOPE + i, QK_NOPE + half + i] = 1.0
    return jnp.asarray(m)


def _f_gated_norm(y, z, gain):
    g = y * _silu(z)
    n = g.shape[1] // SSD_GROUPS
    return (jnp.concatenate([_rms(g[:, i * n:(i + 1) * n], gain[:, i * n:(i + 1) * n]) for i in range(SSD_GROUPS)], axis=1),)


HALO = 8


def _conv_taps(ext, w, rows, off):
    acc = None
    for k in range(CONV_WIDTH):
        term = w[k:k + 1, :] * pltpu.roll(ext, CONV_WIDTH - 1 - k, 0)[off:off + rows]
        acc = term if acc is None else acc + term
    return acc


def _conv_specs(S, tm, tc):
    tile = pl.BlockSpec((tm, tc), lambda j, i: (i, j))
    prev = pl.BlockSpec((HALO, tc), lambda j, i: (jnp.maximum(i * (tm // HALO) - 1, 0), j))
    nxt = pl.BlockSpec((HALO, tc), lambda j, i: (jnp.minimum((i + 1) * (tm // HALO), S // HALO - 1), j))
    wspec = pl.BlockSpec((CONV_WIDTH, tc), lambda j, i: (0, j))
    bspec = pl.BlockSpec((1, tc), lambda j, i: (0, j))
    return tile, prev, nxt, wspec, bspec


def conv_silu_fwd(u, w, b, tm=512, tc=1024):
    S, C = u.shape
    tm = _div_tile(S, tm, HALO)

    def body(u_ref, p_ref, w_ref, b_ref, o_ref):
        prev = jnp.where(pl.program_id(1) == 0, 0.0, p_ref[...])
        ext = jnp.concatenate([prev, u_ref[...]], axis=0)
        conv = _conv_taps(ext, w_ref[...], tm, HALO) + b_ref[...]
        o_ref[...] = conv * jax.nn.sigmoid(conv)

    tile, prev, _, wspec, bspec = _conv_specs(S, tm, tc)
    return pl.pallas_call(
        body, name="conv_silu_fwd", grid=(C // tc, S // tm), in_specs=[tile, prev, wspec, bspec], out_specs=tile,
        out_shape=jax.ShapeDtypeStruct((S, C), F32), compiler_params=_params(("parallel", "arbitrary")),
    )(u, u, w, b)


def conv_silu_bwd(u, w, b, dout, tm=512, tc=1024):
    S, C = u.shape
    tm = _div_tile(S, tm, HALO)
    n = S // tm
    ext_rows = tm + HALO

    def body(u_ref, p_ref, n_ref, g_ref, gn_ref, w_ref, b_ref, du_ref, dw_ref, db_ref):
        i = pl.program_id(1)
        wv = w_ref[...]
        prev = jnp.where(i == 0, 0.0, p_ref[...])
        ext = jnp.concatenate([prev, u_ref[...], n_ref[...]], axis=0)
        conv = _conv_taps(ext, wv, ext_rows, HALO) + b_ref[...]
        g_ext = jnp.concatenate([g_ref[...], jnp.where(i == n - 1, 0.0, gn_ref[...])], axis=0)
        sg = jax.nn.sigmoid(conv)
        dconv = g_ext * (sg * (1.0 + conv * (1.0 - sg)))
        du = None
        for k in range(CONV_WIDTH):
            s = CONV_WIDTH - 1 - k
            term = wv[k:k + 1, :] * pltpu.roll(dconv, (ext_rows - s) % ext_rows, 0)[:tm]
            du = term if du is None else du + term
        du_ref[...] = du
        dc = dconv[:tm]
        dw = jnp.concatenate([jnp.sum(dc * pltpu.roll(ext, CONV_WIDTH - 1 - k, 0)[HALO:HALO + tm], axis=0, keepdims=True)
                              for k in range(CONV_WIDTH)], axis=0)
        dbv = jnp.sum(dc, axis=0, keepdims=True)

        @pl.when(i == 0)
        def _():
            dw_ref[...] = dw
            db_ref[...] = dbv

        @pl.when(i > 0)
        def _():
            dw_ref[...] += dw
            db_ref[...] += dbv

    tile, prev, nxt, wspec, bspec = _conv_specs(S, tm, tc)
    return pl.pallas_call(
        body, name="conv_silu_bwd", grid=(C // tc, n), in_specs=[tile, prev, nxt, tile, nxt, wspec, bspec],
        out_specs=(tile, wspec, bspec),
        out_shape=(jax.ShapeDtypeStruct((S, C), F32), jax.ShapeDtypeStruct((CONV_WIDTH, C), F32),
                   jax.ShapeDtypeStruct((1, C), F32)),
        compiler_params=_params(("parallel", "arbitrary")),
    )(u, u, u, dout, dout, w, b)


@jax.custom_vjp
def conv_silu(u, w, b):
    return conv_silu_fwd(u, w, b)


def _conv_silu_fwd(u, w, b):
    return conv_silu_fwd(u, w, b), (u, w, b)


def _conv_silu_bwd(res, dout):
    return conv_silu_bwd(*res, dout)


conv_silu.defvjp(_conv_silu_fwd, _conv_silu_bwd)


_NT = (((1,), (1,)), ((), ()))


def _dot(a, b):
    return jnp.dot(a.astype(BF16), b.astype(BF16), preferred_element_type=F32)


def _dot_nt(a, b):
    return lax.dot_general(a.astype(BF16), b.astype(BF16), _NT, preferred_element_type=F32)


def _attn_tile(S):
    return _div_tile(S, 512, 128)


def _causal(t, transposed=False):
    r = lax.broadcasted_iota(jnp.int32, (t, t), 0)
    c = lax.broadcasted_iota(jnp.int32, (t, t), 1)
    return r <= c if transposed else r >= c


def _tri_tables(n, by_key):
    if by_key:
        pairs = [(i, j) for j in range(n) for i in range(j, n)]
    else:
        pairs = [(i, j) for i in range(n) for j in range(i + 1)]
    return (jnp.asarray(np.array([p[0] for p in pairs], np.int32)), jnp.asarray(np.array([p[1] for p in pairs], np.int32)))


def attn_fwd(q, k, v):
    H, S, dk = q.shape
    dv = v.shape[-1]
    t, hb = _attn_tile(S), ATTN_HEADS_PER_STEP
    n = S // t
    scale = dk ** -0.5
    qi_tab, kj_tab = _tri_tables(n, by_key=False)

    def body(qi_ref, kj_ref, q_ref, k_ref, v_ref, o_ref, lse_ref, m_s, l_s, acc_s):
        qi, kj = qi_ref[pl.program_id(1)], kj_ref[pl.program_id(1)]

        @pl.when(kj == 0)
        def _():
            m_s[...] = jnp.full(m_s.shape, -jnp.inf, F32)
            l_s[...] = jnp.zeros(l_s.shape, F32)
            acc_s[...] = jnp.zeros(acc_s.shape, F32)

        def step(masked):
            ss = [_dot_nt(q_ref[j], k_ref[j]) for j in range(hb)]
            new = []
            for j in range(hb):
                s = ss[j] * (scale * LOG2E)
                if masked:
                    s = jnp.where(_causal(t), s, -jnp.inf)
                m_old = m_s[j]
                m_new = jnp.maximum(m_old, jnp.max(s, axis=-1, keepdims=True))
                alpha = jnp.exp2(m_old - m_new)
                p = jnp.exp2(s - jnp.tile(m_new, (1, t // LANES)))
                new.append((m_new, alpha * l_s[j] + jnp.sum(p, axis=-1, keepdims=True),
                            alpha[:, :dv] * acc_s[j] + _dot(p, v_ref[j])))
            for j in range(hb):
                m_s[j], l_s[j], acc_s[j] = new[j]

        @pl.when(kj < qi)
        def _():
            step(False)

        @pl.when(kj == qi)
        def _():
            step(True)
            l = l_s[...]
            o_ref[...] = acc_s[...] / l[:, :, :dv]
            lse_ref[...] = m_s[...] + jnp.log2(l)

    qmap = lambda h, s, qi, kj: (h, qi[s], 0)
    kmap = lambda h, s, qi, kj: (h, kj[s], 0)
    return pl.pallas_call(
        body, name="attn_fwd",
        grid_spec=pltpu.PrefetchScalarGridSpec(
            num_scalar_prefetch=2, grid=(H // hb, qi_tab.shape[0]),
            in_specs=[pl.BlockSpec((hb, t, dk), qmap), pl.BlockSpec((hb, t, dk), kmap), pl.BlockSpec((hb, t, dv), kmap)],
            out_specs=(pl.BlockSpec((hb, t, dv), qmap), pl.BlockSpec((hb, t, LANES), qmap)),
            scratch_shapes=[pltpu.VMEM((hb, t, LANES), F32), pltpu.VMEM((hb, t, LANES), F32), pltpu.VMEM((hb, t, dv), F32)]),
        out_shape=(jax.ShapeDtypeStruct((H, S, dv), F32), jax.ShapeDtypeStruct((H, S, LANES), F32)),
        compiler_params=_params(("parallel", "arbitrary")),
    )(qi_tab, kj_tab, q, k, v)


def attn_bwd(q, k, v, o, lse, do):
    H, S, dk = q.shape
    dv = v.shape[-1]
    t, hb = _attn_tile(S), ATTN_HEADS_PER_STEP
    n = S // t
    scale = dk ** -0.5
    hb = min(hb, H)
    lse_r = lse[:, :, 0].reshape(H, 1, S)
    qi_tab, kj_tab = _tri_tables(n, by_key=True)
    tn_dims = (((0,), (0,)), ((), ()))

    def body(qi_ref, kj_ref, q_ref, k_ref, v_ref, o_ref, do_ref, lse_ref, dq_ref, dk_ref, dv_ref):
        pair = pl.program_id(1)
        qi, kj = qi_ref[pair], kj_ref[pair]

        @pl.when(pair == 0)
        def _():
            dq_ref[...] = jnp.zeros(dq_ref.shape, F32)

        ones = jnp.ones((8, dv), F32)

        def step(masked):
            sts = [_dot_nt(k_ref[j], q_ref[j]) for j in range(hb)]
            for j in range(hb):
                dof = do_ref[j]
                dob = dof.astype(BF16)
                delta = lax.dot_general(ones, dof * o_ref[j], _NT, precision=lax.Precision.HIGHEST,
                                        preferred_element_type=F32)[0:1]
                pt = jnp.exp2(sts[j] * (scale * LOG2E) - lse_ref[j])
                if masked:
                    pt = jnp.where(_causal(t, transposed=True), pt, 0.0)
                dvj = _dot(pt, dob)
                dst = (pt * (_dot_nt(v_ref[j], dob) - delta)).astype(BF16)
                dkj = _dot(dst, q_ref[j]) * scale
                rows = pl.ds(pl.multiple_of(qi * t, t), t)
                dq_ref[j, rows, :] += lax.dot_general(dst, k_ref[j], tn_dims, preferred_element_type=F32) * scale
                if masked:
                    dv_ref[j] = dvj
                    dk_ref[j] = dkj
                else:
                    dv_ref[j] += dvj
                    dk_ref[j] += dkj

        @pl.when(qi == kj)
        def _():
            step(True)

        @pl.when(qi > kj)
        def _():
            step(False)

    qmap = lambda h, s, qi, kj: (h, qi[s], 0)
    kmap = lambda h, s, qi, kj: (h, kj[s], 0)
    rowq = lambda h, s, qi, kj: (h, 0, qi[s])
    whole = lambda h, s, qi, kj: (h, 0, 0)
    return pl.pallas_call(
        body, name="attn_bwd",
        grid_spec=pltpu.PrefetchScalarGridSpec(
            num_scalar_prefetch=2, grid=(H // hb, qi_tab.shape[0]),
            in_specs=[pl.BlockSpec((hb, t, dk), qmap), pl.BlockSpec((hb, t, dk), kmap), pl.BlockSpec((hb, t, dv), kmap),
                      pl.BlockSpec((hb, t, dv), qmap), pl.BlockSpec((hb, t, dv), qmap), pl.BlockSpec((hb, 1, t), rowq)],
            out_specs=(pl.BlockSpec((hb, S, dk), whole), pl.BlockSpec((hb, t, dk), kmap), pl.BlockSpec((hb, t, dv), kmap))),
        out_shape=(jax.ShapeDtypeStruct((H, S, dk), F32), jax.ShapeDtypeStruct((H, S, dk), F32),
                   jax.ShapeDtypeStruct((H, S, dv), F32)),
        compiler_params=_params(("parallel", "arbitrary")),
    )(qi_tab, kj_tab, q, k, v, o, do, lse_r)


@jax.custom_vjp
def attention(q, k, v):
    return attn_fwd(q.astype(BF16), k.astype(BF16), v.astype(BF16))[0]


def _attention_fwd(q, k, v):
    qb, kb, vb = q.astype(BF16), k.astype(BF16), v.astype(BF16)
    o, lse = attn_fwd(qb, kb, vb)
    return o, (qb, kb, vb, o, lse)


def _attention_bwd(res, do):
    return attn_bwd(*res, do)


attention.defvjp(_attention_fwd, _attention_bwd)


def _ssd_specs(hb, L, P, N, order):
    xs = pl.BlockSpec((hb, L, P), lambda g, c: (g, order(c), 0))
    col = pl.BlockSpec((None, L, hb), lambda g, c: (g, order(c), 0))
    row = pl.BlockSpec((hb, 1, L), lambda g, c: (g, 0, order(c)))
    bc = pl.BlockSpec((None, L, N), lambda g, c: (g, order(c), 0))
    st = pl.BlockSpec((hb, None, N, P), lambda g, c: (g, order(c), 0, 0))
    return xs, col, row, bc, st


def _ssd_rows(cols, H):
    G, S, hb = cols.shape
    return cols.transpose(0, 2, 1).reshape(H, 1, S)


def ssd_fwd(x, dt, ac, Bm, Cm):
    H, S, P = x.shape
    G, _, N = Bm.shape
    hb, L = H // G, CHUNK
    nc = S // L
    acr = _ssd_rows(ac, H)

    def body(x_ref, dt_ref, ac_ref, acr_ref, b_ref, c_ref, y_ref, hp_ref, h_s):
        @pl.when(pl.program_id(1) == 0)
        def _():
            h_s[...] = jnp.zeros((hb, N, P), F32)

        Bv, Cv = b_ref[...], c_ref[...]
        cb = _dot_nt(Cv, Bv)
        bt = Bv.T
        mask = _causal(L)
        ac_all, dt_all = ac_ref[...], dt_ref[...]
        for j in range(hb):
            a = jnp.broadcast_to(ac_all[:, j:j + 1], (L, L))
            dtv = jnp.broadcast_to(dt_all[:, j:j + 1], (L, L))
            lm = jnp.exp(jnp.where(mask, a - acr_ref[j], -jnp.inf))
            xdt = x_ref[j] * dtv[:, :P]
            h = h_s[j]
            hp_ref[j] = h
            y_ref[j] = _dot(cb * lm, xdt) + jnp.exp(a)[:, :P] * _dot(Cv, h)
            al = a[L - 1:L, :]
            h_s[j] = jnp.exp(al)[:, :P] * h + _dot(bt, xdt * jnp.exp(al - a)[:, :P])

    xs, col, row, bc, st = _ssd_specs(hb, L, P, N, lambda c: c)
    return pl.pallas_call(
        body, name="ssd_fwd", grid=(G, nc), in_specs=[xs, col, col, row, bc, bc], out_specs=(xs, st),
        out_shape=(jax.ShapeDtypeStruct((H, S, P), F32), jax.ShapeDtypeStruct((H, nc, N, P), F32)),
        scratch_shapes=[pltpu.VMEM((hb, N, P), F32)],
        compiler_params=_params(("parallel", "arbitrary")),
    )(x, dt, ac, acr, Bm, Cm)


def ssd_bwd(x, dt, ac, Bm, Cm, hp, dy):
    H, S, P = x.shape
    G, _, N = Bm.shape
    hb, L = H // G, CHUNK
    nc = S // L
    acr = _ssd_rows(ac, H)

    def body(x_ref, dt_ref, ac_ref, acr_ref, b_ref, c_ref, hp_ref, dy_ref,
             dx_ref, ddt_ref, dac_ref, dacr_ref, db_ref, dc_ref, dh_s):
        @pl.when(pl.program_id(1) == 0)
        def _():
            dh_s[...] = jnp.zeros((hb, N, P), F32)

        Bv, Cv = b_ref[...], c_ref[...]
        cb = _dot_nt(Cv, Bv)
        cbt = _dot_nt(Bv, Cv)
        ct = Cv.T
        mask, maskt = _causal(L), _causal(L, transposed=True)
        last = lax.broadcasted_iota(jnp.int32, (L, 1), 0) == L - 1
        lane = lax.broadcasted_iota(jnp.int32, (L, hb), 1)
        db = jnp.zeros((L, N), F32)
        dc = jnp.zeros((L, N), F32)
        dac_all = jnp.zeros((L, hb), F32)
        ddt_all = jnp.zeros((L, hb), F32)
        ac_all, dt_all = ac_ref[...], dt_ref[...]
        for j in range(hb):
            ar, xv, g, h, dh = acr_ref[j], x_ref[j], dy_ref[j], hp_ref[j], dh_s[j]
            a = jnp.broadcast_to(ac_all[:, j:j + 1], (L, L))
            dtv = jnp.broadcast_to(dt_all[:, j:j + 1], (L, L))
            lm = jnp.exp(jnp.where(mask, a - ar, -jnp.inf))
            lmt = jnp.exp(jnp.where(maskt, ar - a, -jnp.inf))
            xdt = xv * dtv[:, :P]
            e = jnp.exp(a)
            al = a[L - 1:L, :]
            dte = jnp.exp(al - a)
            el = jnp.exp(al)
            dcb = _dot_nt(g, xdt) * lm
            dcbt = _dot_nt(xdt, g) * lmt
            dseg = dcb * cb
            ch = _dot(Cv, h)
            bdh = _dot(Bv, dh)
            dxdt = _dot(cbt * lmt, g) + dte[:, :P] * bdh
            dc += _dot(dcb, Bv) + e * _dot_nt(g, h)
            db += _dot(dcbt, Cv) + _dot_nt(xdt * dte[:, :P], dh)
            d_e = jnp.sum(g * ch, axis=-1, keepdims=True)
            d_dte = jnp.sum(xdt * bdh, axis=-1, keepdims=True)
            d_el = jnp.sum(h * dh, keepdims=True)
            d_al = jnp.sum(d_dte * dte[:, :1], keepdims=True) + d_el * el[:, :1]
            dac_j = (jnp.sum(dseg, axis=-1, keepdims=True) + d_e * e[:, :1] - d_dte * dte[:, :1]
                     + jnp.where(last, d_al, 0.0))
            dac_all = jnp.where(lane == j, dac_j, dac_all)
            dacr_ref[j] = -jnp.sum(dseg, axis=0, keepdims=True)
            dx_ref[j] = dxdt * dtv[:, :P]
            ddt_all = jnp.where(lane == j, jnp.sum(dxdt * xv, axis=-1, keepdims=True), ddt_all)
            dh_s[j] = el[:, :P] * dh + _dot(ct, e[:, :P] * g)
        dac_ref[...] = dac_all
        ddt_ref[...] = ddt_all
        db_ref[...] = db
        dc_ref[...] = dc

    xs, col, row, bc, st = _ssd_specs(hb, L, P, N, lambda c: nc - 1 - c)
    dx, ddt, dac, dacr, db, dc = pl.pallas_call(
        body, name="ssd_bwd", grid=(G, nc), in_specs=[xs, col, col, row, bc, bc, st, xs],
        out_specs=(xs, col, col, row, bc, bc),
        out_shape=(jax.ShapeDtypeStruct((H, S, P), F32), jax.ShapeDtypeStruct((G, S, hb), F32),
                   jax.ShapeDtypeStruct((G, S, hb), F32), jax.ShapeDtypeStruct((H, 1, S), F32),
                   jax.ShapeDtypeStruct((G, S, N), F32), jax.ShapeDtypeStruct((G, S, N), F32)),
        scratch_shapes=[pltpu.VMEM((hb, N, P), F32)],
        compiler_params=_params(("parallel", "arbitrary")),
    )(x, dt, ac, acr, Bm, Cm, hp, dy)
    return dx, ddt, dac + dacr.reshape(G, hb, S).transpose(0, 2, 1), db, dc


@jax.custom_vjp
def ssd_scan(x, dt, ac, Bm, Cm):
    return ssd_fwd(x, dt, ac, Bm, Cm)[0]


def _ssd_scan_fwd(x, dt, ac, Bm, Cm):
    y, hp = ssd_fwd(x, dt, ac, Bm, Cm)
    return y, (x, dt, ac, Bm, Cm, hp)


def _ssd_scan_bwd(res, dy):
    return ssd_bwd(*res, dy)


ssd_scan.defvjp(_ssd_scan_fwd, _ssd_scan_bwd)


def _vec(v):
    return v.reshape(1, -1)


def _ffn(x, gain, m3, w_gu, gu_row, w_down, x_gu, x_down):
    h, = rowwise_op(_f_modulate, "modulate", 1, out_dtype=BF16)(x, _vec(gain), _vec(m3[0]), _vec(m3[1]))
    gu = mm_op("ffn_gu", out_dtype=BF16, stack=N_CHIPS, b_slot=(gu_row, D_MODEL))(h, w_gu, x_gu)
    a, = rowwise_op(_f_swiglu, "swiglu", 1, out_dtype=BF16)(gu)
    y = mm_op("ffn_down")(a, w_down, x_down)
    return rowwise_op(_f_resid(0.5), "resid_half", 2)(x, y, _vec(m3[2]))[0]


def _rope_tables(positions):
    inv = 1.0 / (ROPE_THETA ** (jnp.arange(0, QK_ROPE, 2, dtype=F32) / QK_ROPE))
    ang = positions.astype(F32)[:, None] * inv
    S = positions.shape[0]
    cosf = jnp.concatenate([jnp.ones((S, QK_NOPE), F32), jnp.cos(ang), jnp.cos(ang)], axis=1)
    sinf = jnp.concatenate([jnp.zeros((S, QK_NOPE), F32), jnp.sin(ang), jnp.sin(ang)], axis=1)
    return cosf, sinf


def _heads_first(t):
    return t.transpose(1, 0, 2).reshape(-1, t.shape[-1])


def _mla(h, cos, sin, P, W, X, j):
    S = h.shape[0]
    lat = mm_op("mla_a")(h, W["mla_w_a"][j], X["mla_w_a"][j])
    q_lat, kv_lat, k_rope = lat[:, :Q_LORA], lat[:, Q_LORA:Q_LORA + KV_LORA], lat[:, Q_LORA + KV_LORA:]
    qn, = rowwise_op(_f_rms, "rms_lat", 1, out_dtype=BF16)(q_lat, _vec(P["mla_q_a_gain"][j]))
    kvn, = rowwise_op(_f_rms, "rms_lat", 1, out_dtype=BF16)(kv_lat, _vec(P["mla_kv_a_gain"][j]))
    q = mm_op("mla_qb", stack=N_CHIPS)(qn, W["mla_w_qb"][j], X["mla_w_qb"][j]).reshape(S, MLA_HEADS, QK_HEAD)
    kv = mm_op("mla_kvb", stack=N_CHIPS)(kvn, W["mla_w_kvb"][j], X["mla_w_kvb"][j]).reshape(S, MLA_HEADS, QK_NOPE + V_HEAD)
    k_nope, v = kv[..., :QK_NOPE], kv[..., QK_NOPE:]
    k = jnp.concatenate([k_nope, jnp.broadcast_to(k_rope[:, None, :], (S, MLA_HEADS, QK_ROPE))], axis=-1)
    head_rope = rowwise_op(_f_head_rope, "head_rope", 3, diff_rows=(True, False, False), n_const=1)
    swap = _rope_swap()
    q, = head_rope(_heads_first(q), cos, sin, _vec(P["mla_q_gain"][j]), swap)
    k, = head_rope(_heads_first(k), cos, sin, _vec(P["mla_k_gain"][j]), swap)
    o = attention(q.reshape(MLA_HEADS, S, QK_HEAD), k.reshape(MLA_HEADS, S, QK_HEAD), v.transpose(1, 0, 2))
    o = o.transpose(1, 0, 2).reshape(S, MLA_HEADS * V_HEAD).astype(BF16)
    return mm_op("mla_o")(o, W["mla_w_o"][j], X["mla_w_o"][j])


def _ssd(h, P, W, X, j):
    S = h.shape[0]
    z = mm_op("ssd_in_z")(h, W["ssd_w_z"][j], X["ssd_w_z"][j])
    xbc = mm_op("ssd_in_xbc")(h, W["ssd_w_xbc"][j], X["ssd_w_xbc"][j])
    dtr = mm_op("ssd_in_dt")(h, W["ssd_w_dt"][j], X["ssd_w_dt"][j])
    xbc = conv_silu(xbc, P["ssd_conv_w"][j], _vec(P["ssd_conv_b"][j]))
    xs =xbc[:, :D_INNER].reshape(S, SSD_HEADS, SSD_HEAD_DIM).transpose(1, 0, 2)
    Bm = xbc[:, D_INNER:D_INNER + SSD_GROUPS * SSD_STATE].reshape(S, SSD_GROUPS, SSD_STATE).transpose(1, 0, 2)
    Cm = xbc[:, D_INNER + SSD_GROUPS * SSD_STATE:].reshape(S, SSD_GROUPS, SSD_STATE).transpose(1, 0, 2)
    dt = jax.nn.softplus(dtr + P["ssd_dt_bias"][j][None, :])
    A = -jnp.exp(P["ssd_a_log"][j])
    a = (dt * A[None, :]).reshape(S // CHUNK, CHUNK, SSD_HEADS)
    ac = jnp.cumsum(a, axis=1).reshape(S, SSD_HEADS)
    by_group = lambda t: t.reshape(S, SSD_GROUPS, SSD_HEADS // SSD_GROUPS).transpose(1, 0, 2)
    y = ssd_scan(xs, by_group(dt), by_group(ac), Bm, Cm)
    y = y + P["ssd_d"][j][:, None, None] * xs
    y = y.transpose(1, 0, 2).reshape(S, D_INNER)
    g, = rowwise_op(_f_gated_norm, "gated_norm", 2, out_dtype=BF16)(y, z, _vec(P["ssd_norm_gain"][j]))
    return mm_op("ssd_out")(g, W["ssd_w_out"][j], X["ssd_w_out"][j])


def trunk(x, mods, P, W, X, positions):
    cos, sin = _rope_tables(positions)
    for i in range(DEPTH):
        m, j = mods[i], i // 2
        x = _ffn(x, P["norm_gain"][i, 0], m[0], W["ffn_w_gu"], (2 * i) * D_MODEL, W["ffn_w_down"][i][0],
                 X["ffn_w_gu"][i][0], X["ffn_w_down"][i][0])
        h, = rowwise_op(_f_modulate, "modulate", 1, out_dtype=BF16)(x, _vec(P["norm_gain"][i, 1]), _vec(m[1, 0]), _vec(m[1, 1]))
        y = _mla(h, cos, sin, P, W, X, j) if i % 2 == 0 else _ssd(h, P, W, X, j)
        x, = rowwise_op(_f_resid(1.0), "resid_full", 2)(x, y, _vec(m[1, 2]))
        x = _ffn(x, P["norm_gain"][i, 2], m[2], W["ffn_w_gu"], (2 * i + 1) * D_MODEL, W["ffn_w_down"][i][1],
                 X["ffn_w_gu"][i][1], X["ffn_w_down"][i][1])
    return x


def stand_ins(W):
    def one(name, w):
        if name in ("mla_w_qb", "mla_w_kvb"):
            return jnp.zeros((N_CHIPS, w.shape[0], w.shape[1] // N_CHIPS), F32)
        return jnp.zeros(w.shape, F32)
    X = {n: jax.tree.map(lambda w, n=n: one(n, w), W[n]) for n in W if n != "ffn_w_gu"}
    g = W["ffn_w_gu"]
    X["ffn_w_gu"] = [[jnp.zeros((g.shape[0], D_MODEL, g.shape[2]), F32) for _ in range(2)] for _ in range(DEPTH)]
    return X


def loss_head(y, target):
    S, D = y.shape
    tm = _row_tile(S, [D])

    def body(y_ref, t_ref, dy_ref, l_ref):
        d = y_ref[...] - t_ref[...]
        dy_ref[...] = d * (1.0 / D)
        part = jnp.sum(d * d, axis=0, keepdims=True) * (0.5 / D)

        @pl.when(pl.program_id(0) == 0)
        def _():
            l_ref[...] = part

        @pl.when(pl.program_id(0) > 0)
        def _():
            l_ref[...] += part

    return pl.pallas_call(
        body, name="loss_head", grid=(S // tm,),
        in_specs=[pl.BlockSpec((tm, D), lambda i: (i, 0))] * 2,
        out_specs=(pl.BlockSpec((tm, D), lambda i: (i, 0)), pl.BlockSpec((1, D), lambda i: (0, 0))),
        out_shape=(jax.ShapeDtypeStruct((S, D), F32), jax.ShapeDtypeStruct((1, D), F32)),
        compiler_params=_params(("arbitrary",)),
    )(y, target)


def _stream_rows(R, C):
    return _div_tile(R, max(16, (1 << 19) // C // 16 * 16), 16)


def adamw(w, m, v, g):
    R, C = w.shape
    tr = _stream_rows(R, C)
    c1 = 1.0 / (1.0 - ADAM_B1 ** ADAM_STEP)
    c2 = 1.0 / (1.0 - ADAM_B2 ** ADAM_STEP)

    def body(w_ref, m_ref, v_ref, g_ref, d_ref, nm_ref, nv_ref):
        gv = g_ref[...]
        nm = ADAM_B1 * m_ref[...] + (1.0 - ADAM_B1) * gv
        nv = ADAM_B2 * v_ref[...] + (1.0 - ADAM_B2) * (gv * gv)
        d_ref[...] = -ADAM_LR * ((nm * c1) / (jnp.sqrt(nv * c2) + ADAM_EPS) + ADAM_WD * w_ref[...])
        nm_ref[...] = nm
        nv_ref[...] = nv

    spec = pl.BlockSpec((tr, C), lambda i: (i, 0))
    return pl.pallas_call(
        body, name="adamw", grid=(R // tr,), in_specs=[spec] * 4, out_specs=(spec,) * 3,
        out_shape=(jax.ShapeDtypeStruct((R, C), F32),) * 3, compiler_params=_params(("parallel",)),
    )(w, m, v, g)


def sum_parts(parts, name, out_dtype=F32):
    R, C = parts[0].shape
    tr = _stream_rows(R, C)
    n = len(parts)

    def body(*refs):
        acc = refs[0][...].astype(F32)
        for r in refs[1:n]:
            acc = acc + r[...].astype(F32)
        refs[n][...] = acc.astype(out_dtype)

    spec = pl.BlockSpec((tr, C), lambda i: (i, 0))
    return pl.pallas_call(
        body, name=name, grid=(R // tr,), in_specs=[spec] * n, out_specs=spec,
        out_shape=jax.ShapeDtypeStruct((R, C), out_dtype), compiler_params=_params(("parallel",)),
    )(*parts)


def sum_own_half(full, theirs, c, name, out_dtype):
    n, R, C = full.shape
    h = R // 2
    tr = _stream_rows(h, C)
    nb = h // tr

    def body(c_ref, a_ref, b_ref, o_ref):
        o_ref[...] = (a_ref[...].astype(F32) + b_ref[...].astype(F32)).astype(out_dtype)

    return pl.pallas_call(
        body, name=name,
        grid_spec=pltpu.PrefetchScalarGridSpec(
            num_scalar_prefetch=1, grid=(n, nb),
            in_specs=[pl.BlockSpec((None, tr, C), lambda p, i, cr: (p, cr[0] * nb + i, 0)),
                      pl.BlockSpec((None, tr, C), lambda p, i, cr: (p, i, 0))],
            out_specs=pl.BlockSpec((None, tr, C), lambda p, i, cr: (p, i, 0))),
        out_shape=jax.ShapeDtypeStruct((n, h, C), out_dtype), compiler_params=_params(("parallel", "parallel")),
    )(jnp.reshape(c, (1,)).astype(jnp.int32), full, theirs)


def sum_slots(a, name, out_dtype=F32):
    n, R, C = a.shape
    tr = _stream_rows(R, C)

    def body(*refs):
        acc = refs[0][...].astype(F32)
        for r in refs[1:n]:
            acc = acc + r[...].astype(F32)
        refs[n][...] = acc.astype(out_dtype)

    return pl.pallas_call(
        body, name=name, grid=(R // tr,),
        in_specs=[pl.BlockSpec((None, tr, C), lambda i, p=p: (p, i, 0)) for p in range(n)],
        out_specs=pl.BlockSpec((tr, C), lambda i: (i, 0)),
        out_shape=jax.ShapeDtypeStruct((R, C), out_dtype), compiler_params=_params(("parallel",)),
    )(*([a] * n))


def _coords():
    return lax.axis_index("x"), lax.axis_index("y"), lax.axis_index("c")


def _other_chips(x, y):
    return [(1 - x, y), (x, 1 - y), (1 - x, 1 - y)]


def _hbm_call(body, name, ins, out_shapes, n_sems):
    return pl.pallas_call(
        body, name=name, out_shape=tuple(out_shapes),
        in_specs=[pl.BlockSpec(memory_space=pl.ANY)] * len(ins),
        out_specs=tuple(pl.BlockSpec(memory_space=pl.ANY) for _ in out_shapes),
        scratch_shapes=[pltpu.SemaphoreType.DMA((n_sems,)), pltpu.SemaphoreType.DMA((n_sems,))],
    )(*ins)


def allgather_small(v, name):
    m_per, n = v.shape

    def body(x_ref, out_ref, send_sems, recv_sems, local_sem):
        x, y, c = _coords()
        me, sibling = (x, y, c), (x, y, 1 - c)
        chips = _other_chips(x, y)

        def rows(px, py, pc):
            return out_ref.at[pl.ds((4 * px + 2 * py + pc) * m_per, m_per), :]

        def copy(k, block, to, src=None):
            return pltpu.make_async_remote_copy(
                src_ref=rows(*block) if src is None else src, dst_ref=rows(*block),
                send_sem=send_sems.at[k], recv_sem=recv_sems.at[k], device_id=to, device_id_type=MESH)

        mine = pltpu.make_async_copy(x_ref, rows(*me), local_sem)
        mine.start()
        first = [copy(0, me, sibling, src=x_ref)]
        first += [copy(1 + j, me, (*chip, c), src=x_ref) for j, chip in enumerate(chips)]
        for cp in first:
            cp.start()
        passed = [copy(4 + j, (*chip, c), sibling) for j, chip in enumerate(chips)]
        for j, chip in enumerate(chips):
            copy(1 + j, (*chip, c), me).wait_recv()
            passed[j].start()
        copy(0, sibling, me).wait_recv()
        for j, chip in enumerate(chips):
            copy(4 + j, (*chip, 1 - c), me).wait_recv()
        for cp in first + passed:
            cp.wait_send()
        mine.wait()

    return pl.pallas_call(
        body, name=name, out_shape=jax.ShapeDtypeStruct((N_DEV * m_per, n), v.dtype),
        in_specs=[pl.BlockSpec(memory_space=pltpu.VMEM)], out_specs=pl.BlockSpec(memory_space=pltpu.VMEM),
        scratch_shapes=[pltpu.SemaphoreType.DMA((7,)), pltpu.SemaphoreType.DMA((7,)), pltpu.SemaphoreType.DMA],
        compiler_params=pltpu.CompilerParams(vmem_limit_bytes=V7X_VMEM_LIMIT),
    )(v)


def allgather_chips(arrs, name):
    n = len(arrs)
    halves = [a.shape[0] // 2 for a in arrs]

    def body(*refs):
        xs, outs = refs[:n], refs[n:2 * n]
        send_sems, recv_sems = refs[2 * n:]
        x, y, c = _coords()
        me, sibling, chips = 2 * x + y, (x, y, 1 - c), _other_chips(x, y)

        def half(ref, cc, i):
            return ref.at[pl.ds(cc * halves[i], halves[i]), :]

        def copy(i, k, src, dst, to):
            return pltpu.make_async_remote_copy(src_ref=src, dst_ref=dst, send_sem=send_sems.at[6 * i + k],
                                                recv_sem=recv_sems.at[6 * i + k], device_id=to, device_id_type=MESH)

        sends = [copy(i, k, half(xs[i], c, i), half(outs[i].at[me], c, i), (*chip, c))
                 for k, chip in enumerate(chips) for i in range(n)]
        for cp in sends:
            cp.start()
        passed = []
        for k, (px, py) in enumerate(chips):
            for i in range(n):
                landed = half(outs[i].at[2 * px + py], c, i)
                copy(i, k, landed, landed, (px, py, c)).wait_recv()
                passed.append(copy(i, 3 + k, landed, landed, sibling))
                passed[-1].start()
        for k, (px, py) in enumerate(chips):
            for i in range(n):
                theirs = half(outs[i].at[2 * px + py], 1 - c, i)
                copy(i, 3 + k, theirs, theirs, sibling).wait_recv()
        for cp in sends + passed:
            cp.wait_send()

    return _hbm_call(body, name, arrs, [jax.ShapeDtypeStruct((N_CHIPS,) + a.shape, a.dtype) for a in arrs], 6 * n)


def pair_exchange(arrs, name):
    n = len(arrs)
    halves = [a.shape[1] // 2 for a in arrs]

    def body(*refs):
        xs, theirs = refs[:n], refs[n:2 * n]
        send_sems, recv_sems = refs[2 * n:]
        x, y, c = _coords()
        cps = [pltpu.make_async_remote_copy(
            src_ref=xs[i].at[:, pl.ds((1 - c) * halves[i], halves[i]), :], dst_ref=theirs[i],
            send_sem=send_sems.at[i], recv_sem=recv_sems.at[i], device_id=(x, y, 1 - c), device_id_type=MESH)
            for i in range(n)]
        for cp in cps:
            cp.start()
        for cp in cps:
            cp.wait()

    shapes = [jax.ShapeDtypeStruct((a.shape[0], a.shape[1] // 2, a.shape[2]), a.dtype) for a in arrs]
    return _hbm_call(body, name, arrs, shapes, n)


def scatter_chips(arrs, name):
    n = len(arrs)

    def body(*refs):
        xs, outs = refs[:n], refs[n:2 * n]
        send_sems, recv_sems = refs[2 * n:]
        x, y, c = _coords()
        me, chips = 2 * x + y, _other_chips(x, y)

        def copy(i, k, src_slot, dst_slot, to):
            return pltpu.make_async_remote_copy(
                src_ref=xs[i].at[src_slot], dst_ref=outs[i].at[dst_slot], send_sem=send_sems.at[3 * i + k],
                recv_sem=recv_sems.at[3 * i + k], device_id=to, device_id_type=MESH)

        sends = [copy(i, k, 2 * px + py, me, (px, py, c)) for k, (px, py) in enumerate(chips) for i in range(n)]
        for cp in sends:
            cp.start()
        for k, (px, py) in enumerate(chips):
            for i in range(n):
                copy(i, k, me, 2 * px + py, (px, py, c)).wait_recv()
        for cp in sends:
            cp.wait_send()

    return _hbm_call(body, name, arrs, [jax.ShapeDtypeStruct(a.shape, a.dtype) for a in arrs], 3 * n)


def pair_allgather(arrs, name):
    n = len(arrs)

    def body(*refs):
        xs, outs = refs[:n], refs[n:2 * n]
        send_sems, recv_sems = refs[2 * n:]
        x, y, c = _coords()
        cps = [pltpu.make_async_remote_copy(
            src_ref=xs[i], dst_ref=outs[i].at[pl.ds(c * xs[i].shape[0], xs[i].shape[0]), :], send_sem=send_sems.at[i],
            recv_sem=recv_sems.at[i], device_id=(x, y, 1 - c), device_id_type=MESH) for i in range(n)]
        for cp in cps:
            cp.start()
        for cp in cps:
            cp.wait()

    return _hbm_call(body, name, arrs, [jax.ShapeDtypeStruct((2 * a.shape[0], a.shape[1]), a.dtype) for a in arrs], n)


GROUPS = (("ffn_w_gu",), ("ffn_w_down", "mla_w_o", "ssd_w_out"), ("mla_w_a",), ("mla_w_qb",), ("mla_w_kvb",), ("ssd_w_in",))


def _pad_rows(a, mult):
    r = (-a.shape[0]) % mult
    return a if r == 0 else jnp.concatenate([a, jnp.zeros((r, a.shape[1]), a.dtype)], axis=0)


def _rows2d(a):
    return a.reshape(-1, a.shape[-1])


def _unstack(st, axis):
    full = jnp.moveaxis(st, 0, axis)
    sh = list(full.shape)
    sh[axis:axis + 2] = [sh[axis] * sh[axis + 1]]
    return full.reshape(sh)


def _stack(full, axis):
    sh = list(full.shape)
    sh[axis:axis + 1] = [N_CHIPS, sh[axis] // N_CHIPS]
    return jnp.moveaxis(full.reshape(sh), axis, 0)


def kernel(x, c, positions, norm_gain, ada_w, ada_b, ffn_w_gu, ffn_w_down, mla_w_a, mla_q_a_gain, mla_kv_a_gain, mla_w_qb, mla_w_kvb, mla_q_gain, mla_k_gain, mla_w_o, ssd_w_in, ssd_conv_w, ssd_conv_b, ssd_dt_bias, ssd_a_log, ssd_d, ssd_norm_gain, ssd_w_out, loss_target, m_norm_gain, m_ada_w, m_ada_b, m_ffn_w_gu, m_ffn_w_down, m_mla_w_a, m_mla_q_a_gain, m_mla_kv_a_gain, m_mla_w_qb, m_mla_w_kvb, m_mla_q_gain, m_mla_k_gain, m_mla_w_o, m_ssd_w_in, m_ssd_conv_w, m_ssd_conv_b, m_ssd_dt_bias, m_ssd_a_log, m_ssd_d, m_ssd_norm_gain, m_ssd_w_out, v_norm_gain, v_ada_w, v_ada_b, v_ffn_w_gu, v_ffn_w_down, v_mla_w_a, v_mla_q_a_gain, v_mla_kv_a_gain, v_mla_w_qb, v_mla_w_kvb, v_mla_q_gain, v_mla_k_gain, v_mla_w_o, v_ssd_w_in, v_ssd_conv_w, v_ssd_conv_b, v_ssd_dt_bias, v_ssd_a_log, v_ssd_d, v_ssd_norm_gain, v_ssd_w_out):
    w_in = dict(norm_gain=norm_gain, ada_w=ada_w, ada_b=ada_b, ffn_w_gu=ffn_w_gu, ffn_w_down=ffn_w_down, mla_w_a=mla_w_a, mla_q_a_gain=mla_q_a_gain, mla_kv_a_gain=mla_kv_a_gain, mla_w_qb=mla_w_qb, mla_w_kvb=mla_w_kvb, mla_q_gain=mla_q_gain, mla_k_gain=mla_k_gain, mla_w_o=mla_w_o, ssd_w_in=ssd_w_in, ssd_conv_w=ssd_conv_w, ssd_conv_b=ssd_conv_b, ssd_dt_bias=ssd_dt_bias, ssd_a_log=ssd_a_log, ssd_d=ssd_d, ssd_norm_gain=ssd_norm_gain, ssd_w_out=ssd_w_out)
    m_in = dict(norm_gain=m_norm_gain, ada_w=m_ada_w, ada_b=m_ada_b, ffn_w_gu=m_ffn_w_gu, ffn_w_down=m_ffn_w_down, mla_w_a=m_mla_w_a, mla_q_a_gain=m_mla_q_a_gain, mla_kv_a_gain=m_mla_kv_a_gain, mla_w_qb=m_mla_w_qb, mla_w_kvb=m_mla_w_kvb, mla_q_gain=m_mla_q_gain, mla_k_gain=m_mla_k_gain, mla_w_o=m_mla_w_o, ssd_w_in=m_ssd_w_in, ssd_conv_w=m_ssd_conv_w, ssd_conv_b=m_ssd_conv_b, ssd_dt_bias=m_ssd_dt_bias, ssd_a_log=m_ssd_a_log, ssd_d=m_ssd_d, ssd_norm_gain=m_ssd_norm_gain, ssd_w_out=m_ssd_w_out)
    v_in = dict(norm_gain=v_norm_gain, ada_w=v_ada_w, ada_b=v_ada_b, ffn_w_gu=v_ffn_w_gu, ffn_w_down=v_ffn_w_down, mla_w_a=v_mla_w_a, mla_q_a_gain=v_mla_q_a_gain, mla_kv_a_gain=v_mla_kv_a_gain, mla_w_qb=v_mla_w_qb, mla_w_kvb=v_mla_w_kvb, mla_q_gain=v_mla_q_gain, mla_k_gain=v_mla_k_gain, mla_w_o=v_mla_w_o, ssd_w_in=v_ssd_w_in, ssd_conv_w=v_ssd_conv_w, ssd_conv_b=v_ssd_conv_b, ssd_dt_bias=v_ssd_dt_bias, ssd_a_log=v_ssd_a_log, ssd_d=v_ssd_d, ssd_norm_gain=v_ssd_norm_gain, ssd_w_out=v_ssd_w_out)
    names = list(w_in)
    xi, yi, ci = _coords()
    chip = 2 * xi + yi
    batch = 4 * xi + 2 * yi + ci
    x2, target = x[0], loss_target[0]

    small_sharded = ("norm_gain", "ssd_conv_w", "ssd_conv_b", "ssd_norm_gain")
    pack0 = jnp.concatenate([c.reshape(-1)] + [w_in[n].reshape(-1) for n in small_sharded])
    pack0 = _pad_rows(pack0.reshape(-1, 128), 8)
    g0 = allgather_small(pack0, "gather_small").reshape(N_DEV, -1)
    c_all = g0[:, :D_MODEL]
    P, off = {}, D_MODEL
    for n in small_sharded:
        sz = w_in[n].size
        st = g0[0::2, off:off + sz].reshape((N_CHIPS,) + w_in[n].shape)
        P[n] = _unstack(st, w_in[n].ndim - 1)
        off += sz
    for n in ("mla_q_a_gain", "mla_kv_a_gain", "mla_q_gain", "mla_k_gain", "ssd_dt_bias", "ssd_a_log", "ssd_d"):
        P[n] = w_in[n]

    sc = _silu(c_all)
    n_ada = ada_w.shape[2]
    b_sh = lax.dynamic_slice_in_dim(ada_b, chip * n_ada, n_ada, axis=1)
    mods_sh = jnp.stack([matmul(sc, ada_w[l], "nn", "ada_fwd") for l in range(DEPTH)]) + b_sh[:, None, :]
    g1 = allgather_small(mods_sh.reshape(-1, 128), "gather_mods").reshape(N_DEV, DEPTH, N_DEV, n_ada)
    mods = lax.dynamic_index_in_dim(g1[0::2], batch, axis=2, keepdims=False)
    mods = mods.transpose(1, 0, 2).reshape(DEPTH, 3, 3, D_MODEL)

    shard_groups = [jnp.concatenate([_rows2d(w_in[n]).astype(BF16) for n in grp], axis=0) for grp in GROUPS]
    gathered = allgather_chips(shard_groups, "gather_weights")
    gathered = [lax.dynamic_update_slice(g, s[None], (chip, 0, 0)) for g, s in zip(gathered, shard_groups)]
    G = {}
    for grp, arr in zip(GROUPS, gathered):
        off = 0
        for n in grp:
            rows = w_in[n].size // w_in[n].shape[-1]
            G[n] = arr[:, off:off + rows].reshape((N_CHIPS,) + w_in[n].shape)
            off += rows
    W = {
        "ffn_w_gu": gathered[0],
        "ffn_w_down": [[_unstack(G["ffn_w_down"][:, i, t], 0) for t in range(2)] for i in range(DEPTH)],
        "mla_w_a": [_unstack(G["mla_w_a"][:, j], 0) for j in range(2)],
        "mla_w_qb": [_unstack(G["mla_w_qb"][:, j], 1) for j in range(2)],
        "mla_w_kvb": [_unstack(G["mla_w_kvb"][:, j], 1) for j in range(2)],
        "mla_w_o": [_unstack(G["mla_w_o"][:, j], 0) for j in range(2)],
        "ssd_w_out": [_unstack(G["ssd_w_out"][:, j], 0) for j in range(2)],
    }
    w_in_full = [_unstack(G["ssd_w_in"][:, j], 1) for j in range(2)]
    W["ssd_w_z"] = [w[:, :D_INNER] for w in w_in_full]
    W["ssd_w_xbc"] = [w[:, D_INNER:D_INNER + CONV_DIM] for w in w_in_full]
    W["ssd_w_dt"] = [w[:, D_INNER + CONV_DIM:] for w in w_in_full]
    X = stand_ins(W)

    pos = positions[0]
    y, vjp = jax.vjp(lambda a, b, p_, x_: trunk(a, b, p_, W, x_, pos), x2, mods, P, X)
    dy, loss_cols = loss_head(y, target)
    dx, dmods, dP, dX = vjp(dy)
    loss = lax.psum(jnp.sum(loss_cols), ("x", "y", "c"))

    small_names = ("norm_gain", "ssd_conv_w", "ssd_conv_b", "ssd_norm_gain", "mla_q_a_gain", "mla_kv_a_gain",
                   "mla_q_gain", "mla_k_gain", "ssd_dt_bias", "ssd_a_log", "ssd_d")
    pack1 = jnp.concatenate([dmods.reshape(-1)] + [dP[n].reshape(-1) for n in small_names])
    pack1 = _pad_rows(jnp.pad(pack1, (0, (-pack1.size) % 128)).reshape(-1, 128), 8)
    rows1 = pack1.shape[0]
    g2 = allgather_small(pack1, "gather_small_grads")
    tot = sum_slots(g2.reshape(N_DEV, rows1, 128), "sum_small_grads").reshape(-1)
    n_mod = DEPTH * 9 * D_MODEL
    grads = {"ada_b": tot[:n_mod].reshape(DEPTH, 9 * D_MODEL)}
    off = n_mod
    for n in small_names:
        sz = dP[n].size
        full = tot[off:off + sz].reshape(dP[n].shape)
        off += sz
        if n in small_sharded:
            k = w_in[n].shape[-1]
            full = lax.dynamic_slice_in_dim(full, chip * k, k, axis=full.ndim - 1)
        grads[n] = full
    dmods_all = g2.reshape(N_DEV, -1)[:, :n_mod].reshape(N_DEV, DEPTH, 9 * D_MODEL)
    dm_sh = lax.dynamic_slice_in_dim(dmods_all, chip * n_ada, n_ada, axis=2)
    grads["ada_w"] = jnp.stack([matmul(sc, dm_sh[:, l], "tn", "ada_dw") for l in range(DEPTH)])

    w_in_g = [jnp.concatenate([dX["ssd_w_z"][j], dX["ssd_w_xbc"][j], dX["ssd_w_dt"][j]], axis=1) for j in range(2)]
    per_name = {
        "ffn_w_gu": [dX["ffn_w_gu"][i][t] for i in range(DEPTH) for t in range(2)],
        "ffn_w_down": [dX["ffn_w_down"][i][t].reshape(N_CHIPS, -1, D_MODEL) for i in range(DEPTH) for t in range(2)],
        "mla_w_a": [g.reshape(N_CHIPS, -1, g.shape[-1]) for g in dX["mla_w_a"]],
        "mla_w_qb": dX["mla_w_qb"], "mla_w_kvb": dX["mla_w_kvb"],
        "mla_w_o": [g.reshape(N_CHIPS, -1, D_MODEL) for g in dX["mla_w_o"]],
        "ssd_w_out": [g.reshape(N_CHIPS, -1, D_MODEL) for g in dX["ssd_w_out"]],
        "ssd_w_in": [_stack(g, 1) for g in w_in_g],
    }
    grad_groups = [jnp.concatenate([p for n in grp for p in per_name[n]], axis=1) for grp in GROUPS]
    theirs = pair_exchange(grad_groups, "grads_to_sibling")
    pair = [sum_own_half(a, b, ci, "sum_pair", BF16) for a, b in zip(grad_groups, theirs)]
    landed = scatter_chips(pair, "grads_to_chips")
    landed = [lax.dynamic_update_slice(a, lax.dynamic_slice_in_dim(p, chip, 1, axis=0), (chip, 0, 0)) for a, p in zip(landed, pair)]
    half_sums = [sum_slots(a, "sum_chips") for a in landed]
    totals = pair_allgather(half_sums, "grad_halves_swap")
    totals = [lax.dynamic_update_slice(t, h, (ci * h.shape[0], 0)) for t, h in zip(totals, half_sums)]
    for grp, arr in zip(GROUPS, totals):
        off = 0
        for n in grp:
            rows = w_in[n].size // w_in[n].shape[-1]
            grads[n] = arr[off:off + rows].reshape(w_in[n].shape)
            off += rows

    deltas, new_m, new_v = {}, {}, {}
    for n in names:
        w = w_in[n]
        d, nm, nv = adamw(_rows2d(w), _rows2d(m_in[n]), _rows2d(v_in[n]), _rows2d(grads[n]))
        deltas[n], new_m[n], new_v[n] = d.reshape(w.shape), nm.reshape(w.shape), nv.reshape(w.shape)

    return (loss, dx[None], *[grads[n] for n in names], *[deltas[n] for n in names],
            *[new_m[n] for n in names], *[new_v[n] for n in names])
```

```python
import numpy as np

import jax
import jax.numpy as jnp
from jax import lax
from jax.experimental import pallas as pl
from jax.experimental.pallas import tpu as pltpu

F32 = jnp.float32
BF16 = jnp.bfloat16
MESH = pl.DeviceIdType.MESH

D_MODEL = 1024
DEPTH = 4
EPS = 1e-6
D_FF = 2816
MLA_HEADS = 16
Q_LORA = 384
KV_LORA = 256
QK_NOPE = 64
QK_ROPE = 32
QK_HEAD = QK_NOPE + QK_ROPE
V_HEAD = 64
ROPE_THETA = 10000.0
D_INNER = 2048
SSD_HEAD_DIM = 64
SSD_HEADS = 32
SSD_GROUPS = 4
SSD_STATE = 128
CONV_WIDTH = 4
CHUNK = 128
CONV_DIM = D_INNER + 2 * SSD_GROUPS * SSD_STATE
ADAM_LR = 0.001
ADAM_B1 = 0.9
ADAM_B2 = 0.999
ADAM_EPS = 1e-08
ADAM_WD = 0.01
ADAM_STEP = 10

N_CHIPS = 4
N_DEV = 8
V7X_VMEM_LIMIT = 56 * 1024 * 1024
ATTN_HEADS_PER_STEP = 4
LANES = 128
LOG2E = 1.4426950408889634


def _params(sem=None):
    return pltpu.CompilerParams(dimension_semantics=sem, vmem_limit_bytes=V7X_VMEM_LIMIT)


def _div_tile(n, pref, quantum):
    if n <= pref:
        return n
    t = (pref // quantum) * quantum
    while t >= quantum:
        if n % t == 0:
            return t
        t -= quantum
    return n


def matmul(a, b, mode, name, out_dtype=F32, stack=0, b_slot=None, tm=1024, tn=1408, tk=2816):
    if b_slot is not None:
        (roff, rows), (nsl, _, n) = b_slot, b.shape
        (M, K) = a.shape
        K2, N = (rows, nsl * n) if mode == "nn" else (nsl * n, rows)
    elif mode == "nn":
        (M, K), (K2, N) = a.shape, b.shape
    elif mode == "nt":
        (M, K), (N, K2) = a.shape, b.shape
    else:
        (K, M), (K2, N) = a.shape, b.shape
    assert K == K2, (a.shape, b.shape, mode)
    tm = _div_tile(M, tm, 128 if mode == "tn" else 16)
    tn = _div_tile(N // stack if stack else (n if b_slot and mode == "nn" else N), tn, 128)
    tk = _div_tile(n if b_slot and mode == "nt" else K, tk, 128)
    nk = K // tk
    if mode == "nn":
        a_spec = pl.BlockSpec((tm, tk), lambda i, j, k: (i, k))
        b_spec = pl.BlockSpec((tk, tn), lambda i, j, k: (k, j))
        dims = (((1,), (0,)), ((), ()))
        if b_slot:
            assert roff % tk == 0
            nbn, rb = n // tn, roff // tk
            b_spec = pl.BlockSpec((None, tk, tn), lambda i, j, k: (j // nbn, rb + k, j % nbn))
    elif mode == "nt":
        a_spec = pl.BlockSpec((tm, tk), lambda i, j, k: (i, k))
        b_spec = pl.BlockSpec((tn, tk), lambda i, j, k: (j, k))
        dims = (((1,), (1,)), ((), ()))
        if b_slot:
            assert roff % tn == 0
            nbk, rb = n // tk, roff // tn
            b_spec = pl.BlockSpec((None, tn, tk), lambda i, j, k: (k // nbk, rb + j, k % nbk))
    else:
        a_spec = pl.BlockSpec((tk, tm), lambda i, j, k: (k, i))
        b_spec = pl.BlockSpec((tk, tn), lambda i, j, k: (k, j))
        dims = (((0,), (0,)), ((), ()))
    if stack:
        nb = N // stack // tn
        out_spec = pl.BlockSpec((None, tm, tn), lambda i, j, k: (j // nb, i, j % nb))
        out_shape = jax.ShapeDtypeStruct((stack, M, N // stack), out_dtype)
    else:
        out_spec = pl.BlockSpec((tm, tn), lambda i, j, k: (i, j))
        out_shape = jax.ShapeDtypeStruct((M, N), out_dtype)
    use_acc = nk > 1 and out_dtype != F32

    def body(a_ref, b_ref, o_ref, *acc):
        p = lax.dot_general(a_ref[...].astype(BF16), b_ref[...].astype(BF16), dims, preferred_element_type=F32)
        if nk == 1:
            o_ref[...] = p.astype(out_dtype)
            return
        acc_ref = acc[0] if use_acc else o_ref
        k = pl.program_id(2)

        @pl.when(k == 0)
        def _():
            acc_ref[...] = p

        @pl.when(k > 0)
        def _():
            acc_ref[...] += p

        if use_acc:
            @pl.when(k == nk - 1)
            def _():
                o_ref[...] = acc_ref[...].astype(out_dtype)

    return pl.pallas_call(
        body, name=name, grid=(M // tm, N // tn, nk), in_specs=[a_spec, b_spec], out_specs=out_spec, out_shape=out_shape,
        scratch_shapes=[pltpu.VMEM((tm, tn), F32)] if use_acc else [],
        compiler_params=_params(("parallel", "parallel", "arbitrary")),
    )(a, b)


def mm_op(name, out_dtype=F32, stack=0, b_slot=None):
    @jax.custom_vjp
    def op(a, w, wp):
        return matmul(a, w, "nn", name + "_fwd", out_dtype=out_dtype, b_slot=b_slot)

    def fwd(a, w, wp):
        return op(a, w, wp), (a, w)

    def bwd(res, g):
        a, w = res
        return (matmul(g, w, "nt", name + "_dx", out_dtype=a.dtype, b_slot=b_slot), jnp.zeros_like(w),
                matmul(a, g, "tn", name + "_dw", stack=stack, tn=1408, tk=1024))

    op.defvjp(fwd, bwd)
    return op


def _row_tile(rows, widths):
    w = max(widths)
    t = 128 if w > 4096 else (256 if w > 1024 else (512 if w > 128 else 2048))
    return _div_tile(rows, t, 16)


def _row_spec(r, tm):
    nb = r.shape[0] // tm
    return pl.BlockSpec((tm, r.shape[1]), lambda i: (i % nb, 0))


def _rowwise_fwd(f, rows, vecs, name, out_dtype):
    n_r, n_v = len(rows), len(vecs)
    S = rows[0].shape[0]
    tm = _row_tile(min(r.shape[0] for r in rows), [r.shape[1] for r in rows])
    outs = jax.eval_shape(f, *[jax.ShapeDtypeStruct((tm, r.shape[1]), F32) for r in rows], *vecs)
    out_dtypes = out_dtype if isinstance(out_dtype, tuple) else (out_dtype,) * len(outs)

    def body(*refs):
        res = f(*[r[...].astype(F32) for r in refs[: n_r + n_v]])
        for o, r in zip(refs[n_r + n_v:], res):
            o[...] = r.astype(o.dtype)

    return pl.pallas_call(
        body, name=name, grid=(S // tm,),
        in_specs=[_row_spec(r, tm) for r in rows] + [pl.BlockSpec(v.shape, lambda i: (0, 0)) for v in vecs],
        out_specs=tuple(pl.BlockSpec((tm, o.shape[1]), lambda i: (i, 0)) for o in outs),
        out_shape=tuple(jax.ShapeDtypeStruct((S, o.shape[1]), dt) for o, dt in zip(outs, out_dtypes)),
        compiler_params=_params(("parallel",)),
    )(*rows, *vecs)


def _rowwise_bwd(f, rows, vecs, douts, diff_rows, n_const, name):
    n_r, n_o = len(rows), len(douts)
    consts, vecs = vecs[len(vecs) - n_const:], vecs[:len(vecs) - n_const]
    n_v = len(vecs)
    S = rows[0].shape[0]
    tm = _row_tile(min(r.shape[0] for r in rows), [r.shape[1] for r in rows] + [d.shape[1] for d in douts])
    d_idx = [i for i in range(n_r) if diff_rows[i]]

    def body(*refs):
        row_v = [r[...].astype(F32) for r in refs[:n_r]]
        vec_v = [r[...] for r in refs[n_r:n_r + n_v]]
        const_v = [r[...] for r in refs[n_r + n_v:n_r + n_v + n_const]]
        dout_v = tuple(r[...].astype(F32) for r in refs[n_r + n_v + n_const:n_r + n_v + n_const + n_o])
        out_refs = refs[n_r + n_v + n_const + n_o:]

        def g(*args):
            full = list(row_v)
            for j, i in enumerate(d_idx):
                full[i] = args[j]
            return f(*full, *args[len(d_idx):], *const_v)

        _, vjp = jax.vjp(g, *[row_v[i] for i in d_idx], *vec_v)
        grads = vjp(dout_v)
        for j in range(len(d_idx)):
            out_refs[j][...] = grads[j].astype(out_refs[j].dtype)
        step = pl.program_id(0)
        for j in range(n_v):
            gv, o = grads[len(d_idx) + j], out_refs[len(d_idx) + j]

            @pl.when(step == 0)
            def _(gv=gv, o=o):
                o[...] = gv

            @pl.when(step > 0)
            def _(gv=gv, o=o):
                o[...] += gv

    res = pl.pallas_call(
        body, name=name, grid=(S // tm,),
        in_specs=[_row_spec(r, tm) for r in rows] + [pl.BlockSpec(v.shape, lambda i: (0, 0)) for v in vecs + consts]
        + [pl.BlockSpec((tm, d.shape[1]), lambda i: (i, 0)) for d in douts],
        out_specs=tuple([pl.BlockSpec((tm, rows[i].shape[1]), lambda i_: (i_, 0)) for i in d_idx]
                        + [pl.BlockSpec(v.shape, lambda i: (0, 0)) for v in vecs]),
        out_shape=tuple([jax.ShapeDtypeStruct(rows[i].shape, rows[i].dtype) for i in d_idx]
                        + [jax.ShapeDtypeStruct(v.shape, F32) for v in vecs]),
        compiler_params=_params(("arbitrary",)),
    )(*rows, *vecs, *consts, *douts)
    drows = [None] * n_r
    for j, i in enumerate(d_idx):
        drows[i] = res[j]
    for i in range(n_r):
        if drows[i] is None:
            drows[i] = jnp.zeros_like(rows[i])
    return tuple(drows) + tuple(res[len(d_idx):]) + tuple(jnp.zeros_like(k) for k in consts)


def rowwise_op(f, name, n_rows, diff_rows=None, out_dtype=F32, n_const=0):
    diff = tuple(diff_rows) if diff_rows is not None else (True,) * n_rows

    @jax.custom_vjp
    def op(*args):
        return _rowwise_fwd(f, args[:n_rows], args[n_rows:], name + "_fwd", out_dtype)

    def fwd(*args):
        return op(*args), args

    def bwd(args, douts):
        return _rowwise_bwd(f, args[:n_rows], args[n_rows:], douts, diff, n_const, name + "_bwd")

    op.defvjp(fwd, bwd)
    return op


def _rms(x, gain):
    return x * lax.rsqrt(jnp.mean(x * x, axis=-1, keepdims=True) + EPS) * gain


def _silu(x):
    return x * jax.nn.sigmoid(x)


def _f_modulate(x, gain, shift, scale):
    return (_rms(x, gain) * (1.0 + scale) + shift,)


def _f_rms(x, gain):
    return (_rms(x, gain),)


def _f_swiglu(gu):
    n = gu.shape[1] // 2
    return (_silu(gu[:, :n]) * gu[:, n:],)


def _f_resid(coef):
    def f(x, y, gate):
        return (x + (coef * gate) * y,)
    return f


def _f_resid_modulate(coef):
    def f(x, y, gate, gain, shift, scale):
        x1 = x + (coef * gate) * y
        return (x1, _rms(x1, gain) * (1.0 + scale) + shift)
    return f


def _f_head_rope(x, cosf, sinf, gain, swap):
    y = _rms(x, gain)
    return (y * cosf + jnp.dot(y, swap, precision=lax.Precision.HIGHEST, preferred_element_type=F32) * sinf,)


def _rope_swap():
    m = np.zeros((QK_HEAD, QK_HEAD), np.float32)
    half = QK_ROPE // 2
    for i in range(half):
        m[QK_NOPE + half + i, QK_NOPE + i] = -1.0
        m[QK_NOPE + i, QK_NOPE + half + i] = 1.0
    return jnp.asarray(m)


def _f_gated_norm(y, z, gain):
    g = y * _silu(z)
    n = g.shape[1] // SSD_GROUPS
    return (jnp.concatenate([_rms(g[:, i * n:(i + 1) * n], gain[:, i * n:(i + 1) * n]) for i in range(SSD_GROUPS)], axis=1),)


HALO = 8


def _conv_taps(ext, w, rows, off):
    acc = None
    for k in range(CONV_WIDTH):
        term = w[k:k + 1, :] * pltpu.roll(ext, CONV_WIDTH - 1 - k, 0)[off:off + rows]
        acc = term if acc is None else acc + term
    return acc


def _conv_specs(S, tm, tc):
    tile = pl.BlockSpec((tm, tc), lambda j, i: (i, j))
    prev = pl.BlockSpec((HALO, tc), lambda j, i: (jnp.maximum(i * (tm // HALO) - 1, 0), j))
    nxt = pl.BlockSpec((HALO, tc), lambda j, i: (jnp.minimum((i + 1) * (tm // HALO), S // HALO - 1), j))
    wspec = pl.BlockSpec((CONV_WIDTH, tc), lambda j, i: (0, j))
    bspec = pl.BlockSpec((1, tc), lambda j, i: (0, j))
    return tile, prev, nxt, wspec, bspec


def conv_silu_fwd(u, w, b, tm=512, tc=1024):
    S, C = u.shape
    tm = _div_tile(S, tm, HALO)

    def body(u_ref, p_ref, w_ref, b_ref, o_ref):
        prev = jnp.where(pl.program_id(1) == 0, 0.0, p_ref[...])
        ext = jnp.concatenate([prev, u_ref[...]], axis=0)
        conv = _conv_taps(ext, w_ref[...], tm, HALO) + b_ref[...]
        o_ref[...] = conv * jax.nn.sigmoid(conv)

    tile, prev, _, wspec, bspec = _conv_specs(S, tm, tc)
    return pl.pallas_call(
        body, name="conv_silu_fwd", grid=(C // tc, S // tm), in_specs=[tile, prev, wspec, bspec], out_specs=tile,
        out_shape=jax.ShapeDtypeStruct((S, C), F32), compiler_params=_params(("parallel", "arbitrary")),
    )(u, u, w, b)


def conv_silu_bwd(u, w, b, dout, tm=512, tc=1024):
    S, C = u.shape
    tm = _div_tile(S, tm, HALO)
    n = S // tm
    ext_rows = tm + HALO

    def body(u_ref, p_ref, n_ref, g_ref, gn_ref, w_ref, b_ref, du_ref, dw_ref, db_ref):
        i = pl.program_id(1)
        wv = w_ref[...]
        prev = jnp.where(i == 0, 0.0, p_ref[...])
        ext = jnp.concatenate([prev, u_ref[...], n_ref[...]], axis=0)
        conv = _conv_taps(ext, wv, ext_rows, HALO) + b_ref[...]
        g_ext = jnp.concatenate([g_ref[...], jnp.where(i == n - 1, 0.0, gn_ref[...])], axis=0)
        sg = jax.nn.sigmoid(conv)
        dconv = g_ext * (sg * (1.0 + conv * (1.0 - sg)))
        du = None
        for k in range(CONV_WIDTH):
            s = CONV_WIDTH - 1 - k
            term = wv[k:k + 1, :] * pltpu.roll(dconv, (ext_rows - s) % ext_rows, 0)[:tm]
            du = term if du is None else du + term
        du_ref[...] = du
        dc = dconv[:tm]
        dw = jnp.concatenate([jnp.sum(dc * pltpu.roll(ext, CONV_WIDTH - 1 - k, 0)[HALO:HALO + tm], axis=0, keepdims=True)
                              for k in range(CONV_WIDTH)], axis=0)
        dbv = jnp.sum(dc, axis=0, keepdims=True)

        @pl.when(i == 0)
        def _():
            dw_ref[...] = dw
            db_ref[...] = dbv

        @pl.when(i > 0)
        def _():
            dw_ref[...] += dw
            db_ref[...] += dbv

    tile, prev, nxt, wspec, bspec = _conv_specs(S, tm, tc)
    return pl.pallas_call(
        body, name="conv_silu_bwd", grid=(C // tc, n), in_specs=[tile, prev, nxt, tile, nxt, wspec, bspec],
        out_specs=(tile, wspec, bspec),
        out_shape=(jax.ShapeDtypeStruct((S, C), F32), jax.ShapeDtypeStruct((CONV_WIDTH, C), F32),
                   jax.ShapeDtypeStruct((1, C), F32)),
        compiler_params=_params(("parallel", "arbitrary")),
    )(u, u, u, dout, dout, w, b)


@jax.custom_vjp
def conv_silu(u, w, b):
    return conv_silu_fwd(u, w, b)


def _conv_silu_fwd(u, w, b):
    return conv_silu_fwd(u, w, b), (u, w, b)


def _conv_silu_bwd(res, dout):
    return conv_silu_bwd(*res, dout)


conv_silu.defvjp(_conv_silu_fwd, _conv_silu_bwd)


_NT = (((1,), (1,)), ((), ()))


def _dot(a, b):
    return jnp.dot(a.astype(BF16), b.astype(BF16), preferred_element_type=F32)


def _dot_nt(a, b):
    return lax.dot_general(a.astype(BF16), b.astype(BF16), _NT, preferred_element_type=F32)


def _attn_tile(S):
    return _div_tile(S, 512, 128)


def _causal(t, transposed=False):
    r = lax.broadcasted_iota(jnp.int32, (t, t), 0)
    c = lax.broadcasted_iota(jnp.int32, (t, t), 1)
    return r <= c if transposed else r >= c


def _tri_tables(n, by_key):
    if by_key:
        pairs = [(i, j) for j in range(n) for i in range(j, n)]
    else:
        pairs = [(i, j) for i in range(n) for j in range(i + 1)]
    return (jnp.asarray(np.array([p[0] for p in pairs], np.int32)), jnp.asarray(np.array([p[1] for p in pairs], np.int32)))


def attn_fwd(q, k, v):
    H, S, dk = q.shape
    dv = v.shape[-1]
    t, hb = _attn_tile(S), ATTN_HEADS_PER_STEP
    n = S // t
    scale = dk ** -0.5
    qi_tab, kj_tab = _tri_tables(n, by_key=False)

    def body(qi_ref, kj_ref, q_ref, k_ref, v_ref, o_ref, lse_ref, m_s, l_s, acc_s):
        qi, kj = qi_ref[pl.program_id(1)], kj_ref[pl.program_id(1)]

        @pl.when(kj == 0)
        def _():
            m_s[...] = jnp.full(m_s.shape, -jnp.inf, F32)
            l_s[...] = jnp.zeros(l_s.shape, F32)
            acc_s[...] = jnp.zeros(acc_s.shape, F32)

        def step(masked):
            ss = [_dot_nt(q_ref[j], k_ref[j]) for j in range(hb)]
            new = []
            for j in range(hb):
                s = ss[j] * (scale * LOG2E)
                if masked:
                    s = jnp.where(_causal(t), s, -jnp.inf)
                m_old = m_s[j]
                m_new = jnp.maximum(m_old, jnp.max(s, axis=-1, keepdims=True))
                alpha = jnp.exp2(m_old - m_new)
                p = jnp.exp2(s - jnp.tile(m_new, (1, t // LANES)))
                new.append((m_new, alpha * l_s[j] + jnp.sum(p, axis=-1, keepdims=True),
                            alpha[:, :dv] * acc_s[j] + _dot(p, v_ref[j])))
            for j in range(hb):
                m_s[j], l_s[j], acc_s[j] = new[j]

        @pl.when(kj < qi)
        def _():
            step(False)

        @pl.when(kj == qi)
        def _():
            step(True)
            l = l_s[...]
            o_ref[...] = acc_s[...] / l[:, :, :dv]
            lse_ref[...] = m_s[...] + jnp.log2(l)

    qmap = lambda h, s, qi, kj: (h, qi[s], 0)
    kmap = lambda h, s, qi, kj: (h, kj[s], 0)
    return pl.pallas_call(
        body, name="attn_fwd",
        grid_spec=pltpu.PrefetchScalarGridSpec(
            num_scalar_prefetch=2, grid=(H // hb, qi_tab.shape[0]),
            in_specs=[pl.BlockSpec((hb, t, dk), qmap), pl.BlockSpec((hb, t, dk), kmap), pl.BlockSpec((hb, t, dv), kmap)],
            out_specs=(pl.BlockSpec((hb, t, dv), qmap), pl.BlockSpec((hb, t, LANES), qmap)),
            scratch_shapes=[pltpu.VMEM((hb, t, LANES), F32), pltpu.VMEM((hb, t, LANES), F32), pltpu.VMEM((hb, t, dv), F32)]),
        out_shape=(jax.ShapeDtypeStruct((H, S, dv), F32), jax.ShapeDtypeStruct((H, S, LANES), F32)),
        compiler_params=_params(("parallel", "arbitrary")),
    )(qi_tab, kj_tab, q, k, v)


def attn_bwd(q, k, v, o, lse, do):
    H, S, dk = q.shape
    dv = v.shape[-1]
    t, hb = _attn_tile(S), ATTN_HEADS_PER_STEP
    n = S // t
    scale = dk ** -0.5
    hb = min(hb, H)
    lse_r = lse[:, :, 0].reshape(H, 1, S)
    qi_tab, kj_tab = _tri_tables(n, by_key=True)
    tn_dims = (((0,), (0,)), ((), ()))

    def body(qi_ref, kj_ref, q_ref, k_ref, v_ref, o_ref, do_ref, lse_ref, dq_ref, dk_ref, dv_ref):
        pair = pl.program_id(1)
        qi, kj = qi_ref[pair], kj_ref[pair]

        @pl.when(pair == 0)
        def _():
            dq_ref[...] = jnp.zeros(dq_ref.shape, F32)

        ones = jnp.ones((8, dv), F32)

        def step(masked):
            sts = [_dot_nt(k_ref[j], q_ref[j]) for j in range(hb)]
            for j in range(hb):
                dof = do_ref[j]
                dob = dof.astype(BF16)
                delta = lax.dot_general(ones, dof * o_ref[j], _NT, precision=lax.Precision.HIGHEST,
                                        preferred_element_type=F32)[0:1]
                pt = jnp.exp2(sts[j] * (scale * LOG2E) - lse_ref[j])
                if masked:
                    pt = jnp.where(_causal(t, transposed=True), pt, 0.0)
                dvj = _dot(pt, dob)
                dst = (pt * (_dot_nt(v_ref[j], dob) - delta)).astype(BF16)
                dkj = _dot(dst, q_ref[j]) * scale
                rows = pl.ds(pl.multiple_of(qi * t, t), t)
                dq_ref[j, rows, :] += lax.dot_general(dst, k_ref[j], tn_dims, preferred_element_type=F32) * scale
                if masked:
                    dv_ref[j] = dvj
                    dk_ref[j] = dkj
                else:
                    dv_ref[j] += dvj
                    dk_ref[j] += dkj

        @pl.when(qi == kj)
        def _():
            step(True)

        @pl.when(qi > kj)
        def _():
            step(False)

    qmap = lambda h, s, qi, kj: (h, qi[s], 0)
    kmap = lambda h, s, qi, kj: (h, kj[s], 0)
    rowq = lambda h, s, qi, kj: (h, 0, qi[s])
    whole = lambda h, s, qi, kj: (h, 0, 0)
    return pl.pallas_call(
        body, name="attn_bwd",
        grid_spec=pltpu.PrefetchScalarGridSpec(
            num_scalar_prefetch=2, grid=(H // hb, qi_tab.shape[0]),
            in_specs=[pl.BlockSpec((hb, t, dk), qmap), pl.BlockSpec((hb, t, dk), kmap), pl.BlockSpec((hb, t, dv), kmap),
                      pl.BlockSpec((hb, t, dv), qmap), pl.BlockSpec((hb, t, dv), qmap), pl.BlockSpec((hb, 1, t), rowq)],
            out_specs=(pl.BlockSpec((hb, S, dk), whole), pl.BlockSpec((hb, t, dk), kmap), pl.BlockSpec((hb, t, dv), kmap))),
        out_shape=(jax.ShapeDtypeStruct((H, S, dk), F32), jax.ShapeDtypeStruct((H, S, dk), F32),
                   jax.ShapeDtypeStruct((H, S, dv), F32)),
        compiler_params=_params(("parallel", "arbitrary")),
    )(qi_tab, kj_tab, q, k, v, o, do, lse_r)


@jax.custom_vjp
def attention(q, k, v):
    return attn_fwd(q.astype(BF16), k.astype(BF16), v.astype(BF16))[0]


def _attention_fwd(q, k, v):
    qb, kb, vb = q.astype(BF16), k.astype(BF16), v.astype(BF16)
    o, lse = attn_fwd(qb, kb, vb)
    return o, (qb, kb, vb, o, lse)


def _attention_bwd(res, do):
    return attn_bwd(*res, do)


attention.defvjp(_attention_fwd, _attention_bwd)


def _ssd_specs(hb, L, P, N, order):
    xs = pl.BlockSpec((hb, L, P), lambda g, c: (g, order(c), 0))
    col = pl.BlockSpec((None, L, hb), lambda g, c: (g, order(c), 0))
    row = pl.BlockSpec((hb, 1, L), lambda g, c: (g, 0, order(c)))
    bc = pl.BlockSpec((None, L, N), lambda g, c: (g, order(c), 0))
    st = pl.BlockSpec((hb, None, N, P), lambda g, c: (g, order(c), 0, 0))
    return xs, col, row, bc, st


def _ssd_rows(cols, H):
    G, S, hb = cols.shape
    return cols.transpose(0, 2, 1).reshape(H, 1, S)


def ssd_fwd(x, dt, ac, Bm, Cm):
    H, S, P = x.shape
    G, _, N = Bm.shape
    hb, L = H // G, CHUNK
    nc = S // L
    acr = _ssd_rows(ac, H)

    def body(x_ref, dt_ref, ac_ref, acr_ref, b_ref, c_ref, y_ref, hp_ref, h_s):
        @pl.when(pl.program_id(1) == 0)
        def _():
            h_s[...] = jnp.zeros((hb, N, P), F32)

        Bv, Cv = b_ref[...], c_ref[...]
        cb = _dot_nt(Cv, Bv)
        bt = Bv.T
        mask = _causal(L)
        ac_all, dt_all = ac_ref[...], dt_ref[...]
        for j in range(hb):
            a = jnp.broadcast_to(ac_all[:, j:j + 1], (L, L))
            dtv = jnp.broadcast_to(dt_all[:, j:j + 1], (L, L))
            lm = jnp.exp(jnp.where(mask, a - acr_ref[j], -jnp.inf))
            xdt = x_ref[j] * dtv[:, :P]
            h = h_s[j]
            hp_ref[j] = h
            y_ref[j] = _dot(cb * lm, xdt) + jnp.exp(a)[:, :P] * _dot(Cv, h)
            al = a[L - 1:L, :]
            h_s[j] = jnp.exp(al)[:, :P] * h + _dot(bt, xdt * jnp.exp(al - a)[:, :P])

    xs, col, row, bc, st = _ssd_specs(hb, L, P, N, lambda c: c)
    return pl.pallas_call(
        body, name="ssd_fwd", grid=(G, nc), in_specs=[xs, col, col, row, bc, bc], out_specs=(xs, st),
        out_shape=(jax.ShapeDtypeStruct((H, S, P), F32), jax.ShapeDtypeStruct((H, nc, N, P), F32)),
        scratch_shapes=[pltpu.VMEM((hb, N, P), F32)],
        compiler_params=_params(("parallel", "arbitrary")),
    )(x, dt, ac, acr, Bm, Cm)


def ssd_bwd(x, dt, ac, Bm, Cm, hp, dy):
    H, S, P = x.shape
    G, _, N = Bm.shape
    hb, L = H // G, CHUNK
    nc = S // L
    acr = _ssd_rows(ac, H)

    def body(x_ref, dt_ref, ac_ref, acr_ref, b_ref, c_ref, hp_ref, dy_ref,
             dx_ref, ddt_ref, dac_ref, dacr_ref, db_ref, dc_ref, dh_s):
        @pl.when(pl.program_id(1) == 0)
        def _():
            dh_s[...] = jnp.zeros((hb, N, P), F32)

        Bv, Cv = b_ref[...], c_ref[...]
        cb = _dot_nt(Cv, Bv)
        cbt = _dot_nt(Bv, Cv)
        ct = Cv.T
        mask, maskt = _causal(L), _causal(L, transposed=True)
        last = lax.broadcasted_iota(jnp.int32, (L, 1), 0) == L - 1
        lane = lax.broadcasted_iota(jnp.int32, (L, hb), 1)
        db = jnp.zeros((L, N), F32)
        dc = jnp.zeros((L, N), F32)
        dac_all = jnp.zeros((L, hb), F32)
        ddt_all = jnp.zeros((L, hb), F32)
        ac_all, dt_all = ac_ref[...], dt_ref[...]
        for j in range(hb):
            ar, xv, g, h, dh = acr_ref[j], x_ref[j], dy_ref[j], hp_ref[j], dh_s[j]
            a = jnp.broadcast_to(ac_all[:, j:j + 1], (L, L))
            dtv = jnp.broadcast_to(dt_all[:, j:j + 1], (L, L))
            lm = jnp.exp(jnp.where(mask, a - ar, -jnp.inf))
            lmt = jnp.exp(jnp.where(maskt, ar - a, -jnp.inf))
            xdt = xv * dtv[:, :P]
            e = jnp.exp(a)
            al = a[L - 1:L, :]
            dte = jnp.exp(al - a)
            el = jnp.exp(al)
            dcb = _dot_nt(g, xdt) * lm
            dcbt = _dot_nt(xdt, g) * lmt
            dseg = dcb * cb
            ch = _dot(Cv, h)
            bdh = _dot(Bv, dh)
            dxdt = _dot(cbt * lmt, g) + dte[:, :P] * bdh
            dc += _dot(dcb, Bv) + e * _dot_nt(g, h)
            db += _dot(dcbt, Cv) + _dot_nt(xdt * dte[:, :P], dh)
            d_e = jnp.sum(g * ch, axis=-1, keepdims=True)
            d_dte = jnp.sum(xdt * bdh, axis=-1, keepdims=True)
            d_el = jnp.sum(h * dh, keepdims=True)
            d_al = jnp.sum(d_dte * dte[:, :1], keepdims=True) + d_el * el[:, :1]
            dac_j = (jnp.sum(dseg, axis=-1, keepdims=True) + d_e * e[:, :1] - d_dte * dte[:, :1]
                     + jnp.where(last, d_al, 0.0))
            dac_all = jnp.where(lane == j, dac_j, dac_all)
            dacr_ref[j] = -jnp.sum(dseg, axis=0, keepdims=True)
            dx_ref[j] = dxdt * dtv[:, :P]
            ddt_all = jnp.where(lane == j, jnp.sum(dxdt * xv, axis=-1, keepdims=True), ddt_all)
            dh_s[j] = el[:, :P] * dh + _dot(ct, e[:, :P] * g)
        dac_ref[...] = dac_all
        ddt_ref[...] = ddt_all
        db_ref[...] = db
        dc_ref[...] = dc

    xs, col, row, bc, st = _ssd_specs(hb, L, P, N, lambda c: nc - 1 - c)
    dx, ddt, dac, dacr, db, dc = pl.pallas_call(
        body, name="ssd_bwd", grid=(G, nc), in_specs=[xs, col, col, row, bc, bc, st, xs],
        out_specs=(xs, col, col, row, bc, bc),
        out_shape=(jax.ShapeDtypeStruct((H, S, P), F32), jax.ShapeDtypeStruct((G, S, hb), F32),
                   jax.ShapeDtypeStruct((G, S, hb), F32), jax.ShapeDtypeStruct((H, 1, S), F32),
                   jax.ShapeDtypeStruct((G, S, N), F32), jax.ShapeDtypeStruct((G, S, N), F32)),
        scratch_shapes=[pltpu.VMEM((hb, N, P), F32)],
        compiler_params=_params(("parallel", "arbitrary")),
    )(x, dt, ac, acr, Bm, Cm, hp, dy)
    return dx, ddt, dac + dacr.reshape(G, hb, S).transpose(0, 2, 1), db, dc


@jax.custom_vjp
def ssd_scan(x, dt, ac, Bm, Cm):
    return ssd_fwd(x, dt, ac, Bm, Cm)[0]


def _ssd_scan_fwd(x, dt, ac, Bm, Cm):
    y, hp = ssd_fwd(x, dt, ac, Bm, Cm)
    return y, (x, dt, ac, Bm, Cm, hp)


def _ssd_scan_bwd(res, dy):
    return ssd_bwd(*res, dy)


ssd_scan.defvjp(_ssd_scan_fwd, _ssd_scan_bwd)


def _vec(v):
    return v.reshape(1, -1)


def _ffn(h, w_gu, gu_row, w_down, x_gu, x_down):
    gu = mm_op("ffn_gu", out_dtype=BF16, stack=N_CHIPS, b_slot=(gu_row, D_MODEL))(h, w_gu, x_gu)
    a, = rowwise_op(_f_swiglu, "swiglu", 1, out_dtype=BF16)(gu)
    return mm_op("ffn_down")(a, w_down, x_down)


def _rope_tables(positions):
    inv = 1.0 / (ROPE_THETA ** (jnp.arange(0, QK_ROPE, 2, dtype=F32) / QK_ROPE))
    ang = positions.astype(F32)[:, None] * inv
    S = positions.shape[0]
    cosf = jnp.concatenate([jnp.ones((S, QK_NOPE), F32), jnp.cos(ang), jnp.cos(ang)], axis=1)
    sinf = jnp.concatenate([jnp.zeros((S, QK_NOPE), F32), jnp.sin(ang), jnp.sin(ang)], axis=1)
    return cosf, sinf


def _heads_first(t):
    return t.transpose(1, 0, 2).reshape(-1, t.shape[-1])


def _mla(h, cos, sin, P, W, X, j):
    S = h.shape[0]
    lat = mm_op("mla_a")(h, W["mla_w_a"][j], X["mla_w_a"][j])
    q_lat, kv_lat, k_rope = jnp.split(lat, [Q_LORA, Q_LORA + KV_LORA], axis=1)
    qn, = rowwise_op(_f_rms, "rms_lat", 1, out_dtype=BF16)(q_lat, _vec(P["mla_q_a_gain"][j]))
    kvn, = rowwise_op(_f_rms, "rms_lat", 1, out_dtype=BF16)(kv_lat, _vec(P["mla_kv_a_gain"][j]))
    q = mm_op("mla_qb", stack=N_CHIPS)(qn, W["mla_w_qb"][j], X["mla_w_qb"][j]).reshape(S, MLA_HEADS, QK_HEAD)
    kv = mm_op("mla_kvb", stack=N_CHIPS)(kvn, W["mla_w_kvb"][j], X["mla_w_kvb"][j]).reshape(S, MLA_HEADS, QK_NOPE + V_HEAD)
    k_nope, v = jnp.split(kv, [QK_NOPE], axis=-1)
    k = jnp.concatenate([k_nope, jnp.broadcast_to(k_rope[:, None, :], (S, MLA_HEADS, QK_ROPE))], axis=-1)
    head_rope = rowwise_op(_f_head_rope, "head_rope", 3, diff_rows=(True, False, False), n_const=1)
    swap = _rope_swap()
    q, = head_rope(_heads_first(q), cos, sin, _vec(P["mla_q_gain"][j]), swap)
    k, = head_rope(_heads_first(k), cos, sin, _vec(P["mla_k_gain"][j]), swap)
    o = attention(q.reshape(MLA_HEADS, S, QK_HEAD), k.reshape(MLA_HEADS, S, QK_HEAD), v.transpose(1, 0, 2))
    o = o.transpose(1, 0, 2).reshape(S, MLA_HEADS * V_HEAD).astype(BF16)
    return mm_op("mla_o")(o, W["mla_w_o"][j], X["mla_w_o"][j])


def _ssd(h, P, W, X, j):
    S = h.shape[0]
    z = mm_op("ssd_in_z")(h, W["ssd_w_z"][j], X["ssd_w_z"][j])
    xbc = mm_op("ssd_in_xbc")(h, W["ssd_w_xbc"][j], X["ssd_w_xbc"][j])
    dtr = mm_op("ssd_in_dt")(h, W["ssd_w_dt"][j], X["ssd_w_dt"][j])
    xbc = conv_silu(xbc, P["ssd_conv_w"][j], _vec(P["ssd_conv_b"][j]))
    xs, Bm, Cm = jnp.split(xbc, [D_INNER, D_INNER + SSD_GROUPS * SSD_STATE], axis=1)
    xs = xs.reshape(S, SSD_HEADS, SSD_HEAD_DIM).transpose(1, 0, 2)
    Bm = Bm.reshape(S, SSD_GROUPS, SSD_STATE).transpose(1, 0, 2)
    Cm = Cm.reshape(S, SSD_GROUPS, SSD_STATE).transpose(1, 0, 2)
    dt = jax.nn.softplus(dtr + P["ssd_dt_bias"][j][None, :])
    A = -jnp.exp(P["ssd_a_log"][j])
    a = (dt * A[None, :]).reshape(S // CHUNK, CHUNK, SSD_HEADS)
    ac = jnp.cumsum(a, axis=1).reshape(S, SSD_HEADS)
    by_group = lambda t: t.reshape(S, SSD_GROUPS, SSD_HEADS // SSD_GROUPS).transpose(1, 0, 2)
    y = ssd_scan(xs, by_group(dt), by_group(ac), Bm, Cm)
    y = y + P["ssd_d"][j][:, None, None] * xs
    y = y.transpose(1, 0, 2).reshape(S, D_INNER)
    g, = rowwise_op(_f_gated_norm, "gated_norm", 2, out_dtype=BF16)(y, z, _vec(P["ssd_norm_gain"][j]))
    return mm_op("ssd_out")(g, W["ssd_w_out"][j], X["ssd_w_out"][j])


def trunk(x, mods, P, W, X, positions):
    cos, sin = _rope_tables(positions)

    def sub(i, s, h):
        if s == 1:
            return _mla(h, cos, sin, P, W, X, i // 2) if i % 2 == 0 else _ssd(h, P, W, X, i // 2)
        t = s // 2
        return _ffn(h, W["ffn_w_gu"], (2 * i + t) * D_MODEL, W["ffn_w_down"][i][t], X["ffn_w_gu"][i][t], X["ffn_w_down"][i][t])

    order = [(i, s) for i in range(DEPTH) for s in range(3)]
    i0, s0 = order[0]
    h, = rowwise_op(_f_modulate, "modulate", 1, out_dtype=BF16)(
        x, _vec(P["norm_gain"][i0, s0]), _vec(mods[i0, s0, 0]), _vec(mods[i0, s0, 1]))
    for n, (i, s) in enumerate(order):
        y = sub(i, s, h)
        coef = 1.0 if s == 1 else 0.5
        gate = _vec(mods[i, s, 2])
        if n + 1 < len(order):
            i1, s1 = order[n + 1]
            x, h = rowwise_op(_f_resid_modulate(coef), "resid_modulate", 2, out_dtype=(F32, BF16))(
                x, y, gate, _vec(P["norm_gain"][i1, s1]), _vec(mods[i1, s1, 0]), _vec(mods[i1, s1, 1]))
        else:
            x, = rowwise_op(_f_resid(coef), "resid", 2)(x, y, gate)
    return x


def stand_ins(W):
    def one(name, w):
        if name in ("mla_w_qb", "mla_w_kvb"):
            return jnp.zeros((N_CHIPS, w.shape[0], w.shape[1] // N_CHIPS), F32)
        return jnp.zeros(w.shape, F32)
    X = {n: jax.tree.map(lambda w, n=n: one(n, w), W[n]) for n in W if n != "ffn_w_gu"}
    g = W["ffn_w_gu"]
    X["ffn_w_gu"] = [[jnp.zeros((g.shape[0], D_MODEL, g.shape[2]), F32) for _ in range(2)] for _ in range(DEPTH)]
    return X


def loss_head(y, target):
    S, D = y.shape
    tm = _row_tile(S, [D])

    def body(y_ref, t_ref, dy_ref, l_ref):
        d = y_ref[...] - t_ref[...]
        dy_ref[...] = d * (1.0 / D)
        part = jnp.sum(d * d, axis=0, keepdims=True) * (0.5 / D)

        @pl.when(pl.program_id(0) == 0)
        def _():
            l_ref[...] = part

        @pl.when(pl.program_id(0) > 0)
        def _():
            l_ref[...] += part

    return pl.pallas_call(
        body, name="loss_head", grid=(S // tm,),
        in_specs=[pl.BlockSpec((tm, D), lambda i: (i, 0))] * 2,
        out_specs=(pl.BlockSpec((tm, D), lambda i: (i, 0)), pl.BlockSpec((1, D), lambda i: (0, 0))),
        out_shape=(jax.ShapeDtypeStruct((S, D), F32), jax.ShapeDtypeStruct((1, D), F32)),
        compiler_params=_params(("arbitrary",)),
    )(y, target)


def _stream_rows(R, C):
    return _div_tile(R, max(16, (1 << 19) // C // 16 * 16), 16)


def adamw(w, m, v, g):
    R, C = w.shape
    tr = _stream_rows(R, C)
    c1 = 1.0 / (1.0 - ADAM_B1 ** ADAM_STEP)
    c2 = 1.0 / (1.0 - ADAM_B2 ** ADAM_STEP)

    def body(w_ref, m_ref, v_ref, g_ref, d_ref, nm_ref, nv_ref):
        gv = g_ref[...]
        nm = ADAM_B1 * m_ref[...] + (1.0 - ADAM_B1) * gv
        nv = ADAM_B2 * v_ref[...] + (1.0 - ADAM_B2) * (gv * gv)
        d_ref[...] = -ADAM_LR * ((nm * c1) / (jnp.sqrt(nv * c2) + ADAM_EPS) + ADAM_WD * w_ref[...])
        nm_ref[...] = nm
        nv_ref[...] = nv

    spec = pl.BlockSpec((tr, C), lambda i: (i, 0))
    return pl.pallas_call(
        body, name="adamw", grid=(R // tr,), in_specs=[spec] * 4, out_specs=(spec,) * 3,
        out_shape=(jax.ShapeDtypeStruct((R, C), F32),) * 3, compiler_params=_params(("parallel",)),
    )(w, m, v, g)


def sum_parts(parts, name, out_dtype=F32):
    R, C = parts[0].shape
    tr = _stream_rows(R, C)
    n = len(parts)

    def body(*refs):
        acc = refs[0][...].astype(F32)
        for r in refs[1:n]:
            acc = acc + r[...].astype(F32)
        refs[n][...] = acc.astype(out_dtype)

    spec = pl.BlockSpec((tr, C), lambda i: (i, 0))
    return pl.pallas_call(
        body, name=name, grid=(R // tr,), in_specs=[spec] * n, out_specs=spec,
        out_shape=jax.ShapeDtypeStruct((R, C), out_dtype), compiler_params=_params(("parallel",)),
    )(*parts)


def sum_own_half(full, theirs, c, name, out_dtype):
    n, R, C = full.shape
    h = R // 2
    tr = _stream_rows(h, C)
    nb = h // tr

    def body(c_ref, a_ref, b_ref, o_ref):
        o_ref[...] = (a_ref[...].astype(F32) + b_ref[...].astype(F32)).astype(out_dtype)

    return pl.pallas_call(
        body, name=name,
        grid_spec=pltpu.PrefetchScalarGridSpec(
            num_scalar_prefetch=1, grid=(n, nb),
            in_specs=[pl.BlockSpec((None, tr, C), lambda p, i, cr: (p, cr[0] * nb + i, 0)),
                      pl.BlockSpec((None, tr, C), lambda p, i, cr: (p, i, 0))],
            out_specs=pl.BlockSpec((None, tr, C), lambda p, i, cr: (p, i, 0))),
        out_shape=jax.ShapeDtypeStruct((n, h, C), out_dtype), compiler_params=_params(("parallel", "parallel")),
    )(jnp.reshape(c, (1,)).astype(jnp.int32), full, theirs)


def sum_slots(a, name, out_dtype=F32):
    n, R, C = a.shape
    tr = _stream_rows(R, C)

    def body(*refs):
        acc = refs[0][...].astype(F32)
        for r in refs[1:n]:
            acc = acc + r[...].astype(F32)
        refs[n][...] = acc.astype(out_dtype)

    return pl.pallas_call(
        body, name=name, grid=(R // tr,),
        in_specs=[pl.BlockSpec((None, tr, C), lambda i, p=p: (p, i, 0)) for p in range(n)],
        out_specs=pl.BlockSpec((tr, C), lambda i: (i, 0)),
        out_shape=jax.ShapeDtypeStruct((R, C), out_dtype), compiler_params=_params(("parallel",)),
    )(*([a] * n))


def _coords():
    return lax.axis_index("x"), lax.axis_index("y"), lax.axis_index("c")


def _other_chips(x, y):
    return [(1 - x, y), (x, 1 - y), (1 - x, 1 - y)]


def _hbm_call(body, name, ins, out_shapes, n_sems):
    return pl.pallas_call(
        body, name=name, out_shape=tuple(out_shapes),
        in_specs=[pl.BlockSpec(memory_space=pl.ANY)] * len(ins),
        out_specs=tuple(pl.BlockSpec(memory_space=pl.ANY) for _ in out_shapes),
        scratch_shapes=[pltpu.SemaphoreType.DMA((n_sems,)), pltpu.SemaphoreType.DMA((n_sems,))],
    )(*ins)


def allgather_small(v, name):
    m_per, n = v.shape

    def body(x_ref, out_ref, send_sems, recv_sems, local_sem):
        x, y, c = _coords()
        me, sibling = (x, y, c), (x, y, 1 - c)
        chips = _other_chips(x, y)

        def rows(px, py, pc):
            return out_ref.at[pl.ds((4 * px + 2 * py + pc) * m_per, m_per), :]

        def copy(k, block, to, src=None):
            return pltpu.make_async_remote_copy(
                src_ref=rows(*block) if src is None else src, dst_ref=rows(*block),
                send_sem=send_sems.at[k], recv_sem=recv_sems.at[k], device_id=to, device_id_type=MESH)

        mine = pltpu.make_async_copy(x_ref, rows(*me), local_sem)
        mine.start()
        first = [copy(0, me, sibling, src=x_ref)]
        first += [copy(1 + j, me, (*chip, c), src=x_ref) for j, chip in enumerate(chips)]
        for cp in first:
            cp.start()
        passed = [copy(4 + j, (*chip, c), sibling) for j, chip in enumerate(chips)]
        for j, chip in enumerate(chips):
            copy(1 + j, (*chip, c), me).wait_recv()
            passed[j].start()
        copy(0, sibling, me).wait_recv()
        for j, chip in enumerate(chips):
            copy(4 + j, (*chip, 1 - c), me).wait_recv()
        for cp in first + passed:
            cp.wait_send()
        mine.wait()

    return pl.pallas_call(
        body, name=name, out_shape=jax.ShapeDtypeStruct((N_DEV * m_per, n), v.dtype),
        in_specs=[pl.BlockSpec(memory_space=pltpu.VMEM)], out_specs=pl.BlockSpec(memory_space=pltpu.VMEM),
        scratch_shapes=[pltpu.SemaphoreType.DMA((7,)), pltpu.SemaphoreType.DMA((7,)), pltpu.SemaphoreType.DMA],
        compiler_params=pltpu.CompilerParams(vmem_limit_bytes=V7X_VMEM_LIMIT),
    )(v)


def allgather_chips(arrs, name):
    n = len(arrs)
    halves = [a.shape[0] // 2 for a in arrs]

    def body(*refs):
        xs, outs = refs[:n], refs[n:2 * n]
        send_sems, recv_sems = refs[2 * n:]
        x, y, c = _coords()
        me, sibling, chips = 2 * x + y, (x, y, 1 - c), _other_chips(x, y)

        def half(ref, cc, i):
            return ref.at[pl.ds(cc * halves[i], halves[i]), :]

        def copy(i, k, src, dst, to):
            return pltpu.make_async_remote_copy(src_ref=src, dst_ref=dst, send_sem=send_sems.at[6 * i + k],
                                                recv_sem=recv_sems.at[6 * i + k], device_id=to, device_id_type=MESH)

        sends = [copy(i, k, half(xs[i], c, i), half(outs[i].at[me], c, i), (*chip, c))
                 for k, chip in enumerate(chips) for i in range(n)]
        for cp in sends:
            cp.start()
        passed = []
        for k, (px, py) in enumerate(chips):
            for i in range(n):
                landed = half(outs[i].at[2 * px + py], c, i)
                copy(i, k, landed, landed, (px, py, c)).wait_recv()
                passed.append(copy(i, 3 + k, landed, landed, sibling))
                passed[-1].start()
        for k, (px, py) in enumerate(chips):
            for i in range(n):
                theirs = half(outs[i].at[2 * px + py], 1 - c, i)
                copy(i, 3 + k, theirs, theirs, sibling).wait_recv()
        for cp in sends + passed:
            cp.wait_send()

    return _hbm_call(body, name, arrs, [jax.ShapeDtypeStruct((N_CHIPS,) + a.shape, a.dtype) for a in arrs], 6 * n)


def pair_exchange(arrs, name):
    n = len(arrs)
    halves = [a.shape[1] // 2 for a in arrs]

    def body(*refs):
        xs, theirs = refs[:n], refs[n:2 * n]
        send_sems, recv_sems = refs[2 * n:]
        x, y, c = _coords()
        cps = [pltpu.make_async_remote_copy(
            src_ref=xs[i].at[:, pl.ds((1 - c) * halves[i], halves[i]), :], dst_ref=theirs[i],
            send_sem=send_sems.at[i], recv_sem=recv_sems.at[i], device_id=(x, y, 1 - c), device_id_type=MESH)
            for i in range(n)]
        for cp in cps:
            cp.start()
        for cp in cps:
            cp.wait()

    shapes = [jax.ShapeDtypeStruct((a.shape[0], a.shape[1] // 2, a.shape[2]), a.dtype) for a in arrs]
    return _hbm_call(body, name, arrs, shapes, n)


def scatter_chips(arrs, name):
    n = len(arrs)

    def body(*refs):
        xs, outs = refs[:n], refs[n:2 * n]
        send_sems, recv_sems = refs[2 * n:]
        x, y, c = _coords()
        me, chips = 2 * x + y, _other_chips(x, y)

        def copy(i, k, src_slot, dst_slot, to):
            return pltpu.make_async_remote_copy(
                src_ref=xs[i].at[src_slot], dst_ref=outs[i].at[dst_slot], send_sem=send_sems.at[3 * i + k],
                recv_sem=recv_sems.at[3 * i + k], device_id=to, device_id_type=MESH)

        sends = [copy(i, k, 2 * px + py, me, (px, py, c)) for k, (px, py) in enumerate(chips) for i in range(n)]
        for cp in sends:
            cp.start()
        for k, (px, py) in enumerate(chips):
            for i in range(n):
                copy(i, k, me, 2 * px + py, (px, py, c)).wait_recv()
        for cp in sends:
            cp.wait_send()

    return _hbm_call(body, name, arrs, [jax.ShapeDtypeStruct(a.shape, a.dtype) for a in arrs], 3 * n)


def pair_allgather(arrs, name):
    n = len(arrs)

    def body(*refs):
        xs, outs = refs[:n], refs[n:2 * n]
        send_sems, recv_sems = refs[2 * n:]
        x, y, c = _coords()
        cps = [pltpu.make_async_remote_copy(
            src_ref=xs[i], dst_ref=outs[i].at[pl.ds(c * xs[i].shape[0], xs[i].shape[0]), :], send_sem=send_sems.at[i],
            recv_sem=recv_sems.at[i], device_id=(x, y, 1 - c), device_id_type=MESH) for i in range(n)]
        for cp in cps:
            cp.start()
        for cp in cps:
            cp.wait()

    return _hbm_call(body, name, arrs, [jax.ShapeDtypeStruct((2 * a.shape[0], a.shape[1]), a.dtype) for a in arrs], n)


GROUPS = (("ffn_w_gu",), ("ffn_w_down", "mla_w_o", "ssd_w_out"), ("mla_w_a",), ("mla_w_qb",), ("mla_w_kvb",), ("ssd_w_in",))


def _pad_rows(a, mult):
    r = (-a.shape[0]) % mult
    return a if r == 0 else jnp.concatenate([a, jnp.zeros((r, a.shape[1]), a.dtype)], axis=0)


def _rows2d(a):
    return a.reshape(-1, a.shape[-1])


def _unstack(st, axis):
    full = jnp.moveaxis(st, 0, axis)
    sh = list(full.shape)
    sh[axis:axis + 2] = [sh[axis] * sh[axis + 1]]
    return full.reshape(sh)


def _stack(full, axis):
    sh = list(full.shape)
    sh[axis:axis + 1] = [N_CHIPS, sh[axis] // N_CHIPS]
    return jnp.moveaxis(full.reshape(sh), axis, 0)


def kernel(x, c, positions, norm_gain, ada_w, ada_b, ffn_w_gu, ffn_w_down, mla_w_a, mla_q_a_gain, mla_kv_a_gain, mla_w_qb, mla_w_kvb, mla_q_gain, mla_k_gain, mla_w_o, ssd_w_in, ssd_conv_w, ssd_conv_b, ssd_dt_bias, ssd_a_log, ssd_d, ssd_norm_gain, ssd_w_out, loss_target, m_norm_gain, m_ada_w, m_ada_b, m_ffn_w_gu, m_ffn_w_down, m_mla_w_a, m_mla_q_a_gain, m_mla_kv_a_gain, m_mla_w_qb, m_mla_w_kvb, m_mla_q_gain, m_mla_k_gain, m_mla_w_o, m_ssd_w_in, m_ssd_conv_w, m_ssd_conv_b, m_ssd_dt_bias, m_ssd_a_log, m_ssd_d, m_ssd_norm_gain, m_ssd_w_out, v_norm_gain, v_ada_w, v_ada_b, v_ffn_w_gu, v_ffn_w_down, v_mla_w_a, v_mla_q_a_gain, v_mla_kv_a_gain, v_mla_w_qb, v_mla_w_kvb, v_mla_q_gain, v_mla_k_gain, v_mla_w_o, v_ssd_w_in, v_ssd_conv_w, v_ssd_conv_b, v_ssd_dt_bias, v_ssd_a_log, v_ssd_d, v_ssd_norm_gain, v_ssd_w_out):
    w_in = dict(norm_gain=norm_gain, ada_w=ada_w, ada_b=ada_b, ffn_w_gu=ffn_w_gu, ffn_w_down=ffn_w_down, mla_w_a=mla_w_a, mla_q_a_gain=mla_q_a_gain, mla_kv_a_gain=mla_kv_a_gain, mla_w_qb=mla_w_qb, mla_w_kvb=mla_w_kvb, mla_q_gain=mla_q_gain, mla_k_gain=mla_k_gain, mla_w_o=mla_w_o, ssd_w_in=ssd_w_in, ssd_conv_w=ssd_conv_w, ssd_conv_b=ssd_conv_b, ssd_dt_bias=ssd_dt_bias, ssd_a_log=ssd_a_log, ssd_d=ssd_d, ssd_norm_gain=ssd_norm_gain, ssd_w_out=ssd_w_out)
    m_in = dict(norm_gain=m_norm_gain, ada_w=m_ada_w, ada_b=m_ada_b, ffn_w_gu=m_ffn_w_gu, ffn_w_down=m_ffn_w_down, mla_w_a=m_mla_w_a, mla_q_a_gain=m_mla_q_a_gain, mla_kv_a_gain=m_mla_kv_a_gain, mla_w_qb=m_mla_w_qb, mla_w_kvb=m_mla_w_kvb, mla_q_gain=m_mla_q_gain, mla_k_gain=m_mla_k_gain, mla_w_o=m_mla_w_o, ssd_w_in=m_ssd_w_in, ssd_conv_w=m_ssd_conv_w, ssd_conv_b=m_ssd_conv_b, ssd_dt_bias=m_ssd_dt_bias, ssd_a_log=m_ssd_a_log, ssd_d=m_ssd_d, ssd_norm_gain=m_ssd_norm_gain, ssd_w_out=m_ssd_w_out)
    v_in = dict(norm_gain=v_norm_gain, ada_w=v_ada_w, ada_b=v_ada_b, ffn_w_gu=v_ffn_w_gu, ffn_w_down=v_ffn_w_down, mla_w_a=v_mla_w_a, mla_q_a_gain=v_mla_q_a_gain, mla_kv_a_gain=v_mla_kv_a_gain, mla_w_qb=v_mla_w_qb, mla_w_kvb=v_mla_w_kvb, mla_q_gain=v_mla_q_gain, mla_k_gain=v_mla_k_gain, mla_w_o=v_mla_w_o, ssd_w_in=v_ssd_w_in, ssd_conv_w=v_ssd_conv_w, ssd_conv_b=v_ssd_conv_b, ssd_dt_bias=v_ssd_dt_bias, ssd_a_log=v_ssd_a_log, ssd_d=v_ssd_d, ssd_norm_gain=v_ssd_norm_gain, ssd_w_out=v_ssd_w_out)
    names = list(w_in)
    xi, yi, ci = _coords()
    chip = 2 * xi + yi
    batch = 4 * xi + 2 * yi + ci
    x2, target = x[0], loss_target[0]

    small_sharded = ("norm_gain", "ssd_conv_w", "ssd_conv_b", "ssd_norm_gain")
    pack0 = jnp.concatenate([c.reshape(-1)] + [w_in[n].reshape(-1) for n in small_sharded])
    pack0 = _pad_rows(pack0.reshape(-1, 128), 8)
    g0 = allgather_small(pack0, "gather_small").reshape(N_DEV, -1)
    c_all = g0[:, :D_MODEL]
    P, off = {}, D_MODEL
    for n in small_sharded:
        sz = w_in[n].size
        st = g0[0::2, off:off + sz].reshape((N_CHIPS,) + w_in[n].shape)
        P[n] = _unstack(st, w_in[n].ndim - 1)
        off += sz
    for n in ("mla_q_a_gain", "mla_kv_a_gain", "mla_q_gain", "mla_k_gain", "ssd_dt_bias", "ssd_a_log", "ssd_d"):
        P[n] = w_in[n]

    sc = _silu(c_all)
    n_ada = ada_w.shape[2]
    b_sh = lax.dynamic_slice_in_dim(ada_b, chip * n_ada, n_ada, axis=1)
    mods_sh = jnp.stack([matmul(sc, ada_w[l], "nn", "ada_fwd") for l in range(DEPTH)]) + b_sh[:, None, :]
    g1 = allgather_small(mods_sh.reshape(-1, 128), "gather_mods").reshape(N_DEV, DEPTH, N_DEV, n_ada)
    mods = lax.dynamic_index_in_dim(g1[0::2], batch, axis=2, keepdims=False)
    mods = mods.transpose(1, 0, 2).reshape(DEPTH, 3, 3, D_MODEL)

    shard_groups = [jnp.concatenate([_rows2d(w_in[n]).astype(BF16) for n in grp], axis=0) for grp in GROUPS]
    gathered = allgather_chips(shard_groups, "gather_weights")
    gathered = [lax.dynamic_update_slice(g, s[None], (chip, 0, 0)) for g, s in zip(gathered, shard_groups)]
    G = {}
    for grp, arr in zip(GROUPS, gathered):
        off = 0
        for n in grp:
            rows = w_in[n].size // w_in[n].shape[-1]
            G[n] = arr[:, off:off + rows].reshape((N_CHIPS,) + w_in[n].shape)
            off += rows
    W = {
        "ffn_w_gu": gathered[0],
        "ffn_w_down": [[_unstack(G["ffn_w_down"][:, i, t], 0) for t in range(2)] for i in range(DEPTH)],
        "mla_w_a": [_unstack(G["mla_w_a"][:, j], 0) for j in range(2)],
        "mla_w_qb": [_unstack(G["mla_w_qb"][:, j], 1) for j in range(2)],
        "mla_w_kvb": [_unstack(G["mla_w_kvb"][:, j], 1) for j in range(2)],
        "mla_w_o": [_unstack(G["mla_w_o"][:, j], 0) for j in range(2)],
        "ssd_w_out": [_unstack(G["ssd_w_out"][:, j], 0) for j in range(2)],
    }
    w_in_full = [_unstack(G["ssd_w_in"][:, j], 1) for j in range(2)]
    W["ssd_w_z"] = [w[:, :D_INNER] for w in w_in_full]
    W["ssd_w_xbc"] = [w[:, D_INNER:D_INNER + CONV_DIM] for w in w_in_full]
    W["ssd_w_dt"] = [w[:, D_INNER + CONV_DIM:] for w in w_in_full]
    X = stand_ins(W)

    pos = positions[0]
    y, vjp = jax.vjp(lambda a, b, p_, x_: trunk(a, b, p_, W, x_, pos), x2, mods, P, X)
    dy, loss_cols = loss_head(y, target)
    dx, dmods, dP, dX = vjp(dy)
    loss = lax.psum(jnp.sum(loss_cols), ("x", "y", "c"))

    small_names = ("norm_gain", "ssd_conv_w", "ssd_conv_b", "ssd_norm_gain", "mla_q_a_gain", "mla_kv_a_gain",
                   "mla_q_gain", "mla_k_gain", "ssd_dt_bias", "ssd_a_log", "ssd_d")
    pack1 = jnp.concatenate([dmods.reshape(-1)] + [dP[n].reshape(-1) for n in small_names])
    pack1 = _pad_rows(jnp.pad(pack1, (0, (-pack1.size) % 128)).reshape(-1, 128), 8)
    rows1 = pack1.shape[0]
    g2 = allgather_small(pack1, "gather_small_grads")
    tot = sum_slots(g2.reshape(N_DEV, rows1, 128), "sum_small_grads").reshape(-1)
    n_mod = DEPTH * 9 * D_MODEL
    grads = {"ada_b": tot[:n_mod].reshape(DEPTH, 9 * D_MODEL)}
    off = n_mod
    for n in small_names:
        sz = dP[n].size
        full = tot[off:off + sz].reshape(dP[n].shape)
        off += sz
        if n in small_sharded:
            k = w_in[n].shape[-1]
            full = lax.dynamic_slice_in_dim(full, chip * k, k, axis=full.ndim - 1)
        grads[n] = full
    dmods_all = g2.reshape(N_DEV, -1)[:, :n_mod].reshape(N_DEV, DEPTH, 9 * D_MODEL)
    dm_sh = lax.dynamic_slice_in_dim(dmods_all, chip * n_ada, n_ada, axis=2)
    grads["ada_w"] = jnp.stack([matmul(sc, dm_sh[:, l], "tn", "ada_dw") for l in range(DEPTH)])

    w_in_g = [jnp.concatenate([dX["ssd_w_z"][j], dX["ssd_w_xbc"][j], dX["ssd_w_dt"][j]], axis=1) for j in range(2)]
    per_name = {
        "ffn_w_gu": [dX["ffn_w_gu"][i][t] for i in range(DEPTH) for t in range(2)],
        "ffn_w_down": [dX["ffn_w_down"][i][t].reshape(N_CHIPS, -1, D_MODEL) for i in range(DEPTH) for t in range(2)],
        "mla_w_a": [g.reshape(N_CHIPS, -1, g.shape[-1]) for g in dX["mla_w_a"]],
        "mla_w_qb": dX["mla_w_qb"], "mla_w_kvb": dX["mla_w_kvb"],
        "mla_w_o": [g.reshape(N_CHIPS, -1, D_MODEL) for g in dX["mla_w_o"]],
        "ssd_w_out": [g.reshape(N_CHIPS, -1, D_MODEL) for g in dX["ssd_w_out"]],
        "ssd_w_in": [_stack(g, 1) for g in w_in_g],
    }
    grad_groups = [jnp.concatenate([p for n in grp for p in per_name[n]], axis=1) for grp in GROUPS]
    theirs = pair_exchange(grad_groups, "grads_to_sibling")
    pair = [sum_own_half(a, b, ci, "sum_pair", BF16) for a, b in zip(grad_groups, theirs)]
    landed = scatter_chips(pair, "grads_to_chips")
    landed = [lax.dynamic_update_slice(a, lax.dynamic_slice_in_dim(p, chip, 1, axis=0), (chip, 0, 0)) for a, p in zip(landed, pair)]
    half_sums = [sum_slots(a, "sum_chips") for a in landed]
    totals = pair_allgather(half_sums, "grad_halves_swap")
    totals = [lax.dynamic_update_slice(t, h, (ci * h.shape[0], 0)) for t, h in zip(totals, half_sums)]
    for grp, arr in zip(GROUPS, totals):
        off = 0
        for n in grp:
            rows = w_in[n].size // w_in[n].shape[-1]
            grads[n] = arr[off:off + rows].reshape(w_in[n].shape)
            off += rows

    deltas, new_m, new_v = {}, {}, {}
    for n in names:
        w = w_in[n]
        d, nm, nv = adamw(_rows2d(w), _rows2d(m_in[n]), _rows2d(v_in[n]), _rows2d(grads[n]))
        deltas[n], new_m[n], new_v[n] = d.reshape(w.shape), nm.reshape(w.shape), nv.reshape(w.shape)

    return (loss, dx[None], *[grads[n] for n in names], *[deltas[n] for n in names],
            *[new_m[n] for n in names], *[new_v[n] for n in names])
```

```python
import numpy as np

import jax
import jax.numpy as jnp
from jax import lax
from jax.experimental import pallas as pl
from jax.experimental.pallas import tpu as pltpu

F32 = jnp.float32
BF16 = jnp.bfloat16
MESH = pl.DeviceIdType.MESH

D_MODEL = 1024
DEPTH = 4
EPS = 1e-6
D_FF = 2816
MLA_HEADS = 16
Q_LORA = 384
KV_LORA = 256
QK_NOPE = 64
QK_ROPE = 32
QK_HEAD = QK_NOPE + QK_ROPE
V_HEAD = 64
ROPE_THETA = 10000.0
D_INNER = 2048
SSD_HEAD_DIM = 64
SSD_HEADS = 32
SSD_GROUPS = 4
SSD_STATE = 128
CONV_WIDTH = 4
CHUNK = 128
CONV_DIM = D_INNER + 2 * SSD_GROUPS * SSD_STATE
ADAM_LR = 0.001
ADAM_B1 = 0.9
ADAM_B2 = 0.999
ADAM_EPS = 1e-08
ADAM_WD = 0.01
ADAM_STEP = 10

N_CHIPS = 4
N_DEV = 8
V7X_VMEM_LIMIT = 56 * 1024 * 1024
ATTN_HEADS_PER_STEP = 4
LANES = 128
LOG2E = 1.4426950408889634


def _params(sem=None):
    return pltpu.CompilerParams(dimension_semantics=sem, vmem_limit_bytes=V7X_VMEM_LIMIT)


def _div_tile(n, pref, quantum):
    if n <= pref:
        return n
    t = (pref // quantum) * quantum
    while t >= quantum:
        if n % t == 0:
            return t
        t -= quantum
    return n


def matmul(a, b, mode, name, out_dtype=F32, stack=0, b_slot=None, tm=1024, tn=1408, tk=2816):
    if b_slot is not None:
        (roff, rows), (nsl, _, n) = b_slot, b.shape
        (M, K) = a.shape
        K2, N = (rows, nsl * n) if mode == "nn" else (nsl * n, rows)
    elif mode == "nn":
        (M, K), (K2, N) = a.shape, b.shape
    elif mode == "nt":
        (M, K), (N, K2) = a.shape, b.shape
    else:
        (K, M), (K2, N) = a.shape, b.shape
    assert K == K2, (a.shape, b.shape, mode)
    tm = _div_tile(M, tm, 128 if mode == "tn" else 16)
    tn = _div_tile(N // stack if stack else (n if b_slot and mode == "nn" else N), tn, 128)
    tk = _div_tile(n if b_slot and mode == "nt" else K, tk, 128)
    nk = K // tk
    if mode == "nn":
        a_spec = pl.BlockSpec((tm, tk), lambda i, j, k: (i, k))
        b_spec = pl.BlockSpec((tk, tn), lambda i, j, k: (k, j))
        dims = (((1,), (0,)), ((), ()))
        if b_slot:
            assert roff % tk == 0
            nbn, rb = n // tn, roff // tk
            b_spec = pl.BlockSpec((None, tk, tn), lambda i, j, k: (j // nbn, rb + k, j % nbn))
    elif mode == "nt":
        a_spec = pl.BlockSpec((tm, tk), lambda i, j, k: (i, k))
        b_spec = pl.BlockSpec((tn, tk), lambda i, j, k: (j, k))
        dims = (((1,), (1,)), ((), ()))
        if b_slot:
            assert roff % tn == 0
            nbk, rb = n // tk, roff // tn
            b_spec = pl.BlockSpec((None, tn, tk), lambda i, j, k: (k // nbk, rb + j, k % nbk))
    else:
        a_spec = pl.BlockSpec((tk, tm), lambda i, j, k: (k, i))
        b_spec = pl.BlockSpec((tk, tn), lambda i, j, k: (k, j))
        dims = (((0,), (0,)), ((), ()))
    if stack:
        nb = N // stack // tn
        out_spec = pl.BlockSpec((None, tm, tn), lambda i, j, k: (j // nb, i, j % nb))
        out_shape = jax.ShapeDtypeStruct((stack, M, N // stack), out_dtype)
    else:
        out_spec = pl.BlockSpec((tm, tn), lambda i, j, k: (i, j))
        out_shape = jax.ShapeDtypeStruct((M, N), out_dtype)
    use_acc = nk > 1 and out_dtype != F32

    def body(a_ref, b_ref, o_ref, *acc):
        p = lax.dot_general(a_ref[...].astype(BF16), b_ref[...].astype(BF16), dims, preferred_element_type=F32)
        if nk == 1:
            o_ref[...] = p.astype(out_dtype)
            return
        acc_ref = acc[0] if use_acc else o_ref
        k = pl.program_id(2)

        @pl.when(k == 0)
        def _():
            acc_ref[...] = p

        @pl.when(k > 0)
        def _():
            acc_ref[...] += p

        if use_acc:
            @pl.when(k == nk - 1)
            def _():
                o_ref[...] = acc_ref[...].astype(out_dtype)

    return pl.pallas_call(
        body, name=name, grid=(M // tm, N // tn, nk), in_specs=[a_spec, b_spec], out_specs=out_spec, out_shape=out_shape,
        scratch_shapes=[pltpu.VMEM((tm, tn), F32)] if use_acc else [],
        compiler_params=_params(("parallel", "parallel", "arbitrary")),
    )(a, b)


def mm_op(name, out_dtype=F32, stack=0, b_slot=None):
    @jax.custom_vjp
    def op(a, w, wp):
        return matmul(a, w, "nn", name + "_fwd", out_dtype=out_dtype, b_slot=b_slot)

    def fwd(a, w, wp):
        return op(a, w, wp), (a, w)

    def bwd(res, g):
        a, w = res
        return (matmul(g, w, "nt", name + "_dx", out_dtype=a.dtype, b_slot=b_slot), jnp.zeros_like(w),
                matmul(a, g, "tn", name + "_dw", stack=stack, tn=1408, tk=1024))

    op.defvjp(fwd, bwd)
    return op


def _row_tile(rows, widths):
    w = max(widths)
    t = 128 if w > 4096 else (256 if w > 1024 else (512 if w > 128 else 2048))
    return _div_tile(rows, t, 16)


def _row_spec(r, tm):
    nb = r.shape[0] // tm
    return pl.BlockSpec((tm, r.shape[1]), lambda i: (i % nb, 0))


def _rowwise_fwd(f, rows, vecs, name, out_dtype):
    n_r, n_v = len(rows), len(vecs)
    S = rows[0].shape[0]
    tm = _row_tile(min(r.shape[0] for r in rows), [r.shape[1] for r in rows])
    outs = jax.eval_shape(f, *[jax.ShapeDtypeStruct((tm, r.shape[1]), F32) for r in rows], *vecs)
    out_dtypes = out_dtype if isinstance(out_dtype, tuple) else (out_dtype,) * len(outs)

    def body(*refs):
        res = f(*[r[...].astype(F32) for r in refs[: n_r + n_v]])
        for o, r in zip(refs[n_r + n_v:], res):
            o[...] = r.astype(o.dtype)

    return pl.pallas_call(
        body, name=name, grid=(S // tm,),
        in_specs=[_row_spec(r, tm) for r in rows] + [pl.BlockSpec(v.shape, lambda i: (0, 0)) for v in vecs],
        out_specs=tuple(pl.BlockSpec((tm, o.shape[1]), lambda i: (i, 0)) for o in outs),
        out_shape=tuple(jax.ShapeDtypeStruct((S, o.shape[1]), dt) for o, dt in zip(outs, out_dtypes)),
        compiler_params=_params(("parallel",)),
    )(*rows, *vecs)


def _rowwise_bwd(f, rows, vecs, douts, diff_rows, n_const, name):
    n_r, n_o = len(rows), len(douts)
    consts, vecs = vecs[len(vecs) - n_const:], vecs[:len(vecs) - n_const]
    n_v = len(vecs)
    S = rows[0].shape[0]
    tm = _row_tile(min(r.shape[0] for r in rows), [r.shape[1] for r in rows] + [d.shape[1] for d in douts])
    d_idx = [i for i in range(n_r) if diff_rows[i]]

    def body(*refs):
        row_v = [r[...].astype(F32) for r in refs[:n_r]]
        vec_v = [r[...] for r in refs[n_r:n_r + n_v]]
        const_v = [r[...] for r in refs[n_r + n_v:n_r + n_v + n_const]]
        dout_v = tuple(r[...].astype(F32) for r in refs[n_r + n_v + n_const:n_r + n_v + n_const + n_o])
        out_refs = refs[n_r + n_v + n_const + n_o:]

        def g(*args):
            full = list(row_v)
            for j, i in enumerate(d_idx):
                full[i] = args[j]
            return f(*full, *args[len(d_idx):], *const_v)

        _, vjp = jax.vjp(g, *[row_v[i] for i in d_idx], *vec_v)
        grads = vjp(dout_v)
        for j in range(len(d_idx)):
            out_refs[j][...] = grads[j].astype(out_refs[j].dtype)
        step = pl.program_id(0)
        for j in range(n_v):
            gv, o = grads[len(d_idx) + j], out_refs[len(d_idx) + j]

            @pl.when(step == 0)
            def _(gv=gv, o=o):
                o[...] = gv

            @pl.when(step > 0)
            def _(gv=gv, o=o):
                o[...] += gv

    res = pl.pallas_call(
        body, name=name, grid=(S // tm,),
        in_specs=[_row_spec(r, tm) for r in rows] + [pl.BlockSpec(v.shape, lambda i: (0, 0)) for v in vecs + consts]
        + [pl.BlockSpec((tm, d.shape[1]), lambda i: (i, 0)) for d in douts],
        out_specs=tuple([pl.BlockSpec((tm, rows[i].shape[1]), lambda i_: (i_, 0)) for i in d_idx]
                        + [pl.BlockSpec(v.shape, lambda i: (0, 0)) for v in vecs]),
        out_shape=tuple([jax.ShapeDtypeStruct(rows[i].shape, rows[i].dtype) for i in d_idx]
                        + [jax.ShapeDtypeStruct(v.shape, F32) for v in vecs]),
        compiler_params=_params(("arbitrary",)),
    )(*rows, *vecs, *consts, *douts)
    drows = [None] * n_r
    for j, i in enumerate(d_idx):
        drows[i] = res[j]
    for i in range(n_r):
        if drows[i] is None:
            drows[i] = jnp.zeros_like(rows[i])
    return tuple(drows) + tuple(res[len(d_idx):]) + tuple(jnp.zeros_like(k) for k in consts)


def rowwise_op(f, name, n_rows, diff_rows=None, out_dtype=F32, n_const=0):
    diff = tuple(diff_rows) if diff_rows is not None else (True,) * n_rows

    @jax.custom_vjp
    def op(*args):
        return _rowwise_fwd(f, args[:n_rows], args[n_rows:], name + "_fwd", out_dtype)

    def fwd(*args):
        return op(*args), args

    def bwd(args, douts):
        return _rowwise_bwd(f, args[:n_rows], args[n_rows:], douts, diff, n_const, name + "_bwd")

    op.defvjp(fwd, bwd)
    return op


def _rms(x, gain):
    return x * lax.rsqrt(jnp.mean(x * x, axis=-1, keepdims=True) + EPS) * gain


def _silu(x):
    return x * jax.nn.sigmoid(x)


def _f_modulate(x, gain, shift, scale):
    return (_rms(x, gain) * (1.0 + scale) + shift,)


def _f_rms(x, gain):
    return (_rms(x, gain),)


def _f_swiglu(gu):
    n = gu.shape[1] // 2
    return (_silu(gu[:, :n]) * gu[:, n:],)


def _f_resid(coef):
    def f(x, y, gate):
        return (x + (coef * gate) * y,)
    return f


def _f_resid_modulate(coef):
    def f(x, y, gate, gain, shift, scale):
        x1 = x + (coef * gate) * y
        return (x1, _rms(x1, gain) * (1.0 + scale) + shift)
    return f


def _f_head_rope(x, cosf, sinf, gain, swap):
    y = _rms(x, gain)
    return (y * cosf + jnp.dot(y, swap, precision=lax.Precision.HIGHEST, preferred_element_type=F32) * sinf,)


def _rope_swap():
    m = np.zeros((QK_HEAD, QK_HEAD), np.float32)
    half = QK_ROPE // 2
    for i in range(half):
        m[QK_NOPE + half + i, QK_NOPE + i] = -1.0
        m[QK_NOPE + i, QK_NOPE + half + i] = 1.0
    return jnp.asarray(m)


def _f_gated_norm(y, z, gain):
    g = y * _silu(z)
    n = g.shape[1] // SSD_GROUPS
    return (jnp.concatenate([_rms(g[:, i * n:(i + 1) * n], gain[:, i * n:(i + 1) * n]) for i in range(SSD_GROUPS)], axis=1),)


HALO = 8


def _conv_taps(ext, w, rows, off):
    acc = None
    for k in range(CONV_WIDTH):
        term = w[k:k + 1, :] * pltpu.roll(ext, CONV_WIDTH - 1 - k, 0)[off:off + rows]
        acc = term if acc is None else acc + term
    return acc


def _conv_specs(S, tm, tc):
    tile = pl.BlockSpec((tm, tc), lambda j, i: (i, j))
    prev = pl.BlockSpec((HALO, tc), lambda j, i: (jnp.maximum(i * (tm // HALO) - 1, 0), j))
    nxt = pl.BlockSpec((HALO, tc), lambda j, i: (jnp.minimum((i + 1) * (tm // HALO), S // HALO - 1), j))
    wspec = pl.BlockSpec((CONV_WIDTH, tc), lambda j, i: (0, j))
    bspec = pl.BlockSpec((1, tc), lambda j, i: (0, j))
    return tile, prev, nxt, wspec, bspec


def conv_silu_fwd(u, w, b, tm=512, tc=1024):
    S, C = u.shape
    tm = _div_tile(S, tm, HALO)

    def body(u_ref, p_ref, w_ref, b_ref, o_ref):
        prev = jnp.where(pl.program_id(1) == 0, 0.0, p_ref[...])
        ext = jnp.concatenate([prev, u_ref[...]], axis=0)
        conv = _conv_taps(ext, w_ref[...], tm, HALO) + b_ref[...]
        o_ref[...] = conv * jax.nn.sigmoid(conv)

    tile, prev, _, wspec, bspec = _conv_specs(S, tm, tc)
    return pl.pallas_call(
        body, name="conv_silu_fwd", grid=(C // tc, S // tm), in_specs=[tile, prev, wspec, bspec], out_specs=tile,
        out_shape=jax.ShapeDtypeStruct((S, C), F32), compiler_params=_params(("parallel", "arbitrary")),
    )(u, u, w, b)


def conv_silu_bwd(u, w, b, dout, tm=512, tc=1024):
    S, C = u.shape
    tm = _div_tile(S, tm, HALO)
    n = S // tm
    ext_rows = tm + HALO

    def body(u_ref, p_ref, n_ref, g_ref, gn_ref, w_ref, b_ref, du_ref, dw_ref, db_ref):
        i = pl.program_id(1)
        wv = w_ref[...]
        prev = jnp.where(i == 0, 0.0, p_ref[...])
        ext = jnp.concatenate([prev, u_ref[...], n_ref[...]], axis=0)
        conv = _conv_taps(ext, wv, ext_rows, HALO) + b_ref[...]
        g_ext = jnp.concatenate([g_ref[...], jnp.where(i == n - 1, 0.0, gn_ref[...])], axis=0)
        sg = jax.nn.sigmoid(conv)
        dconv = g_ext * (sg * (1.0 + conv * (1.0 - sg)))
        du = None
        for k in range(CONV_WIDTH):
            s = CONV_WIDTH - 1 - k
            term = wv[k:k + 1, :] * pltpu.roll(dconv, (ext_rows - s) % ext_rows, 0)[:tm]
            du = term if du is None else du + term
        du_ref[...] = du
        dc = dconv[:tm]
        dw = jnp.concatenate([jnp.sum(dc * pltpu.roll(ext, CONV_WIDTH - 1 - k, 0)[HALO:HALO + tm], axis=0, keepdims=True)
                              for k in range(CONV_WIDTH)], axis=0)
        dbv = jnp.sum(dc, axis=0, keepdims=True)

        @pl.when(i == 0)
        def _():
            dw_ref[...] = dw
            db_ref[...] = dbv

        @pl.when(i > 0)
        def _():
            dw_ref[...] += dw
            db_ref[...] += dbv

    tile, prev, nxt, wspec, bspec = _conv_specs(S, tm, tc)
    return pl.pallas_call(
        body, name="conv_silu_bwd", grid=(C // tc, n), in_specs=[tile, prev, nxt, tile, nxt, wspec, bspec],
        out_specs=(tile, wspec, bspec),
        out_shape=(jax.ShapeDtypeStruct((S, C), F32), jax.ShapeDtypeStruct((CONV_WIDTH, C), F32),
                   jax.ShapeDtypeStruct((1, C), F32)),
        compiler_params=_params(("parallel", "arbitrary")),
    )(u, u, u, dout, dout, w, b)


@jax.custom_vjp
def conv_silu(u, w, b):
    return conv_silu_fwd(u, w, b)


def _conv_silu_fwd(u, w, b):
    return conv_silu_fwd(u, w, b), (u, w, b)


def _conv_silu_bwd(res, dout):
    return conv_silu_bwd(*res, dout)


conv_silu.defvjp(_conv_silu_fwd, _conv_silu_bwd)


_NT = (((1,), (1,)), ((), ()))


def _dot(a, b):
    return jnp.dot(a.astype(BF16), b.astype(BF16), preferred_element_type=F32)


def _dot_nt(a, b):
    return lax.dot_general(a.astype(BF16), b.astype(BF16), _NT, preferred_element_type=F32)


def _attn_tile(S):
    return _div_tile(S, 512, 128)


def _causal(t, transposed=False):
    r = lax.broadcasted_iota(jnp.int32, (t, t), 0)
    c = lax.broadcasted_iota(jnp.int32, (t, t), 1)
    return r <= c if transposed else r >= c


def _tri_tables(n, by_key):
    if by_key:
        pairs = [(i, j) for j in range(n) for i in range(j, n)]
    else:
        pairs = [(i, j) for i in range(n) for j in range(i + 1)]
    return (jnp.asarray(np.array([p[0] for p in pairs], np.int32)), jnp.asarray(np.array([p[1] for p in pairs], np.int32)))


def attn_fwd(q, k, v):
    H, S, dk = q.shape
    dv = v.shape[-1]
    t, hb = _attn_tile(S), ATTN_HEADS_PER_STEP
    n = S // t
    scale = dk ** -0.5
    qi_tab, kj_tab = _tri_tables(n, by_key=False)

    def body(qi_ref, kj_ref, q_ref, k_ref, v_ref, o_ref, lse_ref, m_s, l_s, acc_s):
        qi, kj = qi_ref[pl.program_id(1)], kj_ref[pl.program_id(1)]

        @pl.when(kj == 0)
        def _():
            m_s[...] = jnp.full(m_s.shape, -jnp.inf, F32)
            l_s[...] = jnp.zeros(l_s.shape, F32)
            acc_s[...] = jnp.zeros(acc_s.shape, F32)

        def step(masked):
            ss = [_dot_nt(q_ref[j], k_ref[j]) for j in range(hb)]
            new = []
            for j in range(hb):
                s = ss[j] * (scale * LOG2E)
                if masked:
                    s = jnp.where(_causal(t), s, -jnp.inf)
                m_old = m_s[j]
                m_new = jnp.maximum(m_old, jnp.max(s, axis=-1, keepdims=True))
                alpha = jnp.exp2(m_old - m_new)
                p = jnp.exp2(s - jnp.tile(m_new, (1, t // LANES)))
                new.append((m_new, alpha * l_s[j] + jnp.sum(p, axis=-1, keepdims=True),
                            alpha[:, :dv] * acc_s[j] + _dot(p, v_ref[j])))
            for j in range(hb):
                m_s[j], l_s[j], acc_s[j] = new[j]

        @pl.when(kj < qi)
        def _():
            step(False)

        @pl.when(kj == qi)
        def _():
            step(True)
            l = l_s[...]
            o_ref[...] = acc_s[...] / l[:, :, :dv]
            lse_ref[...] = (m_s[...] + jnp.log2(l))[:, :, :1]

    qmap = lambda h, s, qi, kj: (h, qi[s], 0)
    kmap = lambda h, s, qi, kj: (h, kj[s], 0)
    return pl.pallas_call(
        body, name="attn_fwd",
        grid_spec=pltpu.PrefetchScalarGridSpec(
            num_scalar_prefetch=2, grid=(H // hb, qi_tab.shape[0]),
            in_specs=[pl.BlockSpec((hb, t, dk), qmap), pl.BlockSpec((hb, t, dk), kmap), pl.BlockSpec((hb, t, dv), kmap)],
            out_specs=(pl.BlockSpec((hb, t, dv), qmap), pl.BlockSpec((hb, t, 1), qmap)),
            scratch_shapes=[pltpu.VMEM((hb, t, LANES), F32), pltpu.VMEM((hb, t, LANES), F32), pltpu.VMEM((hb, t, dv), F32)]),
        out_shape=(jax.ShapeDtypeStruct((H, S, dv), F32), jax.ShapeDtypeStruct((H, S, 1), F32)),
        compiler_params=_params(("parallel", "arbitrary")),
    )(qi_tab, kj_tab, q, k, v)


def attn_bwd(q, k, v, o, lse, do):
    H, S, dk = q.shape
    dv = v.shape[-1]
    t, hb = _attn_tile(S), ATTN_HEADS_PER_STEP
    n = S // t
    scale = dk ** -0.5
    hb = min(hb, H)
    lse_r = lse.reshape(H, 1, S)
    qi_tab, kj_tab = _tri_tables(n, by_key=True)
    tn_dims = (((0,), (0,)), ((), ()))

    def body(qi_ref, kj_ref, q_ref, k_ref, v_ref, o_ref, do_ref, lse_ref, dq_ref, dk_ref, dv_ref):
        pair = pl.program_id(1)
        qi, kj = qi_ref[pair], kj_ref[pair]

        @pl.when(pair == 0)
        def _():
            dq_ref[...] = jnp.zeros(dq_ref.shape, F32)

        ones = jnp.ones((8, dv), F32)

        def step(masked):
            sts = [_dot_nt(k_ref[j], q_ref[j]) for j in range(hb)]
            for j in range(hb):
                dof = do_ref[j]
                dob = dof.astype(BF16)
                delta = lax.dot_general(ones, dof * o_ref[j], _NT, precision=lax.Precision.HIGHEST,
                                        preferred_element_type=F32)[0:1]
                pt = jnp.exp2(sts[j] * (scale * LOG2E) - lse_ref[j])
                if masked:
                    pt = jnp.where(_causal(t, transposed=True), pt, 0.0)
                dvj = _dot(pt, dob)
                dst = (pt * (_dot_nt(v_ref[j], dob) - delta)).astype(BF16)
                dkj = _dot(dst, q_ref[j]) * scale
                rows = pl.ds(pl.multiple_of(qi * t, t), t)
                dq_ref[j, rows, :] += lax.dot_general(dst, k_ref[j], tn_dims, preferred_element_type=F32) * scale
                if masked:
                    dv_ref[j] = dvj
                    dk_ref[j] = dkj
                else:
                    dv_ref[j] += dvj
                    dk_ref[j] += dkj

        @pl.when(qi == kj)
        def _():
            step(True)

        @pl.when(qi > kj)
        def _():
            step(False)

    qmap = lambda h, s, qi, kj: (h, qi[s], 0)
    kmap = lambda h, s, qi, kj: (h, kj[s], 0)
    rowq = lambda h, s, qi, kj: (h, 0, qi[s])
    whole = lambda h, s, qi, kj: (h, 0, 0)
    return pl.pallas_call(
        body, name="attn_bwd",
        grid_spec=pltpu.PrefetchScalarGridSpec(
            num_scalar_prefetch=2, grid=(H // hb, qi_tab.shape[0]),
            in_specs=[pl.BlockSpec((hb, t, dk), qmap), pl.BlockSpec((hb, t, dk), kmap), pl.BlockSpec((hb, t, dv), kmap),
                      pl.BlockSpec((hb, t, dv), qmap), pl.BlockSpec((hb, t, dv), qmap), pl.BlockSpec((hb, 1, t), rowq)],
            out_specs=(pl.BlockSpec((hb, S, dk), whole), pl.BlockSpec((hb, t, dk), kmap), pl.BlockSpec((hb, t, dv), kmap))),
        out_shape=(jax.ShapeDtypeStruct((H, S, dk), F32), jax.ShapeDtypeStruct((H, S, dk), F32),
                   jax.ShapeDtypeStruct((H, S, dv), F32)),
        compiler_params=_params(("parallel", "arbitrary")),
    )(qi_tab, kj_tab, q, k, v, o, do, lse_r)


@jax.custom_vjp
def attention(q, k, v):
    return attn_fwd(q.astype(BF16), k.astype(BF16), v.astype(BF16))[0]


def _attention_fwd(q, k, v):
    qb, kb, vb = q.astype(BF16), k.astype(BF16), v.astype(BF16)
    o, lse = attn_fwd(qb, kb, vb)
    return o, (qb, kb, vb, o, lse)


def _attention_bwd(res, do):
    return attn_bwd(*res, do)


attention.defvjp(_attention_fwd, _attention_bwd)


def _ssd_specs(hb, L, P, N, order):
    xs = pl.BlockSpec((hb, L, P), lambda g, c: (g, order(c), 0))
    col = pl.BlockSpec((None, L, hb), lambda g, c: (g, order(c), 0))
    row = pl.BlockSpec((hb, 1, L), lambda g, c: (g, 0, order(c)))
    bc = pl.BlockSpec((None, L, N), lambda g, c: (g, order(c), 0))
    st = pl.BlockSpec((hb, None, N, P), lambda g, c: (g, order(c), 0, 0))
    return xs, col, row, bc, st


def _ssd_head_scalar(hb):
    return pl.BlockSpec((hb, 1, 1), lambda g, c: (g, 0, 0))


def _ssd_rows(cols, H):
    G, S, hb = cols.shape
    return cols.transpose(0, 2, 1).reshape(H, 1, S)


def ssd_fwd(x, dt, ac, Bm, Cm, skip):
    H, S, P = x.shape
    G, _, N = Bm.shape
    hb, L = H // G, CHUNK
    nc = S // L
    acr = _ssd_rows(ac, H)

    def body(x_ref, dt_ref, ac_ref, acr_ref, b_ref, c_ref, sk_ref, y_ref, hp_ref, h_s):
        @pl.when(pl.program_id(1) == 0)
        def _():
            h_s[...] = jnp.zeros((hb, N, P), F32)

        Bv, Cv = b_ref[...], c_ref[...]
        cb = _dot_nt(Cv, Bv)
        bt = Bv.T
        mask = _causal(L)
        ac_all, dt_all = ac_ref[...], dt_ref[...]
        for j in range(hb):
            a = jnp.broadcast_to(ac_all[:, j:j + 1], (L, L))
            dtv = jnp.broadcast_to(dt_all[:, j:j + 1], (L, L))
            lm = jnp.exp(jnp.where(mask, a - acr_ref[j], -jnp.inf))
            xdt = x_ref[j] * dtv[:, :P]
            h = h_s[j]
            hp_ref[j] = h
            y_ref[j] = _dot(cb * lm, xdt) + jnp.exp(a)[:, :P] * _dot(Cv, h) + sk_ref[j] * x_ref[j]
            al = a[L - 1:L, :]
            h_s[j] = jnp.exp(al)[:, :P] * h + _dot(bt, xdt * jnp.exp(al - a)[:, :P])

    xs, col, row, bc, st = _ssd_specs(hb, L, P, N, lambda c: c)
    return pl.pallas_call(
        body, name="ssd_fwd", grid=(G, nc), in_specs=[xs, col, col, row, bc, bc, _ssd_head_scalar(hb)], out_specs=(xs, st),
        out_shape=(jax.ShapeDtypeStruct((H, S, P), F32), jax.ShapeDtypeStruct((H, nc, N, P), F32)),
        scratch_shapes=[pltpu.VMEM((hb, N, P), F32)],
        compiler_params=_params(("parallel", "arbitrary")),
    )(x, dt, ac, acr, Bm, Cm, skip)


def ssd_bwd(x, dt, ac, Bm, Cm, skip, hp, dy):
    H, S, P = x.shape
    G, _, N = Bm.shape
    hb, L = H // G, CHUNK
    nc = S // L
    acr = _ssd_rows(ac, H)

    def body(x_ref, dt_ref, ac_ref, acr_ref, b_ref, c_ref, sk_ref, hp_ref, dy_ref,
             dx_ref, ddt_ref, dac_ref, dacr_ref, db_ref, dc_ref, dsk_ref, dh_s):
        @pl.when(pl.program_id(1) == 0)
        def _():
            dh_s[...] = jnp.zeros((hb, N, P), F32)
            dsk_ref[...] = jnp.zeros((hb, 1, 1), F32)

        Bv, Cv = b_ref[...], c_ref[...]
        cb = _dot_nt(Cv, Bv)
        cbt = _dot_nt(Bv, Cv)
        ct = Cv.T
        mask, maskt = _causal(L), _causal(L, transposed=True)
        last = lax.broadcasted_iota(jnp.int32, (L, 1), 0) == L - 1
        lane = lax.broadcasted_iota(jnp.int32, (L, hb), 1)
        db = jnp.zeros((L, N), F32)
        dc = jnp.zeros((L, N), F32)
        dac_all = jnp.zeros((L, hb), F32)
        ddt_all = jnp.zeros((L, hb), F32)
        ac_all, dt_all = ac_ref[...], dt_ref[...]
        for j in range(hb):
            ar, xv, g, h, dh = acr_ref[j], x_ref[j], dy_ref[j], hp_ref[j], dh_s[j]
            a = jnp.broadcast_to(ac_all[:, j:j + 1], (L, L))
            dtv = jnp.broadcast_to(dt_all[:, j:j + 1], (L, L))
            lm = jnp.exp(jnp.where(mask, a - ar, -jnp.inf))
            lmt = jnp.exp(jnp.where(maskt, ar - a, -jnp.inf))
            xdt = xv * dtv[:, :P]
            e = jnp.exp(a)
            al = a[L - 1:L, :]
            dte = jnp.exp(al - a)
            el = jnp.exp(al)
            dcb = _dot_nt(g, xdt) * lm
            dcbt = _dot_nt(xdt, g) * lmt
            dseg = dcb * cb
            ch = _dot(Cv, h)
            bdh = _dot(Bv, dh)
            dxdt = _dot(cbt * lmt, g) + dte[:, :P] * bdh
            dc += _dot(dcb, Bv) + e * _dot_nt(g, h)
            db += _dot(dcbt, Cv) + _dot_nt(xdt * dte[:, :P], dh)
            d_e = jnp.sum(g * ch, axis=-1, keepdims=True)
            d_dte = jnp.sum(xdt * bdh, axis=-1, keepdims=True)
            d_el = jnp.sum(h * dh, keepdims=True)
            d_al = jnp.sum(d_dte * dte[:, :1], keepdims=True) + d_el * el[:, :1]
            dac_j = (jnp.sum(dseg, axis=-1, keepdims=True) + d_e * e[:, :1] - d_dte * dte[:, :1]
                     + jnp.where(last, d_al, 0.0))
            dac_all = jnp.where(lane == j, dac_j, dac_all)
            dacr_ref[j] = -jnp.sum(dseg, axis=0, keepdims=True)
            dx_ref[j] = dxdt * dtv[:, :P] + sk_ref[j] * g
            dsk_ref[j] += jnp.sum(g * xv, keepdims=True)
            ddt_all = jnp.where(lane == j, jnp.sum(dxdt * xv, axis=-1, keepdims=True), ddt_all)
            dh_s[j] = el[:, :P] * dh + _dot(ct, e[:, :P] * g)
        dac_ref[...] = dac_all
        ddt_ref[...] = ddt_all
        db_ref[...] = db
        dc_ref[...] = dc

    xs, col, row, bc, st = _ssd_specs(hb, L, P, N, lambda c: nc - 1 - c)
    one = _ssd_head_scalar(hb)
    dx, ddt, dac, dacr, db, dc, dsk = pl.pallas_call(
        body, name="ssd_bwd", grid=(G, nc), in_specs=[xs, col, col, row, bc, bc, one, st, xs],
        out_specs=(xs, col, col, row, bc, bc, one),
        out_shape=(jax.ShapeDtypeStruct((H, S, P), F32), jax.ShapeDtypeStruct((G, S, hb), F32),
                   jax.ShapeDtypeStruct((G, S, hb), F32), jax.ShapeDtypeStruct((H, 1, S), F32),
                   jax.ShapeDtypeStruct((G, S, N), F32), jax.ShapeDtypeStruct((G, S, N), F32),
                   jax.ShapeDtypeStruct((H, 1, 1), F32)),
        scratch_shapes=[pltpu.VMEM((hb, N, P), F32)],
        compiler_params=_params(("parallel", "arbitrary")),
    )(x, dt, ac, acr, Bm, Cm, skip, hp, dy)
    return dx, ddt, dac + dacr.reshape(G, hb, S).transpose(0, 2, 1), db, dc, dsk


@jax.custom_vjp
def ssd_scan(x, dt, ac, Bm, Cm, skip):
    return ssd_fwd(x, dt, ac, Bm, Cm, skip)[0]


def _ssd_scan_fwd(x, dt, ac, Bm, Cm, skip):
    y, hp = ssd_fwd(x, dt, ac, Bm, Cm, skip)
    return y, (x, dt, ac, Bm, Cm, skip, hp)


def _ssd_scan_bwd(res, dy):
    return ssd_bwd(*res, dy)


ssd_scan.defvjp(_ssd_scan_fwd, _ssd_scan_bwd)


def _vec(v):
    return v.reshape(1, -1)


def _ffn(h, w_gu, gu_row, w_down, x_gu, x_down):
    gu = mm_op("ffn_gu", out_dtype=BF16, stack=N_CHIPS, b_slot=(gu_row, D_MODEL))(h, w_gu, x_gu)
    a, = rowwise_op(_f_swiglu, "swiglu", 1, out_dtype=BF16)(gu)
    return mm_op("ffn_down")(a, w_down, x_down)


def _rope_tables(positions):
    inv = 1.0 / (ROPE_THETA ** (jnp.arange(0, QK_ROPE, 2, dtype=F32) / QK_ROPE))
    ang = positions.astype(F32)[:, None] * inv
    S = positions.shape[0]
    cosf = jnp.concatenate([jnp.ones((S, QK_NOPE), F32), jnp.cos(ang), jnp.cos(ang)], axis=1)
    sinf = jnp.concatenate([jnp.zeros((S, QK_NOPE), F32), jnp.sin(ang), jnp.sin(ang)], axis=1)
    return cosf, sinf


def _heads_first(t):
    return t.transpose(1, 0, 2).reshape(-1, t.shape[-1])


def _mla(h, cos, sin, P, W, X, j):
    S = h.shape[0]
    lat = mm_op("mla_a")(h, W["mla_w_a"][j], X["mla_w_a"][j])
    q_lat, kv_lat, k_rope = jnp.split(lat, [Q_LORA, Q_LORA + KV_LORA], axis=1)
    qn, = rowwise_op(_f_rms, "rms_lat", 1, out_dtype=BF16)(q_lat, _vec(P["mla_q_a_gain"][j]))
    kvn, = rowwise_op(_f_rms, "rms_lat", 1, out_dtype=BF16)(kv_lat, _vec(P["mla_kv_a_gain"][j]))
    q = mm_op("mla_qb", stack=N_CHIPS)(qn, W["mla_w_qb"][j], X["mla_w_qb"][j]).reshape(S, MLA_HEADS, QK_HEAD)
    kv = mm_op("mla_kvb", stack=N_CHIPS)(kvn, W["mla_w_kvb"][j], X["mla_w_kvb"][j]).reshape(S, MLA_HEADS, QK_NOPE + V_HEAD)
    k_nope, v = jnp.split(kv, [QK_NOPE], axis=-1)
    k = jnp.concatenate([k_nope, jnp.broadcast_to(k_rope[:, None, :], (S, MLA_HEADS, QK_ROPE))], axis=-1)
    head_rope = rowwise_op(_f_head_rope, "head_rope", 3, diff_rows=(True, False, False), n_const=1)
    swap = _rope_swap()
    q, = head_rope(_heads_first(q), cos, sin, _vec(P["mla_q_gain"][j]), swap)
    k, = head_rope(_heads_first(k), cos, sin, _vec(P["mla_k_gain"][j]), swap)
    o = attention(q.reshape(MLA_HEADS, S, QK_HEAD), k.reshape(MLA_HEADS, S, QK_HEAD), v.transpose(1, 0, 2))
    o = o.transpose(1, 0, 2).reshape(S, MLA_HEADS * V_HEAD).astype(BF16)
    return mm_op("mla_o")(o, W["mla_w_o"][j], X["mla_w_o"][j])


def _ssd(h, P, W, X, j):
    S = h.shape[0]
    z = mm_op("ssd_in_z")(h, W["ssd_w_z"][j], X["ssd_w_z"][j])
    xbc = mm_op("ssd_in_xbc")(h, W["ssd_w_xbc"][j], X["ssd_w_xbc"][j])
    dtr = mm_op("ssd_in_dt")(h, W["ssd_w_dt"][j], X["ssd_w_dt"][j])
    xbc = conv_silu(xbc, P["ssd_conv_w"][j], _vec(P["ssd_conv_b"][j]))
    xs, Bm, Cm = jnp.split(xbc, [D_INNER, D_INNER + SSD_GROUPS * SSD_STATE], axis=1)
    xs = xs.reshape(S, SSD_HEADS, SSD_HEAD_DIM).transpose(1, 0, 2)
    Bm = Bm.reshape(S, SSD_GROUPS, SSD_STATE).transpose(1, 0, 2)
    Cm = Cm.reshape(S, SSD_GROUPS, SSD_STATE).transpose(1, 0, 2)
    dt = jax.nn.softplus(dtr + P["ssd_dt_bias"][j][None, :])
    A = -jnp.exp(P["ssd_a_log"][j])
    a = (dt * A[None, :]).reshape(S // CHUNK, CHUNK, SSD_HEADS)
    ac = jnp.cumsum(a, axis=1).reshape(S, SSD_HEADS)
    by_group = lambda t: t.reshape(S, SSD_GROUPS, SSD_HEADS // SSD_GROUPS).transpose(1, 0, 2)
    y = ssd_scan(xs, by_group(dt), by_group(ac), Bm, Cm, P["ssd_d"][j].reshape(SSD_HEADS, 1, 1))
    y = y.transpose(1, 0, 2).reshape(S, D_INNER)
    g, = rowwise_op(_f_gated_norm, "gated_norm", 2, out_dtype=BF16)(y, z, _vec(P["ssd_norm_gain"][j]))
    return mm_op("ssd_out")(g, W["ssd_w_out"][j], X["ssd_w_out"][j])


def trunk(x, mods, P, W, X, positions):
    cos, sin = _rope_tables(positions)

    def sub(i, s, h):
        if s == 1:
            return _mla(h, cos, sin, P, W, X, i // 2) if i % 2 == 0 else _ssd(h, P, W, X, i // 2)
        t = s // 2
        return _ffn(h, W["ffn_w_gu"], (2 * i + t) * D_MODEL, W["ffn_w_down"][i][t], X["ffn_w_gu"][i][t], X["ffn_w_down"][i][t])

    order = [(i, s) for i in range(DEPTH) for s in range(3)]
    i0, s0 = order[0]
    h, = rowwise_op(_f_modulate, "modulate", 1, out_dtype=BF16)(
        x, _vec(P["norm_gain"][i0, s0]), _vec(mods[i0, s0, 0]), _vec(mods[i0, s0, 1]))
    for n, (i, s) in enumerate(order):
        y = sub(i, s, h)
        coef = 1.0 if s == 1 else 0.5
        gate = _vec(mods[i, s, 2])
        if n + 1 < len(order):
            i1, s1 = order[n + 1]
            x, h = rowwise_op(_f_resid_modulate(coef), "resid_modulate", 2, out_dtype=(F32, BF16))(
                x, y, gate, _vec(P["norm_gain"][i1, s1]), _vec(mods[i1, s1, 0]), _vec(mods[i1, s1, 1]))
        else:
            x, = rowwise_op(_f_resid(coef), "resid", 2)(x, y, gate)
    return x


def stand_ins(W):
    def one(name, w):
        if name in ("mla_w_qb", "mla_w_kvb"):
            return jnp.zeros((N_CHIPS, w.shape[0], w.shape[1] // N_CHIPS), F32)
        return jnp.zeros(w.shape, F32)
    X = {n: jax.tree.map(lambda w, n=n: one(n, w), W[n]) for n in W if n != "ffn_w_gu"}
    g = W["ffn_w_gu"]
    X["ffn_w_gu"] = [[jnp.zeros((g.shape[0], D_MODEL, g.shape[2]), F32) for _ in range(2)] for _ in range(DEPTH)]
    return X


def loss_head(y, target):
    S, D = y.shape
    tm = _row_tile(S, [D])

    def body(y_ref, t_ref, dy_ref, l_ref):
        d = y_ref[...] - t_ref[...]
        dy_ref[...] = d * (1.0 / D)
        part = jnp.sum(d * d, axis=0, keepdims=True) * (0.5 / D)

        @pl.when(pl.program_id(0) == 0)
        def _():
            l_ref[...] = part

        @pl.when(pl.program_id(0) > 0)
        def _():
            l_ref[...] += part

    return pl.pallas_call(
        body, name="loss_head", grid=(S // tm,),
        in_specs=[pl.BlockSpec((tm, D), lambda i: (i, 0))] * 2,
        out_specs=(pl.BlockSpec((tm, D), lambda i: (i, 0)), pl.BlockSpec((1, D), lambda i: (0, 0))),
        out_shape=(jax.ShapeDtypeStruct((S, D), F32), jax.ShapeDtypeStruct((1, D), F32)),
        compiler_params=_params(("arbitrary",)),
    )(y, target)


def _stream_rows(R, C):
    return _div_tile(R, max(16, (1 << 19) // C // 16 * 16), 16)


def adamw(w, m, v, g):
    R, C = w.shape
    tr = _stream_rows(R, C)
    c1 = 1.0 / (1.0 - ADAM_B1 ** ADAM_STEP)
    c2 = 1.0 / (1.0 - ADAM_B2 ** ADAM_STEP)

    def body(w_ref, m_ref, v_ref, g_ref, d_ref, nm_ref, nv_ref):
        gv = g_ref[...]
        nm = ADAM_B1 * m_ref[...] + (1.0 - ADAM_B1) * gv
        nv = ADAM_B2 * v_ref[...] + (1.0 - ADAM_B2) * (gv * gv)
        d_ref[...] = -ADAM_LR * ((nm * c1) / (jnp.sqrt(nv * c2) + ADAM_EPS) + ADAM_WD * w_ref[...])
        nm_ref[...] = nm
        nv_ref[...] = nv

    spec = pl.BlockSpec((tr, C), lambda i: (i, 0))
    return pl.pallas_call(
        body, name="adamw", grid=(R // tr,), in_specs=[spec] * 4, out_specs=(spec,) * 3,
        out_shape=(jax.ShapeDtypeStruct((R, C), F32),) * 3, compiler_params=_params(("parallel",)),
    )(w, m, v, g)


def sum_parts(parts, name, out_dtype=F32):
    R, C = parts[0].shape
    tr = _stream_rows(R, C)
    n = len(parts)

    def body(*refs):
        acc = refs[0][...].astype(F32)
        for r in refs[1:n]:
            acc = acc + r[...].astype(F32)
        refs[n][...] = acc.astype(out_dtype)

    spec = pl.BlockSpec((tr, C), lambda i: (i, 0))
    return pl.pallas_call(
        body, name=name, grid=(R // tr,), in_specs=[spec] * n, out_specs=spec,
        out_shape=jax.ShapeDtypeStruct((R, C), out_dtype), compiler_params=_params(("parallel",)),
    )(*parts)


def sum_own_half(full, theirs, c, name, out_dtype):
    n, R, C = full.shape
    h = R // 2
    tr = _stream_rows(h, C)
    nb = h // tr

    def body(c_ref, a_ref, b_ref, o_ref):
        o_ref[...] = (a_ref[...].astype(F32) + b_ref[...].astype(F32)).astype(out_dtype)

    return pl.pallas_call(
        body, name=name,
        grid_spec=pltpu.PrefetchScalarGridSpec(
            num_scalar_prefetch=1, grid=(n, nb),
            in_specs=[pl.BlockSpec((None, tr, C), lambda p, i, cr: (p, cr[0] * nb + i, 0)),
                      pl.BlockSpec((None, tr, C), lambda p, i, cr: (p, i, 0))],
            out_specs=pl.BlockSpec((None, tr, C), lambda p, i, cr: (p, i, 0))),
        out_shape=jax.ShapeDtypeStruct((n, h, C), out_dtype), compiler_params=_params(("parallel", "parallel")),
    )(jnp.reshape(c, (1,)).astype(jnp.int32), full, theirs)


def sum_slots(a, name, out_dtype=F32):
    n, R, C = a.shape
    tr = _stream_rows(R, C)

    def body(*refs):
        acc = refs[0][...].astype(F32)
        for r in refs[1:n]:
            acc = acc + r[...].astype(F32)
        refs[n][...] = acc.astype(out_dtype)

    return pl.pallas_call(
        body, name=name, grid=(R // tr,),
        in_specs=[pl.BlockSpec((None, tr, C), lambda i, p=p: (p, i, 0)) for p in range(n)],
        out_specs=pl.BlockSpec((tr, C), lambda i: (i, 0)),
        out_shape=jax.ShapeDtypeStruct((R, C), out_dtype), compiler_params=_params(("parallel",)),
    )(*([a] * n))


def _coords():
    return lax.axis_index("x"), lax.axis_index("y"), lax.axis_index("c")


def _other_chips(x, y):
    return [(1 - x, y), (x, 1 - y), (1 - x, 1 - y)]


def _hbm_call(body, name, ins, out_shapes, n_sems):
    return pl.pallas_call(
        body, name=name, out_shape=tuple(out_shapes),
        in_specs=[pl.BlockSpec(memory_space=pl.ANY)] * len(ins),
        out_specs=tuple(pl.BlockSpec(memory_space=pl.ANY) for _ in out_shapes),
        scratch_shapes=[pltpu.SemaphoreType.DMA((n_sems,)), pltpu.SemaphoreType.DMA((n_sems,))],
    )(*ins)


def allgather_small(v, name):
    m_per, n = v.shape

    def body(x_ref, out_ref, send_sems, recv_sems, local_sem):
        x, y, c = _coords()
        me, sibling = (x, y, c), (x, y, 1 - c)
        chips = _other_chips(x, y)

        def rows(px, py, pc):
            return out_ref.at[pl.ds((4 * px + 2 * py + pc) * m_per, m_per), :]

        def copy(k, block, to, src=None):
            return pltpu.make_async_remote_copy(
                src_ref=rows(*block) if src is None else src, dst_ref=rows(*block),
                send_sem=send_sems.at[k], recv_sem=recv_sems.at[k], device_id=to, device_id_type=MESH)

        mine = pltpu.make_async_copy(x_ref, rows(*me), local_sem)
        mine.start()
        first = [copy(0, me, sibling, src=x_ref)]
        first += [copy(1 + j, me, (*chip, c), src=x_ref) for j, chip in enumerate(chips)]
        for cp in first:
            cp.start()
        passed = [copy(4 + j, (*chip, c), sibling) for j, chip in enumerate(chips)]
        for j, chip in enumerate(chips):
            copy(1 + j, (*chip, c), me).wait_recv()
            passed[j].start()
        copy(0, sibling, me).wait_recv()
        for j, chip in enumerate(chips):
            copy(4 + j, (*chip, 1 - c), me).wait_recv()
        for cp in first + passed:
            cp.wait_send()
        mine.wait()

    return pl.pallas_call(
        body, name=name, out_shape=jax.ShapeDtypeStruct((N_DEV * m_per, n), v.dtype),
        in_specs=[pl.BlockSpec(memory_space=pltpu.VMEM)], out_specs=pl.BlockSpec(memory_space=pltpu.VMEM),
        scratch_shapes=[pltpu.SemaphoreType.DMA((7,)), pltpu.SemaphoreType.DMA((7,)), pltpu.SemaphoreType.DMA],
        compiler_params=pltpu.CompilerParams(vmem_limit_bytes=V7X_VMEM_LIMIT),
    )(v)


def allgather_chips(arrs, name):
    n = len(arrs)
    halves = [a.shape[0] // 2 for a in arrs]

    def body(*refs):
        xs, outs = refs[:n], refs[n:2 * n]
        send_sems, recv_sems = refs[2 * n:]
        x, y, c = _coords()
        me, sibling, chips = 2 * x + y, (x, y, 1 - c), _other_chips(x, y)

        def half(ref, cc, i):
            return ref.at[pl.ds(cc * halves[i], halves[i]), :]

        def copy(i, k, src, dst, to):
            return pltpu.make_async_remote_copy(src_ref=src, dst_ref=dst, send_sem=send_sems.at[6 * i + k],
                                                recv_sem=recv_sems.at[6 * i + k], device_id=to, device_id_type=MESH)

        sends = [copy(i, k, half(xs[i], c, i), half(outs[i].at[me], c, i), (*chip, c))
                 for k, chip in enumerate(chips) for i in range(n)]
        for cp in sends:
            cp.start()
        passed = []
        for k, (px, py) in enumerate(chips):
            for i in range(n):
                landed = half(outs[i].at[2 * px + py], c, i)
                copy(i, k, landed, landed, (px, py, c)).wait_recv()
                passed.append(copy(i, 3 + k, landed, landed, sibling))
                passed[-1].start()
        for k, (px, py) in enumerate(chips):
            for i in range(n):
                theirs = half(outs[i].at[2 * px + py], 1 - c, i)
                copy(i, 3 + k, theirs, theirs, sibling).wait_recv()
        for cp in sends + passed:
            cp.wait_send()

    return _hbm_call(body, name, arrs, [jax.ShapeDtypeStruct((N_CHIPS,) + a.shape, a.dtype) for a in arrs], 6 * n)


def pair_exchange(arrs, name):
    n = len(arrs)
    halves = [a.shape[1] // 2 for a in arrs]

    def body(*refs):
        xs, theirs = refs[:n], refs[n:2 * n]
        send_sems, recv_sems = refs[2 * n:]
        x, y, c = _coords()
        cps = [pltpu.make_async_remote_copy(
            src_ref=xs[i].at[:, pl.ds((1 - c) * halves[i], halves[i]), :], dst_ref=theirs[i],
            send_sem=send_sems.at[i], recv_sem=recv_sems.at[i], device_id=(x, y, 1 - c), device_id_type=MESH)
            for i in range(n)]
        for cp in cps:
            cp.start()
        for cp in cps:
            cp.wait()

    shapes = [jax.ShapeDtypeStruct((a.shape[0], a.shape[1] // 2, a.shape[2]), a.dtype) for a in arrs]
    return _hbm_call(body, name, arrs, shapes, n)


def scatter_chips(arrs, name):
    n = len(arrs)

    def body(*refs):
        xs, outs = refs[:n], refs[n:2 * n]
        send_sems, recv_sems = refs[2 * n:]
        x, y, c = _coords()
        me, chips = 2 * x + y, _other_chips(x, y)

        def copy(i, k, src_slot, dst_slot, to):
            return pltpu.make_async_remote_copy(
                src_ref=xs[i].at[src_slot], dst_ref=outs[i].at[dst_slot], send_sem=send_sems.at[3 * i + k],
                recv_sem=recv_sems.at[3 * i + k], device_id=to, device_id_type=MESH)

        sends = [copy(i, k, 2 * px + py, me, (px, py, c)) for k, (px, py) in enumerate(chips) for i in range(n)]
        for cp in sends:
            cp.start()
        for k, (px, py) in enumerate(chips):
            for i in range(n):
                copy(i, k, me, 2 * px + py, (px, py, c)).wait_recv()
        for cp in sends:
            cp.wait_send()

    return _hbm_call(body, name, arrs, [jax.ShapeDtypeStruct(a.shape, a.dtype) for a in arrs], 3 * n)


def pair_allgather(arrs, name):
    n = len(arrs)

    def body(*refs):
        xs, outs = refs[:n], refs[n:2 * n]
        send_sems, recv_sems = refs[2 * n:]
        x, y, c = _coords()
        cps = [pltpu.make_async_remote_copy(
            src_ref=xs[i], dst_ref=outs[i].at[pl.ds(c * xs[i].shape[0], xs[i].shape[0]), :], send_sem=send_sems.at[i],
            recv_sem=recv_sems.at[i], device_id=(x, y, 1 - c), device_id_type=MESH) for i in range(n)]
        for cp in cps:
            cp.start()
        for cp in cps:
            cp.wait()

    return _hbm_call(body, name, arrs, [jax.ShapeDtypeStruct((2 * a.shape[0], a.shape[1]), a.dtype) for a in arrs], n)


GROUPS = (("ffn_w_gu",), ("ffn_w_down", "mla_w_o", "ssd_w_out"), ("mla_w_a",), ("mla_w_qb",), ("mla_w_kvb",), ("ssd_w_in",))


def _pad_rows(a, mult):
    r = (-a.shape[0]) % mult
    return a if r == 0 else jnp.concatenate([a, jnp.zeros((r, a.shape[1]), a.dtype)], axis=0)


def _rows2d(a):
    return a.reshape(-1, a.shape[-1])


def _unstack(st, axis):
    full = jnp.moveaxis(st, 0, axis)
    sh = list(full.shape)
    sh[axis:axis + 2] = [sh[axis] * sh[axis + 1]]
    return full.reshape(sh)


def _stack(full, axis):
    sh = list(full.shape)
    sh[axis:axis + 1] = [N_CHIPS, sh[axis] // N_CHIPS]
    return jnp.moveaxis(full.reshape(sh), axis, 0)


def kernel(x, c, positions, norm_gain, ada_w, ada_b, ffn_w_gu, ffn_w_down, mla_w_a, mla_q_a_gain, mla_kv_a_gain, mla_w_qb, mla_w_kvb, mla_q_gain, mla_k_gain, mla_w_o, ssd_w_in, ssd_conv_w, ssd_conv_b, ssd_dt_bias, ssd_a_log, ssd_d, ssd_norm_gain, ssd_w_out, loss_target, m_norm_gain, m_ada_w, m_ada_b, m_ffn_w_gu, m_ffn_w_down, m_mla_w_a, m_mla_q_a_gain, m_mla_kv_a_gain, m_mla_w_qb, m_mla_w_kvb, m_mla_q_gain, m_mla_k_gain, m_mla_w_o, m_ssd_w_in, m_ssd_conv_w, m_ssd_conv_b, m_ssd_dt_bias, m_ssd_a_log, m_ssd_d, m_ssd_norm_gain, m_ssd_w_out, v_norm_gain, v_ada_w, v_ada_b, v_ffn_w_gu, v_ffn_w_down, v_mla_w_a, v_mla_q_a_gain, v_mla_kv_a_gain, v_mla_w_qb, v_mla_w_kvb, v_mla_q_gain, v_mla_k_gain, v_mla_w_o, v_ssd_w_in, v_ssd_conv_w, v_ssd_conv_b, v_ssd_dt_bias, v_ssd_a_log, v_ssd_d, v_ssd_norm_gain, v_ssd_w_out):
    w_in = dict(norm_gain=norm_gain, ada_w=ada_w, ada_b=ada_b, ffn_w_gu=ffn_w_gu, ffn_w_down=ffn_w_down, mla_w_a=mla_w_a, mla_q_a_gain=mla_q_a_gain, mla_kv_a_gain=mla_kv_a_gain, mla_w_qb=mla_w_qb, mla_w_kvb=mla_w_kvb, mla_q_gain=mla_q_gain, mla_k_gain=mla_k_gain, mla_w_o=mla_w_o, ssd_w_in=ssd_w_in, ssd_conv_w=ssd_conv_w, ssd_conv_b=ssd_conv_b, ssd_dt_bias=ssd_dt_bias, ssd_a_log=ssd_a_log, ssd_d=ssd_d, ssd_norm_gain=ssd_norm_gain, ssd_w_out=ssd_w_out)
    m_in = dict(norm_gain=m_norm_gain, ada_w=m_ada_w, ada_b=m_ada_b, ffn_w_gu=m_ffn_w_gu, ffn_w_down=m_ffn_w_down, mla_w_a=m_mla_w_a, mla_q_a_gain=m_mla_q_a_gain, mla_kv_a_gain=m_mla_kv_a_gain, mla_w_qb=m_mla_w_qb, mla_w_kvb=m_mla_w_kvb, mla_q_gain=m_mla_q_gain, mla_k_gain=m_mla_k_gain, mla_w_o=m_mla_w_o, ssd_w_in=m_ssd_w_in, ssd_conv_w=m_ssd_conv_w, ssd_conv_b=m_ssd_conv_b, ssd_dt_bias=m_ssd_dt_bias, ssd_a_log=m_ssd_a_log, ssd_d=m_ssd_d, ssd_norm_gain=m_ssd_norm_gain, ssd_w_out=m_ssd_w_out)
    v_in = dict(norm_gain=v_norm_gain, ada_w=v_ada_w, ada_b=v_ada_b, ffn_w_gu=v_ffn_w_gu, ffn_w_down=v_ffn_w_down, mla_w_a=v_mla_w_a, mla_q_a_gain=v_mla_q_a_gain, mla_kv_a_gain=v_mla_kv_a_gain, mla_w_qb=v_mla_w_qb, mla_w_kvb=v_mla_w_kvb, mla_q_gain=v_mla_q_gain, mla_k_gain=v_mla_k_gain, mla_w_o=v_mla_w_o, ssd_w_in=v_ssd_w_in, ssd_conv_w=v_ssd_conv_w, ssd_conv_b=v_ssd_conv_b, ssd_dt_bias=v_ssd_dt_bias, ssd_a_log=v_ssd_a_log, ssd_d=v_ssd_d, ssd_norm_gain=v_ssd_norm_gain, ssd_w_out=v_ssd_w_out)
    names = list(w_in)
    xi, yi, ci = _coords()
    chip = 2 * xi + yi
    batch = 4 * xi + 2 * yi + ci
    x2, target = x[0], loss_target[0]

    small_sharded = ("norm_gain", "ssd_conv_w", "ssd_conv_b", "ssd_norm_gain")
    pack0 = jnp.concatenate([c.reshape(-1)] + [w_in[n].reshape(-1) for n in small_sharded])
    pack0 = _pad_rows(pack0.reshape(-1, 128), 8)
    g0 = allgather_small(pack0, "gather_small").reshape(N_DEV, -1)
    c_all = g0[:, :D_MODEL]
    P, off = {}, D_MODEL
    for n in small_sharded:
        sz = w_in[n].size
        st = g0[0::2, off:off + sz].reshape((N_CHIPS,) + w_in[n].shape)
        P[n] = _unstack(st, w_in[n].ndim - 1)
        off += sz
    for n in ("mla_q_a_gain", "mla_kv_a_gain", "mla_q_gain", "mla_k_gain", "ssd_dt_bias", "ssd_a_log", "ssd_d"):
        P[n] = w_in[n]

    sc = _silu(c_all)
    n_ada = ada_w.shape[2]
    b_sh = lax.dynamic_slice_in_dim(ada_b, chip * n_ada, n_ada, axis=1)
    mods_sh = jnp.stack([matmul(sc, ada_w[l], "nn", "ada_fwd") for l in range(DEPTH)]) + b_sh[:, None, :]
    g1 = allgather_small(mods_sh.reshape(-1, 128), "gather_mods").reshape(N_DEV, DEPTH, N_DEV, n_ada)
    mods = lax.dynamic_index_in_dim(g1[0::2], batch, axis=2, keepdims=False)
    mods = mods.transpose(1, 0, 2).reshape(DEPTH, 3, 3, D_MODEL)

    shard_groups = [jnp.concatenate([_rows2d(w_in[n]).astype(BF16) for n in grp], axis=0) for grp in GROUPS]
    gathered = allgather_chips(shard_groups, "gather_weights")
    gathered = [lax.dynamic_update_slice(g, s[None], (chip, 0, 0)) for g, s in zip(gathered, shard_groups)]
    G = {}
    for grp, arr in zip(GROUPS, gathered):
        off = 0
        for n in grp:
            rows = w_in[n].size // w_in[n].shape[-1]
            G[n] = arr[:, off:off + rows].reshape((N_CHIPS,) + w_in[n].shape)
            off += rows
    W = {
        "ffn_w_gu": gathered[0],
        "ffn_w_down": [[_unstack(G["ffn_w_down"][:, i, t], 0) for t in range(2)] for i in range(DEPTH)],
        "mla_w_a": [_unstack(G["mla_w_a"][:, j], 0) for j in range(2)],
        "mla_w_qb": [_unstack(G["mla_w_qb"][:, j], 1) for j in range(2)],
        "mla_w_kvb": [_unstack(G["mla_w_kvb"][:, j], 1) for j in range(2)],
        "mla_w_o": [_unstack(G["mla_w_o"][:, j], 0) for j in range(2)],
        "ssd_w_out": [_unstack(G["ssd_w_out"][:, j], 0) for j in range(2)],
    }
    w_in_full = [_unstack(G["ssd_w_in"][:, j], 1) for j in range(2)]
    W["ssd_w_z"] = [w[:, :D_INNER] for w in w_in_full]
    W["ssd_w_xbc"] = [w[:, D_INNER:D_INNER + CONV_DIM] for w in w_in_full]
    W["ssd_w_dt"] = [w[:, D_INNER + CONV_DIM:] for w in w_in_full]
    X = stand_ins(W)

    pos = positions[0]
    y, vjp = jax.vjp(lambda a, b, p_, x_: trunk(a, b, p_, W, x_, pos), x2, mods, P, X)
    dy, loss_cols = loss_head(y, target)
    dx, dmods, dP, dX = vjp(dy)
    loss = lax.psum(jnp.sum(loss_cols), ("x", "y", "c"))

    small_names = ("norm_gain", "ssd_conv_w", "ssd_conv_b", "ssd_norm_gain", "mla_q_a_gain", "mla_kv_a_gain",
                   "mla_q_gain", "mla_k_gain", "ssd_dt_bias", "ssd_a_log", "ssd_d")
    pack1 = jnp.concatenate([dmods.reshape(-1)] + [dP[n].reshape(-1) for n in small_names])
    pack1 = _pad_rows(jnp.pad(pack1, (0, (-pack1.size) % 128)).reshape(-1, 128), 8)
    rows1 = pack1.shape[0]
    g2 = allgather_small(pack1, "gather_small_grads")
    tot = sum_slots(g2.reshape(N_DEV, rows1, 128), "sum_small_grads").reshape(-1)
    n_mod = DEPTH * 9 * D_MODEL
    grads = {"ada_b": tot[:n_mod].reshape(DEPTH, 9 * D_MODEL)}
    off = n_mod
    for n in small_names:
        sz = dP[n].size
        full = tot[off:off + sz].reshape(dP[n].shape)
        off += sz
        if n in small_sharded:
            k = w_in[n].shape[-1]
            full = lax.dynamic_slice_in_dim(full, chip * k, k, axis=full.ndim - 1)
        grads[n] = full
    dmods_all = g2.reshape(N_DEV, -1)[:, :n_mod].reshape(N_DEV, DEPTH, 9 * D_MODEL)
    dm_sh = lax.dynamic_slice_in_dim(dmods_all, chip * n_ada, n_ada, axis=2)
    grads["ada_w"] = jnp.stack([matmul(sc, dm_sh[:, l], "tn", "ada_dw") for l in range(DEPTH)])

    w_in_g = [jnp.concatenate([dX["ssd_w_z"][j], dX["ssd_w_xbc"][j], dX["ssd_w_dt"][j]], axis=1) for j in range(2)]
    per_name = {
        "ffn_w_gu": [dX["ffn_w_gu"][i][t] for i in range(DEPTH) for t in range(2)],
        "ffn_w_down": [dX["ffn_w_down"][i][t].reshape(N_CHIPS, -1, D_MODEL) for i in range(DEPTH) for t in range(2)],
        "mla_w_a": [g.reshape(N_CHIPS, -1, g.shape[-1]) for g in dX["mla_w_a"]],
        "mla_w_qb": dX["mla_w_qb"], "mla_w_kvb": dX["mla_w_kvb"],
        "mla_w_o": [g.reshape(N_CHIPS, -1, D_MODEL) for g in dX["mla_w_o"]],
        "ssd_w_out": [g.reshape(N_CHIPS, -1, D_MODEL) for g in dX["ssd_w_out"]],
        "ssd_w_in": [_stack(g, 1) for g in w_in_g],
    }
    grad_groups = [jnp.concatenate([p for n in grp for p in per_name[n]], axis=1) for grp in GROUPS]
    theirs = pair_exchange(grad_groups, "grads_to_sibling")
    pair = [sum_own_half(a, b, ci, "sum_pair", BF16) for a, b in zip(grad_groups, theirs)]
    landed = scatter_chips(pair, "grads_to_chips")
    landed = [lax.dynamic_update_slice(a, lax.dynamic_slice_in_dim(p, chip, 1, axis=0), (chip, 0, 0)) for a, p in zip(landed, pair)]
    half_sums = [sum_slots(a, "sum_chips") for a in landed]
    totals = pair_allgather(half_sums, "grad_halves_swap")
    totals = [lax.dynamic_update_slice(t, h, (ci * h.shape[0], 0)) for t, h in zip(totals, half_sums)]
    for grp, arr in zip(GROUPS, totals):
        off = 0
        for n in grp:
            rows = w_in[n].size // w_in[n].shape[-1]
            grads[n] = arr[off:off + rows].reshape(w_in[n].shape)
            off += rows

    deltas, new_m, new_v = {}, {}, {}
    for n in names:
        w = w_in[n]
        d, nm, nv = adamw(_rows2d(w), _rows2d(m_in[n]), _rows2d(v_in[n]), _rows2d(grads[n]))
        deltas[n], new_m[n], new_v[n] = d.reshape(w.shape), nm.reshape(w.shape), nv.reshape(w.shape)

    return (loss, dx[None], *[grads[n] for n in names], *[deltas[n] for n in names],
            *[new_m[n] for n in names], *[new_v[n] for n in names])
```

```python
import numpy as np

import jax
import jax.numpy as jnp
from jax import lax
from jax.experimental import pallas as pl
from jax.experimental.pallas import tpu as pltpu

F32 = jnp.float32
BF16 = jnp.bfloat16
MESH = pl.DeviceIdType.MESH

D_MODEL = 1024
DEPTH = 4
EPS = 1e-6
D_FF = 2816
MLA_HEADS = 16
Q_LORA = 384
KV_LORA = 256
QK_NOPE = 64
QK_ROPE = 32
QK_HEAD = QK_NOPE + QK_ROPE
V_HEAD = 64
ROPE_THETA = 10000.0
D_INNER = 2048
SSD_HEAD_DIM = 64
SSD_HEADS = 32
SSD_GROUPS = 4
SSD_STATE = 128
CONV_WIDTH = 4
CHUNK = 128
CONV_DIM = D_INNER + 2 * SSD_GROUPS * SSD_STATE
ADAM_LR = 0.001
ADAM_B1 = 0.9
ADAM_B2 = 0.999
ADAM_EPS = 1e-08
ADAM_WD = 0.01
ADAM_STEP = 10

N_CHIPS = 4
N_DEV = 8
V7X_VMEM_LIMIT = 56 * 1024 * 1024
ATTN_HEADS_PER_STEP = 4
LANES = 128
LOG2E = 1.4426950408889634


def _params(sem=None):
    return pltpu.CompilerParams(dimension_semantics=sem, vmem_limit_bytes=V7X_VMEM_LIMIT)


def _div_tile(n, pref, quantum):
    if n <= pref:
        return n
    t = (pref // quantum) * quantum
    while t >= quantum:
        if n % t == 0:
            return t
        t -= quantum
    return n


def matmul(a, b, mode, name, out_dtype=F32, stack=0, b_slot=None, tm=1024, tn=1408, tk=2816):
    if b_slot is not None:
        (roff, rows), (nsl, _, n) = b_slot, b.shape
        (M, K) = a.shape
        K2, N = (rows, nsl * n) if mode == "nn" else (nsl * n, rows)
    elif mode == "nn":
        (M, K), (K2, N) = a.shape, b.shape
    elif mode == "nt":
        (M, K), (N, K2) = a.shape, b.shape
    else:
        (K, M), (K2, N) = a.shape, b.shape
    assert K == K2, (a.shape, b.shape, mode)
    tm = _div_tile(M, tm, 128 if mode == "tn" else 16)
    tn = _div_tile(N // stack if stack else (n if b_slot and mode == "nn" else N), tn, 128)
    tk = _div_tile(n if b_slot and mode == "nt" else K, tk, 128)
    nk = K // tk
    if mode == "nn":
        a_spec = pl.BlockSpec((tm, tk), lambda i, j, k: (i, k))
        b_spec = pl.BlockSpec((tk, tn), lambda i, j, k: (k, j))
        dims = (((1,), (0,)), ((), ()))
        if b_slot:
            assert roff % tk == 0
            nbn, rb = n // tn, roff // tk
            b_spec = pl.BlockSpec((None, tk, tn), lambda i, j, k: (j // nbn, rb + k, j % nbn))
    elif mode == "nt":
        a_spec = pl.BlockSpec((tm, tk), lambda i, j, k: (i, k))
        b_spec = pl.BlockSpec((tn, tk), lambda i, j, k: (j, k))
        dims = (((1,), (1,)), ((), ()))
        if b_slot:
            assert roff % tn == 0
            nbk, rb = n // tk, roff // tn
            b_spec = pl.BlockSpec((None, tn, tk), lambda i, j, k: (k // nbk, rb + j, k % nbk))
    else:
        a_spec = pl.BlockSpec((tk, tm), lambda i, j, k: (k, i))
        b_spec = pl.BlockSpec((tk, tn), lambda i, j, k: (k, j))
        dims = (((0,), (0,)), ((), ()))
    if stack:
        nb = N // stack // tn
        out_spec = pl.BlockSpec((None, tm, tn), lambda i, j, k: (j // nb, i, j % nb))
        out_shape = jax.ShapeDtypeStruct((stack, M, N // stack), out_dtype)
    else:
        out_spec = pl.BlockSpec((tm, tn), lambda i, j, k: (i, j))
        out_shape = jax.ShapeDtypeStruct((M, N), out_dtype)
    use_acc = nk > 1 and out_dtype != F32

    def body(a_ref, b_ref, o_ref, *acc):
        p = lax.dot_general(a_ref[...].astype(BF16), b_ref[...].astype(BF16), dims, preferred_element_type=F32)
        if nk == 1:
            o_ref[...] = p.astype(out_dtype)
            return
        acc_ref = acc[0] if use_acc else o_ref
        k = pl.program_id(2)

        @pl.when(k == 0)
        def _():
            acc_ref[...] = p

        @pl.when(k > 0)
        def _():
            acc_ref[...] += p

        if use_acc:
            @pl.when(k == nk - 1)
            def _():
                o_ref[...] = acc_ref[...].astype(out_dtype)

    return pl.pallas_call(
        body, name=name, grid=(M // tm, N // tn, nk), in_specs=[a_spec, b_spec], out_specs=out_spec, out_shape=out_shape,
        scratch_shapes=[pltpu.VMEM((tm, tn), F32)] if use_acc else [],
        compiler_params=_params(("parallel", "parallel", "arbitrary")),
    )(a, b)


def mm_op(name, out_dtype=F32, stack=0, b_slot=None):
    @jax.custom_vjp
    def op(a, w, wp):
        return matmul(a, w, "nn", name + "_fwd", out_dtype=out_dtype, b_slot=b_slot)

    def fwd(a, w, wp):
        return op(a, w, wp), (a, w)

    def bwd(res, g):
        a, w = res
        return (matmul(g, w, "nt", name + "_dx", out_dtype=a.dtype, b_slot=b_slot), jnp.zeros_like(w),
                matmul(a, g, "tn", name + "_dw", stack=stack, tn=1408, tk=1024))

    op.defvjp(fwd, bwd)
    return op


def _row_tile(rows, widths):
    w = max(widths)
    t = 128 if w > 4096 else (256 if w > 1024 else (512 if w > 128 else 2048))
    return _div_tile(rows, t, 16)


def _row_spec(r, tm):
    nb = r.shape[0] // tm
    return pl.BlockSpec((tm, r.shape[1]), lambda i: (i % nb, 0))


def _rowwise_fwd(f, rows, vecs, name, out_dtype):
    n_r, n_v = len(rows), len(vecs)
    S = rows[0].shape[0]
    tm = _row_tile(min(r.shape[0] for r in rows), [r.shape[1] for r in rows])
    outs = jax.eval_shape(f, *[jax.ShapeDtypeStruct((tm, r.shape[1]), F32) for r in rows], *vecs)
    out_dtypes = out_dtype if isinstance(out_dtype, tuple) else (out_dtype,) * len(outs)

    def body(*refs):
        res = f(*[r[...].astype(F32) for r in refs[: n_r + n_v]])
        for o, r in zip(refs[n_r + n_v:], res):
            o[...] = r.astype(o.dtype)

    return pl.pallas_call(
        body, name=name, grid=(S // tm,),
        in_specs=[_row_spec(r, tm) for r in rows] + [pl.BlockSpec(v.shape, lambda i: (0, 0)) for v in vecs],
        out_specs=tuple(pl.BlockSpec((tm, o.shape[1]), lambda i: (i, 0)) for o in outs),
        out_shape=tuple(jax.ShapeDtypeStruct((S, o.shape[1]), dt) for o, dt in zip(outs, out_dtypes)),
        compiler_params=_params(("parallel",)),
    )(*rows, *vecs)


def _rowwise_bwd(f, rows, vecs, douts, diff_rows, n_const, name):
    n_r, n_o = len(rows), len(douts)
    consts, vecs = vecs[len(vecs) - n_const:], vecs[:len(vecs) - n_const]
    n_v = len(vecs)
    S = rows[0].shape[0]
    tm = _row_tile(min(r.shape[0] for r in rows), [r.shape[1] for r in rows] + [d.shape[1] for d in douts])
    d_idx = [i for i in range(n_r) if diff_rows[i]]

    def body(*refs):
        row_v = [r[...].astype(F32) for r in refs[:n_r]]
        vec_v = [r[...] for r in refs[n_r:n_r + n_v]]
        const_v = [r[...] for r in refs[n_r + n_v:n_r + n_v + n_const]]
        dout_v = tuple(r[...].astype(F32) for r in refs[n_r + n_v + n_const:n_r + n_v + n_const + n_o])
        out_refs = refs[n_r + n_v + n_const + n_o:]

        def g(*args):
            full = list(row_v)
            for j, i in enumerate(d_idx):
                full[i] = args[j]
            return f(*full, *args[len(d_idx):], *const_v)

        _, vjp = jax.vjp(g, *[row_v[i] for i in d_idx], *vec_v)
        grads = vjp(dout_v)
        for j in range(len(d_idx)):
            out_refs[j][...] = grads[j].astype(out_refs[j].dtype)
        step = pl.program_id(0)
        for j in range(n_v):
            gv, o = grads[len(d_idx) + j], out_refs[len(d_idx) + j]

            @pl.when(step == 0)
            def _(gv=gv, o=o):
                o[...] = gv

            @pl.when(step > 0)
            def _(gv=gv, o=o):
                o[...] += gv

    res = pl.pallas_call(
        body, name=name, grid=(S // tm,),
        in_specs=[_row_spec(r, tm) for r in rows] + [pl.BlockSpec(v.shape, lambda i: (0, 0)) for v in vecs + consts]
        + [pl.BlockSpec((tm, d.shape[1]), lambda i: (i, 0)) for d in douts],
        out_specs=tuple([pl.BlockSpec((tm, rows[i].shape[1]), lambda i_: (i_, 0)) for i in d_idx]
                        + [pl.BlockSpec(v.shape, lambda i: (0, 0)) for v in vecs]),
        out_shape=tuple([jax.ShapeDtypeStruct(rows[i].shape, rows[i].dtype) for i in d_idx]
                        + [jax.ShapeDtypeStruct(v.shape, F32) for v in vecs]),
        compiler_params=_params(("arbitrary",)),
    )(*rows, *vecs, *consts, *douts)
    drows = [None] * n_r
    for j, i in enumerate(d_idx):
        drows[i] = res[j]
    for i in range(n_r):
        if drows[i] is None:
            drows[i] = jnp.zeros_like(rows[i])
    return tuple(drows) + tuple(res[len(d_idx):]) + tuple(jnp.zeros_like(k) for k in consts)


def rowwise_op(f, name, n_rows, diff_rows=None, out_dtype=F32, n_const=0):
    diff = tuple(diff_rows) if diff_rows is not None else (True,) * n_rows

    @jax.custom_vjp
    def op(*args):
        return _rowwise_fwd(f, args[:n_rows], args[n_rows:], name + "_fwd", out_dtype)

    def fwd(*args):
        return op(*args), args

    def bwd(args, douts):
        return _rowwise_bwd(f, args[:n_rows], args[n_rows:], douts, diff, n_const, name + "_bwd")

    op.defvjp(fwd, bwd)
    return op


def _rms(x, gain):
    return x * lax.rsqrt(jnp.mean(x * x, axis=-1, keepdims=True) + EPS) * gain


def _silu(x):
    return x * jax.nn.sigmoid(x)


def _f_modulate(x, gain, shift, scale):
    return (_rms(x, gain) * (1.0 + scale) + shift,)


def _f_rms(x, gain):
    return (_rms(x, gain),)


def _f_swiglu(gu):
    n = gu.shape[1] // 2
    return (_silu(gu[:, :n]) * gu[:, n:],)


def _f_resid(coef):
    def f(x, y, gate):
        return (x + (coef * gate) * y,)
    return f


def _f_resid_modulate(coef):
    def f(x, y, gate, gain, shift, scale):
        x1 = x + (coef * gate) * y
        return (x1, _rms(x1, gain) * (1.0 + scale) + shift)
    return f


def _f_head_rope(x, cosf, sinf, gain, swap):
    y = _rms(x, gain)
    return (y * cosf + jnp.dot(y, swap, precision=lax.Precision.HIGHEST, preferred_element_type=F32) * sinf,)


def _rope_swap():
    m = np.zeros((QK_HEAD, QK_HEAD), np.float32)
    half = QK_ROPE // 2
    for i in range(half):
        m[QK_NOPE + half + i, QK_NOPE + i] = -1.0
        m[QK_NOPE + i, QK_NOPE + half + i] = 1.0
    return jnp.asarray(m)


def _f_gated_norm(y, z, gain):
    g = y * _silu(z)
    n = g.shape[1] // SSD_GROUPS
    return (jnp.concatenate([_rms(g[:, i * n:(i + 1) * n], gain[:, i * n:(i + 1) * n]) for i in range(SSD_GROUPS)], axis=1),)


HALO = 8


def _conv_taps(ext, w, rows, off):
    acc = None
    for k in range(CONV_WIDTH):
        term = w[k:k + 1, :] * pltpu.roll(ext, CONV_WIDTH - 1 - k, 0)[off:off + rows]
        acc = term if acc is None else acc + term
    return acc


def _conv_specs(S, tm, tc):
    tile = pl.BlockSpec((tm, tc), lambda j, i: (i, j))
    prev = pl.BlockSpec((HALO, tc), lambda j, i: (jnp.maximum(i * (tm // HALO) - 1, 0), j))
    nxt = pl.BlockSpec((HALO, tc), lambda j, i: (jnp.minimum((i + 1) * (tm // HALO), S // HALO - 1), j))
    wspec = pl.BlockSpec((CONV_WIDTH, tc), lambda j, i: (0, j))
    bspec = pl.BlockSpec((1, tc), lambda j, i: (0, j))
    return tile, prev, nxt, wspec, bspec


def conv_silu_fwd(u, w, b, tm=512, tc=1024):
    S, C = u.shape
    tm = _div_tile(S, tm, HALO)

    def body(u_ref, p_ref, w_ref, b_ref, o_ref):
        prev = jnp.where(pl.program_id(1) == 0, 0.0, p_ref[...])
        ext = jnp.concatenate([prev, u_ref[...]], axis=0)
        conv = _conv_taps(ext, w_ref[...], tm, HALO) + b_ref[...]
        o_ref[...] = conv * jax.nn.sigmoid(conv)

    tile, prev, _, wspec, bspec = _conv_specs(S, tm, tc)
    return pl.pallas_call(
        body, name="conv_silu_fwd", grid=(C // tc, S // tm), in_specs=[tile, prev, wspec, bspec], out_specs=tile,
        out_shape=jax.ShapeDtypeStruct((S, C), F32), compiler_params=_params(("parallel", "arbitrary")),
    )(u, u, w, b)


def conv_silu_bwd(u, w, b, dout, tm=512, tc=1024):
    S, C = u.shape
    tm = _div_tile(S, tm, HALO)
    n = S // tm
    ext_rows = tm + HALO

    def body(u_ref, p_ref, n_ref, g_ref, gn_ref, w_ref, b_ref, du_ref, dw_ref, db_ref):
        i = pl.program_id(1)
        wv = w_ref[...]
        prev = jnp.where(i == 0, 0.0, p_ref[...])
        ext = jnp.concatenate([prev, u_ref[...], n_ref[...]], axis=0)
        conv = _conv_taps(ext, wv, ext_rows, HALO) + b_ref[...]
        g_ext = jnp.concatenate([g_ref[...], jnp.where(i == n - 1, 0.0, gn_ref[...])], axis=0)
        sg = jax.nn.sigmoid(conv)
        dconv = g_ext * (sg * (1.0 + conv * (1.0 - sg)))
        du = None
        for k in range(CONV_WIDTH):
            s = CONV_WIDTH - 1 - k
            term = wv[k:k + 1, :] * pltpu.roll(dconv, (ext_rows - s) % ext_rows, 0)[:tm]
            du = term if du is None else du + term
        du_ref[...] = du
        dc = dconv[:tm]
        dw = jnp.concatenate([jnp.sum(dc * pltpu.roll(ext, CONV_WIDTH - 1 - k, 0)[HALO:HALO + tm], axis=0, keepdims=True)
                              for k in range(CONV_WIDTH)], axis=0)
        dbv = jnp.sum(dc, axis=0, keepdims=True)

        @pl.when(i == 0)
        def _():
            dw_ref[...] = dw
            db_ref[...] = dbv

        @pl.when(i > 0)
        def _():
            dw_ref[...] += dw
            db_ref[...] += dbv

    tile, prev, nxt, wspec, bspec = _conv_specs(S, tm, tc)
    return pl.pallas_call(
        body, name="conv_silu_bwd", grid=(C // tc, n), in_specs=[tile, prev, nxt, tile, nxt, wspec, bspec],
        out_specs=(tile, wspec, bspec),
        out_shape=(jax.ShapeDtypeStruct((S, C), F32), jax.ShapeDtypeStruct((CONV_WIDTH, C), F32),
                   jax.ShapeDtypeStruct((1, C), F32)),
        compiler_params=_params(("parallel", "arbitrary")),
    )(u, u, u, dout, dout, w, b)


@jax.custom_vjp
def conv_silu(u, w, b):
    return conv_silu_fwd(u, w, b)


def _conv_silu_fwd(u, w, b):
    return conv_silu_fwd(u, w, b), (u, w, b)


def _conv_silu_bwd(res, dout):
    return conv_silu_bwd(*res, dout)


conv_silu.defvjp(_conv_silu_fwd, _conv_silu_bwd)


_NT = (((1,), (1,)), ((), ()))


def _dot(a, b):
    return jnp.dot(a.astype(BF16), b.astype(BF16), preferred_element_type=F32)


def _dot_nt(a, b):
    return lax.dot_general(a.astype(BF16), b.astype(BF16), _NT, preferred_element_type=F32)


def _attn_tile(S):
    return _div_tile(S, 512, 128)


def _causal(t, transposed=False):
    r = lax.broadcasted_iota(jnp.int32, (t, t), 0)
    c = lax.broadcasted_iota(jnp.int32, (t, t), 1)
    return r <= c if transposed else r >= c


def _tri_tables(n, by_key):
    if by_key:
        pairs = [(i, j) for j in range(n) for i in range(j, n)]
    else:
        pairs = [(i, j) for i in range(n) for j in range(i + 1)]
    return (jnp.asarray(np.array([p[0] for p in pairs], np.int32)), jnp.asarray(np.array([p[1] for p in pairs], np.int32)))


def attn_fwd(q, k, v):
    H, S, dk = q.shape
    dv = v.shape[-1]
    t, hb = _attn_tile(S), ATTN_HEADS_PER_STEP
    n = S // t
    scale = dk ** -0.5
    qi_tab, kj_tab = _tri_tables(n, by_key=False)

    def body(qi_ref, kj_ref, q_ref, k_ref, v_ref, o_ref, lse_ref, m_s, l_s, acc_s):
        qi, kj = qi_ref[pl.program_id(1)], kj_ref[pl.program_id(1)]

        @pl.when(kj == 0)
        def _():
            m_s[...] = jnp.full(m_s.shape, -jnp.inf, F32)
            l_s[...] = jnp.zeros(l_s.shape, F32)
            acc_s[...] = jnp.zeros(acc_s.shape, F32)

        def step(masked):
            ss = [_dot_nt(q_ref[j], k_ref[j]) for j in range(hb)]
            new = []
            for j in range(hb):
                s = ss[j] * (scale * LOG2E)
                if masked:
                    s = jnp.where(_causal(t), s, -jnp.inf)
                m_old = m_s[j]
                m_new = jnp.maximum(m_old, jnp.max(s, axis=-1, keepdims=True))
                alpha = jnp.exp2(m_old - m_new)
                p = jnp.exp2(s - jnp.tile(m_new, (1, t // LANES)))
                new.append((m_new, alpha * l_s[j] + jnp.sum(p, axis=-1, keepdims=True),
                            alpha[:, :dv] * acc_s[j] + _dot(p, v_ref[j])))
            for j in range(hb):
                m_s[j], l_s[j], acc_s[j] = new[j]

        @pl.when(kj < qi)
        def _():
            step(False)

        @pl.when(kj == qi)
        def _():
            step(True)
            l = l_s[...]
            o_ref[...] = acc_s[...] / l[:, :, :dv]
            lse_ref[...] = (m_s[...] + jnp.log2(l))[:, :, :1]

    qmap = lambda h, s, qi, kj: (h, qi[s], 0)
    kmap = lambda h, s, qi, kj: (h, kj[s], 0)
    return pl.pallas_call(
        body, name="attn_fwd",
        grid_spec=pltpu.PrefetchScalarGridSpec(
            num_scalar_prefetch=2, grid=(H // hb, qi_tab.shape[0]),
            in_specs=[pl.BlockSpec((hb, t, dk), qmap), pl.BlockSpec((hb, t, dk), kmap), pl.BlockSpec((hb, t, dv), kmap)],
            out_specs=(pl.BlockSpec((hb, t, dv), qmap), pl.BlockSpec((hb, t, 1), qmap)),
            scratch_shapes=[pltpu.VMEM((hb, t, LANES), F32), pltpu.VMEM((hb, t, LANES), F32), pltpu.VMEM((hb, t, dv), F32)]),
        out_shape=(jax.ShapeDtypeStruct((H, S, dv), F32), jax.ShapeDtypeStruct((H, S, 1), F32)),
        compiler_params=_params(("parallel", "arbitrary")),
    )(qi_tab, kj_tab, q, k, v)


def attn_bwd(q, k, v, o, lse, do):
    H, S, dk = q.shape
    dv = v.shape[-1]
    t, hb = _attn_tile(S), ATTN_HEADS_PER_STEP
    n = S // t
    scale = dk ** -0.5
    hb = min(hb, H)
    lse_r = lse.reshape(H, 1, S)
    qi_tab, kj_tab = _tri_tables(n, by_key=True)
    tn_dims = (((0,), (0,)), ((), ()))

    def body(qi_ref, kj_ref, q_ref, k_ref, v_ref, o_ref, do_ref, lse_ref, dq_ref, dk_ref, dv_ref):
        pair = pl.program_id(1)
        qi, kj = qi_ref[pair], kj_ref[pair]

        @pl.when(pair == 0)
        def _():
            dq_ref[...] = jnp.zeros(dq_ref.shape, F32)

        ones = jnp.ones((8, dv), F32)

        def step(masked):
            sts = [_dot_nt(k_ref[j], q_ref[j]) for j in range(hb)]
            for j in range(hb):
                dof = do_ref[j]
                dob = dof.astype(BF16)
                delta = lax.dot_general(ones, dof * o_ref[j], _NT, precision=lax.Precision.HIGHEST,
                                        preferred_element_type=F32)[0:1]
                pt = jnp.exp2(sts[j] * (scale * LOG2E) - lse_ref[j])
                if masked:
                    pt = jnp.where(_causal(t, transposed=True), pt, 0.0)
                dvj = _dot(pt, dob)
                dst = (pt * (_dot_nt(v_ref[j], dob) - delta)).astype(BF16)
                dkj = _dot(dst, q_ref[j]) * scale
                rows = pl.ds(pl.multiple_of(qi * t, t), t)
                dq_ref[j, rows, :] += lax.dot_general(dst, k_ref[j], tn_dims, preferred_element_type=F32) * scale
                if masked:
                    dv_ref[j] = dvj
                    dk_ref[j] = dkj
                else:
                    dv_ref[j] += dvj
                    dk_ref[j] += dkj

        @pl.when(qi == kj)
        def _():
            step(True)

        @pl.when(qi > kj)
        def _():
            step(False)

    qmap = lambda h, s, qi, kj: (h, qi[s], 0)
    kmap = lambda h, s, qi, kj: (h, kj[s], 0)
    rowq = lambda h, s, qi, kj: (h, 0, qi[s])
    whole = lambda h, s, qi, kj: (h, 0, 0)
    return pl.pallas_call(
        body, name="attn_bwd",
        grid_spec=pltpu.PrefetchScalarGridSpec(
            num_scalar_prefetch=2, grid=(H // hb, qi_tab.shape[0]),
            in_specs=[pl.BlockSpec((hb, t, dk), qmap), pl.BlockSpec((hb, t, dk), kmap), pl.BlockSpec((hb, t, dv), kmap),
                      pl.BlockSpec((hb, t, dv), qmap), pl.BlockSpec((hb, t, dv), qmap), pl.BlockSpec((hb, 1, t), rowq)],
            out_specs=(pl.BlockSpec((hb, S, dk), whole), pl.BlockSpec((hb, t, dk), kmap), pl.BlockSpec((hb, t, dv), kmap))),
        out_shape=(jax.ShapeDtypeStruct((H, S, dk), F32), jax.ShapeDtypeStruct((H, S, dk), F32),
                   jax.ShapeDtypeStruct((H, S, dv), F32)),
        compiler_params=_params(("parallel", "arbitrary")),
    )(qi_tab, kj_tab, q, k, v, o, do, lse_r)


@jax.custom_vjp
def attention(q, k, v):
    return attn_fwd(q.astype(BF16), k.astype(BF16), v.astype(BF16))[0]


def _attention_fwd(q, k, v):
    qb, kb, vb = q.astype(BF16), k.astype(BF16), v.astype(BF16)
    o, lse = attn_fwd(qb, kb, vb)
    return o, (qb, kb, vb, o, lse)


def _attention_bwd(res, do):
    return attn_bwd(*res, do)


attention.defvjp(_attention_fwd, _attention_bwd)


def _ssd_specs(hb, L, P, N, order):
    xs = pl.BlockSpec((hb, L, P), lambda g, c: (g, order(c), 0))
    col = pl.BlockSpec((None, L, hb), lambda g, c: (g, order(c), 0))
    row = pl.BlockSpec((hb, 1, L), lambda g, c: (g, 0, order(c)))
    bc = pl.BlockSpec((None, L, N), lambda g, c: (g, order(c), 0))
    st = pl.BlockSpec((hb, None, N, P), lambda g, c: (g, order(c), 0, 0))
    return xs, col, row, bc, st


def _ssd_head_scalar(hb):
    return pl.BlockSpec((hb, 1, 1), lambda g, c: (g, 0, 0))


def _ssd_rows(cols, H):
    G, S, hb = cols.shape
    return cols.transpose(0, 2, 1).reshape(H, 1, S)


def ssd_fwd(x, dt, ac, Bm, Cm, skip):
    H, S, P = x.shape
    G, _, N = Bm.shape
    hb, L = H // G, CHUNK
    nc = S // L
    acr = _ssd_rows(ac, H)

    def body(x_ref, dt_ref, ac_ref, acr_ref, b_ref, c_ref, sk_ref, y_ref, hp_ref, h_s):
        @pl.when(pl.program_id(1) == 0)
        def _():
            h_s[...] = jnp.zeros((hb, N, P), F32)

        Bv, Cv = b_ref[...], c_ref[...]
        cb = _dot_nt(Cv, Bv)
        bt = Bv.T
        mask = _causal(L)
        ac_all, dt_all = ac_ref[...], dt_ref[...]
        for j in range(hb):
            a = jnp.broadcast_to(ac_all[:, j:j + 1], (L, L))
            dtv = jnp.broadcast_to(dt_all[:, j:j + 1], (L, L))
            lm = jnp.exp(jnp.where(mask, a - acr_ref[j], -jnp.inf))
            xdt = x_ref[j] * dtv[:, :P]
            h = h_s[j]
            hp_ref[j] = h
            y_ref[j] = _dot(cb * lm, xdt) + jnp.exp(a)[:, :P] * _dot(Cv, h) + sk_ref[j] * x_ref[j]
            al = a[L - 1:L, :]
            h_s[j] = jnp.exp(al)[:, :P] * h + _dot(bt, xdt * jnp.exp(al - a)[:, :P])

    xs, col, row, bc, st = _ssd_specs(hb, L, P, N, lambda c: c)
    return pl.pallas_call(
        body, name="ssd_fwd", grid=(G, nc), in_specs=[xs, col, col, row, bc, bc, _ssd_head_scalar(hb)], out_specs=(xs, st),
        out_shape=(jax.ShapeDtypeStruct((H, S, P), F32), jax.ShapeDtypeStruct((H, nc, N, P), F32)),
        scratch_shapes=[pltpu.VMEM((hb, N, P), F32)],
        compiler_params=_params(("parallel", "arbitrary")),
    )(x, dt, ac, acr, Bm, Cm, skip)


def ssd_bwd(x, dt, ac, Bm, Cm, skip, hp, dy):
    H, S, P = x.shape
    G, _, N = Bm.shape
    hb, L = H // G, CHUNK
    nc = S // L
    acr = _ssd_rows(ac, H)

    def body(x_ref, dt_ref, ac_ref, acr_ref, b_ref, c_ref, sk_ref, hp_ref, dy_ref,
             dx_ref, ddt_ref, dac_ref, dacr_ref, db_ref, dc_ref, dsk_ref, dh_s):
        @pl.when(pl.program_id(1) == 0)
        def _():
            dh_s[...] = jnp.zeros((hb, N, P), F32)
            dsk_ref[...] = jnp.zeros((hb, 1, 1), F32)

        Bv, Cv = b_ref[...], c_ref[...]
        cb = _dot_nt(Cv, Bv)
        cbt = _dot_nt(Bv, Cv)
        ct = Cv.T
        mask, maskt = _causal(L), _causal(L, transposed=True)
        last = lax.broadcasted_iota(jnp.int32, (L, 1), 0) == L - 1
        lane = lax.broadcasted_iota(jnp.int32, (L, hb), 1)
        db = jnp.zeros((L, N), F32)
        dc = jnp.zeros((L, N), F32)
        dac_all = jnp.zeros((L, hb), F32)
        ddt_all = jnp.zeros((L, hb), F32)
        ac_all, dt_all = ac_ref[...], dt_ref[...]
        for j in range(hb):
            ar, xv, g, h, dh = acr_ref[j], x_ref[j], dy_ref[j], hp_ref[j], dh_s[j]
            a = jnp.broadcast_to(ac_all[:, j:j + 1], (L, L))
            dtv = jnp.broadcast_to(dt_all[:, j:j + 1], (L, L))
            lm = jnp.exp(jnp.where(mask, a - ar, -jnp.inf))
            lmt = jnp.exp(jnp.where(maskt, ar - a, -jnp.inf))
            xdt = xv * dtv[:, :P]
            e = jnp.exp(a)
            al = a[L - 1:L, :]
            dte = jnp.exp(al - a)
            el = jnp.exp(al)
            dcb = _dot_nt(g, xdt) * lm
            dcbt = _dot_nt(xdt, g) * lmt
            dseg = dcb * cb
            ch = _dot(Cv, h)
            bdh = _dot(Bv, dh)
            dxdt = _dot(cbt * lmt, g) + dte[:, :P] * bdh
            dc += _dot(dcb, Bv) + e * _dot_nt(g, h)
            db += _dot(dcbt, Cv) + _dot_nt(xdt * dte[:, :P], dh)
            d_e = jnp.sum(g * ch, axis=-1, keepdims=True)
            d_dte = jnp.sum(xdt * bdh, axis=-1, keepdims=True)
            d_el = jnp.sum(h * dh, keepdims=True)
            d_al = jnp.sum(d_dte * dte[:, :1], keepdims=True) + d_el * el[:, :1]
            dac_j = (jnp.sum(dseg, axis=-1, keepdims=True) + d_e * e[:, :1] - d_dte * dte[:, :1]
                     + jnp.where(last, d_al, 0.0))
            dac_all = jnp.where(lane == j, dac_j, dac_all)
            dacr_ref[j] = -jnp.sum(dseg, axis=0, keepdims=True)
            dx_ref[j] = dxdt * dtv[:, :P] + sk_ref[j] * g
            dsk_ref[j] += jnp.sum(g * xv, keepdims=True)
            ddt_all = jnp.where(lane == j, jnp.sum(dxdt * xv, axis=-1, keepdims=True), ddt_all)
            dh_s[j] = el[:, :P] * dh + _dot(ct, e[:, :P] * g)
        dac_ref[...] = dac_all
        ddt_ref[...] = ddt_all
        db_ref[...] = db
        dc_ref[...] = dc

    xs, col, row, bc, st = _ssd_specs(hb, L, P, N, lambda c: nc - 1 - c)
    one = _ssd_head_scalar(hb)
    dx, ddt, dac, dacr, db, dc, dsk = pl.pallas_call(
        body, name="ssd_bwd", grid=(G, nc), in_specs=[xs, col, col, row, bc, bc, one, st, xs],
        out_specs=(xs, col, col, row, bc, bc, one),
        out_shape=(jax.ShapeDtypeStruct((H, S, P), F32), jax.ShapeDtypeStruct((G, S, hb), F32),
                   jax.ShapeDtypeStruct((G, S, hb), F32), jax.ShapeDtypeStruct((H, 1, S), F32),
                   jax.ShapeDtypeStruct((G, S, N), F32), jax.ShapeDtypeStruct((G, S, N), F32),
                   jax.ShapeDtypeStruct((H, 1, 1), F32)),
        scratch_shapes=[pltpu.VMEM((hb, N, P), F32)],
        compiler_params=_params(("parallel", "arbitrary")),
    )(x, dt, ac, acr, Bm, Cm, skip, hp, dy)
    return dx, ddt, dac + dacr.reshape(G, hb, S).transpose(0, 2, 1), db, dc, dsk


@jax.custom_vjp
def ssd_scan(x, dt, ac, Bm, Cm, skip):
    return ssd_fwd(x, dt, ac, Bm, Cm, skip)[0]


def _ssd_scan_fwd(x, dt, ac, Bm, Cm, skip):
    y, hp = ssd_fwd(x, dt, ac, Bm, Cm, skip)
    return y, (x, dt, ac, Bm, Cm, skip, hp)


def _ssd_scan_bwd(res, dy):
    return ssd_bwd(*res, dy)


ssd_scan.defvjp(_ssd_scan_fwd, _ssd_scan_bwd)


def _vec(v):
    return v.reshape(1, -1)


def _ffn(h, w_gu, gu_row, w_down, x_gu, x_down):
    gu = mm_op("ffn_gu", out_dtype=BF16, stack=N_CHIPS, b_slot=(gu_row, D_MODEL))(h, w_gu, x_gu)
    a, = rowwise_op(_f_swiglu, "swiglu", 1, out_dtype=BF16)(gu)
    return mm_op("ffn_down")(a, w_down, x_down)


def _rope_tables(positions):
    inv = 1.0 / (ROPE_THETA ** (jnp.arange(0, QK_ROPE, 2, dtype=F32) / QK_ROPE))
    ang = positions.astype(F32)[:, None] * inv
    S = positions.shape[0]
    cosf = jnp.concatenate([jnp.ones((S, QK_NOPE), F32), jnp.cos(ang), jnp.cos(ang)], axis=1)
    sinf = jnp.concatenate([jnp.zeros((S, QK_NOPE), F32), jnp.sin(ang), jnp.sin(ang)], axis=1)
    return cosf, sinf


def _heads_first(t):
    return t.transpose(1, 0, 2).reshape(-1, t.shape[-1])


def _mla(h, cos, sin, P, W, X, j):
    S = h.shape[0]
    lat = mm_op("mla_a")(h, W["mla_w_a"][j], X["mla_w_a"][j])
    q_lat, kv_lat, k_rope = jnp.split(lat, [Q_LORA, Q_LORA + KV_LORA], axis=1)
    qn, = rowwise_op(_f_rms, "rms_lat", 1, out_dtype=BF16)(q_lat, _vec(P["mla_q_a_gain"][j]))
    kvn, = rowwise_op(_f_rms, "rms_lat", 1, out_dtype=BF16)(kv_lat, _vec(P["mla_kv_a_gain"][j]))
    q = mm_op("mla_qb", stack=N_CHIPS)(qn, W["mla_w_qb"][j], X["mla_w_qb"][j]).reshape(S, MLA_HEADS, QK_HEAD)
    kv = mm_op("mla_kvb", stack=N_CHIPS)(kvn, W["mla_w_kvb"][j], X["mla_w_kvb"][j]).reshape(S, MLA_HEADS, QK_NOPE + V_HEAD)
    k_nope, v = jnp.split(kv, [QK_NOPE], axis=-1)
    k = jnp.concatenate([k_nope, jnp.broadcast_to(k_rope[:, None, :], (S, MLA_HEADS, QK_ROPE))], axis=-1)
    head_rope = rowwise_op(_f_head_rope, "head_rope", 3, diff_rows=(True, False, False), n_const=1)
    swap = _rope_swap()
    q, = head_rope(_heads_first(q), cos, sin, _vec(P["mla_q_gain"][j]), swap)
    k, = head_rope(_heads_first(k), cos, sin, _vec(P["mla_k_gain"][j]), swap)
    o = attention(q.reshape(MLA_HEADS, S, QK_HEAD), k.reshape(MLA_HEADS, S, QK_HEAD), v.transpose(1, 0, 2))
    o = o.transpose(1, 0, 2).reshape(S, MLA_HEADS * V_HEAD).astype(BF16)
    return mm_op("mla_o")(o, W["mla_w_o"][j], X["mla_w_o"][j])


def _ssd(h, P, W, X, j):
    S = h.shape[0]
    z, xbc = jnp.split(mm_op("ssd_in_zx")(h, W["ssd_w_zx"][j], X["ssd_w_zx"][j]), [D_INNER], axis=1)
    dtr = mm_op("ssd_in_dt")(h, W["ssd_w_dt"][j], X["ssd_w_dt"][j])
    xbc = conv_silu(xbc, P["ssd_conv_w"][j], _vec(P["ssd_conv_b"][j]))
    xs, Bm, Cm = jnp.split(xbc, [D_INNER, D_INNER + SSD_GROUPS * SSD_STATE], axis=1)
    xs = xs.reshape(S, SSD_HEADS, SSD_HEAD_DIM).transpose(1, 0, 2)
    Bm = Bm.reshape(S, SSD_GROUPS, SSD_STATE).transpose(1, 0, 2)
    Cm = Cm.reshape(S, SSD_GROUPS, SSD_STATE).transpose(1, 0, 2)
    dt = jax.nn.softplus(dtr + P["ssd_dt_bias"][j][None, :])
    A = -jnp.exp(P["ssd_a_log"][j])
    a = (dt * A[None, :]).reshape(S // CHUNK, CHUNK, SSD_HEADS)
    ac = jnp.cumsum(a, axis=1).reshape(S, SSD_HEADS)
    by_group = lambda t: t.reshape(S, SSD_GROUPS, SSD_HEADS // SSD_GROUPS).transpose(1, 0, 2)
    y = ssd_scan(xs, by_group(dt), by_group(ac), Bm, Cm, P["ssd_d"][j].reshape(SSD_HEADS, 1, 1))
    y = y.transpose(1, 0, 2).reshape(S, D_INNER)
    g, = rowwise_op(_f_gated_norm, "gated_norm", 2, out_dtype=BF16)(y, z, _vec(P["ssd_norm_gain"][j]))
    return mm_op("ssd_out")(g, W["ssd_w_out"][j], X["ssd_w_out"][j])


def trunk(x, mods, P, W, X, positions):
    cos, sin = _rope_tables(positions)

    def sub(i, s, h):
        if s == 1:
            return _mla(h, cos, sin, P, W, X, i // 2) if i % 2 == 0 else _ssd(h, P, W, X, i // 2)
        t = s // 2
        return _ffn(h, W["ffn_w_gu"], (2 * i + t) * D_MODEL, W["ffn_w_down"][i][t], X["ffn_w_gu"][i][t], X["ffn_w_down"][i][t])

    order = [(i, s) for i in range(DEPTH) for s in range(3)]
    i0, s0 = order[0]
    h, = rowwise_op(_f_modulate, "modulate", 1, out_dtype=BF16)(
        x, _vec(P["norm_gain"][i0, s0]), _vec(mods[i0, s0, 0]), _vec(mods[i0, s0, 1]))
    for n, (i, s) in enumerate(order):
        y = sub(i, s, h)
        coef = 1.0 if s == 1 else 0.5
        gate = _vec(mods[i, s, 2])
        if n + 1 < len(order):
            i1, s1 = order[n + 1]
            x, h = rowwise_op(_f_resid_modulate(coef), "resid_modulate", 2, out_dtype=(F32, BF16))(
                x, y, gate, _vec(P["norm_gain"][i1, s1]), _vec(mods[i1, s1, 0]), _vec(mods[i1, s1, 1]))
        else:
            x, = rowwise_op(_f_resid(coef), "resid", 2)(x, y, gate)
    return x


def stand_ins(W):
    def one(name, w):
        if name in ("mla_w_qb", "mla_w_kvb"):
            return jnp.zeros((N_CHIPS, w.shape[0], w.shape[1] // N_CHIPS), F32)
        return jnp.zeros(w.shape, F32)
    X = {n: jax.tree.map(lambda w, n=n: one(n, w), W[n]) for n in W if n != "ffn_w_gu"}
    g = W["ffn_w_gu"]
    X["ffn_w_gu"] = [[jnp.zeros((g.shape[0], D_MODEL, g.shape[2]), F32) for _ in range(2)] for _ in range(DEPTH)]
    return X


def loss_head(y, target):
    S, D = y.shape
    tm = _row_tile(S, [D])

    def body(y_ref, t_ref, dy_ref, l_ref):
        d = y_ref[...] - t_ref[...]
        dy_ref[...] = d * (1.0 / D)
        part = jnp.sum(d * d, axis=0, keepdims=True) * (0.5 / D)

        @pl.when(pl.program_id(0) == 0)
        def _():
            l_ref[...] = part

        @pl.when(pl.program_id(0) > 0)
        def _():
            l_ref[...] += part

    return pl.pallas_call(
        body, name="loss_head", grid=(S // tm,),
        in_specs=[pl.BlockSpec((tm, D), lambda i: (i, 0))] * 2,
        out_specs=(pl.BlockSpec((tm, D), lambda i: (i, 0)), pl.BlockSpec((1, D), lambda i: (0, 0))),
        out_shape=(jax.ShapeDtypeStruct((S, D), F32), jax.ShapeDtypeStruct((1, D), F32)),
        compiler_params=_params(("arbitrary",)),
    )(y, target)


def _stream_rows(R, C):
    return _div_tile(R, max(16, (1 << 19) // C // 16 * 16), 16)


def adamw(w, m, v, g):
    R, C = w.shape
    tr = _stream_rows(R, C)
    c1 = 1.0 / (1.0 - ADAM_B1 ** ADAM_STEP)
    c2 = 1.0 / (1.0 - ADAM_B2 ** ADAM_STEP)

    def body(w_ref, m_ref, v_ref, g_ref, d_ref, nm_ref, nv_ref):
        gv = g_ref[...]
        nm = ADAM_B1 * m_ref[...] + (1.0 - ADAM_B1) * gv
        nv = ADAM_B2 * v_ref[...] + (1.0 - ADAM_B2) * (gv * gv)
        d_ref[...] = -ADAM_LR * ((nm * c1) / (jnp.sqrt(nv * c2) + ADAM_EPS) + ADAM_WD * w_ref[...])
        nm_ref[...] = nm
        nv_ref[...] = nv

    spec = pl.BlockSpec((tr, C), lambda i: (i, 0))
    return pl.pallas_call(
        body, name="adamw", grid=(R // tr,), in_specs=[spec] * 4, out_specs=(spec,) * 3,
        out_shape=(jax.ShapeDtypeStruct((R, C), F32),) * 3, compiler_params=_params(("parallel",)),
    )(w, m, v, g)


def sum_parts(parts, name, out_dtype=F32):
    R, C = parts[0].shape
    tr = _stream_rows(R, C)
    n = len(parts)

    def body(*refs):
        acc = refs[0][...].astype(F32)
        for r in refs[1:n]:
            acc = acc + r[...].astype(F32)
        refs[n][...] = acc.astype(out_dtype)

    spec = pl.BlockSpec((tr, C), lambda i: (i, 0))
    return pl.pallas_call(
        body, name=name, grid=(R // tr,), in_specs=[spec] * n, out_specs=spec,
        out_shape=jax.ShapeDtypeStruct((R, C), out_dtype), compiler_params=_params(("parallel",)),
    )(*parts)


def sum_own_half(full, theirs, c, name, out_dtype):
    n, R, C = full.shape
    h = R // 2
    tr = _stream_rows(h, C)
    nb = h // tr

    def body(c_ref, a_ref, b_ref, o_ref):
        o_ref[...] = (a_ref[...].astype(F32) + b_ref[...].astype(F32)).astype(out_dtype)

    return pl.pallas_call(
        body, name=name,
        grid_spec=pltpu.PrefetchScalarGridSpec(
            num_scalar_prefetch=1, grid=(n, nb),
            in_specs=[pl.BlockSpec((None, tr, C), lambda p, i, cr: (p, cr[0] * nb + i, 0)),
                      pl.BlockSpec((None, tr, C), lambda p, i, cr: (p, i, 0))],
            out_specs=pl.BlockSpec((None, tr, C), lambda p, i, cr: (p, i, 0))),
        out_shape=jax.ShapeDtypeStruct((n, h, C), out_dtype), compiler_params=_params(("parallel", "parallel")),
    )(jnp.reshape(c, (1,)).astype(jnp.int32), full, theirs)


def sum_slots(a, name, out_dtype=F32):
    n, R, C = a.shape
    tr = _stream_rows(R, C)

    def body(*refs):
        acc = refs[0][...].astype(F32)
        for r in refs[1:n]:
            acc = acc + r[...].astype(F32)
        refs[n][...] = acc.astype(out_dtype)

    return pl.pallas_call(
        body, name=name, grid=(R // tr,),
        in_specs=[pl.BlockSpec((None, tr, C), lambda i, p=p: (p, i, 0)) for p in range(n)],
        out_specs=pl.BlockSpec((tr, C), lambda i: (i, 0)),
        out_shape=jax.ShapeDtypeStruct((R, C), out_dtype), compiler_params=_params(("parallel",)),
    )(*([a] * n))


def _coords():
    return lax.axis_index("x"), lax.axis_index("y"), lax.axis_index("c")


def _other_chips(x, y):
    return [(1 - x, y), (x, 1 - y), (1 - x, 1 - y)]


def _hbm_call(body, name, ins, out_shapes, n_sems):
    return pl.pallas_call(
        body, name=name, out_shape=tuple(out_shapes),
        in_specs=[pl.BlockSpec(memory_space=pl.ANY)] * len(ins),
        out_specs=tuple(pl.BlockSpec(memory_space=pl.ANY) for _ in out_shapes),
        scratch_shapes=[pltpu.SemaphoreType.DMA((n_sems,)), pltpu.SemaphoreType.DMA((n_sems,))],
    )(*ins)


def allgather_small(v, name):
    m_per, n = v.shape

    def body(x_ref, out_ref, send_sems, recv_sems, local_sem):
        x, y, c = _coords()
        me, sibling = (x, y, c), (x, y, 1 - c)
        chips = _other_chips(x, y)

        def rows(px, py, pc):
            return out_ref.at[pl.ds((4 * px + 2 * py + pc) * m_per, m_per), :]

        def copy(k, block, to, src=None):
            return pltpu.make_async_remote_copy(
                src_ref=rows(*block) if src is None else src, dst_ref=rows(*block),
                send_sem=send_sems.at[k], recv_sem=recv_sems.at[k], device_id=to, device_id_type=MESH)

        mine = pltpu.make_async_copy(x_ref, rows(*me), local_sem)
        mine.start()
        first = [copy(0, me, sibling, src=x_ref)]
        first += [copy(1 + j, me, (*chip, c), src=x_ref) for j, chip in enumerate(chips)]
        for cp in first:
            cp.start()
        passed = [copy(4 + j, (*chip, c), sibling) for j, chip in enumerate(chips)]
        for j, chip in enumerate(chips):
            copy(1 + j, (*chip, c), me).wait_recv()
            passed[j].start()
        copy(0, sibling, me).wait_recv()
        for j, chip in enumerate(chips):
            copy(4 + j, (*chip, 1 - c), me).wait_recv()
        for cp in first + passed:
            cp.wait_send()
        mine.wait()

    return pl.pallas_call(
        body, name=name, out_shape=jax.ShapeDtypeStruct((N_DEV * m_per, n), v.dtype),
        in_specs=[pl.BlockSpec(memory_space=pltpu.VMEM)], out_specs=pl.BlockSpec(memory_space=pltpu.VMEM),
        scratch_shapes=[pltpu.SemaphoreType.DMA((7,)), pltpu.SemaphoreType.DMA((7,)), pltpu.SemaphoreType.DMA],
        compiler_params=pltpu.CompilerParams(vmem_limit_bytes=V7X_VMEM_LIMIT),
    )(v)


def allgather_chips(arrs, name):
    n = len(arrs)
    halves = [a.shape[0] // 2 for a in arrs]

    def body(*refs):
        xs, outs = refs[:n], refs[n:2 * n]
        send_sems, recv_sems = refs[2 * n:]
        x, y, c = _coords()
        me, sibling, chips = 2 * x + y, (x, y, 1 - c), _other_chips(x, y)

        def half(ref, cc, i):
            return ref.at[pl.ds(cc * halves[i], halves[i]), :]

        def copy(i, k, src, dst, to):
            return pltpu.make_async_remote_copy(src_ref=src, dst_ref=dst, send_sem=send_sems.at[6 * i + k],
                                                recv_sem=recv_sems.at[6 * i + k], device_id=to, device_id_type=MESH)

        sends = [copy(i, k, half(xs[i], c, i), half(outs[i].at[me], c, i), (*chip, c))
                 for k, chip in enumerate(chips) for i in range(n)]
        for cp in sends:
            cp.start()
        passed = []
        for k, (px, py) in enumerate(chips):
            for i in range(n):
                landed = half(outs[i].at[2 * px + py], c, i)
                copy(i, k, landed, landed, (px, py, c)).wait_recv()
                passed.append(copy(i, 3 + k, landed, landed, sibling))
                passed[-1].start()
        for k, (px, py) in enumerate(chips):
            for i in range(n):
                theirs = half(outs[i].at[2 * px + py], 1 - c, i)
                copy(i, 3 + k, theirs, theirs, sibling).wait_recv()
        for cp in sends + passed:
            cp.wait_send()

    return _hbm_call(body, name, arrs, [jax.ShapeDtypeStruct((N_CHIPS,) + a.shape, a.dtype) for a in arrs], 6 * n)


def pair_exchange(arrs, name):
    n = len(arrs)
    halves = [a.shape[1] // 2 for a in arrs]

    def body(*refs):
        xs, theirs = refs[:n], refs[n:2 * n]
        send_sems, recv_sems = refs[2 * n:]
        x, y, c = _coords()
        cps = [pltpu.make_async_remote_copy(
            src_ref=xs[i].at[:, pl.ds((1 - c) * halves[i], halves[i]), :], dst_ref=theirs[i],
            send_sem=send_sems.at[i], recv_sem=recv_sems.at[i], device_id=(x, y, 1 - c), device_id_type=MESH)
            for i in range(n)]
        for cp in cps:
            cp.start()
        for cp in cps:
            cp.wait()

    shapes = [jax.ShapeDtypeStruct((a.shape[0], a.shape[1] // 2, a.shape[2]), a.dtype) for a in arrs]
    return _hbm_call(body, name, arrs, shapes, n)


def scatter_chips(arrs, name):
    n = len(arrs)

    def body(*refs):
        xs, outs = refs[:n], refs[n:2 * n]
        send_sems, recv_sems = refs[2 * n:]
        x, y, c = _coords()
        me, chips = 2 * x + y, _other_chips(x, y)

        def copy(i, k, src_slot, dst_slot, to):
            return pltpu.make_async_remote_copy(
                src_ref=xs[i].at[src_slot], dst_ref=outs[i].at[dst_slot], send_sem=send_sems.at[3 * i + k],
                recv_sem=recv_sems.at[3 * i + k], device_id=to, device_id_type=MESH)

        sends = [copy(i, k, 2 * px + py, me, (px, py, c)) for k, (px, py) in enumerate(chips) for i in range(n)]
        for cp in sends:
            cp.start()
        for k, (px, py) in enumerate(chips):
            for i in range(n):
                copy(i, k, me, 2 * px + py, (px, py, c)).wait_recv()
        for cp in sends:
            cp.wait_send()

    return _hbm_call(body, name, arrs, [jax.ShapeDtypeStruct(a.shape, a.dtype) for a in arrs], 3 * n)


def pair_allgather(arrs, name):
    n = len(arrs)

    def body(*refs):
        xs, outs = refs[:n], refs[n:2 * n]
        send_sems, recv_sems = refs[2 * n:]
        x, y, c = _coords()
        cps = [pltpu.make_async_remote_copy(
            src_ref=xs[i], dst_ref=outs[i].at[pl.ds(c * xs[i].shape[0], xs[i].shape[0]), :], send_sem=send_sems.at[i],
            recv_sem=recv_sems.at[i], device_id=(x, y, 1 - c), device_id_type=MESH) for i in range(n)]
        for cp in cps:
            cp.start()
        for cp in cps:
            cp.wait()

    return _hbm_call(body, name, arrs, [jax.ShapeDtypeStruct((2 * a.shape[0], a.shape[1]), a.dtype) for a in arrs], n)


GROUPS = (("ffn_w_gu",), ("ffn_w_down", "mla_w_o", "ssd_w_out"), ("mla_w_a",), ("mla_w_qb",), ("mla_w_kvb",), ("ssd_w_in",))


def _pad_rows(a, mult):
    r = (-a.shape[0]) % mult
    return a if r == 0 else jnp.concatenate([a, jnp.zeros((r, a.shape[1]), a.dtype)], axis=0)


def _rows2d(a):
    return a.reshape(-1, a.shape[-1])


def _unstack(st, axis):
    full = jnp.moveaxis(st, 0, axis)
    sh = list(full.shape)
    sh[axis:axis + 2] = [sh[axis] * sh[axis + 1]]
    return full.reshape(sh)


def _stack(full, axis):
    sh = list(full.shape)
    sh[axis:axis + 1] = [N_CHIPS, sh[axis] // N_CHIPS]
    return jnp.moveaxis(full.reshape(sh), axis, 0)


def kernel(x, c, positions, norm_gain, ada_w, ada_b, ffn_w_gu, ffn_w_down, mla_w_a, mla_q_a_gain, mla_kv_a_gain, mla_w_qb, mla_w_kvb, mla_q_gain, mla_k_gain, mla_w_o, ssd_w_in, ssd_conv_w, ssd_conv_b, ssd_dt_bias, ssd_a_log, ssd_d, ssd_norm_gain, ssd_w_out, loss_target, m_norm_gain, m_ada_w, m_ada_b, m_ffn_w_gu, m_ffn_w_down, m_mla_w_a, m_mla_q_a_gain, m_mla_kv_a_gain, m_mla_w_qb, m_mla_w_kvb, m_mla_q_gain, m_mla_k_gain, m_mla_w_o, m_ssd_w_in, m_ssd_conv_w, m_ssd_conv_b, m_ssd_dt_bias, m_ssd_a_log, m_ssd_d, m_ssd_norm_gain, m_ssd_w_out, v_norm_gain, v_ada_w, v_ada_b, v_ffn_w_gu, v_ffn_w_down, v_mla_w_a, v_mla_q_a_gain, v_mla_kv_a_gain, v_mla_w_qb, v_mla_w_kvb, v_mla_q_gain, v_mla_k_gain, v_mla_w_o, v_ssd_w_in, v_ssd_conv_w, v_ssd_conv_b, v_ssd_dt_bias, v_ssd_a_log, v_ssd_d, v_ssd_norm_gain, v_ssd_w_out):
    w_in = dict(norm_gain=norm_gain, ada_w=ada_w, ada_b=ada_b, ffn_w_gu=ffn_w_gu, ffn_w_down=ffn_w_down, mla_w_a=mla_w_a, mla_q_a_gain=mla_q_a_gain, mla_kv_a_gain=mla_kv_a_gain, mla_w_qb=mla_w_qb, mla_w_kvb=mla_w_kvb, mla_q_gain=mla_q_gain, mla_k_gain=mla_k_gain, mla_w_o=mla_w_o, ssd_w_in=ssd_w_in, ssd_conv_w=ssd_conv_w, ssd_conv_b=ssd_conv_b, ssd_dt_bias=ssd_dt_bias, ssd_a_log=ssd_a_log, ssd_d=ssd_d, ssd_norm_gain=ssd_norm_gain, ssd_w_out=ssd_w_out)
    m_in = dict(norm_gain=m_norm_gain, ada_w=m_ada_w, ada_b=m_ada_b, ffn_w_gu=m_ffn_w_gu, ffn_w_down=m_ffn_w_down, mla_w_a=m_mla_w_a, mla_q_a_gain=m_mla_q_a_gain, mla_kv_a_gain=m_mla_kv_a_gain, mla_w_qb=m_mla_w_qb, mla_w_kvb=m_mla_w_kvb, mla_q_gain=m_mla_q_gain, mla_k_gain=m_mla_k_gain, mla_w_o=m_mla_w_o, ssd_w_in=m_ssd_w_in, ssd_conv_w=m_ssd_conv_w, ssd_conv_b=m_ssd_conv_b, ssd_dt_bias=m_ssd_dt_bias, ssd_a_log=m_ssd_a_log, ssd_d=m_ssd_d, ssd_norm_gain=m_ssd_norm_gain, ssd_w_out=m_ssd_w_out)
    v_in = dict(norm_gain=v_norm_gain, ada_w=v_ada_w, ada_b=v_ada_b, ffn_w_gu=v_ffn_w_gu, ffn_w_down=v_ffn_w_down, mla_w_a=v_mla_w_a, mla_q_a_gain=v_mla_q_a_gain, mla_kv_a_gain=v_mla_kv_a_gain, mla_w_qb=v_mla_w_qb, mla_w_kvb=v_mla_w_kvb, mla_q_gain=v_mla_q_gain, mla_k_gain=v_mla_k_gain, mla_w_o=v_mla_w_o, ssd_w_in=v_ssd_w_in, ssd_conv_w=v_ssd_conv_w, ssd_conv_b=v_ssd_conv_b, ssd_dt_bias=v_ssd_dt_bias, ssd_a_log=v_ssd_a_log, ssd_d=v_ssd_d, ssd_norm_gain=v_ssd_norm_gain, ssd_w_out=v_ssd_w_out)
    names = list(w_in)
    xi, yi, ci = _coords()
    chip = 2 * xi + yi
    batch = 4 * xi + 2 * yi + ci
    x2, target = x[0], loss_target[0]

    small_sharded = ("norm_gain", "ssd_conv_w", "ssd_conv_b", "ssd_norm_gain")
    pack0 = jnp.concatenate([c.reshape(-1)] + [w_in[n].reshape(-1) for n in small_sharded])
    pack0 = _pad_rows(pack0.reshape(-1, 128), 8)
    g0 = allgather_small(pack0, "gather_small").reshape(N_DEV, -1)
    c_all = g0[:, :D_MODEL]
    P, off = {}, D_MODEL
    for n in small_sharded:
        sz = w_in[n].size
        st = g0[0::2, off:off + sz].reshape((N_CHIPS,) + w_in[n].shape)
        P[n] = _unstack(st, w_in[n].ndim - 1)
        off += sz
    for n in ("mla_q_a_gain", "mla_kv_a_gain", "mla_q_gain", "mla_k_gain", "ssd_dt_bias", "ssd_a_log", "ssd_d"):
        P[n] = w_in[n]

    sc = _silu(c_all)
    n_ada = ada_w.shape[2]
    b_sh = lax.dynamic_slice_in_dim(ada_b, chip * n_ada, n_ada, axis=1)
    mods_sh = jnp.stack([matmul(sc, ada_w[l], "nn", "ada_fwd") for l in range(DEPTH)]) + b_sh[:, None, :]
    g1 = allgather_small(mods_sh.reshape(-1, 128), "gather_mods").reshape(N_DEV, DEPTH, N_DEV, n_ada)
    mods = lax.dynamic_index_in_dim(g1[0::2], batch, axis=2, keepdims=False)
    mods = mods.transpose(1, 0, 2).reshape(DEPTH, 3, 3, D_MODEL)

    shard_groups = [jnp.concatenate([_rows2d(w_in[n]).astype(BF16) for n in grp], axis=0) for grp in GROUPS]
    gathered = allgather_chips(shard_groups, "gather_weights")
    gathered = [lax.dynamic_update_slice(g, s[None], (chip, 0, 0)) for g, s in zip(gathered, shard_groups)]
    G = {}
    for grp, arr in zip(GROUPS, gathered):
        off = 0
        for n in grp:
            rows = w_in[n].size // w_in[n].shape[-1]
            G[n] = arr[:, off:off + rows].reshape((N_CHIPS,) + w_in[n].shape)
            off += rows
    W = {
        "ffn_w_gu": gathered[0],
        "ffn_w_down": [[_unstack(G["ffn_w_down"][:, i, t], 0) for t in range(2)] for i in range(DEPTH)],
        "mla_w_a": [_unstack(G["mla_w_a"][:, j], 0) for j in range(2)],
        "mla_w_qb": [_unstack(G["mla_w_qb"][:, j], 1) for j in range(2)],
        "mla_w_kvb": [_unstack(G["mla_w_kvb"][:, j], 1) for j in range(2)],
        "mla_w_o": [_unstack(G["mla_w_o"][:, j], 0) for j in range(2)],
        "ssd_w_out": [_unstack(G["ssd_w_out"][:, j], 0) for j in range(2)],
    }
    w_in_full = [_unstack(G["ssd_w_in"][:, j], 1) for j in range(2)]
    W["ssd_w_zx"] = [w[:, :D_INNER + CONV_DIM] for w in w_in_full]
    W["ssd_w_dt"] = [w[:, D_INNER + CONV_DIM:] for w in w_in_full]
    X = stand_ins(W)

    pos = positions[0]
    y, vjp = jax.vjp(lambda a, b, p_, x_: trunk(a, b, p_, W, x_, pos), x2, mods, P, X)
    dy, loss_cols = loss_head(y, target)
    dx, dmods, dP, dX = vjp(dy)
    loss = lax.psum(jnp.sum(loss_cols), ("x", "y", "c"))

    small_names = ("norm_gain", "ssd_conv_w", "ssd_conv_b", "ssd_norm_gain", "mla_q_a_gain", "mla_kv_a_gain",
                   "mla_q_gain", "mla_k_gain", "ssd_dt_bias", "ssd_a_log", "ssd_d")
    pack1 = jnp.concatenate([dmods.reshape(-1)] + [dP[n].reshape(-1) for n in small_names])
    pack1 = _pad_rows(jnp.pad(pack1, (0, (-pack1.size) % 128)).reshape(-1, 128), 8)
    rows1 = pack1.shape[0]
    g2 = allgather_small(pack1, "gather_small_grads")
    tot = sum_slots(g2.reshape(N_DEV, rows1, 128), "sum_small_grads").reshape(-1)
    n_mod = DEPTH * 9 * D_MODEL
    grads = {"ada_b": tot[:n_mod].reshape(DEPTH, 9 * D_MODEL)}
    off = n_mod
    for n in small_names:
        sz = dP[n].size
        full = tot[off:off + sz].reshape(dP[n].shape)
        off += sz
        if n in small_sharded:
            k = w_in[n].shape[-1]
            full = lax.dynamic_slice_in_dim(full, chip * k, k, axis=full.ndim - 1)
        grads[n] = full
    dmods_all = g2.reshape(N_DEV, -1)[:, :n_mod].reshape(N_DEV, DEPTH, 9 * D_MODEL)
    dm_sh = lax.dynamic_slice_in_dim(dmods_all, chip * n_ada, n_ada, axis=2)
    grads["ada_w"] = jnp.stack([matmul(sc, dm_sh[:, l], "tn", "ada_dw") for l in range(DEPTH)])

    w_in_g = [jnp.concatenate([dX["ssd_w_zx"][j], dX["ssd_w_dt"][j]], axis=1) for j in range(2)]
    per_name = {
        "ffn_w_gu": [dX["ffn_w_gu"][i][t] for i in range(DEPTH) for t in range(2)],
        "ffn_w_down": [dX["ffn_w_down"][i][t].reshape(N_CHIPS, -1, D_MODEL) for i in range(DEPTH) for t in range(2)],
        "mla_w_a": [g.reshape(N_CHIPS, -1, g.shape[-1]) for g in dX["mla_w_a"]],
        "mla_w_qb": dX["mla_w_qb"], "mla_w_kvb": dX["mla_w_kvb"],
        "mla_w_o": [g.reshape(N_CHIPS, -1, D_MODEL) for g in dX["mla_w_o"]],
        "ssd_w_out": [g.reshape(N_CHIPS, -1, D_MODEL) for g in dX["ssd_w_out"]],
        "ssd_w_in": [_stack(g, 1) for g in w_in_g],
    }
    grad_groups = [jnp.concatenate([p for n in grp for p in per_name[n]], axis=1) for grp in GROUPS]
    theirs = pair_exchange(grad_groups, "grads_to_sibling")
    pair = [sum_own_half(a, b, ci, "sum_pair", BF16) for a, b in zip(grad_groups, theirs)]
    landed = scatter_chips(pair, "grads_to_chips")
    landed = [lax.dynamic_update_slice(a, lax.dynamic_slice_in_dim(p, chip, 1, axis=0), (chip, 0, 0)) for a, p in zip(landed, pair)]
    half_sums = [sum_slots(a, "sum_chips") for a in landed]
    totals = pair_allgather(half_sums, "grad_halves_swap")
    totals = [lax.dynamic_update_slice(t, h, (ci * h.shape[0], 0)) for t, h in zip(totals, half_sums)]
    for grp, arr in zip(GROUPS, totals):
        off = 0
        for n in grp:
            rows = w_in[n].size // w_in[n].shape[-1]
            grads[n] = arr[off:off + rows].reshape(w_in[n].shape)
            off += rows

    deltas, new_m, new_v = {}, {}, {}
    for n in names:
        w = w_in[n]
        d, nm, nv = adamw(_rows2d(w), _rows2d(m_in[n]), _rows2d(v_in[n]), _rows2d(grads[n]))
        deltas[n], new_m[n], new_v[n] = d.reshape(w.shape), nm.reshape(w.shape), nv.reshape(w.shape)

    return (loss, dx[None], *[grads[n] for n in names], *[deltas[n] for n in names],
            *[new_m[n] for n in names], *[new_v[n] for n in names])
```

```python
import numpy as np

import jax
import jax.numpy as jnp
from jax import lax
from jax.experimental import pallas as pl
from jax.experimental.pallas import tpu as pltpu

F32 = jnp.float32
BF16 = jnp.bfloat16
MESH = pl.DeviceIdType.MESH

D_MODEL = 1024
DEPTH = 4
EPS = 1e-6
D_FF = 2816
MLA_HEADS = 16
Q_LORA = 384
KV_LORA = 256
QK_NOPE = 64
QK_ROPE = 32
QK_HEAD = QK_NOPE + QK_ROPE
V_HEAD = 64
ROPE_THETA = 10000.0
D_INNER = 2048
SSD_HEAD_DIM = 64
SSD_HEADS = 32
SSD_GROUPS = 4
SSD_STATE = 128
CONV_WIDTH = 4
CHUNK = 128
CONV_DIM = D_INNER + 2 * SSD_GROUPS * SSD_STATE
ADAM_LR = 0.001
ADAM_B1 = 0.9
ADAM_B2 = 0.999
ADAM_EPS = 1e-08
ADAM_WD = 0.01
ADAM_STEP = 10

N_CHIPS = 4
N_DEV = 8
V7X_VMEM_LIMIT = 56 * 1024 * 1024
ATTN_HEADS_PER_STEP = 4
LANES = 128
LOG2E = 1.4426950408889634


def _params(sem=None):
    return pltpu.CompilerParams(dimension_semantics=sem, vmem_limit_bytes=V7X_VMEM_LIMIT)


def _div_tile(n, pref, quantum):
    if n <= pref:
        return n
    t = (pref // quantum) * quantum
    while t >= quantum:
        if n % t == 0:
            return t
        t -= quantum
    return n


def matmul(a, b, mode, name, out_dtype=F32, stack=0, b_slot=None, tm=1024, tn=1408, tk=2816):
    if b_slot is not None:
        (roff, rows), (nsl, _, n) = b_slot, b.shape
        (M, K) = a.shape
        K2, N = (rows, nsl * n) if mode == "nn" else (nsl * n, rows)
    elif mode == "nn":
        (M, K), (K2, N) = a.shape, b.shape
    elif mode == "nt":
        (M, K), (N, K2) = a.shape, b.shape
    else:
        (K, M), (K2, N) = a.shape, b.shape
    assert K == K2, (a.shape, b.shape, mode)
    tm = _div_tile(M, tm, 128 if mode == "tn" else 16)
    tn = _div_tile(N // stack if stack else (n if b_slot and mode == "nn" else N), tn, 128)
    tk = _div_tile(n if b_slot and mode == "nt" else K, tk, 128)
    nk = K // tk
    if mode == "nn":
        a_spec = pl.BlockSpec((tm, tk), lambda i, j, k: (i, k))
        b_spec = pl.BlockSpec((tk, tn), lambda i, j, k: (k, j))
        dims = (((1,), (0,)), ((), ()))
        if b_slot:
            assert roff % tk == 0
            nbn, rb = n // tn, roff // tk
            b_spec = pl.BlockSpec((None, tk, tn), lambda i, j, k: (j // nbn, rb + k, j % nbn))
    elif mode == "nt":
        a_spec = pl.BlockSpec((tm, tk), lambda i, j, k: (i, k))
        b_spec = pl.BlockSpec((tn, tk), lambda i, j, k: (j, k))
        dims = (((1,), (1,)), ((), ()))
        if b_slot:
            assert roff % tn == 0
            nbk, rb = n // tk, roff // tn
            b_spec = pl.BlockSpec((None, tn, tk), lambda i, j, k: (k // nbk, rb + j, k % nbk))
    else:
        a_spec = pl.BlockSpec((tk, tm), lambda i, j, k: (k, i))
        b_spec = pl.BlockSpec((tk, tn), lambda i, j, k: (k, j))
        dims = (((0,), (0,)), ((), ()))
    if stack:
        nb = N // stack // tn
        out_spec = pl.BlockSpec((None, tm, tn), lambda i, j, k: (j // nb, i, j % nb))
        out_shape = jax.ShapeDtypeStruct((stack, M, N // stack), out_dtype)
    else:
        out_spec = pl.BlockSpec((tm, tn), lambda i, j, k: (i, j))
        out_shape = jax.ShapeDtypeStruct((M, N), out_dtype)
    use_acc = nk > 1 and out_dtype != F32

    def body(a_ref, b_ref, o_ref, *acc):
        p = lax.dot_general(a_ref[...].astype(BF16), b_ref[...].astype(BF16), dims, preferred_element_type=F32)
        if nk == 1:
            o_ref[...] = p.astype(out_dtype)
            return
        acc_ref = acc[0] if use_acc else o_ref
        k = pl.program_id(2)

        @pl.when(k == 0)
        def _():
            acc_ref[...] = p

        @pl.when(k > 0)
        def _():
            acc_ref[...] += p

        if use_acc:
            @pl.when(k == nk - 1)
            def _():
                o_ref[...] = acc_ref[...].astype(out_dtype)

    return pl.pallas_call(
        body, name=name, grid=(M // tm, N // tn, nk), in_specs=[a_spec, b_spec], out_specs=out_spec, out_shape=out_shape,
        scratch_shapes=[pltpu.VMEM((tm, tn), F32)] if use_acc else [],
        compiler_params=_params(("parallel", "parallel", "arbitrary")),
    )(a, b)


def mm_op(name, out_dtype=F32, stack=0, b_slot=None):
    @jax.custom_vjp
    def op(a, w, wp):
        return matmul(a, w, "nn", name + "_fwd", out_dtype=out_dtype, b_slot=b_slot)

    def fwd(a, w, wp):
        return op(a, w, wp), (a, w)

    def bwd(res, g):
        a, w = res
        return (matmul(g, w, "nt", name + "_dx", out_dtype=a.dtype, b_slot=b_slot), jnp.zeros_like(w),
                matmul(a, g, "tn", name + "_dw", stack=stack, tm=1408, tn=1408, tk=1024))

    op.defvjp(fwd, bwd)
    return op


def _row_tile(rows, widths):
    w = max(widths)
    t = 128 if w > 4096 else (256 if w > 1024 else (512 if w > 128 else 2048))
    return _div_tile(rows, t, 16)


def _row_spec(r, tm):
    nb = r.shape[0] // tm
    return pl.BlockSpec((tm, r.shape[1]), lambda i: (i % nb, 0))


def _rowwise_fwd(f, rows, vecs, name, out_dtype):
    n_r, n_v = len(rows), len(vecs)
    S = rows[0].shape[0]
    tm = _row_tile(min(r.shape[0] for r in rows), [r.shape[1] for r in rows])
    outs = jax.eval_shape(f, *[jax.ShapeDtypeStruct((tm, r.shape[1]), F32) for r in rows], *vecs)
    out_dtypes = out_dtype if isinstance(out_dtype, tuple) else (out_dtype,) * len(outs)

    def body(*refs):
        res = f(*[r[...].astype(F32) for r in refs[: n_r + n_v]])
        for o, r in zip(refs[n_r + n_v:], res):
            o[...] = r.astype(o.dtype)

    return pl.pallas_call(
        body, name=name, grid=(S // tm,),
        in_specs=[_row_spec(r, tm) for r in rows] + [pl.BlockSpec(v.shape, lambda i: (0, 0)) for v in vecs],
        out_specs=tuple(pl.BlockSpec((tm, o.shape[1]), lambda i: (i, 0)) for o in outs),
        out_shape=tuple(jax.ShapeDtypeStruct((S, o.shape[1]), dt) for o, dt in zip(outs, out_dtypes)),
        compiler_params=_params(("parallel",)),
    )(*rows, *vecs)


def _rowwise_bwd(f, rows, vecs, douts, diff_rows, n_const, name):
    n_r, n_o = len(rows), len(douts)
    consts, vecs = vecs[len(vecs) - n_const:], vecs[:len(vecs) - n_const]
    n_v = len(vecs)
    S = rows[0].shape[0]
    tm = _row_tile(min(r.shape[0] for r in rows), [r.shape[1] for r in rows] + [d.shape[1] for d in douts])
    d_idx = [i for i in range(n_r) if diff_rows[i]]

    def body(*refs):
        row_v = [r[...].astype(F32) for r in refs[:n_r]]
        vec_v = [r[...] for r in refs[n_r:n_r + n_v]]
        const_v = [r[...] for r in refs[n_r + n_v:n_r + n_v + n_const]]
        dout_v = tuple(r[...].astype(F32) for r in refs[n_r + n_v + n_const:n_r + n_v + n_const + n_o])
        out_refs = refs[n_r + n_v + n_const + n_o:]

        def g(*args):
            full = list(row_v)
            for j, i in enumerate(d_idx):
                full[i] = args[j]
            return f(*full, *args[len(d_idx):], *const_v)

        _, vjp = jax.vjp(g, *[row_v[i] for i in d_idx], *vec_v)
        grads = vjp(dout_v)
        for j in range(len(d_idx)):
            out_refs[j][...] = grads[j].astype(out_refs[j].dtype)
        step = pl.program_id(0)
        for j in range(n_v):
            gv, o = grads[len(d_idx) + j], out_refs[len(d_idx) + j]

            @pl.when(step == 0)
            def _(gv=gv, o=o):
                o[...] = gv

            @pl.when(step > 0)
            def _(gv=gv, o=o):
                o[...] += gv

    res = pl.pallas_call(
        body, name=name, grid=(S // tm,),
        in_specs=[_row_spec(r, tm) for r in rows] + [pl.BlockSpec(v.shape, lambda i: (0, 0)) for v in vecs + consts]
        + [pl.BlockSpec((tm, d.shape[1]), lambda i: (i, 0)) for d in douts],
        out_specs=tuple([pl.BlockSpec((tm, rows[i].shape[1]), lambda i_: (i_, 0)) for i in d_idx]
                        + [pl.BlockSpec(v.shape, lambda i: (0, 0)) for v in vecs]),
        out_shape=tuple([jax.ShapeDtypeStruct(rows[i].shape, rows[i].dtype) for i in d_idx]
                        + [jax.ShapeDtypeStruct(v.shape, F32) for v in vecs]),
        compiler_params=_params(("arbitrary",)),
    )(*rows, *vecs, *consts, *douts)
    drows = [None] * n_r
    for j, i in enumerate(d_idx):
        drows[i] = res[j]
    for i in range(n_r):
        if drows[i] is None:
            drows[i] = jnp.zeros_like(rows[i])
    return tuple(drows) + tuple(res[len(d_idx):]) + tuple(jnp.zeros_like(k) for k in consts)


def rowwise_op(f, name, n_rows, diff_rows=None, out_dtype=F32, n_const=0):
    diff = tuple(diff_rows) if diff_rows is not None else (True,) * n_rows

    @jax.custom_vjp
    def op(*args):
        return _rowwise_fwd(f, args[:n_rows], args[n_rows:], name + "_fwd", out_dtype)

    def fwd(*args):
        return op(*args), args

    def bwd(args, douts):
        return _rowwise_bwd(f, args[:n_rows], args[n_rows:], douts, diff, n_const, name + "_bwd")

    op.defvjp(fwd, bwd)
    return op


def _rms(x, gain):
    return x * lax.rsqrt(jnp.mean(x * x, axis=-1, keepdims=True) + EPS) * gain


def _silu(x):
    return x * jax.nn.sigmoid(x)


def _f_modulate(x, gain, shift, scale):
    return (_rms(x, gain) * (1.0 + scale) + shift,)


def _f_rms(x, gain):
    return (_rms(x, gain),)


def _f_swiglu(gu):
    n = gu.shape[1] // 2
    return (_silu(gu[:, :n]) * gu[:, n:],)


def _f_resid(coef):
    def f(x, y, gate):
        return (x + (coef * gate) * y,)
    return f


def _f_resid_modulate(coef):
    def f(x, y, gate, gain, shift, scale):
        x1 = x + (coef * gate) * y
        return (x1, _rms(x1, gain) * (1.0 + scale) + shift)
    return f


def _f_head_rope(x, cosf, sinf, gain, swap):
    y = _rms(x, gain)
    return (y * cosf + jnp.dot(y, swap, precision=lax.Precision.HIGHEST, preferred_element_type=F32) * sinf,)


def _rope_swap():
    m = np.zeros((QK_HEAD, QK_HEAD), np.float32)
    half = QK_ROPE // 2
    for i in range(half):
        m[QK_NOPE + half + i, QK_NOPE + i] = -1.0
        m[QK_NOPE + i, QK_NOPE + half + i] = 1.0
    return jnp.asarray(m)


def _f_gated_norm(y, z, gain):
    g = y * _silu(z)
    n = g.shape[1] // SSD_GROUPS
    return (jnp.concatenate([_rms(g[:, i * n:(i + 1) * n], gain[:, i * n:(i + 1) * n]) for i in range(SSD_GROUPS)], axis=1),)


HALO = 8


def _conv_taps(ext, w, rows, off):
    acc = None
    for k in range(CONV_WIDTH):
        term = w[k:k + 1, :] * pltpu.roll(ext, CONV_WIDTH - 1 - k, 0)[off:off + rows]
        acc = term if acc is None else acc + term
    return acc


def _conv_specs(S, tm, tc):
    tile = pl.BlockSpec((tm, tc), lambda j, i: (i, j))
    prev = pl.BlockSpec((HALO, tc), lambda j, i: (jnp.maximum(i * (tm // HALO) - 1, 0), j))
    nxt = pl.BlockSpec((HALO, tc), lambda j, i: (jnp.minimum((i + 1) * (tm // HALO), S // HALO - 1), j))
    wspec = pl.BlockSpec((CONV_WIDTH, tc), lambda j, i: (0, j))
    bspec = pl.BlockSpec((1, tc), lambda j, i: (0, j))
    return tile, prev, nxt, wspec, bspec


def conv_silu_fwd(u, w, b, tm=512, tc=1024):
    S, C = u.shape
    tm = _div_tile(S, tm, HALO)

    def body(u_ref, p_ref, w_ref, b_ref, o_ref):
        prev = jnp.where(pl.program_id(1) == 0, 0.0, p_ref[...])
        ext = jnp.concatenate([prev, u_ref[...]], axis=0)
        conv = _conv_taps(ext, w_ref[...], tm, HALO) + b_ref[...]
        o_ref[...] = conv * jax.nn.sigmoid(conv)

    tile, prev, _, wspec, bspec = _conv_specs(S, tm, tc)
    return pl.pallas_call(
        body, name="conv_silu_fwd", grid=(C // tc, S // tm), in_specs=[tile, prev, wspec, bspec], out_specs=tile,
        out_shape=jax.ShapeDtypeStruct((S, C), F32), compiler_params=_params(("parallel", "arbitrary")),
    )(u, u, w, b)


def conv_silu_bwd(u, w, b, dout, tm=512, tc=1024):
    S, C = u.shape
    tm = _div_tile(S, tm, HALO)
    n = S // tm
    ext_rows = tm + HALO

    def body(u_ref, p_ref, n_ref, g_ref, gn_ref, w_ref, b_ref, du_ref, dw_ref, db_ref):
        i = pl.program_id(1)
        wv = w_ref[...]
        prev = jnp.where(i == 0, 0.0, p_ref[...])
        ext = jnp.concatenate([prev, u_ref[...], n_ref[...]], axis=0)
        conv = _conv_taps(ext, wv, ext_rows, HALO) + b_ref[...]
        g_ext = jnp.concatenate([g_ref[...], jnp.where(i == n - 1, 0.0, gn_ref[...])], axis=0)
        sg = jax.nn.sigmoid(conv)
        dconv = g_ext * (sg * (1.0 + conv * (1.0 - sg)))
        du = None
        for k in range(CONV_WIDTH):
            s = CONV_WIDTH - 1 - k
            term = wv[k:k + 1, :] * pltpu.roll(dconv, (ext_rows - s) % ext_rows, 0)[:tm]
            du = term if du is None else du + term
        du_ref[...] = du
        dc = dconv[:tm]
        dw = jnp.concatenate([jnp.sum(dc * pltpu.roll(ext, CONV_WIDTH - 1 - k, 0)[HALO:HALO + tm], axis=0, keepdims=True)
                              for k in range(CONV_WIDTH)], axis=0)
        dbv = jnp.sum(dc, axis=0, keepdims=True)

        @pl.when(i == 0)
        def _():
            dw_ref[...] = dw
            db_ref[...] = dbv

        @pl.when(i > 0)
        def _():
            dw_ref[...] += dw
            db_ref[...] += dbv

    tile, prev, nxt, wspec, bspec = _conv_specs(S, tm, tc)
    return pl.pallas_call(
        body, name="conv_silu_bwd", grid=(C // tc, n), in_specs=[tile, prev, nxt, tile, nxt, wspec, bspec],
        out_specs=(tile, wspec, bspec),
        out_shape=(jax.ShapeDtypeStruct((S, C), F32), jax.ShapeDtypeStruct((CONV_WIDTH, C), F32),
                   jax.ShapeDtypeStruct((1, C), F32)),
        compiler_params=_params(("parallel", "arbitrary")),
    )(u, u, u, dout, dout, w, b)


@jax.custom_vjp
def conv_silu(u, w, b):
    return conv_silu_fwd(u, w, b)


def _conv_silu_fwd(u, w, b):
    return conv_silu_fwd(u, w, b), (u, w, b)


def _conv_silu_bwd(res, dout):
    return conv_silu_bwd(*res, dout)


conv_silu.defvjp(_conv_silu_fwd, _conv_silu_bwd)


_NT = (((1,), (1,)), ((), ()))


def _dot(a, b):
    return jnp.dot(a.astype(BF16), b.astype(BF16), preferred_element_type=F32)


def _dot_nt(a, b):
    return lax.dot_general(a.astype(BF16), b.astype(BF16), _NT, preferred_element_type=F32)


def _attn_tile(S):
    return _div_tile(S, 512, 128)


def _causal(t, transposed=False):
    r = lax.broadcasted_iota(jnp.int32, (t, t), 0)
    c = lax.broadcasted_iota(jnp.int32, (t, t), 1)
    return r <= c if transposed else r >= c


def _tri_tables(n, by_key):
    if by_key:
        pairs = [(i, j) for j in range(n) for i in range(j, n)]
    else:
        pairs = [(i, j) for i in range(n) for j in range(i + 1)]
    return (jnp.asarray(np.array([p[0] for p in pairs], np.int32)), jnp.asarray(np.array([p[1] for p in pairs], np.int32)))


def attn_fwd(q, k, v):
    H, S, dk = q.shape
    dv = v.shape[-1]
    t, hb = _attn_tile(S), ATTN_HEADS_PER_STEP
    n = S // t
    scale = dk ** -0.5
    qi_tab, kj_tab = _tri_tables(n, by_key=False)

    def body(qi_ref, kj_ref, q_ref, k_ref, v_ref, o_ref, lse_ref, m_s, l_s, acc_s):
        qi, kj = qi_ref[pl.program_id(1)], kj_ref[pl.program_id(1)]

        @pl.when(kj == 0)
        def _():
            m_s[...] = jnp.full(m_s.shape, -jnp.inf, F32)
            l_s[...] = jnp.zeros(l_s.shape, F32)
            acc_s[...] = jnp.zeros(acc_s.shape, F32)

        def step(masked):
            ss = [_dot_nt(q_ref[j], k_ref[j]) for j in range(hb)]
            new = []
            for j in range(hb):
                s = ss[j] * (scale * LOG2E)
                if masked:
                    s = jnp.where(_causal(t), s, -jnp.inf)
                m_old = m_s[j]
                m_new = jnp.maximum(m_old, jnp.max(s, axis=-1, keepdims=True))
                alpha = jnp.exp2(m_old - m_new)
                p = jnp.exp2(s - jnp.tile(m_new, (1, t // LANES)))
                new.append((m_new, alpha * l_s[j] + jnp.sum(p, axis=-1, keepdims=True),
                            alpha[:, :dv] * acc_s[j] + _dot(p, v_ref[j])))
            for j in range(hb):
                m_s[j], l_s[j], acc_s[j] = new[j]

        @pl.when(kj < qi)
        def _():
            step(False)

        @pl.when(kj == qi)
        def _():
            step(True)
            l = l_s[...]
            o_ref[...] = acc_s[...] / l[:, :, :dv]
            lse_ref[...] = (m_s[...] + jnp.log2(l))[:, :, :1]

    qmap = lambda h, s, qi, kj: (h, qi[s], 0)
    kmap = lambda h, s, qi, kj: (h, kj[s], 0)
    return pl.pallas_call(
        body, name="attn_fwd",
        grid_spec=pltpu.PrefetchScalarGridSpec(
            num_scalar_prefetch=2, grid=(H // hb, qi_tab.shape[0]),
            in_specs=[pl.BlockSpec((hb, t, dk), qmap), pl.BlockSpec((hb, t, dk), kmap), pl.BlockSpec((hb, t, dv), kmap)],
            out_specs=(pl.BlockSpec((hb, t, dv), qmap), pl.BlockSpec((hb, t, 1), qmap)),
            scratch_shapes=[pltpu.VMEM((hb, t, LANES), F32), pltpu.VMEM((hb, t, LANES), F32), pltpu.VMEM((hb, t, dv), F32)]),
        out_shape=(jax.ShapeDtypeStruct((H, S, dv), F32), jax.ShapeDtypeStruct((H, S, 1), F32)),
        compiler_params=_params(("parallel", "arbitrary")),
    )(qi_tab, kj_tab, q, k, v)


def attn_bwd(q, k, v, o, lse, do):
    H, S, dk = q.shape
    dv = v.shape[-1]
    t, hb = _attn_tile(S), ATTN_HEADS_PER_STEP
    n = S // t
    scale = dk ** -0.5
    hb = min(hb, H)
    lse_r = lse.reshape(H, 1, S)
    qi_tab, kj_tab = _tri_tables(n, by_key=True)
    tn_dims = (((0,), (0,)), ((), ()))

    def body(qi_ref, kj_ref, q_ref, k_ref, v_ref, o_ref, do_ref, lse_ref, dq_ref, dk_ref, dv_ref):
        pair = pl.program_id(1)
        qi, kj = qi_ref[pair], kj_ref[pair]

        @pl.when(pair == 0)
        def _():
            dq_ref[...] = jnp.zeros(dq_ref.shape, F32)

        ones = jnp.ones((8, dv), F32)

        def step(masked):
            sts = [_dot_nt(k_ref[j], q_ref[j]) for j in range(hb)]
            for j in range(hb):
                dof = do_ref[j]
                dob = dof.astype(BF16)
                delta = lax.dot_general(ones, dof * o_ref[j], _NT, precision=lax.Precision.HIGHEST,
                                        preferred_element_type=F32)[0:1]
                pt = jnp.exp2(sts[j] * (scale * LOG2E) - lse_ref[j])
                if masked:
                    pt = jnp.where(_causal(t, transposed=True), pt, 0.0)
                dvj = _dot(pt, dob)
                dst = (pt * (_dot_nt(v_ref[j], dob) - delta)).astype(BF16)
                dkj = _dot(dst, q_ref[j]) * scale
                rows = pl.ds(pl.multiple_of(qi * t, t), t)
                dq_ref[j, rows, :] += lax.dot_general(dst, k_ref[j], tn_dims, preferred_element_type=F32) * scale
                if masked:
                    dv_ref[j] = dvj
                    dk_ref[j] = dkj
                else:
                    dv_ref[j] += dvj
                    dk_ref[j] += dkj

        @pl.when(qi == kj)
        def _():
            step(True)

        @pl.when(qi > kj)
        def _():
            step(False)

    qmap = lambda h, s, qi, kj: (h, qi[s], 0)
    kmap = lambda h, s, qi, kj: (h, kj[s], 0)
    rowq = lambda h, s, qi, kj: (h, 0, qi[s])
    whole = lambda h, s, qi, kj: (h, 0, 0)
    return pl.pallas_call(
        body, name="attn_bwd",
        grid_spec=pltpu.PrefetchScalarGridSpec(
            num_scalar_prefetch=2, grid=(H // hb, qi_tab.shape[0]),
            in_specs=[pl.BlockSpec((hb, t, dk), qmap), pl.BlockSpec((hb, t, dk), kmap), pl.BlockSpec((hb, t, dv), kmap),
                      pl.BlockSpec((hb, t, dv), qmap), pl.BlockSpec((hb, t, dv), qmap), pl.BlockSpec((hb, 1, t), rowq)],
            out_specs=(pl.BlockSpec((hb, S, dk), whole), pl.BlockSpec((hb, t, dk), kmap), pl.BlockSpec((hb, t, dv), kmap))),
        out_shape=(jax.ShapeDtypeStruct((H, S, dk), F32), jax.ShapeDtypeStruct((H, S, dk), F32),
                   jax.ShapeDtypeStruct((H, S, dv), F32)),
        compiler_params=_params(("parallel", "arbitrary")),
    )(qi_tab, kj_tab, q, k, v, o, do, lse_r)


@jax.custom_vjp
def attention(q, k, v):
    return attn_fwd(q.astype(BF16), k.astype(BF16), v.astype(BF16))[0]


def _attention_fwd(q, k, v):
    qb, kb, vb = q.astype(BF16), k.astype(BF16), v.astype(BF16)
    o, lse = attn_fwd(qb, kb, vb)
    return o, (qb, kb, vb, o, lse)


def _attention_bwd(res, do):
    return attn_bwd(*res, do)


attention.defvjp(_attention_fwd, _attention_bwd)


def _ssd_specs(hb, L, P, N, order):
    xs = pl.BlockSpec((hb, L, P), lambda g, c: (g, order(c), 0))
    col = pl.BlockSpec((None, L, hb), lambda g, c: (g, order(c), 0))
    row = pl.BlockSpec((hb, 1, L), lambda g, c: (g, 0, order(c)))
    bc = pl.BlockSpec((None, L, N), lambda g, c: (g, order(c), 0))
    st = pl.BlockSpec((hb, None, N, P), lambda g, c: (g, order(c), 0, 0))
    return xs, col, row, bc, st


def _ssd_head_scalar(hb):
    return pl.BlockSpec((hb, 1, 1), lambda g, c: (g, 0, 0))


def _ssd_rows(cols, H):
    G, S, hb = cols.shape
    return cols.transpose(0, 2, 1).reshape(H, 1, S)


def ssd_fwd(x, dt, ac, Bm, Cm, skip):
    H, S, P = x.shape
    G, _, N = Bm.shape
    hb, L = H // G, CHUNK
    nc = S // L
    acr = _ssd_rows(ac, H)

    def body(x_ref, dt_ref, ac_ref, acr_ref, b_ref, c_ref, sk_ref, y_ref, hp_ref, h_s):
        @pl.when(pl.program_id(1) == 0)
        def _():
            h_s[...] = jnp.zeros((hb, N, P), F32)

        Bv, Cv = b_ref[...], c_ref[...]
        cb = _dot_nt(Cv, Bv)
        bt = Bv.T
        mask = _causal(L)
        ac_all, dt_all = ac_ref[...], dt_ref[...]
        for j in range(hb):
            a = jnp.broadcast_to(ac_all[:, j:j + 1], (L, L))
            dtv = jnp.broadcast_to(dt_all[:, j:j + 1], (L, L))
            lm = jnp.exp(jnp.where(mask, a - acr_ref[j], -jnp.inf))
            xdt = x_ref[j] * dtv[:, :P]
            h = h_s[j]
            hp_ref[j] = h
            y_ref[j] = _dot(cb * lm, xdt) + jnp.exp(a)[:, :P] * _dot(Cv, h) + sk_ref[j] * x_ref[j]
            al = a[L - 1:L, :]
            h_s[j] = jnp.exp(al)[:, :P] * h + _dot(bt, xdt * jnp.exp(al - a)[:, :P])

    xs, col, row, bc, st = _ssd_specs(hb, L, P, N, lambda c: c)
    return pl.pallas_call(
        body, name="ssd_fwd", grid=(G, nc), in_specs=[xs, col, col, row, bc, bc, _ssd_head_scalar(hb)], out_specs=(xs, st),
        out_shape=(jax.ShapeDtypeStruct((H, S, P), F32), jax.ShapeDtypeStruct((H, nc, N, P), F32)),
        scratch_shapes=[pltpu.VMEM((hb, N, P), F32)],
        compiler_params=_params(("parallel", "arbitrary")),
    )(x, dt, ac, acr, Bm, Cm, skip)


def ssd_bwd(x, dt, ac, Bm, Cm, skip, hp, dy):
    H, S, P = x.shape
    G, _, N = Bm.shape
    hb, L = H // G, CHUNK
    nc = S // L
    acr = _ssd_rows(ac, H)

    def body(x_ref, dt_ref, ac_ref, acr_ref, b_ref, c_ref, sk_ref, hp_ref, dy_ref,
             dx_ref, ddt_ref, dac_ref, dacr_ref, db_ref, dc_ref, dsk_ref, dh_s):
        @pl.when(pl.program_id(1) == 0)
        def _():
            dh_s[...] = jnp.zeros((hb, N, P), F32)
            dsk_ref[...] = jnp.zeros((hb, 1, 1), F32)

        Bv, Cv = b_ref[...], c_ref[...]
        cb = _dot_nt(Cv, Bv)
        cbt = _dot_nt(Bv, Cv)
        ct = Cv.T
        mask, maskt = _causal(L), _causal(L, transposed=True)
        last = lax.broadcasted_iota(jnp.int32, (L, 1), 0) == L - 1
        lane = lax.broadcasted_iota(jnp.int32, (L, hb), 1)
        db = jnp.zeros((L, N), F32)
        dc = jnp.zeros((L, N), F32)
        dac_all = jnp.zeros((L, hb), F32)
        ddt_all = jnp.zeros((L, hb), F32)
        ac_all, dt_all = ac_ref[...], dt_ref[...]
        for j in range(hb):
            ar, xv, g, h, dh = acr_ref[j], x_ref[j], dy_ref[j], hp_ref[j], dh_s[j]
            a = jnp.broadcast_to(ac_all[:, j:j + 1], (L, L))
            dtv = jnp.broadcast_to(dt_all[:, j:j + 1], (L, L))
            lm = jnp.exp(jnp.where(mask, a - ar, -jnp.inf))
            lmt = jnp.exp(jnp.where(maskt, ar - a, -jnp.inf))
            xdt = xv * dtv[:, :P]
            e = jnp.exp(a)
            al = a[L - 1:L, :]
            dte = jnp.exp(al - a)
            el = jnp.exp(al)
            dcb = _dot_nt(g, xdt) * lm
            dcbt = _dot_nt(xdt, g) * lmt
            dseg = dcb * cb
            ch = _dot(Cv, h)
            bdh = _dot(Bv, dh)
            dxdt = _dot(cbt * lmt, g) + dte[:, :P] * bdh
            dc += _dot(dcb, Bv) + e * _dot_nt(g, h)
            db += _dot(dcbt, Cv) + _dot_nt(xdt * dte[:, :P], dh)
            d_e = jnp.sum(g * ch, axis=-1, keepdims=True)
            d_dte = jnp.sum(xdt * bdh, axis=-1, keepdims=True)
            d_el = jnp.sum(h * dh, keepdims=True)
            d_al = jnp.sum(d_dte * dte[:, :1], keepdims=True) + d_el * el[:, :1]
            dac_j = (jnp.sum(dseg, axis=-1, keepdims=True) + d_e * e[:, :1] - d_dte * dte[:, :1]
                     + jnp.where(last, d_al, 0.0))
            dac_all = jnp.where(lane == j, dac_j, dac_all)
            dacr_ref[j] = -jnp.sum(dseg, axis=0, keepdims=True)
            dx_ref[j] = dxdt * dtv[:, :P] + sk_ref[j] * g
            dsk_ref[j] += jnp.sum(g * xv, keepdims=True)
            ddt_all = jnp.where(lane == j, jnp.sum(dxdt * xv, axis=-1, keepdims=True), ddt_all)
            dh_s[j] = el[:, :P] * dh + _dot(ct, e[:, :P] * g)
        dac_ref[...] = dac_all
        ddt_ref[...] = ddt_all
        db_ref[...] = db
        dc_ref[...] = dc

    xs, col, row, bc, st = _ssd_specs(hb, L, P, N, lambda c: nc - 1 - c)
    one = _ssd_head_scalar(hb)
    dx, ddt, dac, dacr, db, dc, dsk = pl.pallas_call(
        body, name="ssd_bwd", grid=(G, nc), in_specs=[xs, col, col, row, bc, bc, one, st, xs],
        out_specs=(xs, col, col, row, bc, bc, one),
        out_shape=(jax.ShapeDtypeStruct((H, S, P), F32), jax.ShapeDtypeStruct((G, S, hb), F32),
                   jax.ShapeDtypeStruct((G, S, hb), F32), jax.ShapeDtypeStruct((H, 1, S), F32),
                   jax.ShapeDtypeStruct((G, S, N), F32), jax.ShapeDtypeStruct((G, S, N), F32),
                   jax.ShapeDtypeStruct((H, 1, 1), F32)),
        scratch_shapes=[pltpu.VMEM((hb, N, P), F32)],
        compiler_params=_params(("parallel", "arbitrary")),
    )(x, dt, ac, acr, Bm, Cm, skip, hp, dy)
    return dx, ddt, dac + dacr.reshape(G, hb, S).transpose(0, 2, 1), db, dc, dsk


@jax.custom_vjp
def ssd_scan(x, dt, ac, Bm, Cm, skip):
    return ssd_fwd(x, dt, ac, Bm, Cm, skip)[0]


def _ssd_scan_fwd(x, dt, ac, Bm, Cm, skip):
    y, hp = ssd_fwd(x, dt, ac, Bm, Cm, skip)
    return y, (x, dt, ac, Bm, Cm, skip, hp)


def _ssd_scan_bwd(res, dy):
    return ssd_bwd(*res, dy)


ssd_scan.defvjp(_ssd_scan_fwd, _ssd_scan_bwd)


def _vec(v):
    return v.reshape(1, -1)


def _ffn(h, w_gu, gu_row, w_down, x_gu, x_down):
    gu = mm_op("ffn_gu", out_dtype=BF16, stack=N_CHIPS, b_slot=(gu_row, D_MODEL))(h, w_gu, x_gu)
    a, = rowwise_op(_f_swiglu, "swiglu", 1, out_dtype=BF16)(gu)
    return mm_op("ffn_down")(a, w_down, x_down)


def _rope_tables(positions):
    inv = 1.0 / (ROPE_THETA ** (jnp.arange(0, QK_ROPE, 2, dtype=F32) / QK_ROPE))
    ang = positions.astype(F32)[:, None] * inv
    S = positions.shape[0]
    cosf = jnp.concatenate([jnp.ones((S, QK_NOPE), F32), jnp.cos(ang), jnp.cos(ang)], axis=1)
    sinf = jnp.concatenate([jnp.zeros((S, QK_NOPE), F32), jnp.sin(ang), jnp.sin(ang)], axis=1)
    return cosf, sinf


def _heads_first(t):
    return t.transpose(1, 0, 2).reshape(-1, t.shape[-1])


def _mla(h, cos, sin, P, W, X, j):
    S = h.shape[0]
    lat = mm_op("mla_a")(h, W["mla_w_a"][j], X["mla_w_a"][j])
    q_lat, kv_lat, k_rope = jnp.split(lat, [Q_LORA, Q_LORA + KV_LORA], axis=1)
    qn, = rowwise_op(_f_rms, "rms_lat", 1, out_dtype=BF16)(q_lat, _vec(P["mla_q_a_gain"][j]))
    kvn, = rowwise_op(_f_rms, "rms_lat", 1, out_dtype=BF16)(kv_lat, _vec(P["mla_kv_a_gain"][j]))
    q = mm_op("mla_qb", stack=N_CHIPS)(qn, W["mla_w_qb"][j], X["mla_w_qb"][j]).reshape(S, MLA_HEADS, QK_HEAD)
    kv = mm_op("mla_kvb", stack=N_CHIPS)(kvn, W["mla_w_kvb"][j], X["mla_w_kvb"][j]).reshape(S, MLA_HEADS, QK_NOPE + V_HEAD)
    k_nope, v = jnp.split(kv, [QK_NOPE], axis=-1)
    k = jnp.concatenate([k_nope, jnp.broadcast_to(k_rope[:, None, :], (S, MLA_HEADS, QK_ROPE))], axis=-1)
    head_rope = rowwise_op(_f_head_rope, "head_rope", 3, diff_rows=(True, False, False), n_const=1)
    swap = _rope_swap()
    q, = head_rope(_heads_first(q), cos, sin, _vec(P["mla_q_gain"][j]), swap)
    k, = head_rope(_heads_first(k), cos, sin, _vec(P["mla_k_gain"][j]), swap)
    o = attention(q.reshape(MLA_HEADS, S, QK_HEAD), k.reshape(MLA_HEADS, S, QK_HEAD), v.transpose(1, 0, 2))
    o = o.transpose(1, 0, 2).reshape(S, MLA_HEADS * V_HEAD).astype(BF16)
    return mm_op("mla_o")(o, W["mla_w_o"][j], X["mla_w_o"][j])


def _ssd(h, P, W, X, j):
    S = h.shape[0]
    z = mm_op("ssd_in_z")(h, W["ssd_w_z"][j], X["ssd_w_z"][j])
    xbc = mm_op("ssd_in_xbc")(h, W["ssd_w_xbc"][j], X["ssd_w_xbc"][j])
    dtr = mm_op("ssd_in_dt")(h, W["ssd_w_dt"][j], X["ssd_w_dt"][j])
    xbc = conv_silu(xbc, P["ssd_conv_w"][j], _vec(P["ssd_conv_b"][j]))
    xs, Bm, Cm = jnp.split(xbc, [D_INNER, D_INNER + SSD_GROUPS * SSD_STATE], axis=1)
    xs = xs.reshape(S, SSD_HEADS, SSD_HEAD_DIM).transpose(1, 0, 2)
    Bm = Bm.reshape(S, SSD_GROUPS, SSD_STATE).transpose(1, 0, 2)
    Cm = Cm.reshape(S, SSD_GROUPS, SSD_STATE).transpose(1, 0, 2)
    dt = jax.nn.softplus(dtr + P["ssd_dt_bias"][j][None, :])
    A = -jnp.exp(P["ssd_a_log"][j])
    a = (dt * A[None, :]).reshape(S // CHUNK, CHUNK, SSD_HEADS)
    ac = jnp.cumsum(a, axis=1).reshape(S, SSD_HEADS)
    by_group = lambda t: t.reshape(S, SSD_GROUPS, SSD_HEADS // SSD_GROUPS).transpose(1, 0, 2)
    y = ssd_scan(xs, by_group(dt), by_group(ac), Bm, Cm, P["ssd_d"][j].reshape(SSD_HEADS, 1, 1))
    y = y.transpose(1, 0, 2).reshape(S, D_INNER)
    g, = rowwise_op(_f_gated_norm, "gated_norm", 2, out_dtype=BF16)(y, z, _vec(P["ssd_norm_gain"][j]))
    return mm_op("ssd_out")(g, W["ssd_w_out"][j], X["ssd_w_out"][j])


def trunk(x, mods, P, W, X, positions):
    cos, sin = _rope_tables(positions)

    def sub(i, s, h):
        if s == 1:
            return _mla(h, cos, sin, P, W, X, i // 2) if i % 2 == 0 else _ssd(h, P, W, X, i // 2)
        t = s // 2
        return _ffn(h, W["ffn_w_gu"], (2 * i + t) * D_MODEL, W["ffn_w_down"][i][t], X["ffn_w_gu"][i][t], X["ffn_w_down"][i][t])

    order = [(i, s) for i in range(DEPTH) for s in range(3)]
    i0, s0 = order[0]
    h, = rowwise_op(_f_modulate, "modulate", 1, out_dtype=BF16)(
        x, _vec(P["norm_gain"][i0, s0]), _vec(mods[i0, s0, 0]), _vec(mods[i0, s0, 1]))
    for n, (i, s) in enumerate(order):
        y = sub(i, s, h)
        coef = 1.0 if s == 1 else 0.5
        gate = _vec(mods[i, s, 2])
        if n + 1 < len(order):
            i1, s1 = order[n + 1]
            x, h = rowwise_op(_f_resid_modulate(coef), "resid_modulate", 2, out_dtype=(F32, BF16))(
                x, y, gate, _vec(P["norm_gain"][i1, s1]), _vec(mods[i1, s1, 0]), _vec(mods[i1, s1, 1]))
        else:
            x, = rowwise_op(_f_resid(coef), "resid", 2)(x, y, gate)
    return x


def stand_ins(W):
    def one(name, w):
        if name in ("mla_w_qb", "mla_w_kvb"):
            return jnp.zeros((N_CHIPS, w.shape[0], w.shape[1] // N_CHIPS), F32)
        return jnp.zeros(w.shape, F32)
    X = {n: jax.tree.map(lambda w, n=n: one(n, w), W[n]) for n in W if n != "ffn_w_gu"}
    g = W["ffn_w_gu"]
    X["ffn_w_gu"] = [[jnp.zeros((g.shape[0], D_MODEL, g.shape[2]), F32) for _ in range(2)] for _ in range(DEPTH)]
    return X


def loss_head(y, target):
    S, D = y.shape
    tm = _row_tile(S, [D])

    def body(y_ref, t_ref, dy_ref, l_ref):
        d = y_ref[...] - t_ref[...]
        dy_ref[...] = d * (1.0 / D)
        part = jnp.sum(d * d, axis=0, keepdims=True) * (0.5 / D)

        @pl.when(pl.program_id(0) == 0)
        def _():
            l_ref[...] = part

        @pl.when(pl.program_id(0) > 0)
        def _():
            l_ref[...] += part

    return pl.pallas_call(
        body, name="loss_head", grid=(S // tm,),
        in_specs=[pl.BlockSpec((tm, D), lambda i: (i, 0))] * 2,
        out_specs=(pl.BlockSpec((tm, D), lambda i: (i, 0)), pl.BlockSpec((1, D), lambda i: (0, 0))),
        out_shape=(jax.ShapeDtypeStruct((S, D), F32), jax.ShapeDtypeStruct((1, D), F32)),
        compiler_params=_params(("arbitrary",)),
    )(y, target)


def _stream_rows(R, C):
    return _div_tile(R, max(16, (1 << 19) // C // 16 * 16), 16)


def adamw(w, m, v, g):
    R, C = w.shape
    tr = _stream_rows(R, C)
    c1 = 1.0 / (1.0 - ADAM_B1 ** ADAM_STEP)
    c2 = 1.0 / (1.0 - ADAM_B2 ** ADAM_STEP)

    def body(w_ref, m_ref, v_ref, g_ref, d_ref, nm_ref, nv_ref):
        gv = g_ref[...]
        nm = ADAM_B1 * m_ref[...] + (1.0 - ADAM_B1) * gv
        nv = ADAM_B2 * v_ref[...] + (1.0 - ADAM_B2) * (gv * gv)
        d_ref[...] = -ADAM_LR * ((nm * c1) / (jnp.sqrt(nv * c2) + ADAM_EPS) + ADAM_WD * w_ref[...])
        nm_ref[...] = nm
        nv_ref[...] = nv

    spec = pl.BlockSpec((tr, C), lambda i: (i, 0))
    return pl.pallas_call(
        body, name="adamw", grid=(R // tr,), in_specs=[spec] * 4, out_specs=(spec,) * 3,
        out_shape=(jax.ShapeDtypeStruct((R, C), F32),) * 3, compiler_params=_params(("parallel",)),
    )(w, m, v, g)


def sum_parts(parts, name, out_dtype=F32):
    R, C = parts[0].shape
    tr = _stream_rows(R, C)
    n = len(parts)

    def body(*refs):
        acc = refs[0][...].astype(F32)
        for r in refs[1:n]:
            acc = acc + r[...].astype(F32)
        refs[n][...] = acc.astype(out_dtype)

    spec = pl.BlockSpec((tr, C), lambda i: (i, 0))
    return pl.pallas_call(
        body, name=name, grid=(R // tr,), in_specs=[spec] * n, out_specs=spec,
        out_shape=jax.ShapeDtypeStruct((R, C), out_dtype), compiler_params=_params(("parallel",)),
    )(*parts)


def sum_own_half(full, theirs, c, name, out_dtype):
    n, R, C = full.shape
    h = R // 2
    tr = _stream_rows(h, C)
    nb = h // tr

    def body(c_ref, a_ref, b_ref, o_ref):
        o_ref[...] = (a_ref[...].astype(F32) + b_ref[...].astype(F32)).astype(out_dtype)

    return pl.pallas_call(
        body, name=name,
        grid_spec=pltpu.PrefetchScalarGridSpec(
            num_scalar_prefetch=1, grid=(n, nb),
            in_specs=[pl.BlockSpec((None, tr, C), lambda p, i, cr: (p, cr[0] * nb + i, 0)),
                      pl.BlockSpec((None, tr, C), lambda p, i, cr: (p, i, 0))],
            out_specs=pl.BlockSpec((None, tr, C), lambda p, i, cr: (p, i, 0))),
        out_shape=jax.ShapeDtypeStruct((n, h, C), out_dtype), compiler_params=_params(("parallel", "parallel")),
    )(jnp.reshape(c, (1,)).astype(jnp.int32), full, theirs)


def sum_slots(a, name, out_dtype=F32):
    n, R, C = a.shape
    tr = _stream_rows(R, C)

    def body(*refs):
        acc = refs[0][...].astype(F32)
        for r in refs[1:n]:
            acc = acc + r[...].astype(F32)
        refs[n][...] = acc.astype(out_dtype)

    return pl.pallas_call(
        body, name=name, grid=(R // tr,),
        in_specs=[pl.BlockSpec((None, tr, C), lambda i, p=p: (p, i, 0)) for p in range(n)],
        out_specs=pl.BlockSpec((tr, C), lambda i: (i, 0)),
        out_shape=jax.ShapeDtypeStruct((R, C), out_dtype), compiler_params=_params(("parallel",)),
    )(*([a] * n))


def _coords():
    return lax.axis_index("x"), lax.axis_index("y"), lax.axis_index("c")


def _other_chips(x, y):
    return [(1 - x, y), (x, 1 - y), (1 - x, 1 - y)]


def _hbm_call(body, name, ins, out_shapes, n_sems):
    return pl.pallas_call(
        body, name=name, out_shape=tuple(out_shapes),
        in_specs=[pl.BlockSpec(memory_space=pl.ANY)] * len(ins),
        out_specs=tuple(pl.BlockSpec(memory_space=pl.ANY) for _ in out_shapes),
        scratch_shapes=[pltpu.SemaphoreType.DMA((n_sems,)), pltpu.SemaphoreType.DMA((n_sems,))],
    )(*ins)


def allgather_small(v, name):
    m_per, n = v.shape

    def body(x_ref, out_ref, send_sems, recv_sems, local_sem):
        x, y, c = _coords()
        me, sibling = (x, y, c), (x, y, 1 - c)
        chips = _other_chips(x, y)

        def rows(px, py, pc):
            return out_ref.at[pl.ds((4 * px + 2 * py + pc) * m_per, m_per), :]

        def copy(k, block, to, src=None):
            return pltpu.make_async_remote_copy(
                src_ref=rows(*block) if src is None else src, dst_ref=rows(*block),
                send_sem=send_sems.at[k], recv_sem=recv_sems.at[k], device_id=to, device_id_type=MESH)

        mine = pltpu.make_async_copy(x_ref, rows(*me), local_sem)
        mine.start()
        first = [copy(0, me, sibling, src=x_ref)]
        first += [copy(1 + j, me, (*chip, c), src=x_ref) for j, chip in enumerate(chips)]
        for cp in first:
            cp.start()
        passed = [copy(4 + j, (*chip, c), sibling) for j, chip in enumerate(chips)]
        for j, chip in enumerate(chips):
            copy(1 + j, (*chip, c), me).wait_recv()
            passed[j].start()
        copy(0, sibling, me).wait_recv()
        for j, chip in enumerate(chips):
            copy(4 + j, (*chip, 1 - c), me).wait_recv()
        for cp in first + passed:
            cp.wait_send()
        mine.wait()

    return pl.pallas_call(
        body, name=name, out_shape=jax.ShapeDtypeStruct((N_DEV * m_per, n), v.dtype),
        in_specs=[pl.BlockSpec(memory_space=pltpu.VMEM)], out_specs=pl.BlockSpec(memory_space=pltpu.VMEM),
        scratch_shapes=[pltpu.SemaphoreType.DMA((7,)), pltpu.SemaphoreType.DMA((7,)), pltpu.SemaphoreType.DMA],
        compiler_params=pltpu.CompilerParams(vmem_limit_bytes=V7X_VMEM_LIMIT),
    )(v)


def allgather_chips(arrs, name):
    n = len(arrs)
    halves = [a.shape[0] // 2 for a in arrs]

    def body(*refs):
        xs, outs = refs[:n], refs[n:2 * n]
        send_sems, recv_sems = refs[2 * n:]
        x, y, c = _coords()
        me, sibling, chips = 2 * x + y, (x, y, 1 - c), _other_chips(x, y)

        def half(ref, cc, i):
            return ref.at[pl.ds(cc * halves[i], halves[i]), :]

        def copy(i, k, src, dst, to):
            return pltpu.make_async_remote_copy(src_ref=src, dst_ref=dst, send_sem=send_sems.at[6 * i + k],
                                                recv_sem=recv_sems.at[6 * i + k], device_id=to, device_id_type=MESH)

        sends = [copy(i, k, half(xs[i], c, i), half(outs[i].at[me], c, i), (*chip, c))
                 for k, chip in enumerate(chips) for i in range(n)]
        for cp in sends:
            cp.start()
        passed = []
        for k, (px, py) in enumerate(chips):
            for i in range(n):
                landed = half(outs[i].at[2 * px + py], c, i)
                copy(i, k, landed, landed, (px, py, c)).wait_recv()
                passed.append(copy(i, 3 + k, landed, landed, sibling))
                passed[-1].start()
        for k, (px, py) in enumerate(chips):
            for i in range(n):
                theirs = half(outs[i].at[2 * px + py], 1 - c, i)
                copy(i, 3 + k, theirs, theirs, sibling).wait_recv()
        for cp in sends + passed:
            cp.wait_send()

    return _hbm_call(body, name, arrs, [jax.ShapeDtypeStruct((N_CHIPS,) + a.shape, a.dtype) for a in arrs], 6 * n)


def pair_exchange(arrs, name):
    n = len(arrs)
    halves = [a.shape[1] // 2 for a in arrs]

    def body(*refs):
        xs, theirs = refs[:n], refs[n:2 * n]
        send_sems, recv_sems = refs[2 * n:]
        x, y, c = _coords()
        cps = [pltpu.make_async_remote_copy(
            src_ref=xs[i].at[:, pl.ds((1 - c) * halves[i], halves[i]), :], dst_ref=theirs[i],
            send_sem=send_sems.at[i], recv_sem=recv_sems.at[i], device_id=(x, y, 1 - c), device_id_type=MESH)
            for i in range(n)]
        for cp in cps:
            cp.start()
        for cp in cps:
            cp.wait()

    shapes = [jax.ShapeDtypeStruct((a.shape[0], a.shape[1] // 2, a.shape[2]), a.dtype) for a in arrs]
    return _hbm_call(body, name, arrs, shapes, n)


def scatter_chips(arrs, name):
    n = len(arrs)

    def body(*refs):
        xs, outs = refs[:n], refs[n:2 * n]
        send_sems, recv_sems = refs[2 * n:]
        x, y, c = _coords()
        me, chips = 2 * x + y, _other_chips(x, y)

        def copy(i, k, src_slot, dst_slot, to):
            return pltpu.make_async_remote_copy(
                src_ref=xs[i].at[src_slot], dst_ref=outs[i].at[dst_slot], send_sem=send_sems.at[3 * i + k],
                recv_sem=recv_sems.at[3 * i + k], device_id=to, device_id_type=MESH)

        sends = [copy(i, k, 2 * px + py, me, (px, py, c)) for k, (px, py) in enumerate(chips) for i in range(n)]
        for cp in sends:
            cp.start()
        for k, (px, py) in enumerate(chips):
            for i in range(n):
                copy(i, k, me, 2 * px + py, (px, py, c)).wait_recv()
        for cp in sends:
            cp.wait_send()

    return _hbm_call(body, name, arrs, [jax.ShapeDtypeStruct(a.shape, a.dtype) for a in arrs], 3 * n)


def pair_allgather(arrs, name):
    n = len(arrs)

    def body(*refs):
        xs, outs = refs[:n], refs[n:2 * n]
        send_sems, recv_sems = refs[2 * n:]
        x, y, c = _coords()
        cps = [pltpu.make_async_remote_copy(
            src_ref=xs[i], dst_ref=outs[i].at[pl.ds(c * xs[i].shape[0], xs[i].shape[0]), :], send_sem=send_sems.at[i],
            recv_sem=recv_sems.at[i], device_id=(x, y, 1 - c), device_id_type=MESH) for i in range(n)]
        for cp in cps:
            cp.start()
        for cp in cps:
            cp.wait()

    return _hbm_call(body, name, arrs, [jax.ShapeDtypeStruct((2 * a.shape[0], a.shape[1]), a.dtype) for a in arrs], n)


GROUPS = (("ffn_w_gu",), ("ffn_w_down", "mla_w_o", "ssd_w_out"), ("mla_w_a",), ("mla_w_qb",), ("mla_w_kvb",), ("ssd_w_in",))


def _pad_rows(a, mult):
    r = (-a.shape[0]) % mult
    return a if r == 0 else jnp.concatenate([a, jnp.zeros((r, a.shape[1]), a.dtype)], axis=0)


def _rows2d(a):
    return a.reshape(-1, a.shape[-1])


def _unstack(st, axis):
    full = jnp.moveaxis(st, 0, axis)
    sh = list(full.shape)
    sh[axis:axis + 2] = [sh[axis] * sh[axis + 1]]
    return full.reshape(sh)


def _stack(full, axis):
    sh = list(full.shape)
    sh[axis:axis + 1] = [N_CHIPS, sh[axis] // N_CHIPS]
    return jnp.moveaxis(full.reshape(sh), axis, 0)


def kernel(x, c, positions, norm_gain, ada_w, ada_b, ffn_w_gu, ffn_w_down, mla_w_a, mla_q_a_gain, mla_kv_a_gain, mla_w_qb, mla_w_kvb, mla_q_gain, mla_k_gain, mla_w_o, ssd_w_in, ssd_conv_w, ssd_conv_b, ssd_dt_bias, ssd_a_log, ssd_d, ssd_norm_gain, ssd_w_out, loss_target, m_norm_gain, m_ada_w, m_ada_b, m_ffn_w_gu, m_ffn_w_down, m_mla_w_a, m_mla_q_a_gain, m_mla_kv_a_gain, m_mla_w_qb, m_mla_w_kvb, m_mla_q_gain, m_mla_k_gain, m_mla_w_o, m_ssd_w_in, m_ssd_conv_w, m_ssd_conv_b, m_ssd_dt_bias, m_ssd_a_log, m_ssd_d, m_ssd_norm_gain, m_ssd_w_out, v_norm_gain, v_ada_w, v_ada_b, v_ffn_w_gu, v_ffn_w_down, v_mla_w_a, v_mla_q_a_gain, v_mla_kv_a_gain, v_mla_w_qb, v_mla_w_kvb, v_mla_q_gain, v_mla_k_gain, v_mla_w_o, v_ssd_w_in, v_ssd_conv_w, v_ssd_conv_b, v_ssd_dt_bias, v_ssd_a_log, v_ssd_d, v_ssd_norm_gain, v_ssd_w_out):
    w_in = dict(norm_gain=norm_gain, ada_w=ada_w, ada_b=ada_b, ffn_w_gu=ffn_w_gu, ffn_w_down=ffn_w_down, mla_w_a=mla_w_a, mla_q_a_gain=mla_q_a_gain, mla_kv_a_gain=mla_kv_a_gain, mla_w_qb=mla_w_qb, mla_w_kvb=mla_w_kvb, mla_q_gain=mla_q_gain, mla_k_gain=mla_k_gain, mla_w_o=mla_w_o, ssd_w_in=ssd_w_in, ssd_conv_w=ssd_conv_w, ssd_conv_b=ssd_conv_b, ssd_dt_bias=ssd_dt_bias, ssd_a_log=ssd_a_log, ssd_d=ssd_d, ssd_norm_gain=ssd_norm_gain, ssd_w_out=ssd_w_out)
    m_in = dict(norm_gain=m_norm_gain, ada_w=m_ada_w, ada_b=m_ada_b, ffn_w_gu=m_ffn_w_gu, ffn_w_down=m_ffn_w_down, mla_w_a=m_mla_w_a, mla_q_a_gain=m_mla_q_a_gain, mla_kv_a_gain=m_mla_kv_a_gain, mla_w_qb=m_mla_w_qb, mla_w_kvb=m_mla_w_kvb, mla_q_gain=m_mla_q_gain, mla_k_gain=m_mla_k_gain, mla_w_o=m_mla_w_o, ssd_w_in=m_ssd_w_in, ssd_conv_w=m_ssd_conv_w, ssd_conv_b=m_ssd_conv_b, ssd_dt_bias=m_ssd_dt_bias, ssd_a_log=m_ssd_a_log, ssd_d=m_ssd_d, ssd_norm_gain=m_ssd_norm_gain, ssd_w_out=m_ssd_w_out)
    v_in = dict(norm_gain=v_norm_gain, ada_w=v_ada_w, ada_b=v_ada_b, ffn_w_gu=v_ffn_w_gu, ffn_w_down=v_ffn_w_down, mla_w_a=v_mla_w_a, mla_q_a_gain=v_mla_q_a_gain, mla_kv_a_gain=v_mla_kv_a_gain, mla_w_qb=v_mla_w_qb, mla_w_kvb=v_mla_w_kvb, mla_q_gain=v_mla_q_gain, mla_k_gain=v_mla_k_gain, mla_w_o=v_mla_w_o, ssd_w_in=v_ssd_w_in, ssd_conv_w=v_ssd_conv_w, ssd_conv_b=v_ssd_conv_b, ssd_dt_bias=v_ssd_dt_bias, ssd_a_log=v_ssd_a_log, ssd_d=v_ssd_d, ssd_norm_gain=v_ssd_norm_gain, ssd_w_out=v_ssd_w_out)
    names = list(w_in)
    xi, yi, ci = _coords()
    chip = 2 * xi + yi
    batch = 4 * xi + 2 * yi + ci
    x2, target = x[0], loss_target[0]

    small_sharded = ("norm_gain", "ssd_conv_w", "ssd_conv_b", "ssd_norm_gain")
    pack0 = jnp.concatenate([c.reshape(-1)] + [w_in[n].reshape(-1) for n in small_sharded])
    pack0 = _pad_rows(pack0.reshape(-1, 128), 8)
    g0 = allgather_small(pack0, "gather_small").reshape(N_DEV, -1)
    c_all = g0[:, :D_MODEL]
    P, off = {}, D_MODEL
    for n in small_sharded:
        sz = w_in[n].size
        st = g0[0::2, off:off + sz].reshape((N_CHIPS,) + w_in[n].shape)
        P[n] = _unstack(st, w_in[n].ndim - 1)
        off += sz
    for n in ("mla_q_a_gain", "mla_kv_a_gain", "mla_q_gain", "mla_k_gain", "ssd_dt_bias", "ssd_a_log", "ssd_d"):
        P[n] = w_in[n]

    sc = _silu(c_all)
    n_ada = ada_w.shape[2]
    b_sh = lax.dynamic_slice_in_dim(ada_b, chip * n_ada, n_ada, axis=1)
    mods_sh = jnp.stack([matmul(sc, ada_w[l], "nn", "ada_fwd") for l in range(DEPTH)]) + b_sh[:, None, :]
    g1 = allgather_small(mods_sh.reshape(-1, 128), "gather_mods").reshape(N_DEV, DEPTH, N_DEV, n_ada)
    mods = lax.dynamic_index_in_dim(g1[0::2], batch, axis=2, keepdims=False)
    mods = mods.transpose(1, 0, 2).reshape(DEPTH, 3, 3, D_MODEL)

    shard_groups = [jnp.concatenate([_rows2d(w_in[n]).astype(BF16) for n in grp], axis=0) for grp in GROUPS]
    gathered = allgather_chips(shard_groups, "gather_weights")
    gathered = [lax.dynamic_update_slice(g, s[None], (chip, 0, 0)) for g, s in zip(gathered, shard_groups)]
    G = {}
    for grp, arr in zip(GROUPS, gathered):
        off = 0
        for n in grp:
            rows = w_in[n].size // w_in[n].shape[-1]
            G[n] = arr[:, off:off + rows].reshape((N_CHIPS,) + w_in[n].shape)
            off += rows
    W = {
        "ffn_w_gu": gathered[0],
        "ffn_w_down": [[_unstack(G["ffn_w_down"][:, i, t], 0) for t in range(2)] for i in range(DEPTH)],
        "mla_w_a": [_unstack(G["mla_w_a"][:, j], 0) for j in range(2)],
        "mla_w_qb": [_unstack(G["mla_w_qb"][:, j], 1) for j in range(2)],
        "mla_w_kvb": [_unstack(G["mla_w_kvb"][:, j], 1) for j in range(2)],
        "mla_w_o": [_unstack(G["mla_w_o"][:, j], 0) for j in range(2)],
        "ssd_w_out": [_unstack(G["ssd_w_out"][:, j], 0) for j in range(2)],
    }
    w_in_full = [_unstack(G["ssd_w_in"][:, j], 1) for j in range(2)]
    W["ssd_w_z"] = [w[:, :D_INNER] for w in w_in_full]
    W["ssd_w_xbc"] = [w[:, D_INNER:D_INNER + CONV_DIM] for w in w_in_full]
    W["ssd_w_dt"] = [w[:, D_INNER + CONV_DIM:] for w in w_in_full]
    X = stand_ins(W)

    pos = positions[0]
    y, vjp = jax.vjp(lambda a, b, p_, x_: trunk(a, b, p_, W, x_, pos), x2, mods, P, X)
    dy, loss_cols = loss_head(y, target)
    dx, dmods, dP, dX = vjp(dy)
    loss = lax.psum(jnp.sum(loss_cols), ("x", "y", "c"))

    small_names = ("norm_gain", "ssd_conv_w", "ssd_conv_b", "ssd_norm_gain", "mla_q_a_gain", "mla_kv_a_gain",
                   "mla_q_gain", "mla_k_gain", "ssd_dt_bias", "ssd_a_log", "ssd_d")
    pack1 = jnp.concatenate([dmods.reshape(-1)] + [dP[n].reshape(-1) for n in small_names])
    pack1 = _pad_rows(jnp.pad(pack1, (0, (-pack1.size) % 128)).reshape(-1, 128), 8)
    rows1 = pack1.shape[0]
    g2 = allgather_small(pack1, "gather_small_grads")
    tot = sum_slots(g2.reshape(N_DEV, rows1, 128), "sum_small_grads").reshape(-1)
    n_mod = DEPTH * 9 * D_MODEL
    grads = {"ada_b": tot[:n_mod].reshape(DEPTH, 9 * D_MODEL)}
    off = n_mod
    for n in small_names:
        sz = dP[n].size
        full = tot[off:off + sz].reshape(dP[n].shape)
        off += sz
        if n in small_sharded:
            k = w_in[n].shape[-1]
            full = lax.dynamic_slice_in_dim(full, chip * k, k, axis=full.ndim - 1)
        grads[n] = full
    dmods_all = g2.reshape(N_DEV, -1)[:, :n_mod].reshape(N_DEV, DEPTH, 9 * D_MODEL)
    dm_sh = lax.dynamic_slice_in_dim(dmods_all, chip * n_ada, n_ada, axis=2)
    grads["ada_w"] = jnp.stack([matmul(sc, dm_sh[:, l], "tn", "ada_dw") for l in range(DEPTH)])

    w_in_g = [jnp.concatenate([dX["ssd_w_z"][j], dX["ssd_w_xbc"][j], dX["ssd_w_dt"][j]], axis=1) for j in range(2)]
    per_name = {
        "ffn_w_gu": [dX["ffn_w_gu"][i][t] for i in range(DEPTH) for t in range(2)],
        "ffn_w_down": [dX["ffn_w_down"][i][t].reshape(N_CHIPS, -1, D_MODEL) for i in range(DEPTH) for t in range(2)],
        "mla_w_a": [g.reshape(N_CHIPS, -1, g.shape[-1]) for g in dX["mla_w_a"]],
        "mla_w_qb": dX["mla_w_qb"], "mla_w_kvb": dX["mla_w_kvb"],
        "mla_w_o": [g.reshape(N_CHIPS, -1, D_MODEL) for g in dX["mla_w_o"]],
        "ssd_w_out": [g.reshape(N_CHIPS, -1, D_MODEL) for g in dX["ssd_w_out"]],
        "ssd_w_in": [_stack(g, 1) for g in w_in_g],
    }
    grad_groups = [jnp.concatenate([p for n in grp for p in per_name[n]], axis=1) for grp in GROUPS]
    theirs = pair_exchange(grad_groups, "grads_to_sibling")
    pair = [sum_own_half(a, b, ci, "sum_pair", BF16) for a, b in zip(grad_groups, theirs)]
    landed = scatter_chips(pair, "grads_to_chips")
    landed = [lax.dynamic_update_slice(a, lax.dynamic_slice_in_dim(p, chip, 1, axis=0), (chip, 0, 0)) for a, p in zip(landed, pair)]
    half_sums = [sum_slots(a, "sum_chips") for a in landed]
    totals = pair_allgather(half_sums, "grad_halves_swap")
    totals = [lax.dynamic_update_slice(t, h, (ci * h.shape[0], 0)) for t, h in zip(totals, half_sums)]
    for grp, arr in zip(GROUPS, totals):
        off = 0
        for n in grp:
            rows = w_in[n].size // w_in[n].shape[-1]
            grads[n] = arr[off:off + rows].reshape(w_in[n].shape)
            off += rows

    deltas, new_m, new_v = {}, {}, {}
    for n in names:
        w = w_in[n]
        d, nm, nv = adamw(_rows2d(w), _rows2d(m_in[n]), _rows2d(v_in[n]), _rows2d(grads[n]))
        deltas[n], new_m[n], new_v[n] = d.reshape(w.shape), nm.reshape(w.shape), nv.reshape(w.shape)

    return (loss, dx[None], *[grads[n] for n in names], *[deltas[n] for n in names],
            *[new_m[n] for n in names], *[new_v[n] for n in names])
```

```python
import numpy as np

import jax
import jax.numpy as jnp
from jax import lax
from jax.experimental import pallas as pl
from jax.experimental.pallas import tpu as pltpu

F32 = jnp.float32
BF16 = jnp.bfloat16
MESH = pl.DeviceIdType.MESH

D_MODEL = 1024
DEPTH = 4
EPS = 1e-6
D_FF = 2816
MLA_HEADS = 16
Q_LORA = 384
KV_LORA = 256
QK_NOPE = 64
QK_ROPE = 32
QK_HEAD = QK_NOPE + QK_ROPE
V_HEAD = 64
ROPE_THETA = 10000.0
D_INNER = 2048
SSD_HEAD_DIM = 64
SSD_HEADS = 32
SSD_GROUPS = 4
SSD_STATE = 128
CONV_WIDTH = 4
CHUNK = 128
CONV_DIM = D_INNER + 2 * SSD_GROUPS * SSD_STATE
ADAM_LR = 0.001
ADAM_B1 = 0.9
ADAM_B2 = 0.999
ADAM_EPS = 1e-08
ADAM_WD = 0.01
ADAM_STEP = 10

N_CHIPS = 4
N_DEV = 8
V7X_VMEM_LIMIT = 56 * 1024 * 1024
ATTN_HEADS_PER_STEP = 4
LANES = 128
LOG2E = 1.4426950408889634


def _params(sem=None):
    return pltpu.CompilerParams(dimension_semantics=sem, vmem_limit_bytes=V7X_VMEM_LIMIT)


def _div_tile(n, pref, quantum):
    if n <= pref:
        return n
    t = (pref // quantum) * quantum
    while t >= quantum:
        if n % t == 0:
            return t
        t -= quantum
    return n


def matmul(a, b, mode, name, out_dtype=F32, stack=0, b_slot=None, tm=1024, tn=1408, tk=2816):
    if b_slot is not None:
        (roff, rows), (nsl, _, n) = b_slot, b.shape
        (M, K) = a.shape
        K2, N = (rows, nsl * n) if mode == "nn" else (nsl * n, rows)
    elif mode == "nn":
        (M, K), (K2, N) = a.shape, b.shape
    elif mode == "nt":
        (M, K), (N, K2) = a.shape, b.shape
    else:
        (K, M), (K2, N) = a.shape, b.shape
    assert K == K2, (a.shape, b.shape, mode)
    tm = _div_tile(M, tm, 128 if mode == "tn" else 16)
    tn = _div_tile(N // stack if stack else (n if b_slot and mode == "nn" else N), tn, 128)
    tk = _div_tile(n if b_slot and mode == "nt" else K, tk, 128)
    nk = K // tk
    if mode == "nn":
        a_spec = pl.BlockSpec((tm, tk), lambda i, j, k: (i, k))
        b_spec = pl.BlockSpec((tk, tn), lambda i, j, k: (k, j))
        dims = (((1,), (0,)), ((), ()))
        if b_slot:
            assert roff % tk == 0
            nbn, rb = n // tn, roff // tk
            b_spec = pl.BlockSpec((None, tk, tn), lambda i, j, k: (j // nbn, rb + k, j % nbn))
    elif mode == "nt":
        a_spec = pl.BlockSpec((tm, tk), lambda i, j, k: (i, k))
        b_spec = pl.BlockSpec((tn, tk), lambda i, j, k: (j, k))
        dims = (((1,), (1,)), ((), ()))
        if b_slot:
            assert roff % tn == 0
            nbk, rb = n // tk, roff // tn
            b_spec = pl.BlockSpec((None, tn, tk), lambda i, j, k: (k // nbk, rb + j, k % nbk))
    else:
        a_spec = pl.BlockSpec((tk, tm), lambda i, j, k: (k, i))
        b_spec = pl.BlockSpec((tk, tn), lambda i, j, k: (k, j))
        dims = (((0,), (0,)), ((), ()))
    if stack:
        nb = N // stack // tn
        out_spec = pl.BlockSpec((None, tm, tn), lambda i, j, k: (j // nb, i, j % nb))
        out_shape = jax.ShapeDtypeStruct((stack, M, N // stack), out_dtype)
    else:
        out_spec = pl.BlockSpec((tm, tn), lambda i, j, k: (i, j))
        out_shape = jax.ShapeDtypeStruct((M, N), out_dtype)
    use_acc = nk > 1 and out_dtype != F32

    def body(a_ref, b_ref, o_ref, *acc):
        p = lax.dot_general(a_ref[...].astype(BF16), b_ref[...].astype(BF16), dims, preferred_element_type=F32)
        if nk == 1:
            o_ref[...] = p.astype(out_dtype)
            return
        acc_ref = acc[0] if use_acc else o_ref
        k = pl.program_id(2)

        @pl.when(k == 0)
        def _():
            acc_ref[...] = p

        @pl.when(k > 0)
        def _():
            acc_ref[...] += p

        if use_acc:
            @pl.when(k == nk - 1)
            def _():
                o_ref[...] = acc_ref[...].astype(out_dtype)

    return pl.pallas_call(
        body, name=name, grid=(M // tm, N // tn, nk), in_specs=[a_spec, b_spec], out_specs=out_spec, out_shape=out_shape,
        scratch_shapes=[pltpu.VMEM((tm, tn), F32)] if use_acc else [],
        compiler_params=_params(("parallel", "parallel", "arbitrary")),
    )(a, b)


def mm_op(name, out_dtype=F32, stack=0, b_slot=None):
    @jax.custom_vjp
    def op(a, w, wp):
        return matmul(a, w, "nn", name + "_fwd", out_dtype=out_dtype, b_slot=b_slot)

    def fwd(a, w, wp):
        return op(a, w, wp), (a, w)

    def bwd(res, g):
        a, w = res
        return (matmul(g, w, "nt", name + "_dx", out_dtype=a.dtype, b_slot=b_slot, tm=2048 if b_slot else 1024),
                jnp.zeros_like(w),
                matmul(a, g, "tn", name + "_dw", stack=stack, tm=1408, tn=1408, tk=1024))

    op.defvjp(fwd, bwd)
    return op


def _row_tile(rows, widths):
    w = max(widths)
    t = 128 if w > 4096 else (256 if w > 1024 else (512 if w > 128 else 2048))
    return _div_tile(rows, t, 16)


def _row_spec(r, tm):
    nb = r.shape[0] // tm
    return pl.BlockSpec((tm, r.shape[1]), lambda i: (i % nb, 0))


def _rowwise_fwd(f, rows, vecs, name, out_dtype):
    n_r, n_v = len(rows), len(vecs)
    S = rows[0].shape[0]
    tm = _row_tile(min(r.shape[0] for r in rows), [r.shape[1] for r in rows])
    outs = jax.eval_shape(f, *[jax.ShapeDtypeStruct((tm, r.shape[1]), F32) for r in rows], *vecs)
    out_dtypes = out_dtype if isinstance(out_dtype, tuple) else (out_dtype,) * len(outs)

    def body(*refs):
        res = f(*[r[...].astype(F32) for r in refs[: n_r + n_v]])
        for o, r in zip(refs[n_r + n_v:], res):
            o[...] = r.astype(o.dtype)

    return pl.pallas_call(
        body, name=name, grid=(S // tm,),
        in_specs=[_row_spec(r, tm) for r in rows] + [pl.BlockSpec(v.shape, lambda i: (0, 0)) for v in vecs],
        out_specs=tuple(pl.BlockSpec((tm, o.shape[1]), lambda i: (i, 0)) for o in outs),
        out_shape=tuple(jax.ShapeDtypeStruct((S, o.shape[1]), dt) for o, dt in zip(outs, out_dtypes)),
        compiler_params=_params(("parallel",)),
    )(*rows, *vecs)


def _rowwise_bwd(f, rows, vecs, douts, diff_rows, n_const, name):
    n_r, n_o = len(rows), len(douts)
    consts, vecs = vecs[len(vecs) - n_const:], vecs[:len(vecs) - n_const]
    n_v = len(vecs)
    S = rows[0].shape[0]
    tm = _row_tile(min(r.shape[0] for r in rows), [r.shape[1] for r in rows] + [d.shape[1] for d in douts])
    d_idx = [i for i in range(n_r) if diff_rows[i]]

    def body(*refs):
        row_v = [r[...].astype(F32) for r in refs[:n_r]]
        vec_v = [r[...] for r in refs[n_r:n_r + n_v]]
        const_v = [r[...] for r in refs[n_r + n_v:n_r + n_v + n_const]]
        dout_v = tuple(r[...].astype(F32) for r in refs[n_r + n_v + n_const:n_r + n_v + n_const + n_o])
        out_refs = refs[n_r + n_v + n_const + n_o:]

        def g(*args):
            full = list(row_v)
            for j, i in enumerate(d_idx):
                full[i] = args[j]
            return f(*full, *args[len(d_idx):], *const_v)

        _, vjp = jax.vjp(g, *[row_v[i] for i in d_idx], *vec_v)
        grads = vjp(dout_v)
        for j in range(len(d_idx)):
            out_refs[j][...] = grads[j].astype(out_refs[j].dtype)
        step = pl.program_id(0)
        for j in range(n_v):
            gv, o = grads[len(d_idx) + j], out_refs[len(d_idx) + j]

            @pl.when(step == 0)
            def _(gv=gv, o=o):
                o[...] = gv

            @pl.when(step > 0)
            def _(gv=gv, o=o):
                o[...] += gv

    res = pl.pallas_call(
        body, name=name, grid=(S // tm,),
        in_specs=[_row_spec(r, tm) for r in rows] + [pl.BlockSpec(v.shape, lambda i: (0, 0)) for v in vecs + consts]
        + [pl.BlockSpec((tm, d.shape[1]), lambda i: (i, 0)) for d in douts],
        out_specs=tuple([pl.BlockSpec((tm, rows[i].shape[1]), lambda i_: (i_, 0)) for i in d_idx]
                        + [pl.BlockSpec(v.shape, lambda i: (0, 0)) for v in vecs]),
        out_shape=tuple([jax.ShapeDtypeStruct(rows[i].shape, rows[i].dtype) for i in d_idx]
                        + [jax.ShapeDtypeStruct(v.shape, F32) for v in vecs]),
        compiler_params=_params(("arbitrary",)),
    )(*rows, *vecs, *consts, *douts)
    drows = [None] * n_r
    for j, i in enumerate(d_idx):
        drows[i] = res[j]
    for i in range(n_r):
        if drows[i] is None:
            drows[i] = jnp.zeros_like(rows[i])
    return tuple(drows) + tuple(res[len(d_idx):]) + tuple(jnp.zeros_like(k) for k in consts)


def rowwise_op(f, name, n_rows, diff_rows=None, out_dtype=F32, n_const=0):
    diff = tuple(diff_rows) if diff_rows is not None else (True,) * n_rows

    @jax.custom_vjp
    def op(*args):
        return _rowwise_fwd(f, args[:n_rows], args[n_rows:], name + "_fwd", out_dtype)

    def fwd(*args):
        return op(*args), args

    def bwd(args, douts):
        return _rowwise_bwd(f, args[:n_rows], args[n_rows:], douts, diff, n_const, name + "_bwd")

    op.defvjp(fwd, bwd)
    return op


def _rms(x, gain):
    return x * lax.rsqrt(jnp.mean(x * x, axis=-1, keepdims=True) + EPS) * gain


def _silu(x):
    return x * jax.nn.sigmoid(x)


def _f_modulate(x, gain, shift, scale):
    return (_rms(x, gain) * (1.0 + scale) + shift,)


def _f_rms(x, gain):
    return (_rms(x, gain),)


def _f_swiglu(gu):
    n = gu.shape[1] // 2
    return (_silu(gu[:, :n]) * gu[:, n:],)


def _f_resid(coef):
    def f(x, y, gate):
        return (x + (coef * gate) * y,)
    return f


def _f_resid_modulate(coef):
    def f(x, y, gate, gain, shift, scale):
        x1 = x + (coef * gate) * y
        return (x1, _rms(x1, gain) * (1.0 + scale) + shift)
    return f


def _f_head_rope(x, cosf, sinf, gain, swap):
    y = _rms(x, gain)
    return (y * cosf + jnp.dot(y, swap, precision=lax.Precision.HIGHEST, preferred_element_type=F32) * sinf,)


def _rope_swap():
    m = np.zeros((QK_HEAD, QK_HEAD), np.float32)
    half = QK_ROPE // 2
    for i in range(half):
        m[QK_NOPE + half + i, QK_NOPE + i] = -1.0
        m[QK_NOPE + i, QK_NOPE + half + i] = 1.0
    return jnp.asarray(m)


def _f_gated_norm(y, z, gain):
    g = y * _silu(z)
    n = g.shape[1] // SSD_GROUPS
    return (jnp.concatenate([_rms(g[:, i * n:(i + 1) * n], gain[:, i * n:(i + 1) * n]) for i in range(SSD_GROUPS)], axis=1),)


HALO = 8


def _conv_taps(ext, w, rows, off):
    acc = None
    for k in range(CONV_WIDTH):
        term = w[k:k + 1, :] * pltpu.roll(ext, CONV_WIDTH - 1 - k, 0)[off:off + rows]
        acc = term if acc is None else acc + term
    return acc


def _conv_specs(S, tm, tc):
    tile = pl.BlockSpec((tm, tc), lambda j, i: (i, j))
    prev = pl.BlockSpec((HALO, tc), lambda j, i: (jnp.maximum(i * (tm // HALO) - 1, 0), j))
    nxt = pl.BlockSpec((HALO, tc), lambda j, i: (jnp.minimum((i + 1) * (tm // HALO), S // HALO - 1), j))
    wspec = pl.BlockSpec((CONV_WIDTH, tc), lambda j, i: (0, j))
    bspec = pl.BlockSpec((1, tc), lambda j, i: (0, j))
    return tile, prev, nxt, wspec, bspec


def conv_silu_fwd(u, w, b, tm=512, tc=1024):
    S, C = u.shape
    tm = _div_tile(S, tm, HALO)

    def body(u_ref, p_ref, w_ref, b_ref, o_ref):
        prev = jnp.where(pl.program_id(1) == 0, 0.0, p_ref[...])
        ext = jnp.concatenate([prev, u_ref[...]], axis=0)
        conv = _conv_taps(ext, w_ref[...], tm, HALO) + b_ref[...]
        o_ref[...] = conv * jax.nn.sigmoid(conv)

    tile, prev, _, wspec, bspec = _conv_specs(S, tm, tc)
    return pl.pallas_call(
        body, name="conv_silu_fwd", grid=(C // tc, S // tm), in_specs=[tile, prev, wspec, bspec], out_specs=tile,
        out_shape=jax.ShapeDtypeStruct((S, C), F32), compiler_params=_params(("parallel", "arbitrary")),
    )(u, u, w, b)


def conv_silu_bwd(u, w, b, dout, tm=512, tc=1024):
    S, C = u.shape
    tm = _div_tile(S, tm, HALO)
    n = S // tm
    ext_rows = tm + HALO

    def body(u_ref, p_ref, n_ref, g_ref, gn_ref, w_ref, b_ref, du_ref, dw_ref, db_ref):
        i = pl.program_id(1)
        wv = w_ref[...]
        prev = jnp.where(i == 0, 0.0, p_ref[...])
        ext = jnp.concatenate([prev, u_ref[...], n_ref[...]], axis=0)
        conv = _conv_taps(ext, wv, ext_rows, HALO) + b_ref[...]
        g_ext = jnp.concatenate([g_ref[...], jnp.where(i == n - 1, 0.0, gn_ref[...])], axis=0)
        sg = jax.nn.sigmoid(conv)
        dconv = g_ext * (sg * (1.0 + conv * (1.0 - sg)))
        du = None
        for k in range(CONV_WIDTH):
            s = CONV_WIDTH - 1 - k
            term = wv[k:k + 1, :] * pltpu.roll(dconv, (ext_rows - s) % ext_rows, 0)[:tm]
            du = term if du is None else du + term
        du_ref[...] = du
        dc = dconv[:tm]
        dw = jnp.concatenate([jnp.sum(dc * pltpu.roll(ext, CONV_WIDTH - 1 - k, 0)[HALO:HALO + tm], axis=0, keepdims=True)
                              for k in range(CONV_WIDTH)], axis=0)
        dbv = jnp.sum(dc, axis=0, keepdims=True)

        @pl.when(i == 0)
        def _():
            dw_ref[...] = dw
            db_ref[...] = dbv

        @pl.when(i > 0)
        def _():
            dw_ref[...] += dw
            db_ref[...] += dbv

    tile, prev, nxt, wspec, bspec = _conv_specs(S, tm, tc)
    return pl.pallas_call(
        body, name="conv_silu_bwd", grid=(C // tc, n), in_specs=[tile, prev, nxt, tile, nxt, wspec, bspec],
        out_specs=(tile, wspec, bspec),
        out_shape=(jax.ShapeDtypeStruct((S, C), F32), jax.ShapeDtypeStruct((CONV_WIDTH, C), F32),
                   jax.ShapeDtypeStruct((1, C), F32)),
        compiler_params=_params(("parallel", "arbitrary")),
    )(u, u, u, dout, dout, w, b)


@jax.custom_vjp
def conv_silu(u, w, b):
    return conv_silu_fwd(u, w, b)


def _conv_silu_fwd(u, w, b):
    return conv_silu_fwd(u, w, b), (u, w, b)


def _conv_silu_bwd(res, dout):
    return conv_silu_bwd(*res, dout)


conv_silu.defvjp(_conv_silu_fwd, _conv_silu_bwd)


_NT = (((1,), (1,)), ((), ()))


def _dot(a, b):
    return jnp.dot(a.astype(BF16), b.astype(BF16), preferred_element_type=F32)


def _dot_nt(a, b):
    return lax.dot_general(a.astype(BF16), b.astype(BF16), _NT, preferred_element_type=F32)


def _attn_tile(S):
    return _div_tile(S, 512, 128)


def _causal(t, transposed=False):
    r = lax.broadcasted_iota(jnp.int32, (t, t), 0)
    c = lax.broadcasted_iota(jnp.int32, (t, t), 1)
    return r <= c if transposed else r >= c


def _tri_tables(n, by_key):
    if by_key:
        pairs = [(i, j) for j in range(n) for i in range(j, n)]
    else:
        pairs = [(i, j) for i in range(n) for j in range(i + 1)]
    return (jnp.asarray(np.array([p[0] for p in pairs], np.int32)), jnp.asarray(np.array([p[1] for p in pairs], np.int32)))


def attn_fwd(q, k, v):
    H, S, dk = q.shape
    dv = v.shape[-1]
    t, hb = _attn_tile(S), ATTN_HEADS_PER_STEP
    n = S // t
    scale = dk ** -0.5
    qi_tab, kj_tab = _tri_tables(n, by_key=False)

    def body(qi_ref, kj_ref, q_ref, k_ref, v_ref, o_ref, lse_ref, m_s, l_s, acc_s):
        qi, kj = qi_ref[pl.program_id(1)], kj_ref[pl.program_id(1)]

        @pl.when(kj == 0)
        def _():
            m_s[...] = jnp.full(m_s.shape, -jnp.inf, F32)
            l_s[...] = jnp.zeros(l_s.shape, F32)
            acc_s[...] = jnp.zeros(acc_s.shape, F32)

        def step(masked):
            ss = [_dot_nt(q_ref[j], k_ref[j]) for j in range(hb)]
            new = []
            for j in range(hb):
                s = ss[j] * (scale * LOG2E)
                if masked:
                    s = jnp.where(_causal(t), s, -jnp.inf)
                m_old = m_s[j]
                m_new = jnp.maximum(m_old, jnp.max(s, axis=-1, keepdims=True))
                alpha = jnp.exp2(m_old - m_new)
                p = jnp.exp2(s - jnp.tile(m_new, (1, t // LANES)))
                new.append((m_new, alpha * l_s[j] + jnp.sum(p, axis=-1, keepdims=True),
                            alpha[:, :dv] * acc_s[j] + _dot(p, v_ref[j])))
            for j in range(hb):
                m_s[j], l_s[j], acc_s[j] = new[j]

        @pl.when(kj < qi)
        def _():
            step(False)

        @pl.when(kj == qi)
        def _():
            step(True)
            l = l_s[...]
            o_ref[...] = acc_s[...] / l[:, :, :dv]
            lse_ref[...] = (m_s[...] + jnp.log2(l))[:, :, :1]

    qmap = lambda h, s, qi, kj: (h, qi[s], 0)
    kmap = lambda h, s, qi, kj: (h, kj[s], 0)
    return pl.pallas_call(
        body, name="attn_fwd",
        grid_spec=pltpu.PrefetchScalarGridSpec(
            num_scalar_prefetch=2, grid=(H // hb, qi_tab.shape[0]),
            in_specs=[pl.BlockSpec((hb, t, dk), qmap), pl.BlockSpec((hb, t, dk), kmap), pl.BlockSpec((hb, t, dv), kmap)],
            out_specs=(pl.BlockSpec((hb, t, dv), qmap), pl.BlockSpec((hb, t, 1), qmap)),
            scratch_shapes=[pltpu.VMEM((hb, t, LANES), F32), pltpu.VMEM((hb, t, LANES), F32), pltpu.VMEM((hb, t, dv), F32)]),
        out_shape=(jax.ShapeDtypeStruct((H, S, dv), F32), jax.ShapeDtypeStruct((H, S, 1), F32)),
        compiler_params=_params(("parallel", "arbitrary")),
    )(qi_tab, kj_tab, q, k, v)


def attn_bwd(q, k, v, o, lse, do):
    H, S, dk = q.shape
    dv = v.shape[-1]
    t, hb = _attn_tile(S), ATTN_HEADS_PER_STEP
    n = S // t
    scale = dk ** -0.5
    hb = min(hb, H)
    lse_r = lse.reshape(H, 1, S)
    qi_tab, kj_tab = _tri_tables(n, by_key=True)
    tn_dims = (((0,), (0,)), ((), ()))

    def body(qi_ref, kj_ref, q_ref, k_ref, v_ref, o_ref, do_ref, lse_ref, dq_ref, dk_ref, dv_ref):
        pair = pl.program_id(1)
        qi, kj = qi_ref[pair], kj_ref[pair]

        @pl.when(pair == 0)
        def _():
            dq_ref[...] = jnp.zeros(dq_ref.shape, F32)

        ones = jnp.ones((8, dv), F32)

        def step(masked):
            sts = [_dot_nt(k_ref[j], q_ref[j]) for j in range(hb)]
            for j in range(hb):
                dof = do_ref[j]
                dob = dof.astype(BF16)
                delta = lax.dot_general(ones, dof * o_ref[j], _NT, precision=lax.Precision.HIGHEST,
                                        preferred_element_type=F32)[0:1]
                pt = jnp.exp2(sts[j] * (scale * LOG2E) - lse_ref[j])
                if masked:
                    pt = jnp.where(_causal(t, transposed=True), pt, 0.0)
                dvj = _dot(pt, dob)
                dst = (pt * (_dot_nt(v_ref[j], dob) - delta)).astype(BF16)
                dkj = _dot(dst, q_ref[j]) * scale
                rows = pl.ds(pl.multiple_of(qi * t, t), t)
                dq_ref[j, rows, :] += lax.dot_general(dst, k_ref[j], tn_dims, preferred_element_type=F32) * scale
                if masked:
                    dv_ref[j] = dvj
                    dk_ref[j] = dkj
                else:
                    dv_ref[j] += dvj
                    dk_ref[j] += dkj

        @pl.when(qi == kj)
        def _():
            step(True)

        @pl.when(qi > kj)
        def _():
            step(False)

    qmap = lambda h, s, qi, kj: (h, qi[s], 0)
    kmap = lambda h, s, qi, kj: (h, kj[s], 0)
    rowq = lambda h, s, qi, kj: (h, 0, qi[s])
    whole = lambda h, s, qi, kj: (h, 0, 0)
    return pl.pallas_call(
        body, name="attn_bwd",
        grid_spec=pltpu.PrefetchScalarGridSpec(
            num_scalar_prefetch=2, grid=(H // hb, qi_tab.shape[0]),
            in_specs=[pl.BlockSpec((hb, t, dk), qmap), pl.BlockSpec((hb, t, dk), kmap), pl.BlockSpec((hb, t, dv), kmap),
                      pl.BlockSpec((hb, t, dv), qmap), pl.BlockSpec((hb, t, dv), qmap), pl.BlockSpec((hb, 1, t), rowq)],
            out_specs=(pl.BlockSpec((hb, S, dk), whole), pl.BlockSpec((hb, t, dk), kmap), pl.BlockSpec((hb, t, dv), kmap))),
        out_shape=(jax.ShapeDtypeStruct((H, S, dk), F32), jax.ShapeDtypeStruct((H, S, dk), F32),
                   jax.ShapeDtypeStruct((H, S, dv), F32)),
        compiler_params=_params(("parallel", "arbitrary")),
    )(qi_tab, kj_tab, q, k, v, o, do, lse_r)


@jax.custom_vjp
def attention(q, k, v):
    return attn_fwd(q.astype(BF16), k.astype(BF16), v.astype(BF16))[0]


def _attention_fwd(q, k, v):
    qb, kb, vb = q.astype(BF16), k.astype(BF16), v.astype(BF16)
    o, lse = attn_fwd(qb, kb, vb)
    return o, (qb, kb, vb, o, lse)


def _attention_bwd(res, do):
    return attn_bwd(*res, do)


attention.defvjp(_attention_fwd, _attention_bwd)


def _ssd_specs(hb, L, P, N, order):
    xs = pl.BlockSpec((hb, L, P), lambda g, c: (g, order(c), 0))
    col = pl.BlockSpec((None, L, hb), lambda g, c: (g, order(c), 0))
    row = pl.BlockSpec((hb, 1, L), lambda g, c: (g, 0, order(c)))
    bc = pl.BlockSpec((None, L, N), lambda g, c: (g, order(c), 0))
    st = pl.BlockSpec((hb, None, N, P), lambda g, c: (g, order(c), 0, 0))
    return xs, col, row, bc, st


def _ssd_head_scalar(hb):
    return pl.BlockSpec((hb, 1, 1), lambda g, c: (g, 0, 0))


def _ssd_rows(cols, H):
    G, S, hb = cols.shape
    return cols.transpose(0, 2, 1).reshape(H, 1, S)


def ssd_fwd(x, dt, ac, Bm, Cm, skip):
    H, S, P = x.shape
    G, _, N = Bm.shape
    hb, L = H // G, CHUNK
    nc = S // L
    acr = _ssd_rows(ac, H)

    def body(x_ref, dt_ref, ac_ref, acr_ref, b_ref, c_ref, sk_ref, y_ref, hp_ref, h_s):
        @pl.when(pl.program_id(1) == 0)
        def _():
            h_s[...] = jnp.zeros((hb, N, P), F32)

        Bv, Cv = b_ref[...], c_ref[...]
        cb = _dot_nt(Cv, Bv)
        bt = Bv.T
        mask = _causal(L)
        ac_all, dt_all = ac_ref[...], dt_ref[...]
        for j in range(hb):
            a = jnp.broadcast_to(ac_all[:, j:j + 1], (L, L))
            dtv = jnp.broadcast_to(dt_all[:, j:j + 1], (L, L))
            lm = jnp.exp(jnp.where(mask, a - acr_ref[j], -jnp.inf))
            xdt = x_ref[j] * dtv[:, :P]
            h = h_s[j]
            hp_ref[j] = h
            y_ref[j] = _dot(cb * lm, xdt) + jnp.exp(a)[:, :P] * _dot(Cv, h) + sk_ref[j] * x_ref[j]
            al = a[L - 1:L, :]
            h_s[j] = jnp.exp(al)[:, :P] * h + _dot(bt, xdt * jnp.exp(al - a)[:, :P])

    xs, col, row, bc, st = _ssd_specs(hb, L, P, N, lambda c: c)
    return pl.pallas_call(
        body, name="ssd_fwd", grid=(G, nc), in_specs=[xs, col, col, row, bc, bc, _ssd_head_scalar(hb)], out_specs=(xs, st),
        out_shape=(jax.ShapeDtypeStruct((H, S, P), F32), jax.ShapeDtypeStruct((H, nc, N, P), F32)),
        scratch_shapes=[pltpu.VMEM((hb, N, P), F32)],
        compiler_params=_params(("parallel", "arbitrary")),
    )(x, dt, ac, acr, Bm, Cm, skip)


def ssd_bwd(x, dt, ac, Bm, Cm, skip, hp, dy):
    H, S, P = x.shape
    G, _, N = Bm.shape
    hb, L = H // G, CHUNK
    nc = S // L
    acr = _ssd_rows(ac, H)

    def body(x_ref, dt_ref, ac_ref, acr_ref, b_ref, c_ref, sk_ref, hp_ref, dy_ref,
             dx_ref, ddt_ref, dac_ref, dacr_ref, db_ref, dc_ref, dsk_ref, dh_s):
        @pl.when(pl.program_id(1) == 0)
        def _():
            dh_s[...] = jnp.zeros((hb, N, P), F32)
            dsk_ref[...] = jnp.zeros((hb, 1, 1), F32)

        Bv, Cv = b_ref[...], c_ref[...]
        cb = _dot_nt(Cv, Bv)
        cbt = _dot_nt(Bv, Cv)
        ct = Cv.T
        mask, maskt = _causal(L), _causal(L, transposed=True)
        last = lax.broadcasted_iota(jnp.int32, (L, 1), 0) == L - 1
        lane = lax.broadcasted_iota(jnp.int32, (L, hb), 1)
        db = jnp.zeros((L, N), F32)
        dc = jnp.zeros((L, N), F32)
        dac_all = jnp.zeros((L, hb), F32)
        ddt_all = jnp.zeros((L, hb), F32)
        ac_all, dt_all = ac_ref[...], dt_ref[...]
        for j in range(hb):
            ar, xv, g, h, dh = acr_ref[j], x_ref[j], dy_ref[j], hp_ref[j], dh_s[j]
            a = jnp.broadcast_to(ac_all[:, j:j + 1], (L, L))
            dtv = jnp.broadcast_to(dt_all[:, j:j + 1], (L, L))
            lm = jnp.exp(jnp.where(mask, a - ar, -jnp.inf))
            lmt = jnp.exp(jnp.where(maskt, ar - a, -jnp.inf))
            xdt = xv * dtv[:, :P]
            e = jnp.exp(a)
            al = a[L - 1:L, :]
            dte = jnp.exp(al - a)
            el = jnp.exp(al)
            dcb = _dot_nt(g, xdt) * lm
            dcbt = _dot_nt(xdt, g) * lmt
            dseg = dcb * cb
            ch = _dot(Cv, h)
            bdh = _dot(Bv, dh)
            dxdt = _dot(cbt * lmt, g) + dte[:, :P] * bdh
            dc += _dot(dcb, Bv) + e * _dot_nt(g, h)
            db += _dot(dcbt, Cv) + _dot_nt(xdt * dte[:, :P], dh)
            d_e = jnp.sum(g * ch, axis=-1, keepdims=True)
            d_dte = jnp.sum(xdt * bdh, axis=-1, keepdims=True)
            d_el = jnp.sum(h * dh, keepdims=True)
            d_al = jnp.sum(d_dte * dte[:, :1], keepdims=True) + d_el * el[:, :1]
            dac_j = (jnp.sum(dseg, axis=-1, keepdims=True) + d_e * e[:, :1] - d_dte * dte[:, :1]
                     + jnp.where(last, d_al, 0.0))
            dac_all = jnp.where(lane == j, dac_j, dac_all)
            dacr_ref[j] = -jnp.sum(dseg, axis=0, keepdims=True)
            dx_ref[j] = dxdt * dtv[:, :P] + sk_ref[j] * g
            dsk_ref[j] += jnp.sum(g * xv, keepdims=True)
            ddt_all = jnp.where(lane == j, jnp.sum(dxdt * xv, axis=-1, keepdims=True), ddt_all)
            dh_s[j] = el[:, :P] * dh + _dot(ct, e[:, :P] * g)
        dac_ref[...] = dac_all
        ddt_ref[...] = ddt_all
        db_ref[...] = db
        dc_ref[...] = dc

    xs, col, row, bc, st = _ssd_specs(hb, L, P, N, lambda c: nc - 1 - c)
    one = _ssd_head_scalar(hb)
    dx, ddt, dac, dacr, db, dc, dsk = pl.pallas_call(
        body, name="ssd_bwd", grid=(G, nc), in_specs=[xs, col, col, row, bc, bc, one, st, xs],
        out_specs=(xs, col, col, row, bc, bc, one),
        out_shape=(jax.ShapeDtypeStruct((H, S, P), F32), jax.ShapeDtypeStruct((G, S, hb), F32),
                   jax.ShapeDtypeStruct((G, S, hb), F32), jax.ShapeDtypeStruct((H, 1, S), F32),
                   jax.ShapeDtypeStruct((G, S, N), F32), jax.ShapeDtypeStruct((G, S, N), F32),
                   jax.ShapeDtypeStruct((H, 1, 1), F32)),
        scratch_shapes=[pltpu.VMEM((hb, N, P), F32)],
        compiler_params=_params(("parallel", "arbitrary")),
    )(x, dt, ac, acr, Bm, Cm, skip, hp, dy)
    return dx, ddt, dac + dacr.reshape(G, hb, S).transpose(0, 2, 1), db, dc, dsk


@jax.custom_vjp
def ssd_scan(x, dt, ac, Bm, Cm, skip):
    return ssd_fwd(x, dt, ac, Bm, Cm, skip)[0]


def _ssd_scan_fwd(x, dt, ac, Bm, Cm, skip):
    y, hp = ssd_fwd(x, dt, ac, Bm, Cm, skip)
    return y, (x, dt, ac, Bm, Cm, skip, hp)


def _ssd_scan_bwd(res, dy):
    return ssd_bwd(*res, dy)


ssd_scan.defvjp(_ssd_scan_fwd, _ssd_scan_bwd)


def _vec(v):
    return v.reshape(1, -1)


def _ffn(h, w_gu, gu_row, w_down, x_gu, x_down):
    gu = mm_op("ffn_gu", out_dtype=BF16, stack=N_CHIPS, b_slot=(gu_row, D_MODEL))(h, w_gu, x_gu)
    a, = rowwise_op(_f_swiglu, "swiglu", 1, out_dtype=BF16)(gu)
    return mm_op("ffn_down")(a, w_down, x_down)


def _rope_tables(positions):
    inv = 1.0 / (ROPE_THETA ** (jnp.arange(0, QK_ROPE, 2, dtype=F32) / QK_ROPE))
    ang = positions.astype(F32)[:, None] * inv
    S = positions.shape[0]
    cosf = jnp.concatenate([jnp.ones((S, QK_NOPE), F32), jnp.cos(ang), jnp.cos(ang)], axis=1)
    sinf = jnp.concatenate([jnp.zeros((S, QK_NOPE), F32), jnp.sin(ang), jnp.sin(ang)], axis=1)
    return cosf, sinf


def _heads_first(t):
    return t.transpose(1, 0, 2).reshape(-1, t.shape[-1])


def _mla(h, cos, sin, P, W, X, j):
    S = h.shape[0]
    lat = mm_op("mla_a")(h, W["mla_w_a"][j], X["mla_w_a"][j])
    q_lat, kv_lat, k_rope = jnp.split(lat, [Q_LORA, Q_LORA + KV_LORA], axis=1)
    qn, = rowwise_op(_f_rms, "rms_lat", 1, out_dtype=BF16)(q_lat, _vec(P["mla_q_a_gain"][j]))
    kvn, = rowwise_op(_f_rms, "rms_lat", 1, out_dtype=BF16)(kv_lat, _vec(P["mla_kv_a_gain"][j]))
    q = mm_op("mla_qb", stack=N_CHIPS)(qn, W["mla_w_qb"][j], X["mla_w_qb"][j]).reshape(S, MLA_HEADS, QK_HEAD)
    kv = mm_op("mla_kvb", stack=N_CHIPS)(kvn, W["mla_w_kvb"][j], X["mla_w_kvb"][j]).reshape(S, MLA_HEADS, QK_NOPE + V_HEAD)
    k_nope, v = jnp.split(kv, [QK_NOPE], axis=-1)
    k = jnp.concatenate([k_nope, jnp.broadcast_to(k_rope[:, None, :], (S, MLA_HEADS, QK_ROPE))], axis=-1)
    head_rope = rowwise_op(_f_head_rope, "head_rope", 3, diff_rows=(True, False, False), n_const=1)
    swap = _rope_swap()
    q, = head_rope(_heads_first(q), cos, sin, _vec(P["mla_q_gain"][j]), swap)
    k, = head_rope(_heads_first(k), cos, sin, _vec(P["mla_k_gain"][j]), swap)
    o = attention(q.reshape(MLA_HEADS, S, QK_HEAD), k.reshape(MLA_HEADS, S, QK_HEAD), v.transpose(1, 0, 2))
    o = o.transpose(1, 0, 2).reshape(S, MLA_HEADS * V_HEAD).astype(BF16)
    return mm_op("mla_o")(o, W["mla_w_o"][j], X["mla_w_o"][j])


def _ssd(h, P, W, X, j):
    S = h.shape[0]
    z = mm_op("ssd_in_z")(h, W["ssd_w_z"][j], X["ssd_w_z"][j])
    xbc = mm_op("ssd_in_xbc")(h, W["ssd_w_xbc"][j], X["ssd_w_xbc"][j])
    dtr = mm_op("ssd_in_dt")(h, W["ssd_w_dt"][j], X["ssd_w_dt"][j])
    xbc = conv_silu(xbc, P["ssd_conv_w"][j], _vec(P["ssd_conv_b"][j]))
    xs, Bm, Cm = jnp.split(xbc, [D_INNER, D_INNER + SSD_GROUPS * SSD_STATE], axis=1)
    xs = xs.reshape(S, SSD_HEADS, SSD_HEAD_DIM).transpose(1, 0, 2)
    Bm = Bm.reshape(S, SSD_GROUPS, SSD_STATE).transpose(1, 0, 2)
    Cm = Cm.reshape(S, SSD_GROUPS, SSD_STATE).transpose(1, 0, 2)
    dt = jax.nn.softplus(dtr + P["ssd_dt_bias"][j][None, :])
    A = -jnp.exp(P["ssd_a_log"][j])
    a = (dt * A[None, :]).reshape(S // CHUNK, CHUNK, SSD_HEADS)
    ac = jnp.cumsum(a, axis=1).reshape(S, SSD_HEADS)
    by_group = lambda t: t.reshape(S, SSD_GROUPS, SSD_HEADS // SSD_GROUPS).transpose(1, 0, 2)
    y = ssd_scan(xs, by_group(dt), by_group(ac), Bm, Cm, P["ssd_d"][j].reshape(SSD_HEADS, 1, 1))
    y = y.transpose(1, 0, 2).reshape(S, D_INNER)
    g, = rowwise_op(_f_gated_norm, "gated_norm", 2, out_dtype=BF16)(y, z, _vec(P["ssd_norm_gain"][j]))
    return mm_op("ssd_out")(g, W["ssd_w_out"][j], X["ssd_w_out"][j])


def trunk(x, mods, P, W, X, positions):
    cos, sin = _rope_tables(positions)

    def sub(i, s, h):
        if s == 1:
            return _mla(h, cos, sin, P, W, X, i // 2) if i % 2 == 0 else _ssd(h, P, W, X, i // 2)
        t = s // 2
        return _ffn(h, W["ffn_w_gu"], (2 * i + t) * D_MODEL, W["ffn_w_down"][i][t], X["ffn_w_gu"][i][t], X["ffn_w_down"][i][t])

    order = [(i, s) for i in range(DEPTH) for s in range(3)]
    i0, s0 = order[0]
    h, = rowwise_op(_f_modulate, "modulate", 1, out_dtype=BF16)(
        x, _vec(P["norm_gain"][i0, s0]), _vec(mods[i0, s0, 0]), _vec(mods[i0, s0, 1]))
    for n, (i, s) in enumerate(order):
        y = sub(i, s, h)
        coef = 1.0 if s == 1 else 0.5
        gate = _vec(mods[i, s, 2])
        if n + 1 < len(order):
            i1, s1 = order[n + 1]
            x, h = rowwise_op(_f_resid_modulate(coef), "resid_modulate", 2, out_dtype=(F32, BF16))(
                x, y, gate, _vec(P["norm_gain"][i1, s1]), _vec(mods[i1, s1, 0]), _vec(mods[i1, s1, 1]))
        else:
            x, = rowwise_op(_f_resid(coef), "resid", 2)(x, y, gate)
    return x


def stand_ins(W):
    def one(name, w):
        if name in ("mla_w_qb", "mla_w_kvb"):
            return jnp.zeros((N_CHIPS, w.shape[0], w.shape[1] // N_CHIPS), F32)
        return jnp.zeros(w.shape, F32)
    X = {n: jax.tree.map(lambda w, n=n: one(n, w), W[n]) for n in W if n != "ffn_w_gu"}
    g = W["ffn_w_gu"]
    X["ffn_w_gu"] = [[jnp.zeros((g.shape[0], D_MODEL, g.shape[2]), F32) for _ in range(2)] for _ in range(DEPTH)]
    return X


def loss_head(y, target):
    S, D = y.shape
    tm = _row_tile(S, [D])

    def body(y_ref, t_ref, dy_ref, l_ref):
        d = y_ref[...] - t_ref[...]
        dy_ref[...] = d * (1.0 / D)
        part = jnp.sum(d * d, axis=0, keepdims=True) * (0.5 / D)

        @pl.when(pl.program_id(0) == 0)
        def _():
            l_ref[...] = part

        @pl.when(pl.program_id(0) > 0)
        def _():
            l_ref[...] += part

    return pl.pallas_call(
        body, name="loss_head", grid=(S // tm,),
        in_specs=[pl.BlockSpec((tm, D), lambda i: (i, 0))] * 2,
        out_specs=(pl.BlockSpec((tm, D), lambda i: (i, 0)), pl.BlockSpec((1, D), lambda i: (0, 0))),
        out_shape=(jax.ShapeDtypeStruct((S, D), F32), jax.ShapeDtypeStruct((1, D), F32)),
        compiler_params=_params(("arbitrary",)),
    )(y, target)


def _stream_rows(R, C):
    return _div_tile(R, max(16, (1 << 19) // C // 16 * 16), 16)


def adamw(w, m, v, g):
    R, C = w.shape
    tr = _stream_rows(R, C)
    c1 = 1.0 / (1.0 - ADAM_B1 ** ADAM_STEP)
    c2 = 1.0 / (1.0 - ADAM_B2 ** ADAM_STEP)

    def body(w_ref, m_ref, v_ref, g_ref, d_ref, nm_ref, nv_ref):
        gv = g_ref[...]
        nm = ADAM_B1 * m_ref[...] + (1.0 - ADAM_B1) * gv
        nv = ADAM_B2 * v_ref[...] + (1.0 - ADAM_B2) * (gv * gv)
        d_ref[...] = -ADAM_LR * ((nm * c1) / (jnp.sqrt(nv * c2) + ADAM_EPS) + ADAM_WD * w_ref[...])
        nm_ref[...] = nm
        nv_ref[...] = nv

    spec = pl.BlockSpec((tr, C), lambda i: (i, 0))
    return pl.pallas_call(
        body, name="adamw", grid=(R // tr,), in_specs=[spec] * 4, out_specs=(spec,) * 3,
        out_shape=(jax.ShapeDtypeStruct((R, C), F32),) * 3, compiler_params=_params(("parallel",)),
    )(w, m, v, g)


def sum_parts(parts, name, out_dtype=F32):
    R, C = parts[0].shape
    tr = _stream_rows(R, C)
    n = len(parts)

    def body(*refs):
        acc = refs[0][...].astype(F32)
        for r in refs[1:n]:
            acc = acc + r[...].astype(F32)
        refs[n][...] = acc.astype(out_dtype)

    spec = pl.BlockSpec((tr, C), lambda i: (i, 0))
    return pl.pallas_call(
        body, name=name, grid=(R // tr,), in_specs=[spec] * n, out_specs=spec,
        out_shape=jax.ShapeDtypeStruct((R, C), out_dtype), compiler_params=_params(("parallel",)),
    )(*parts)


def sum_own_half(full, theirs, c, name, out_dtype):
    n, R, C = full.shape
    h = R // 2
    tr = _stream_rows(h, C)
    nb = h // tr

    def body(c_ref, a_ref, b_ref, o_ref):
        o_ref[...] = (a_ref[...].astype(F32) + b_ref[...].astype(F32)).astype(out_dtype)

    return pl.pallas_call(
        body, name=name,
        grid_spec=pltpu.PrefetchScalarGridSpec(
            num_scalar_prefetch=1, grid=(n, nb),
            in_specs=[pl.BlockSpec((None, tr, C), lambda p, i, cr: (p, cr[0] * nb + i, 0)),
                      pl.BlockSpec((None, tr, C), lambda p, i, cr: (p, i, 0))],
            out_specs=pl.BlockSpec((None, tr, C), lambda p, i, cr: (p, i, 0))),
        out_shape=jax.ShapeDtypeStruct((n, h, C), out_dtype), compiler_params=_params(("parallel", "parallel")),
    )(jnp.reshape(c, (1,)).astype(jnp.int32), full, theirs)


def sum_slots(a, name, out_dtype=F32):
    n, R, C = a.shape
    tr = _stream_rows(R, C)

    def body(*refs):
        acc = refs[0][...].astype(F32)
        for r in refs[1:n]:
            acc = acc + r[...].astype(F32)
        refs[n][...] = acc.astype(out_dtype)

    return pl.pallas_call(
        body, name=name, grid=(R // tr,),
        in_specs=[pl.BlockSpec((None, tr, C), lambda i, p=p: (p, i, 0)) for p in range(n)],
        out_specs=pl.BlockSpec((tr, C), lambda i: (i, 0)),
        out_shape=jax.ShapeDtypeStruct((R, C), out_dtype), compiler_params=_params(("parallel",)),
    )(*([a] * n))


def _coords():
    return lax.axis_index("x"), lax.axis_index("y"), lax.axis_index("c")


def _other_chips(x, y):
    return [(1 - x, y), (x, 1 - y), (1 - x, 1 - y)]


def _hbm_call(body, name, ins, out_shapes, n_sems):
    return pl.pallas_call(
        body, name=name, out_shape=tuple(out_shapes),
        in_specs=[pl.BlockSpec(memory_space=pl.ANY)] * len(ins),
        out_specs=tuple(pl.BlockSpec(memory_space=pl.ANY) for _ in out_shapes),
        scratch_shapes=[pltpu.SemaphoreType.DMA((n_sems,)), pltpu.SemaphoreType.DMA((n_sems,))],
    )(*ins)


def allgather_small(v, name):
    m_per, n = v.shape

    def body(x_ref, out_ref, send_sems, recv_sems, local_sem):
        x, y, c = _coords()
        me, sibling = (x, y, c), (x, y, 1 - c)
        chips = _other_chips(x, y)

        def rows(px, py, pc):
            return out_ref.at[pl.ds((4 * px + 2 * py + pc) * m_per, m_per), :]

        def copy(k, block, to, src=None):
            return pltpu.make_async_remote_copy(
                src_ref=rows(*block) if src is None else src, dst_ref=rows(*block),
                send_sem=send_sems.at[k], recv_sem=recv_sems.at[k], device_id=to, device_id_type=MESH)

        mine = pltpu.make_async_copy(x_ref, rows(*me), local_sem)
        mine.start()
        first = [copy(0, me, sibling, src=x_ref)]
        first += [copy(1 + j, me, (*chip, c), src=x_ref) for j, chip in enumerate(chips)]
        for cp in first:
            cp.start()
        passed = [copy(4 + j, (*chip, c), sibling) for j, chip in enumerate(chips)]
        for j, chip in enumerate(chips):
            copy(1 + j, (*chip, c), me).wait_recv()
            passed[j].start()
        copy(0, sibling, me).wait_recv()
        for j, chip in enumerate(chips):
            copy(4 + j, (*chip, 1 - c), me).wait_recv()
        for cp in first + passed:
            cp.wait_send()
        mine.wait()

    return pl.pallas_call(
        body, name=name, out_shape=jax.ShapeDtypeStruct((N_DEV * m_per, n), v.dtype),
        in_specs=[pl.BlockSpec(memory_space=pltpu.VMEM)], out_specs=pl.BlockSpec(memory_space=pltpu.VMEM),
        scratch_shapes=[pltpu.SemaphoreType.DMA((7,)), pltpu.SemaphoreType.DMA((7,)), pltpu.SemaphoreType.DMA],
        compiler_params=pltpu.CompilerParams(vmem_limit_bytes=V7X_VMEM_LIMIT),
    )(v)


def allgather_chips(arrs, name):
    n = len(arrs)
    halves = [a.shape[0] // 2 for a in arrs]

    def body(*refs):
        xs, outs = refs[:n], refs[n:2 * n]
        send_sems, recv_sems = refs[2 * n:]
        x, y, c = _coords()
        me, sibling, chips = 2 * x + y, (x, y, 1 - c), _other_chips(x, y)

        def half(ref, cc, i):
            return ref.at[pl.ds(cc * halves[i], halves[i]), :]

        def copy(i, k, src, dst, to):
            return pltpu.make_async_remote_copy(src_ref=src, dst_ref=dst, send_sem=send_sems.at[6 * i + k],
                                                recv_sem=recv_sems.at[6 * i + k], device_id=to, device_id_type=MESH)

        sends = [copy(i, k, half(xs[i], c, i), half(outs[i].at[me], c, i), (*chip, c))
                 for k, chip in enumerate(chips) for i in range(n)]
        for cp in sends:
            cp.start()
        passed = []
        for k, (px, py) in enumerate(chips):
            for i in range(n):
                landed = half(outs[i].at[2 * px + py], c, i)
                copy(i, k, landed, landed, (px, py, c)).wait_recv()
                passed.append(copy(i, 3 + k, landed, landed, sibling))
                passed[-1].start()
        for k, (px, py) in enumerate(chips):
            for i in range(n):
                theirs = half(outs[i].at[2 * px + py], 1 - c, i)
                copy(i, 3 + k, theirs, theirs, sibling).wait_recv()
        for cp in sends + passed:
            cp.wait_send()

    return _hbm_call(body, name, arrs, [jax.ShapeDtypeStruct((N_CHIPS,) + a.shape, a.dtype) for a in arrs], 6 * n)


def pair_exchange(arrs, name):
    n = len(arrs)
    halves = [a.shape[1] // 2 for a in arrs]

    def body(*refs):
        xs, theirs = refs[:n], refs[n:2 * n]
        send_sems, recv_sems = refs[2 * n:]
        x, y, c = _coords()
        cps = [pltpu.make_async_remote_copy(
            src_ref=xs[i].at[:, pl.ds((1 - c) * halves[i], halves[i]), :], dst_ref=theirs[i],
            send_sem=send_sems.at[i], recv_sem=recv_sems.at[i], device_id=(x, y, 1 - c), device_id_type=MESH)
            for i in range(n)]
        for cp in cps:
            cp.start()
        for cp in cps:
            cp.wait()

    shapes = [jax.ShapeDtypeStruct((a.shape[0], a.shape[1] // 2, a.shape[2]), a.dtype) for a in arrs]
    return _hbm_call(body, name, arrs, shapes, n)


def scatter_chips(arrs, name):
    n = len(arrs)

    def body(*refs):
        xs, outs = refs[:n], refs[n:2 * n]
        send_sems, recv_sems = refs[2 * n:]
        x, y, c = _coords()
        me, chips = 2 * x + y, _other_chips(x, y)

        def copy(i, k, src_slot, dst_slot, to):
            return pltpu.make_async_remote_copy(
                src_ref=xs[i].at[src_slot], dst_ref=outs[i].at[dst_slot], send_sem=send_sems.at[3 * i + k],
                recv_sem=recv_sems.at[3 * i + k], device_id=to, device_id_type=MESH)

        sends = [copy(i, k, 2 * px + py, me, (px, py, c)) for k, (px, py) in enumerate(chips) for i in range(n)]
        for cp in sends:
            cp.start()
        for k, (px, py) in enumerate(chips):
            for i in range(n):
                copy(i, k, me, 2 * px + py, (px, py, c)).wait_recv()
        for cp in sends:
            cp.wait_send()

    return _hbm_call(body, name, arrs, [jax.ShapeDtypeStruct(a.shape, a.dtype) for a in arrs], 3 * n)


def pair_allgather(arrs, name):
    n = len(arrs)

    def body(*refs):
        xs, outs = refs[:n], refs[n:2 * n]
        send_sems, recv_sems = refs[2 * n:]
        x, y, c = _coords()
        cps = [pltpu.make_async_remote_copy(
            src_ref=xs[i], dst_ref=outs[i].at[pl.ds(c * xs[i].shape[0], xs[i].shape[0]), :], send_sem=send_sems.at[i],
            recv_sem=recv_sems.at[i], device_id=(x, y, 1 - c), device_id_type=MESH) for i in range(n)]
        for cp in cps:
            cp.start()
        for cp in cps:
            cp.wait()

    return _hbm_call(body, name, arrs, [jax.ShapeDtypeStruct((2 * a.shape[0], a.shape[1]), a.dtype) for a in arrs], n)


GROUPS = (("ffn_w_gu",), ("ffn_w_down", "mla_w_o", "ssd_w_out"), ("mla_w_a",), ("mla_w_qb",), ("mla_w_kvb",), ("ssd_w_in",))


def _pad_rows(a, mult):
    r = (-a.shape[0]) % mult
    return a if r == 0 else jnp.concatenate([a, jnp.zeros((r, a.shape[1]), a.dtype)], axis=0)


def _rows2d(a):
    return a.reshape(-1, a.shape[-1])


def _unstack(st, axis):
    full = jnp.moveaxis(st, 0, axis)
    sh = list(full.shape)
    sh[axis:axis + 2] = [sh[axis] * sh[axis + 1]]
    return full.reshape(sh)


def _stack(full, axis):
    sh = list(full.shape)
    sh[axis:axis + 1] = [N_CHIPS, sh[axis] // N_CHIPS]
    return jnp.moveaxis(full.reshape(sh), axis, 0)


def kernel(x, c, positions, norm_gain, ada_w, ada_b, ffn_w_gu, ffn_w_down, mla_w_a, mla_q_a_gain, mla_kv_a_gain, mla_w_qb, mla_w_kvb, mla_q_gain, mla_k_gain, mla_w_o, ssd_w_in, ssd_conv_w, ssd_conv_b, ssd_dt_bias, ssd_a_log, ssd_d, ssd_norm_gain, ssd_w_out, loss_target, m_norm_gain, m_ada_w, m_ada_b, m_ffn_w_gu, m_ffn_w_down, m_mla_w_a, m_mla_q_a_gain, m_mla_kv_a_gain, m_mla_w_qb, m_mla_w_kvb, m_mla_q_gain, m_mla_k_gain, m_mla_w_o, m_ssd_w_in, m_ssd_conv_w, m_ssd_conv_b, m_ssd_dt_bias, m_ssd_a_log, m_ssd_d, m_ssd_norm_gain, m_ssd_w_out, v_norm_gain, v_ada_w, v_ada_b, v_ffn_w_gu, v_ffn_w_down, v_mla_w_a, v_mla_q_a_gain, v_mla_kv_a_gain, v_mla_w_qb, v_mla_w_kvb, v_mla_q_gain, v_mla_k_gain, v_mla_w_o, v_ssd_w_in, v_ssd_conv_w, v_ssd_conv_b, v_ssd_dt_bias, v_ssd_a_log, v_ssd_d, v_ssd_norm_gain, v_ssd_w_out):
    w_in = dict(norm_gain=norm_gain, ada_w=ada_w, ada_b=ada_b, ffn_w_gu=ffn_w_gu, ffn_w_down=ffn_w_down, mla_w_a=mla_w_a, mla_q_a_gain=mla_q_a_gain, mla_kv_a_gain=mla_kv_a_gain, mla_w_qb=mla_w_qb, mla_w_kvb=mla_w_kvb, mla_q_gain=mla_q_gain, mla_k_gain=mla_k_gain, mla_w_o=mla_w_o, ssd_w_in=ssd_w_in, ssd_conv_w=ssd_conv_w, ssd_conv_b=ssd_conv_b, ssd_dt_bias=ssd_dt_bias, ssd_a_log=ssd_a_log, ssd_d=ssd_d, ssd_norm_gain=ssd_norm_gain, ssd_w_out=ssd_w_out)
    m_in = dict(norm_gain=m_norm_gain, ada_w=m_ada_w, ada_b=m_ada_b, ffn_w_gu=m_ffn_w_gu, ffn_w_down=m_ffn_w_down, mla_w_a=m_mla_w_a, mla_q_a_gain=m_mla_q_a_gain, mla_kv_a_gain=m_mla_kv_a_gain, mla_w_qb=m_mla_w_qb, mla_w_kvb=m_mla_w_kvb, mla_q_gain=m_mla_q_gain, mla_k_gain=m_mla_k_gain, mla_w_o=m_mla_w_o, ssd_w_in=m_ssd_w_in, ssd_conv_w=m_ssd_conv_w, ssd_conv_b=m_ssd_conv_b, ssd_dt_bias=m_ssd_dt_bias, ssd_a_log=m_ssd_a_log, ssd_d=m_ssd_d, ssd_norm_gain=m_ssd_norm_gain, ssd_w_out=m_ssd_w_out)
    v_in = dict(norm_gain=v_norm_gain, ada_w=v_ada_w, ada_b=v_ada_b, ffn_w_gu=v_ffn_w_gu, ffn_w_down=v_ffn_w_down, mla_w_a=v_mla_w_a, mla_q_a_gain=v_mla_q_a_gain, mla_kv_a_gain=v_mla_kv_a_gain, mla_w_qb=v_mla_w_qb, mla_w_kvb=v_mla_w_kvb, mla_q_gain=v_mla_q_gain, mla_k_gain=v_mla_k_gain, mla_w_o=v_mla_w_o, ssd_w_in=v_ssd_w_in, ssd_conv_w=v_ssd_conv_w, ssd_conv_b=v_ssd_conv_b, ssd_dt_bias=v_ssd_dt_bias, ssd_a_log=v_ssd_a_log, ssd_d=v_ssd_d, ssd_norm_gain=v_ssd_norm_gain, ssd_w_out=v_ssd_w_out)
    names = list(w_in)
    xi, yi, ci = _coords()
    chip = 2 * xi + yi
    batch = 4 * xi + 2 * yi + ci
    x2, target = x[0], loss_target[0]

    small_sharded = ("norm_gain", "ssd_conv_w", "ssd_conv_b", "ssd_norm_gain")
    pack0 = jnp.concatenate([c.reshape(-1)] + [w_in[n].reshape(-1) for n in small_sharded])
    pack0 = _pad_rows(pack0.reshape(-1, 128), 8)
    g0 = allgather_small(pack0, "gather_small").reshape(N_DEV, -1)
    c_all = g0[:, :D_MODEL]
    P, off = {}, D_MODEL
    for n in small_sharded:
        sz = w_in[n].size
        st = g0[0::2, off:off + sz].reshape((N_CHIPS,) + w_in[n].shape)
        P[n] = _unstack(st, w_in[n].ndim - 1)
        off += sz
    for n in ("mla_q_a_gain", "mla_kv_a_gain", "mla_q_gain", "mla_k_gain", "ssd_dt_bias", "ssd_a_log", "ssd_d"):
        P[n] = w_in[n]

    sc = _silu(c_all)
    n_ada = ada_w.shape[2]
    b_sh = lax.dynamic_slice_in_dim(ada_b, chip * n_ada, n_ada, axis=1)
    mods_sh = jnp.stack([matmul(sc, ada_w[l], "nn", "ada_fwd") for l in range(DEPTH)]) + b_sh[:, None, :]
    g1 = allgather_small(mods_sh.reshape(-1, 128), "gather_mods").reshape(N_DEV, DEPTH, N_DEV, n_ada)
    mods = lax.dynamic_index_in_dim(g1[0::2], batch, axis=2, keepdims=False)
    mods = mods.transpose(1, 0, 2).reshape(DEPTH, 3, 3, D_MODEL)

    shard_groups = [jnp.concatenate([_rows2d(w_in[n]).astype(BF16) for n in grp], axis=0) for grp in GROUPS]
    gathered = allgather_chips(shard_groups, "gather_weights")
    gathered = [lax.dynamic_update_slice(g, s[None], (chip, 0, 0)) for g, s in zip(gathered, shard_groups)]
    G = {}
    for grp, arr in zip(GROUPS, gathered):
        off = 0
        for n in grp:
            rows = w_in[n].size // w_in[n].shape[-1]
            G[n] = arr[:, off:off + rows].reshape((N_CHIPS,) + w_in[n].shape)
            off += rows
    W = {
        "ffn_w_gu": gathered[0],
        "ffn_w_down": [[_unstack(G["ffn_w_down"][:, i, t], 0) for t in range(2)] for i in range(DEPTH)],
        "mla_w_a": [_unstack(G["mla_w_a"][:, j], 0) for j in range(2)],
        "mla_w_qb": [_unstack(G["mla_w_qb"][:, j], 1) for j in range(2)],
        "mla_w_kvb": [_unstack(G["mla_w_kvb"][:, j], 1) for j in range(2)],
        "mla_w_o": [_unstack(G["mla_w_o"][:, j], 0) for j in range(2)],
        "ssd_w_out": [_unstack(G["ssd_w_out"][:, j], 0) for j in range(2)],
    }
    w_in_full = [_unstack(G["ssd_w_in"][:, j], 1) for j in range(2)]
    W["ssd_w_z"] = [w[:, :D_INNER] for w in w_in_full]
    W["ssd_w_xbc"] = [w[:, D_INNER:D_INNER + CONV_DIM] for w in w_in_full]
    W["ssd_w_dt"] = [w[:, D_INNER + CONV_DIM:] for w in w_in_full]
    X = stand_ins(W)

    pos = positions[0]
    y, vjp = jax.vjp(lambda a, b, p_, x_: trunk(a, b, p_, W, x_, pos), x2, mods, P, X)
    dy, loss_cols = loss_head(y, target)
    dx, dmods, dP, dX = vjp(dy)
    loss = lax.psum(jnp.sum(loss_cols), ("x", "y", "c"))

    small_names = ("norm_gain", "ssd_conv_w", "ssd_conv_b", "ssd_norm_gain", "mla_q_a_gain", "mla_kv_a_gain",
                   "mla_q_gain", "mla_k_gain", "ssd_dt_bias", "ssd_a_log", "ssd_d")
    pack1 = jnp.concatenate([dmods.reshape(-1)] + [dP[n].reshape(-1) for n in small_names])
    pack1 = _pad_rows(jnp.pad(pack1, (0, (-pack1.size) % 128)).reshape(-1, 128), 8)
    rows1 = pack1.shape[0]
    g2 = allgather_small(pack1, "gather_small_grads")
    tot = sum_slots(g2.reshape(N_DEV, rows1, 128), "sum_small_grads").reshape(-1)
    n_mod = DEPTH * 9 * D_MODEL
    grads = {"ada_b": tot[:n_mod].reshape(DEPTH, 9 * D_MODEL)}
    off = n_mod
    for n in small_names:
        sz = dP[n].size
        full = tot[off:off + sz].reshape(dP[n].shape)
        off += sz
        if n in small_sharded:
            k = w_in[n].shape[-1]
            full = lax.dynamic_slice_in_dim(full, chip * k, k, axis=full.ndim - 1)
        grads[n] = full
    dmods_all = g2.reshape(N_DEV, -1)[:, :n_mod].reshape(N_DEV, DEPTH, 9 * D_MODEL)
    dm_sh = lax.dynamic_slice_in_dim(dmods_all, chip * n_ada, n_ada, axis=2)
    grads["ada_w"] = jnp.stack([matmul(sc, dm_sh[:, l], "tn", "ada_dw") for l in range(DEPTH)])

    w_in_g = [jnp.concatenate([dX["ssd_w_z"][j], dX["ssd_w_xbc"][j], dX["ssd_w_dt"][j]], axis=1) for j in range(2)]
    per_name = {
        "ffn_w_gu": [dX["ffn_w_gu"][i][t] for i in range(DEPTH) for t in range(2)],
        "ffn_w_down": [dX["ffn_w_down"][i][t].reshape(N_CHIPS, -1, D_MODEL) for i in range(DEPTH) for t in range(2)],
        "mla_w_a": [g.reshape(N_CHIPS, -1, g.shape[-1]) for g in dX["mla_w_a"]],
        "mla_w_qb": dX["mla_w_qb"], "mla_w_kvb": dX["mla_w_kvb"],
        "mla_w_o": [g.reshape(N_CHIPS, -1, D_MODEL) for g in dX["mla_w_o"]],
        "ssd_w_out": [g.reshape(N_CHIPS, -1, D_MODEL) for g in dX["ssd_w_out"]],
        "ssd_w_in": [_stack(g, 1) for g in w_in_g],
    }
    grad_groups = [jnp.concatenate([p for n in grp for p in per_name[n]], axis=1) for grp in GROUPS]
    theirs = pair_exchange(grad_groups, "grads_to_sibling")
    pair = [sum_own_half(a, b, ci, "sum_pair", BF16) for a, b in zip(grad_groups, theirs)]
    landed = scatter_chips(pair, "grads_to_chips")
    landed = [lax.dynamic_update_slice(a, lax.dynamic_slice_in_dim(p, chip, 1, axis=0), (chip, 0, 0)) for a, p in zip(landed, pair)]
    half_sums = [sum_slots(a, "sum_chips") for a in landed]
    totals = pair_allgather(half_sums, "grad_halves_swap")
    totals = [lax.dynamic_update_slice(t, h, (ci * h.shape[0], 0)) for t, h in zip(totals, half_sums)]
    for grp, arr in zip(GROUPS, totals):
        off = 0
        for n in grp:
            rows = w_in[n].size // w_in[n].shape[-1]
            grads[n] = arr[off:off + rows].reshape(w_in[n].shape)
            off += rows

    deltas, new_m, new_v = {}, {}, {}
    for n in names:
        w = w_in[n]
        d, nm, nv = adamw(_rows2d(w), _rows2d(m_in[n]), _rows2d(v_in[n]), _rows2d(grads[n]))
        deltas[n], new_m[n], new_v[n] = d.reshape(w.shape), nm.reshape(w.shape), nv.reshape(w.shape)

    return (loss, dx[None], *[grads[n] for n in names], *[deltas[n] for n in names],
            *[new_m[n] for n in names], *[new_v[n] for n in names])
```
